```python
import math
import jax, jax.numpy as jnp
from jax import lax
import numpy as np

D_MODEL = 2048
BATCH = 1
SEQ = 16384
DEPTH = 1

CHUNK = 128
GM_GROUP_DIM = 128
GM_WIDTH = D_MODEL // 2
GM_GROUPS = GM_WIDTH // GM_GROUP_DIM
RET_HEADS = 8
RET_QK_WIDTH = D_MODEL // 2
RET_V_WIDTH = D_MODEL
RET_QK_DIM = RET_QK_WIDTH // RET_HEADS
RET_V_DIM = RET_V_WIDTH // RET_HEADS
N_EXPERTS = 32
TOP_K = 4
D_FF = D_MODEL
SWIGLU_LIMIT = 7.0
SWIGLU_ALPHA = 1.702
MOE_BLOCK = 128
ROPE_BASE = 10000.0
EPS = 1e-6
IN_SPLITS = (GM_WIDTH, GM_WIDTH, RET_QK_WIDTH, RET_QK_WIDTH, RET_V_WIDTH, RET_V_WIDTH, D_MODEL, D_MODEL)
IN_WIDTH = sum(IN_SPLITS)

kernel_name = "hybrid_gmlp_retention_moe_block"


def rms_norm(x, g):
    xf = x.astype(jnp.float32)
    y = xf * lax.rsqrt(jnp.mean(xf * xf, axis=-1, keepdims=True) + EPS)
    return (y * g.astype(jnp.float32)).astype(x.dtype)


def rope(t, cos, sin):
    half = t.shape[-1] // 2
    t1, t2 = t[..., :half], t[..., half:]
    c = cos[None, :, None, :]
    s = sin[None, :, None, :]
    return jnp.concatenate([t1 * c - t2 * s, t2 * c + t1 * s], axis=-1)


def spatial_gating(u, v, ln_g, ln_b, ws, bs):
    B, S, W = u.shape
    N = S // CHUNK
    vf = v.astype(jnp.float32)
    mu = jnp.mean(vf, axis=-1, keepdims=True)
    var = jnp.mean(jnp.square(vf - mu), axis=-1, keepdims=True)
    vn = (vf - mu) * lax.rsqrt(var + EPS) * ln_g.astype(jnp.float32) + ln_b.astype(jnp.float32)
    vn = vn.reshape(B, N, CHUNK, GM_GROUPS, GM_GROUP_DIM)
    causal = jnp.tril(jnp.ones((CHUNK, CHUNK), dtype=bool))
    w = jnp.where(causal[None], ws.astype(jnp.float32), 0.0)
    mixed = jnp.einsum('gts,bnsgd->bntgd', w, vn) + bs.astype(jnp.float32).T[None, None, :, :, None]
    return u * mixed.reshape(B, S, W).astype(u.dtype)


def retention(q, k, v):
    B, S, _ = q.shape
    N = S // CHUNK
    H, dk, dv = RET_HEADS, RET_QK_DIM, RET_V_DIM
    pos = jnp.arange(S, dtype=jnp.float32)
    inv_freq = ROPE_BASE ** (-jnp.arange(0, dk, 2, dtype=jnp.float32) / dk)
    ang = pos[:, None] * inv_freq[None, :]
    cos, sin = jnp.cos(ang), jnp.sin(ang)
    qh = rope(q.astype(jnp.float32).reshape(B, S, H, dk), cos, sin)
    kh = rope(k.astype(jnp.float32).reshape(B, S, H, dk), cos, sin) * (dk ** -0.5)
    vh = v.astype(jnp.float32).reshape(B, S, H, dv)

    log_gamma = jnp.log1p(-jnp.exp2(-5.0 - jnp.arange(H, dtype=jnp.float32)))
    idx = jnp.arange(CHUNK, dtype=jnp.float32)
    diff = idx[:, None] - idx[None, :]
    decay_in = jnp.where(diff[None] >= 0, jnp.exp(log_gamma[:, None, None] * jnp.maximum(diff, 0.0)[None]), 0.0)
    xi = jnp.exp(log_gamma[None, :] * (idx[:, None] + 1.0))
    zeta = jnp.exp(log_gamma[None, :] * (CHUNK - 1.0 - idx[:, None]))
    chunk_decay = jnp.exp(log_gamma * CHUNK)

    def to_chunks(t):
        return jnp.moveaxis(t.reshape(B, N, CHUNK, H, t.shape[-1]), 1, 0)

    def step(state, qkv):
        qc, kc, vc = qkv
        scores = jnp.einsum('bthd,bshd->bhts', qc, kc) * decay_in[None]
        o = jnp.einsum('bhts,bshe->bthe', scores, vc) \
            + jnp.einsum('bthd,bhde->bthe', qc, state) * xi[None, :, :, None]
        state = state * chunk_decay[None, :, None, None] \
            + jnp.einsum('bshd,bshe->bhde', kc * zeta[None, :, :, None], vc)
        return state, o

    state0 = jnp.zeros((B, H, dk, dv), jnp.float32)
    _, o = lax.scan(step, state0, (to_chunks(qh), to_chunks(kh), to_chunks(vh)))
    o = jnp.moveaxis(o, 0, 1).reshape(B, S, H, dv)
    mu = jnp.mean(o, axis=-1, keepdims=True)
    var = jnp.mean(jnp.square(o - mu), axis=-1, keepdims=True)
    o = (o - mu) * lax.rsqrt(var + EPS)
    return o.reshape(B, S, H * dv)


def moe_ffn(h, router_w, router_b, w_gate_up, b_gate_up, w_down, b_down):
    B, S, D = h.shape
    T = B * S
    ht = h.reshape(T, D)
    logits = (ht @ router_w + router_b).astype(jnp.float32)
    top_v, top_i = lax.top_k(logits, TOP_K)
    gate_w = jax.nn.softmax(top_v, axis=-1)

    flat_e = top_i.reshape(-1).astype(jnp.int32)
    flat_tok = jnp.repeat(jnp.arange(T, dtype=jnp.int32), TOP_K)
    flat_w = gate_w.reshape(-1)
    order = jnp.argsort(flat_e)
    se, stok, sw = flat_e[order], flat_tok[order], flat_w[order]
    counts = jnp.zeros((N_EXPERTS,), jnp.int32).at[flat_e].add(1)
    padded = (counts + MOE_BLOCK - 1) // MOE_BLOCK * MOE_BLOCK
    pad_end = jnp.cumsum(padded)
    pad_start = pad_end - padded
    sort_start = jnp.cumsum(counts) - counts
    rank = jnp.arange(T * TOP_K, dtype=jnp.int32) - sort_start[se]
    dest = pad_start[se] + rank
    n_blocks = (T * TOP_K + MOE_BLOCK - 1) // MOE_BLOCK + N_EXPERTS
    rows = n_blocks * MOE_BLOCK
    row_tok = jnp.full((rows,), T, jnp.int32).at[dest].set(stok)
    row_w = jnp.zeros((rows,), jnp.float32).at[dest].set(sw)
    block_start = jnp.arange(n_blocks, dtype=jnp.int32) * MOE_BLOCK
    block_e = jnp.minimum(jnp.searchsorted(pad_end, block_start, side='right'), N_EXPERTS - 1).astype(jnp.int32)
    h_pad = jnp.concatenate([ht, jnp.zeros((1, D), ht.dtype)], axis=0)

    def block_step(acc, blk):
        tok, w, e = blk
        xb = h_pad[tok]
        gu = xb @ w_gate_up[e] + b_gate_up[e]
        gate, up = gu[:, :D_FF], gu[:, D_FF:]
        gate = jnp.minimum(gate, SWIGLU_LIMIT)
        up = jnp.clip(up, -SWIGLU_LIMIT, SWIGLU_LIMIT)
        glu = gate * jax.nn.sigmoid(gate * SWIGLU_ALPHA)
        yb = ((up + 1.0) * glu) @ w_down[e] + b_down[e]
        acc = acc.at[tok].add(yb.astype(jnp.float32) * w[:, None])
        return acc, None

    acc0 = jnp.zeros((T + 1, D), jnp.float32)
    acc, _ = lax.scan(block_step, acc0,
                      (row_tok.reshape(n_blocks, MOE_BLOCK), row_w.reshape(n_blocks, MOE_BLOCK), block_e))
    return acc[:T].astype(h.dtype).reshape(B, S, D)


def setup_inputs(seed: int = 0) -> dict:
    key = jax.random.key(seed)
    ks = jax.random.split(key, 20)
    f32 = jnp.float32
    L, D = DEPTH, D_MODEL

    def nrm(k, shape, scale):
        return jax.random.normal(k, shape, f32) * scale

    return {
        "x": jax.random.normal(ks[0], (BATCH, SEQ, D), f32),
        "norm1_g": 1.0 + nrm(ks[1], (L, D), 0.01),
        "w_in": nrm(ks[2], (L, D, IN_WIDTH), D ** -0.5),
        "gm_ln_g": 1.0 + nrm(ks[3], (L, GM_WIDTH), 0.01),
        "gm_ln_b": nrm(ks[4], (L, GM_WIDTH), 0.01),
        "gm_ws": nrm(ks[5], (L, GM_GROUPS, CHUNK, CHUNK), CHUNK ** -0.5),
        "gm_b": 1.0 + nrm(ks[6], (L, GM_GROUPS, CHUNK), 0.01),
        "w_proj_a": nrm(ks[7], (L, GM_WIDTH, D), GM_WIDTH ** -0.5),
        "w_proj_r": nrm(ks[8], (L, RET_V_WIDTH, D), RET_V_WIDTH ** -0.5),
        "w_out": nrm(ks[9], (L, D, D), D ** -0.5),
        "norm2_g": 1.0 + nrm(ks[10], (L, D), 0.01),
        "router_w": nrm(ks[11], (L, D, N_EXPERTS), D ** -0.5),
        "router_b": nrm(ks[12], (L, N_EXPERTS), 0.01),
        "w_gate_up": nrm(ks[13], (L, N_EXPERTS, D, 2 * D_FF), D ** -0.5),
        "b_gate_up": nrm(ks[14], (L, N_EXPERTS, 2 * D_FF), 0.01),
        "w_down": nrm(ks[15], (L, N_EXPERTS, D_FF, D), D_FF ** -0.5),
        "b_down": nrm(ks[16], (L, N_EXPERTS, D), 0.01),
        "final_norm_g": 1.0 + nrm(ks[17], (D,), 0.01),
    }


def reference(x, norm1_g, w_in, gm_ln_g, gm_ln_b, gm_ws, gm_b, w_proj_a, w_proj_r, w_out,
              norm2_g, router_w, router_b, w_gate_up, b_gate_up, w_down, b_down, final_norm_g):
    split_pts = list(np.cumsum(IN_SPLITS)[:-1])
    for l in range(DEPTH):
        h = rms_norm(x, norm1_g[l])
        proj = h @ w_in[l]
        a_u, a_v, r_q, r_k, r_v, r_g, gate_a, gate_r = jnp.split(proj, split_pts, axis=-1)
        a_u = jax.nn.gelu(a_u, approximate=False)
        a_v = jax.nn.gelu(a_v, approximate=False)
        y_a = spatial_gating(a_u, a_v, gm_ln_g[l], gm_ln_b[l], gm_ws[l], gm_b[l]) @ w_proj_a[l]
        ret = retention(r_q, r_k, r_v)
        y_r = (ret * jax.nn.silu(r_g.astype(jnp.float32))).astype(x.dtype) @ w_proj_r[l]
        merged = jax.nn.sigmoid(gate_a) * y_a + jax.nn.sigmoid(gate_r) * y_r
        x = x + merged @ w_out[l]
        h2 = rms_norm(x, norm2_g[l])
        x = x + moe_ffn(h2, router_w[l], router_b[l], w_gate_up[l], b_gate_up[l], w_down[l], b_down[l])
    return rms_norm(x, final_norm_g)
```

```python
import functools

import jax
import jax.numpy as jnp
from jax import lax
from jax.experimental import pallas as pl
from jax.experimental.pallas import tpu as pltpu

F32 = jnp.float32
BF16 = jnp.bfloat16

CHUNK = 128
GM_GROUP_DIM = 128
RET_HEADS = 8
N_EXPERTS = 32
TOP_K = 4
SWIGLU_LIMIT = 7.0
SWIGLU_ALPHA = 1.702
ROPE_BASE = 10000.0
EPS = 1e-6
LANES = 128

VMEM_LIMIT_BYTES = 56 * 1024 * 1024


def _cparams(sem):
    return pltpu.CompilerParams(dimension_semantics=sem, vmem_limit_bytes=VMEM_LIMIT_BYTES)


def _gelu_exact(a):
    return 0.5 * a * (1.0 + lax.erf(a * (2.0 ** -0.5)))


def _inproj_body(x_ref, g1_ref, w_ref, lng_ref, lnb_ref, cos_ref, sin_ref, o_ref, h_ref, *, seg_tiles, k_scale):
    j = pl.program_id(1)

    @pl.when(j == 0)
    def _():
        xf = x_ref[...]
        ms = jnp.mean(xf * xf, axis=-1, keepdims=True)
        h_ref[...] = (xf * lax.rsqrt(ms + EPS) * g1_ref[...]).astype(BF16)

    acc = jnp.dot(h_ref[...], w_ref[...], preferred_element_type=F32)
    t_u, t_v, t_q, t_k, t_rv, t_rg = seg_tiles

    def rope(a):
        outs = []
        for hd in range(a.shape[1] // LANES):
            ah = a[:, hd * LANES:(hd + 1) * LANES]
            outs.append(ah * cos_ref[...] + pltpu.roll(ah, LANES // 2, axis=1) * sin_ref[...])
        return jnp.concatenate(outs, axis=1)

    @pl.when(j < t_u)
    def _():
        o_ref[...] = _gelu_exact(acc).astype(o_ref.dtype)

    @pl.when((j >= t_u) & (j < t_v))
    def _():
        vf = _gelu_exact(acc)
        mu = jnp.mean(vf, axis=-1, keepdims=True)
        vc = vf - mu
        var = jnp.mean(vc * vc, axis=-1, keepdims=True)
        o_ref[...] = (vc * lax.rsqrt(var + EPS) * lng_ref[...] + lnb_ref[...]).astype(o_ref.dtype)

    @pl.when((j >= t_v) & (j < t_q))
    def _():
        o_ref[...] = rope(acc).astype(o_ref.dtype)

    @pl.when((j >= t_q) & (j < t_k))
    def _():
        o_ref[...] = (rope(acc) * k_scale).astype(o_ref.dtype)

    @pl.when((j >= t_k) & (j < t_rv))
    def _():
        o_ref[...] = acc.astype(o_ref.dtype)

    @pl.when((j >= t_rv) & (j < t_rg))
    def _():
        o_ref[...] = (acc * jax.nn.sigmoid(acc)).astype(o_ref.dtype)

    @pl.when(j >= t_rg)
    def _():
        o_ref[...] = jax.nn.sigmoid(acc).astype(o_ref.dtype)


def _inproj(x2d, g1, w_in_bf, ln_g, ln_b, cos2, sin2, *, gm_width, qk_width, v_width, tm):
    S, D = x2d.shape
    N = w_in_bf.shape[1]
    tn = gm_width
    assert tn == qk_width and v_width % tn == 0 and S % tm == 0 and N % tn == 0
    bounds, acc = [], 0
    for width in (gm_width, gm_width, qk_width, qk_width, v_width, v_width):
        acc += width // tn
        bounds.append(acc)
    body = functools.partial(_inproj_body, seg_tiles=tuple(bounds), k_scale=float(LANES) ** -0.5)
    return pl.pallas_call(
        body,
        grid=(S // tm, N // tn),
        in_specs=[
            pl.BlockSpec((tm, D), lambda i, j: (i, 0)),
            pl.BlockSpec((1, D), lambda i, j: (0, 0)),
            pl.BlockSpec((D, tn), lambda i, j: (0, j)),
            pl.BlockSpec((1, tn), lambda i, j: (0, 0)),
            pl.BlockSpec((1, tn), lambda i, j: (0, 0)),
            pl.BlockSpec((tm, LANES), lambda i, j: (i, 0)),
            pl.BlockSpec((tm, LANES), lambda i, j: (i, 0)),
        ],
        out_specs=pl.BlockSpec((tm, tn), lambda i, j: (i, j)),
        out_shape=jax.ShapeDtypeStruct((S, N), BF16),
        scratch_shapes=[pltpu.VMEM((tm, D), BF16)],
        compiler_params=_cparams(("arbitrary", "arbitrary")),
        name="inproj",
    )(x2d, g1, w_in_bf, ln_g, ln_b, cos2, sin2)


def _mixing_body(cd_ref, u_ref, vn_ref, q_ref, k_ref, v_ref, sg_ref, wm_ref, bs_ref, decay_ref, xi_ref, zeta_ref,
                 ga_ref, gr_ref, state_ref, *, n_chunks, dk, dv):
    @pl.when(pl.program_id(0) == 0)
    def _():
        state_ref[...] = jnp.zeros_like(state_ref)

    def chunk(c, carry):
        rows = pl.ds(pl.multiple_of(c * CHUNK, CHUNK), CHUNK)
        for g in range(wm_ref.shape[0]):
            cols = slice(g * GM_GROUP_DIM, (g + 1) * GM_GROUP_DIM)
            mixed = jnp.dot(wm_ref[g], vn_ref[rows, cols], preferred_element_type=F32) + bs_ref[g]
            ga_ref[rows, cols] = (u_ref[rows, cols].astype(F32) * mixed).astype(ga_ref.dtype)
        for hd in range(RET_HEADS):
            qc = q_ref[rows, hd * dk:(hd + 1) * dk]
            kc = k_ref[rows, hd * dk:(hd + 1) * dk]
            vc = v_ref[rows, hd * dv:(hd + 1) * dv]
            st = state_ref[hd]
            scores = lax.dot_general(qc, kc, (((1,), (1,)), ((), ())), preferred_element_type=F32) * decay_ref[hd]
            inner = jnp.dot(scores.astype(BF16), vc, preferred_element_type=F32)
            cross = jnp.dot(qc, st.astype(BF16), preferred_element_type=F32)
            xi = xi_ref[hd]
            o = inner + cross * jnp.concatenate([xi] * (dv // LANES), axis=1)
            kz = (kc.astype(F32) * zeta_ref[hd]).astype(BF16)
            kv = lax.dot_general(kz, vc, (((0,), (0,)), ((), ())), preferred_element_type=F32)
            state_ref[hd] = st * cd_ref[hd] + kv
            mu = jnp.mean(o, axis=-1, keepdims=True)
            oc = o - mu
            var = jnp.mean(oc * oc, axis=-1, keepdims=True)
            on = oc * lax.rsqrt(var + EPS)
            gr_ref[rows, hd * dv:(hd + 1) * dv] = (on * sg_ref[rows, hd * dv:(hd + 1) * dv].astype(F32)).astype(
                gr_ref.dtype)
        return carry

    lax.fori_loop(0, n_chunks, chunk, 0)


def _mixing(proj, wm, bs_b, decay, xi_b, zeta_b, cd, *, gm_width, qk_width, v_width, tm):
    S = proj.shape[0]
    assert S % tm == 0 and tm % CHUNK == 0
    dk, dv = qk_width // RET_HEADS, v_width // RET_HEADS
    G = wm.shape[0]
    u_blk, vn_blk, q_blk, k_blk = 0, 1, (2 * gm_width) // qk_width, (2 * gm_width + qk_width) // qk_width
    v_blk = (2 * gm_width + 2 * qk_width) // v_width
    sg_blk = v_blk + 1
    body = functools.partial(_mixing_body, n_chunks=tm // CHUNK, dk=dk, dv=dv)
    const3 = lambda i: (0, 0, 0)
    return pl.pallas_call(
        body,
        grid=(S // tm,),
        in_specs=[
            pl.BlockSpec(memory_space=pltpu.SMEM),
            pl.BlockSpec((tm, gm_width), lambda i: (i, u_blk)),
            pl.BlockSpec((tm, gm_width), lambda i: (i, vn_blk)),
            pl.BlockSpec((tm, qk_width), lambda i: (i, q_blk)),
            pl.BlockSpec((tm, qk_width), lambda i: (i, k_blk)),
            pl.BlockSpec((tm, v_width), lambda i: (i, v_blk)),
            pl.BlockSpec((tm, v_width), lambda i: (i, sg_blk)),
            pl.BlockSpec((G, CHUNK, CHUNK), const3),
            pl.BlockSpec((G, CHUNK, GM_GROUP_DIM), const3),
            pl.BlockSpec((RET_HEADS, CHUNK, CHUNK), const3),
            pl.BlockSpec((RET_HEADS, CHUNK, LANES), const3),
            pl.BlockSpec((RET_HEADS, CHUNK, dk), const3),
        ],
        out_specs=[
            pl.BlockSpec((tm, gm_width), lambda i: (i, 0)),
            pl.BlockSpec((tm, v_width), lambda i: (i, 0)),
        ],
        out_shape=[
            jax.ShapeDtypeStruct((S, gm_width), BF16),
            jax.ShapeDtypeStruct((S, v_width), BF16),
        ],
        scratch_shapes=[pltpu.VMEM((RET_HEADS, dk, dv), F32)],
        compiler_params=_cparams(("arbitrary",)),
        name="mixing",
    )(cd, proj, proj, proj, proj, proj, proj, wm, bs_b, decay, xi_b, zeta_b)


def _merge_body(x_ref, ga_ref, gr_ref, sa_ref, sr_ref, wa_ref, wr_ref, wo_ref, g2_ref, rwh_ref, rwl_ref, rb_ref,
                x2_ref, h2_ref, route_ref, cnt_ref, run_ref):
    i = pl.program_id(0)
    tm = x_ref.shape[0]

    @pl.when(i == 0)
    def _():
        run_ref[...] = jnp.zeros_like(run_ref)

    y_a = jnp.dot(ga_ref[...], wa_ref[...], preferred_element_type=F32)
    y_r = jnp.dot(gr_ref[...], wr_ref[...], preferred_element_type=F32)
    merged = sa_ref[...].astype(F32) * y_a + sr_ref[...].astype(F32) * y_r
    x2 = x_ref[...] + jnp.dot(merged.astype(BF16), wo_ref[...], preferred_element_type=F32)
    x2_ref[...] = x2
    ms = jnp.mean(x2 * x2, axis=-1, keepdims=True)
    h2 = x2 * lax.rsqrt(ms + EPS) * g2_ref[...]
    h2_ref[...] = h2

    h_hi = h2.astype(BF16)
    h_lo = (h2 - h_hi.astype(F32)).astype(BF16)
    logits = (jnp.dot(h_hi, rwh_ref[...], preferred_element_type=F32)
              + jnp.dot(h_hi, rwl_ref[...], preferred_element_type=F32)
              + jnp.dot(h_lo, rwh_ref[...], preferred_element_type=F32)) + rb_ref[...]
    lane = lax.broadcasted_iota(jnp.int32, (tm, LANES), 1)
    neg_inf = jnp.float32(-jnp.inf)
    work = jnp.where(lane < N_EXPERTS, logits, neg_inf)

    vals, idxs, sels = [], [], []
    for _ in range(TOP_K):
        m = jnp.max(work, axis=-1, keepdims=True)
        idx = jnp.min(jnp.where(work == m, lane, LANES), axis=-1, keepdims=True)
        sel = lane == idx
        vals.append(m)
        idxs.append(idx)
        sels.append(sel)
        work = jnp.where(sel, neg_inf, work)
    exps = [jnp.exp(v - vals[0]) for v in vals]
    denom = exps[0] + exps[1] + exps[2] + exps[3]
    gates = [e / denom for e in exps]

    onehot = jnp.zeros((tm, LANES), F32)
    for sel in sels:
        onehot = onehot + sel.astype(F32)
    r_iota = lax.broadcasted_iota(jnp.int32, (tm, tm), 0)
    c_iota = lax.broadcasted_iota(jnp.int32, (tm, tm), 1)
    strict_lower = (c_iota < r_iota).astype(BF16)
    before = jnp.dot(strict_lower, onehot.astype(BF16), preferred_element_type=F32) + run_ref[0:1, :]
    ranks = [jnp.sum(jnp.where(sel, before, 0.0), axis=-1, keepdims=True) for sel in sels]
    run_new = run_ref[0:1, :] + jnp.sum(onehot, axis=0, keepdims=True)
    run_ref[...] = jnp.broadcast_to(run_new, run_ref.shape)
    cnt_ref[...] = jnp.broadcast_to(run_new, cnt_ref.shape)

    route = jnp.zeros((tm, LANES), F32)
    for k in range(TOP_K):
        route = jnp.where(lane == k, idxs[k].astype(F32), route)
        route = jnp.where(lane == TOP_K + k, gates[k], route)
        route = jnp.where(lane == 2 * TOP_K + k, ranks[k], route)
    route_ref[...] = route


def _merge(x2d, ga, gr, proj, wa, wr, wo, g2, rw_hi, rw_lo, rb, *, gate_a_blk, gate_r_blk, tm):
    S, D = x2d.shape
    assert S % tm == 0
    const = lambda i: (0, 0)
    resident = functools.partial(pl.BlockSpec, index_map=const, pipeline_mode=pl.Buffered(1))
    return pl.pallas_call(
        _merge_body,
        grid=(S // tm,),
        in_specs=[
            pl.BlockSpec((tm, D), lambda i: (i, 0)),
            pl.BlockSpec((tm, ga.shape[1]), lambda i: (i, 0)),
            pl.BlockSpec((tm, gr.shape[1]), lambda i: (i, 0)),
            pl.BlockSpec((tm, D), lambda i: (i, gate_a_blk)),
            pl.BlockSpec((tm, D), lambda i: (i, gate_r_blk)),
            resident(wa.shape),
            resident(wr.shape),
            resident(wo.shape),
            pl.BlockSpec((1, D), const),
            resident(rw_hi.shape),
            resident(rw_lo.shape),
            pl.BlockSpec((1, LANES), const),
        ],
        out_specs=[
            pl.BlockSpec((tm, D), lambda i: (i, 0)),
            pl.BlockSpec((tm, D), lambda i: (i, 0)),
            pl.BlockSpec((tm, LANES), lambda i: (i, 0)),
            pl.BlockSpec((8, LANES), const),
        ],
        out_shape=[
            jax.ShapeDtypeStruct((S, D), F32),
            jax.ShapeDtypeStruct((S, D), F32),
            jax.ShapeDtypeStruct((S, LANES), F32),
            jax.ShapeDtypeStruct((8, LANES), F32),
        ],
        scratch_shapes=[pltpu.VMEM((8, LANES), F32)],
        compiler_params=_cparams(("arbitrary",)),
        name="merge",
    )(x2d, ga, gr, proj, proj, wa, wr, wo, g2, rw_hi, rw_lo, rb)


def _experts_body(be_ref, nvb_ref, tok_cur_ref, tok_nxt_ref, h2_hbm, wg_ref, wu_ref, wd_ref, bg_ref, bu_ref, bd_ref,
                  y_ref, xs_ref, xb_ref, sem, *, tm):
    i = pl.program_id(0)
    j = pl.program_id(1)
    nj = pl.num_programs(1)
    nvb = nvb_ref[0]
    slot = i % 2

    def start_rows(tok_ref, dst_slot):
        def issue(r, carry):
            t = tok_ref[0, 0, r]
            pltpu.make_async_copy(h2_hbm.at[pl.ds(t, 1)], xs_ref.at[dst_slot, pl.ds(r, 1)], sem.at[dst_slot]).start()
            return carry
        lax.fori_loop(0, tm, issue, 0)

    def wait_rows(dst_slot):
        pltpu.make_async_copy(h2_hbm.at[pl.ds(0, tm)], xs_ref.at[dst_slot], sem.at[dst_slot]).wait()

    @pl.when((j == 0) & (i < nvb))
    def _():
        @pl.when(i == 0)
        def _():
            start_rows(tok_cur_ref, 0)

        wait_rows(slot)

        @pl.when(i + 1 < nvb)
        def _():
            start_rows(tok_nxt_ref, 1 - slot)

        xb_ref[...] = xs_ref[slot].astype(BF16)

    @pl.when(i < nvb)
    def _():
        xb = xb_ref[...]
        gate = jnp.dot(xb, wg_ref[0], preferred_element_type=F32) + bg_ref[0]
        up = jnp.dot(xb, wu_ref[0], preferred_element_type=F32) + bu_ref[0]
        gate = jnp.minimum(gate, SWIGLU_LIMIT)
        up = jnp.clip(up, -SWIGLU_LIMIT, SWIGLU_LIMIT)
        glu = gate * jax.nn.sigmoid(gate * SWIGLU_ALPHA)
        act = ((up + 1.0) * glu).astype(BF16)
        part = jnp.dot(act, wd_ref[0], preferred_element_type=F32)

        @pl.when(j == 0)
        def _():
            y_ref[...] = part + bd_ref[0]

        @pl.when(j > 0)
        def _():
            y_ref[...] += part

    @pl.when((i >= nvb) & (j == 0))
    def _():
        y_ref[...] = jnp.zeros_like(y_ref)


def _experts(h2, row_tok, block_e, nvb, wgu_bf, bgu, wd_bf, bd, *, tm, tf):
    T, D = h2.shape
    E, _, two_ff = wgu_bf.shape
    d_ff = two_ff // 2
    nb = row_tok.shape[0]
    nj = d_ff // tf
    assert d_ff % tf == 0

    def live(i, j, be, nv):
        ii = jnp.minimum(i, nv[0] - 1)
        jj = jnp.where(i < nv[0], j, nj - 1)
        return ii, jj

    def wg_map(i, j, be, nv):
        ii, jj = live(i, j, be, nv)
        return be[ii], 0, jj

    def wu_map(i, j, be, nv):
        ii, jj = live(i, j, be, nv)
        return be[ii], 0, nj + jj

    def wd_map(i, j, be, nv):
        ii, jj = live(i, j, be, nv)
        return be[ii], jj, 0

    def bg_map(i, j, be, nv):
        ii, jj = live(i, j, be, nv)
        return be[ii], 0, jj

    def bu_map(i, j, be, nv):
        ii, jj = live(i, j, be, nv)
        return be[ii], 0, nj + jj

    def bd_map(i, j, be, nv):
        ii, _ = live(i, j, be, nv)
        return be[ii], 0, 0

    def cur_map(i, j, be, nv):
        return jnp.minimum(i, nv[0] - 1), 0, 0

    def nxt_map(i, j, be, nv):
        return jnp.minimum(i + 1, nv[0] - 1), 0, 0

    def y_map(i, j, be, nv):
        return i, 0

    grid_spec = pltpu.PrefetchScalarGridSpec(
        num_scalar_prefetch=2,
        grid=(nb, nj),
        in_specs=[
            pl.BlockSpec((1, 1, tm), cur_map, memory_space=pltpu.SMEM),
            pl.BlockSpec((1, 1, tm), nxt_map, memory_space=pltpu.SMEM),
            pl.BlockSpec(memory_space=pl.ANY),
            pl.BlockSpec((1, D, tf), wg_map),
            pl.BlockSpec((1, D, tf), wu_map),
            pl.BlockSpec((1, tf, D), wd_map),
            pl.BlockSpec((1, 1, tf), bg_map),
            pl.BlockSpec((1, 1, tf), bu_map),
            pl.BlockSpec((1, 1, D), bd_map),
        ],
        out_specs=pl.BlockSpec((tm, D), y_map),
        scratch_shapes=[
            pltpu.VMEM((2, tm, D), F32),
            pltpu.VMEM((tm, D), BF16),
            pltpu.SemaphoreType.DMA((2,)),
        ],
    )
    return pl.pallas_call(
        functools.partial(_experts_body, tm=tm),
        grid_spec=grid_spec,
        out_shape=jax.ShapeDtypeStruct((nb * tm, D), F32),
        compiler_params=_cparams(("arbitrary", "arbitrary")),
        name="experts",
    )(block_e, nvb, row_tok, row_tok, h2, wgu_bf, wgu_bf, wd_bf, bgu, bgu, bd)


def _combine_body(dest_ref, x2_ref, route_ref, gf_ref, ys_hbm, o_ref, buf_ref, sem, *, tm):
    def issue(t, carry):
        for k in range(TOP_K):
            d = dest_ref[0, 0, t * TOP_K + k]
            pltpu.make_async_copy(ys_hbm.at[pl.ds(d, 1)], buf_ref.at[k, pl.ds(t, 1)], sem.at[0]).start()
        return carry

    lax.fori_loop(0, tm, issue, 0)
    for k in range(TOP_K):
        pltpu.make_async_copy(ys_hbm.at[pl.ds(0, tm)], buf_ref.at[k], sem.at[0]).wait()

    route = route_ref[...]
    x3 = x2_ref[...]
    for k in range(TOP_K):
        x3 = x3 + route[:, TOP_K + k:TOP_K + k + 1] * buf_ref[k]
    ms = jnp.mean(x3 * x3, axis=-1, keepdims=True)
    o_ref[...] = x3 * lax.rsqrt(ms + EPS) * gf_ref[...]


def _combine(dest, x2, route, gf, ys, *, tm):
    T, D = x2.shape
    assert T % tm == 0
    return pl.pallas_call(
        functools.partial(_combine_body, tm=tm),
        grid=(T // tm,),
        in_specs=[
            pl.BlockSpec((1, 1, tm * TOP_K), lambda i: (i, 0, 0), memory_space=pltpu.SMEM),
            pl.BlockSpec((tm, D), lambda i: (i, 0)),
            pl.BlockSpec((tm, LANES), lambda i: (i, 0)),
            pl.BlockSpec((1, D), lambda i: (0, 0)),
            pl.BlockSpec(memory_space=pl.ANY),
        ],
        out_specs=pl.BlockSpec((tm, D), lambda i: (i, 0)),
        out_shape=jax.ShapeDtypeStruct((T, D), F32),
        scratch_shapes=[pltpu.VMEM((TOP_K, tm, D), F32), pltpu.SemaphoreType.DMA((1,))],
        compiler_params=_cparams(("arbitrary",)),
        name="combine",
    )(dest, x2, route, gf, ys)


def _retention_tables(S, dk):
    pos = jnp.arange(S, dtype=F32)
    inv_freq = ROPE_BASE ** (-jnp.arange(0, dk, 2, dtype=F32) / dk)
    ang = pos[:, None] * inv_freq[None, :]
    cos, sin = jnp.cos(ang), jnp.sin(ang)
    cos2 = jnp.concatenate([cos, cos], axis=-1)
    sin2 = jnp.concatenate([-sin, sin], axis=-1)
    log_gamma = jnp.log1p(-jnp.exp2(-5.0 - jnp.arange(RET_HEADS, dtype=F32)))
    idx = jnp.arange(CHUNK, dtype=F32)
    diff = idx[:, None] - idx[None, :]
    decay = jnp.where(diff[None] >= 0, jnp.exp(log_gamma[:, None, None] * jnp.maximum(diff, 0.0)[None]), 0.0)
    xi = jnp.exp(log_gamma[:, None] * (idx[None, :] + 1.0))
    zeta = jnp.exp(log_gamma[:, None] * (CHUNK - 1.0 - idx[None, :]))
    cd = jnp.exp(log_gamma * CHUNK)
    xi_b = jnp.broadcast_to(xi[:, :, None], (RET_HEADS, CHUNK, LANES))
    zeta_b = jnp.broadcast_to(zeta[:, :, None], (RET_HEADS, CHUNK, dk))
    return cos2, sin2, decay, xi_b, zeta_b, cd


TM_INPROJ = 512
TM_MIXING = 256
TM_MERGE = 256
TM_EXPERT = 512
TF_EXPERT = 512
TM_COMBINE = 256


def kernel(x, norm1_g, w_in, gm_ln_g, gm_ln_b, gm_ws, gm_b, w_proj_a, w_proj_r, w_out, norm2_g, router_w, router_b,
           w_gate_up, b_gate_up, w_down, b_down, final_norm_g):
    B, S, D = x.shape
    assert B == 1 and norm1_g.shape[0] == 1, "single sequence, depth 1"
    gm_width = w_proj_a.shape[1]
    v_width = w_proj_r.shape[1]
    qk_width = (w_in.shape[2] - 2 * gm_width - 2 * v_width - 2 * D) // 2
    dk = qk_width // RET_HEADS
    assert dk == LANES and gm_ws.shape[2] == CHUNK
    G = gm_ws.shape[1]
    E = router_w.shape[2]
    d_ff = w_down.shape[2]
    x2d = x.reshape(S, D)

    cos2, sin2, decay, xi_b, zeta_b, cd = _retention_tables(S, dk)
    proj = _inproj(x2d, norm1_g[0][None], w_in[0].astype(BF16), gm_ln_g[0][None], gm_ln_b[0][None], cos2, sin2,
                   gm_width=gm_width, qk_width=qk_width, v_width=v_width, tm=min(TM_INPROJ, S))

    causal = jnp.tril(jnp.ones((CHUNK, CHUNK), dtype=bool))
    wm = jnp.where(causal[None], gm_ws[0], 0.0).astype(BF16)
    bs_b = jnp.broadcast_to(gm_b[0][:, :, None], (G, CHUNK, GM_GROUP_DIM))
    ga, gr = _mixing(proj, wm, bs_b, decay, xi_b, zeta_b, cd,
                     gm_width=gm_width, qk_width=qk_width, v_width=v_width, tm=min(TM_MIXING, S))

    rw = jnp.pad(router_w[0], ((0, 0), (0, LANES - E)))
    rw_hi = rw.astype(BF16)
    rw_lo = (rw - rw_hi.astype(F32)).astype(BF16)
    rb = jnp.pad(router_b[0], (0, LANES - E))[None]
    gates_off = 2 * gm_width + 2 * qk_width + 2 * v_width
    x2, h2, route, cnt = _merge(x2d, ga, gr, proj, w_proj_a[0].astype(BF16), w_proj_r[0].astype(BF16),
                                w_out[0].astype(BF16), norm2_g[0][None], rw_hi, rw_lo, rb,
                                gate_a_blk=gates_off // D, gate_r_blk=gates_off // D + 1, tm=min(TM_MERGE, S))

    tm_e = TM_EXPERT
    eidx = route[:, 0:TOP_K].astype(jnp.int32)
    rank = route[:, 2 * TOP_K:3 * TOP_K].astype(jnp.int32)
    counts = cnt[0, :E].astype(jnp.int32)
    padded = (counts + tm_e - 1) // tm_e * tm_e
    pad_end = jnp.cumsum(padded)
    pad_start = pad_end - padded
    dest = pad_start[eidx] + rank
    nb = (S * TOP_K + tm_e - 1) // tm_e + E
    nvb = (pad_end[-1] // tm_e).astype(jnp.int32)[None]
    block_start = jnp.arange(nb, dtype=jnp.int32) * tm_e
    block_e = jnp.minimum(jnp.searchsorted(pad_end, block_start, side='right'), E - 1).astype(jnp.int32)
    tok = jnp.broadcast_to(jnp.arange(S, dtype=jnp.int32)[:, None], (S, TOP_K))
    row_tok = jnp.zeros((nb * tm_e,), jnp.int32).at[dest.reshape(-1)].set(tok.reshape(-1))

    ys = _experts(h2, row_tok.reshape(nb, 1, tm_e), block_e, nvb, w_gate_up[0].astype(BF16),
                  b_gate_up[0][:, None, :], w_down[0].astype(BF16), b_down[0][:, None, :],
                  tm=tm_e, tf=min(TF_EXPERT, d_ff))

    tm_c = min(TM_COMBINE, S)
    out = _combine(dest.reshape(S // tm_c, 1, tm_c * TOP_K), x2, route, final_norm_g[None], ys, tm=tm_c)
    return out.reshape(B, S, D)
```

```python
import functools

import jax
import jax.numpy as jnp
from jax import lax
from jax.experimental import pallas as pl
from jax.experimental.pallas import tpu as pltpu

F32 = jnp.float32
BF16 = jnp.bfloat16

CHUNK = 128
GM_GROUP_DIM = 128
RET_HEADS = 8
N_EXPERTS = 32
TOP_K = 4
SWIGLU_LIMIT = 7.0
SWIGLU_ALPHA = 1.702
ROPE_BASE = 10000.0
EPS = 1e-6
LANES = 128

VMEM_LIMIT_BYTES = 56 * 1024 * 1024


def _cparams(sem):
    return pltpu.CompilerParams(dimension_semantics=sem, vmem_limit_bytes=VMEM_LIMIT_BYTES)


def _gelu_exact(a):
    return 0.5 * a * (1.0 + lax.erf(a * (2.0 ** -0.5)))


def _inproj_body(x_ref, g1_ref, w_ref, lng_ref, lnb_ref, cos_ref, sin_ref, o_ref, h_ref, *, seg_tiles, k_scale):
    j = pl.program_id(1)

    @pl.when(j == 0)
    def _():
        xf = x_ref[...]
        ms = jnp.mean(xf * xf, axis=-1, keepdims=True)
        h_ref[...] = (xf * lax.rsqrt(ms + EPS) * g1_ref[...]).astype(BF16)

    acc = jnp.dot(h_ref[...], w_ref[...], preferred_element_type=F32)
    t_u, t_v, t_q, t_k, t_rv, t_rg = seg_tiles

    def rope(a):
        outs = []
        for hd in range(a.shape[1] // LANES):
            ah = a[:, hd * LANES:(hd + 1) * LANES]
            outs.append(ah * cos_ref[...] + pltpu.roll(ah, LANES // 2, axis=1) * sin_ref[...])
        return jnp.concatenate(outs, axis=1)

    @pl.when(j < t_u)
    def _():
        o_ref[...] = _gelu_exact(acc).astype(o_ref.dtype)

    @pl.when((j >= t_u) & (j < t_v))
    def _():
        vf = _gelu_exact(acc)
        mu = jnp.mean(vf, axis=-1, keepdims=True)
        vc = vf - mu
        var = jnp.mean(vc * vc, axis=-1, keepdims=True)
        o_ref[...] = (vc * lax.rsqrt(var + EPS) * lng_ref[...] + lnb_ref[...]).astype(o_ref.dtype)

    @pl.when((j >= t_v) & (j < t_q))
    def _():
        o_ref[...] = rope(acc).astype(o_ref.dtype)

    @pl.when((j >= t_q) & (j < t_k))
    def _():
        o_ref[...] = (rope(acc) * k_scale).astype(o_ref.dtype)

    @pl.when((j >= t_k) & (j < t_rv))
    def _():
        o_ref[...] = acc.astype(o_ref.dtype)

    @pl.when((j >= t_rv) & (j < t_rg))
    def _():
        o_ref[...] = (acc * jax.nn.sigmoid(acc)).astype(o_ref.dtype)

    @pl.when(j >= t_rg)
    def _():
        o_ref[...] = jax.nn.sigmoid(acc).astype(o_ref.dtype)


def _inproj(x2d, g1, w_in_bf, ln_g, ln_b, cos2, sin2, *, gm_width, qk_width, v_width, tm):
    S, D = x2d.shape
    N = w_in_bf.shape[1]
    tn = gm_width
    assert tn == qk_width and v_width % tn == 0 and S % tm == 0 and N % tn == 0
    bounds, acc = [], 0
    for width in (gm_width, gm_width, qk_width, qk_width, v_width, v_width):
        acc += width // tn
        bounds.append(acc)
    body = functools.partial(_inproj_body, seg_tiles=tuple(bounds), k_scale=float(LANES) ** -0.5)
    return pl.pallas_call(
        body,
        grid=(S // tm, N // tn),
        in_specs=[
            pl.BlockSpec((tm, D), lambda i, j: (i, 0)),
            pl.BlockSpec((1, D), lambda i, j: (0, 0)),
            pl.BlockSpec((D, tn), lambda i, j: (0, j)),
            pl.BlockSpec((1, tn), lambda i, j: (0, 0)),
            pl.BlockSpec((1, tn), lambda i, j: (0, 0)),
            pl.BlockSpec((tm, LANES), lambda i, j: (i, 0)),
            pl.BlockSpec((tm, LANES), lambda i, j: (i, 0)),
        ],
        out_specs=pl.BlockSpec((tm, tn), lambda i, j: (i, j)),
        out_shape=jax.ShapeDtypeStruct((S, N), BF16),
        scratch_shapes=[pltpu.VMEM((tm, D), BF16)],
        compiler_params=_cparams(("arbitrary", "arbitrary")),
        name="inproj",
    )(x2d, g1, w_in_bf, ln_g, ln_b, cos2, sin2)


def _mixing_body(cd_ref, u_ref, vn_ref, q_ref, k_ref, v_ref, sg_ref, wm_ref, bs_ref, decay_ref, xi_ref, zeta_ref,
                 ga_ref, gr_ref, state_ref, *, n_chunks, dk, dv):
    @pl.when(pl.program_id(0) == 0)
    def _():
        state_ref[...] = jnp.zeros_like(state_ref)

    def chunk(c, carry):
        rows = pl.ds(pl.multiple_of(c * CHUNK, CHUNK), CHUNK)
        for g in range(wm_ref.shape[0]):
            cols = slice(g * GM_GROUP_DIM, (g + 1) * GM_GROUP_DIM)
            mixed = jnp.dot(wm_ref[g], vn_ref[rows, cols], preferred_element_type=F32) + bs_ref[g]
            ga_ref[rows, cols] = (u_ref[rows, cols].astype(F32) * mixed).astype(ga_ref.dtype)
        for hd in range(RET_HEADS):
            qc = q_ref[rows, hd * dk:(hd + 1) * dk]
            kc = k_ref[rows, hd * dk:(hd + 1) * dk]
            vc = v_ref[rows, hd * dv:(hd + 1) * dv]
            st = state_ref[hd]
            scores = lax.dot_general(qc, kc, (((1,), (1,)), ((), ())), preferred_element_type=F32) * decay_ref[hd]
            inner = jnp.dot(scores.astype(BF16), vc, preferred_element_type=F32)
            cross = jnp.dot(qc, st.astype(BF16), preferred_element_type=F32)
            xi = xi_ref[hd]
            o = inner + cross * jnp.concatenate([xi] * (dv // LANES), axis=1)
            kz = (kc.astype(F32) * zeta_ref[hd]).astype(BF16)
            kv = lax.dot_general(kz, vc, (((0,), (0,)), ((), ())), preferred_element_type=F32)
            state_ref[hd] = st * cd_ref[hd] + kv
            mu = jnp.mean(o, axis=-1, keepdims=True)
            oc = o - mu
            var = jnp.mean(oc * oc, axis=-1, keepdims=True)
            on = oc * lax.rsqrt(var + EPS)
            gr_ref[rows, hd * dv:(hd + 1) * dv] = (on * sg_ref[rows, hd * dv:(hd + 1) * dv].astype(F32)).astype(
                gr_ref.dtype)
        return carry

    lax.fori_loop(0, n_chunks, chunk, 0)


def _mixing(proj, wm, bs_b, decay, xi_b, zeta_b, cd, *, gm_width, qk_width, v_width, tm):
    S = proj.shape[0]
    assert S % tm == 0 and tm % CHUNK == 0
    dk, dv = qk_width // RET_HEADS, v_width // RET_HEADS
    G = wm.shape[0]
    u_blk, vn_blk, q_blk, k_blk = 0, 1, (2 * gm_width) // qk_width, (2 * gm_width + qk_width) // qk_width
    v_blk = (2 * gm_width + 2 * qk_width) // v_width
    sg_blk = v_blk + 1
    body = functools.partial(_mixing_body, n_chunks=tm // CHUNK, dk=dk, dv=dv)
    const3 = lambda i: (0, 0, 0)
    return pl.pallas_call(
        body,
        grid=(S // tm,),
        in_specs=[
            pl.BlockSpec(memory_space=pltpu.SMEM),
            pl.BlockSpec((tm, gm_width), lambda i: (i, u_blk)),
            pl.BlockSpec((tm, gm_width), lambda i: (i, vn_blk)),
            pl.BlockSpec((tm, qk_width), lambda i: (i, q_blk)),
            pl.BlockSpec((tm, qk_width), lambda i: (i, k_blk)),
            pl.BlockSpec((tm, v_width), lambda i: (i, v_blk)),
            pl.BlockSpec((tm, v_width), lambda i: (i, sg_blk)),
            pl.BlockSpec((G, CHUNK, CHUNK), const3),
            pl.BlockSpec((G, CHUNK, GM_GROUP_DIM), const3),
            pl.BlockSpec((RET_HEADS, CHUNK, CHUNK), const3),
            pl.BlockSpec((RET_HEADS, CHUNK, LANES), const3),
            pl.BlockSpec((RET_HEADS, CHUNK, dk), const3),
        ],
        out_specs=[
            pl.BlockSpec((tm, gm_width), lambda i: (i, 0)),
            pl.BlockSpec((tm, v_width), lambda i: (i, 0)),
        ],
        out_shape=[
            jax.ShapeDtypeStruct((S, gm_width), BF16),
            jax.ShapeDtypeStruct((S, v_width), BF16),
        ],
        scratch_shapes=[pltpu.VMEM((RET_HEADS, dk, dv), F32)],
        compiler_params=_cparams(("arbitrary",)),
        name="mixing",
    )(cd, proj, proj, proj, proj, proj, proj, wm, bs_b, decay, xi_b, zeta_b)


def _merge_body(x_ref, ga_ref, gr_ref, sa_ref, sr_ref, wa_ref, wr_ref, wo_ref, g2_ref, rwh_ref, rwl_ref, rb_ref,
                x2_ref, h2_ref, route_ref, cnt_ref, run_ref):
    i = pl.program_id(0)
    tm = x_ref.shape[0]

    @pl.when(i == 0)
    def _():
        run_ref[...] = jnp.zeros_like(run_ref)

    y_a = jnp.dot(ga_ref[...], wa_ref[...], preferred_element_type=F32)
    y_r = jnp.dot(gr_ref[...], wr_ref[...], preferred_element_type=F32)
    merged = sa_ref[...].astype(F32) * y_a + sr_ref[...].astype(F32) * y_r
    x2 = x_ref[...] + jnp.dot(merged.astype(BF16), wo_ref[...], preferred_element_type=F32)
    x2_ref[...] = x2
    ms = jnp.mean(x2 * x2, axis=-1, keepdims=True)
    h2 = x2 * lax.rsqrt(ms + EPS) * g2_ref[...]
    h2_ref[...] = h2

    h_hi = h2.astype(BF16)
    h_lo = (h2 - h_hi.astype(F32)).astype(BF16)
    logits = (jnp.dot(h_hi, rwh_ref[...], preferred_element_type=F32)
              + jnp.dot(h_hi, rwl_ref[...], preferred_element_type=F32)
              + jnp.dot(h_lo, rwh_ref[...], preferred_element_type=F32)) + rb_ref[...]
    lane = lax.broadcasted_iota(jnp.int32, (tm, LANES), 1)
    neg_inf = jnp.float32(-jnp.inf)
    work = jnp.where(lane < N_EXPERTS, logits, neg_inf)

    vals, idxs, sels = [], [], []
    for _ in range(TOP_K):
        m = jnp.max(work, axis=-1, keepdims=True)
        idx = jnp.min(jnp.where(work == m, lane, LANES), axis=-1, keepdims=True)
        sel = lane == idx
        vals.append(m)
        idxs.append(idx)
        sels.append(sel)
        work = jnp.where(sel, neg_inf, work)
    exps = [jnp.exp(v - vals[0]) for v in vals]
    denom = exps[0] + exps[1] + exps[2] + exps[3]
    gates = [e / denom for e in exps]

    onehot = jnp.zeros((tm, LANES), F32)
    for sel in sels:
        onehot = onehot + sel.astype(F32)
    r_iota = lax.broadcasted_iota(jnp.int32, (tm, tm), 0)
    c_iota = lax.broadcasted_iota(jnp.int32, (tm, tm), 1)
    strict_lower = (c_iota < r_iota).astype(BF16)
    before = jnp.dot(strict_lower, onehot.astype(BF16), preferred_element_type=F32) + run_ref[0:1, :]
    ranks = [jnp.sum(jnp.where(sel, before, 0.0), axis=-1, keepdims=True) for sel in sels]
    run_new = run_ref[0:1, :] + jnp.sum(onehot, axis=0, keepdims=True)
    run_ref[...] = jnp.broadcast_to(run_new, run_ref.shape)
    cnt_ref[...] = jnp.broadcast_to(run_new, cnt_ref.shape)

    route = jnp.zeros((tm, LANES), F32)
    for k in range(TOP_K):
        route = jnp.where(lane == k, idxs[k].astype(F32), route)
        route = jnp.where(lane == TOP_K + k, gates[k], route)
        route = jnp.where(lane == 2 * TOP_K + k, ranks[k], route)
    route_ref[...] = route


def _merge(x2d, ga, gr, proj, wa, wr, wo, g2, rw_hi, rw_lo, rb, *, gate_a_blk, gate_r_blk, tm):
    S, D = x2d.shape
    assert S % tm == 0
    const = lambda i: (0, 0)
    resident = functools.partial(pl.BlockSpec, index_map=const, pipeline_mode=pl.Buffered(1))
    return pl.pallas_call(
        _merge_body,
        grid=(S // tm,),
        in_specs=[
            pl.BlockSpec((tm, D), lambda i: (i, 0)),
            pl.BlockSpec((tm, ga.shape[1]), lambda i: (i, 0)),
            pl.BlockSpec((tm, gr.shape[1]), lambda i: (i, 0)),
            pl.BlockSpec((tm, D), lambda i: (i, gate_a_blk)),
            pl.BlockSpec((tm, D), lambda i: (i, gate_r_blk)),
            resident(wa.shape),
            resident(wr.shape),
            resident(wo.shape),
            pl.BlockSpec((1, D), const),
            resident(rw_hi.shape),
            resident(rw_lo.shape),
            pl.BlockSpec((1, LANES), const),
        ],
        out_specs=[
            pl.BlockSpec((tm, D), lambda i: (i, 0)),
            pl.BlockSpec((tm, D), lambda i: (i, 0)),
            pl.BlockSpec((tm, LANES), lambda i: (i, 0)),
            pl.BlockSpec((8, LANES), const),
        ],
        out_shape=[
            jax.ShapeDtypeStruct((S, D), F32),
            jax.ShapeDtypeStruct((S, D), F32),
            jax.ShapeDtypeStruct((S, LANES), F32),
            jax.ShapeDtypeStruct((8, LANES), F32),
        ],
        scratch_shapes=[pltpu.VMEM((8, LANES), F32)],
        compiler_params=_cparams(("arbitrary",)),
        name="merge",
    )(x2d, ga, gr, proj, proj, wa, wr, wo, g2, rw_hi, rw_lo, rb)


def _experts_body(be_ref, nvb_ref, tok_cur_ref, tok_nxt_ref, h2_hbm, wg_ref, wu_ref, wd_ref, bg_ref, bu_ref, bd_ref,
                  y_ref, xs_ref, xb_ref, sem, *, tm):
    i = pl.program_id(0)
    j = pl.program_id(1)
    nj = pl.num_programs(1)
    nvb = nvb_ref[0]
    slot = i % 2

    def start_rows(tok_ref, dst_slot):
        def issue(r, carry):
            t = tok_ref[0, 0, r]
            pltpu.make_async_copy(h2_hbm.at[pl.ds(t, 1)], xs_ref.at[dst_slot, pl.ds(r, 1)], sem.at[dst_slot]).start()
            return carry
        lax.fori_loop(0, tm, issue, 0)

    def wait_rows(dst_slot):
        pltpu.make_async_copy(h2_hbm.at[pl.ds(0, tm)], xs_ref.at[dst_slot], sem.at[dst_slot]).wait()

    @pl.when((j == 0) & (i < nvb))
    def _():
        @pl.when(i == 0)
        def _():
            start_rows(tok_cur_ref, 0)

        wait_rows(slot)

        @pl.when(i + 1 < nvb)
        def _():
            start_rows(tok_nxt_ref, 1 - slot)

        xb_ref[...] = xs_ref[slot].astype(BF16)

    @pl.when(i < nvb)
    def _():
        xb = xb_ref[...]
        gate = jnp.dot(xb, wg_ref[0].astype(BF16), preferred_element_type=F32) + bg_ref[0]
        up = jnp.dot(xb, wu_ref[0].astype(BF16), preferred_element_type=F32) + bu_ref[0]
        gate = jnp.minimum(gate, SWIGLU_LIMIT)
        up = jnp.clip(up, -SWIGLU_LIMIT, SWIGLU_LIMIT)
        glu = gate * jax.nn.sigmoid(gate * SWIGLU_ALPHA)
        act = ((up + 1.0) * glu).astype(BF16)
        part = jnp.dot(act, wd_ref[0].astype(BF16), preferred_element_type=F32)

        @pl.when(j == 0)
        def _():
            y_ref[...] = part + bd_ref[0]

        @pl.when(j > 0)
        def _():
            y_ref[...] += part

    @pl.when((i >= nvb) & (j == 0))
    def _():
        y_ref[...] = jnp.zeros_like(y_ref)


def _experts(h2, row_tok, block_e, nvb, wgu_bf, bgu, wd_bf, bd, *, tm, tf):
    T, D = h2.shape
    E, _, two_ff = wgu_bf.shape
    d_ff = two_ff // 2
    nb = row_tok.shape[0]
    nj = d_ff // tf
    assert d_ff % tf == 0

    def live(i, j, be, nv):
        ii = jnp.minimum(i, nv[0] - 1)
        jj = jnp.where(i < nv[0], j, nj - 1)
        return ii, jj

    def wg_map(i, j, be, nv):
        ii, jj = live(i, j, be, nv)
        return be[ii], 0, jj

    def wu_map(i, j, be, nv):
        ii, jj = live(i, j, be, nv)
        return be[ii], 0, nj + jj

    def wd_map(i, j, be, nv):
        ii, jj = live(i, j, be, nv)
        return be[ii], jj, 0

    def bg_map(i, j, be, nv):
        ii, jj = live(i, j, be, nv)
        return be[ii], 0, jj

    def bu_map(i, j, be, nv):
        ii, jj = live(i, j, be, nv)
        return be[ii], 0, nj + jj

    def bd_map(i, j, be, nv):
        ii, _ = live(i, j, be, nv)
        return be[ii], 0, 0

    def cur_map(i, j, be, nv):
        return jnp.minimum(i, nv[0] - 1), 0, 0

    def nxt_map(i, j, be, nv):
        return jnp.minimum(i + 1, nv[0] - 1), 0, 0

    def y_map(i, j, be, nv):
        return i, 0

    grid_spec = pltpu.PrefetchScalarGridSpec(
        num_scalar_prefetch=2,
        grid=(nb, nj),
        in_specs=[
            pl.BlockSpec((1, 1, tm), cur_map, memory_space=pltpu.SMEM),
            pl.BlockSpec((1, 1, tm), nxt_map, memory_space=pltpu.SMEM),
            pl.BlockSpec(memory_space=pl.ANY),
            pl.BlockSpec((1, D, tf), wg_map),
            pl.BlockSpec((1, D, tf), wu_map),
            pl.BlockSpec((1, tf, D), wd_map),
            pl.BlockSpec((1, 1, tf), bg_map),
            pl.BlockSpec((1, 1, tf), bu_map),
            pl.BlockSpec((1, 1, D), bd_map),
        ],
        out_specs=pl.BlockSpec((tm, D), y_map),
        scratch_shapes=[
            pltpu.VMEM((2, tm, D), F32),
            pltpu.VMEM((tm, D), BF16),
            pltpu.SemaphoreType.DMA((2,)),
        ],
    )
    return pl.pallas_call(
        functools.partial(_experts_body, tm=tm),
        grid_spec=grid_spec,
        out_shape=jax.ShapeDtypeStruct((nb * tm, D), F32),
        compiler_params=_cparams(("arbitrary", "arbitrary")),
        name="experts",
    )(block_e, nvb, row_tok, row_tok, h2, wgu_bf, wgu_bf, wd_bf, bgu, bgu, bd)


def _combine_body(dest_ref, x2_ref, route_ref, gf_ref, ys_hbm, o_ref, buf_ref, sem, *, tm):
    def issue(t, carry):
        for k in range(TOP_K):
            d = dest_ref[0, 0, t * TOP_K + k]
            pltpu.make_async_copy(ys_hbm.at[pl.ds(d, 1)], buf_ref.at[k, pl.ds(t, 1)], sem.at[0]).start()
        return carry

    lax.fori_loop(0, tm, issue, 0)
    for k in range(TOP_K):
        pltpu.make_async_copy(ys_hbm.at[pl.ds(0, tm)], buf_ref.at[k], sem.at[0]).wait()

    route = route_ref[...]
    x3 = x2_ref[...]
    for k in range(TOP_K):
        x3 = x3 + route[:, TOP_K + k:TOP_K + k + 1] * buf_ref[k]
    ms = jnp.mean(x3 * x3, axis=-1, keepdims=True)
    o_ref[...] = x3 * lax.rsqrt(ms + EPS) * gf_ref[...]


def _combine(dest, x2, route, gf, ys, *, tm):
    T, D = x2.shape
    assert T % tm == 0
    return pl.pallas_call(
        functools.partial(_combine_body, tm=tm),
        grid=(T // tm,),
        in_specs=[
            pl.BlockSpec((1, 1, tm * TOP_K), lambda i: (i, 0, 0), memory_space=pltpu.SMEM),
            pl.BlockSpec((tm, D), lambda i: (i, 0)),
            pl.BlockSpec((tm, LANES), lambda i: (i, 0)),
            pl.BlockSpec((1, D), lambda i: (0, 0)),
            pl.BlockSpec(memory_space=pl.ANY),
        ],
        out_specs=pl.BlockSpec((tm, D), lambda i: (i, 0)),
        out_shape=jax.ShapeDtypeStruct((T, D), F32),
        scratch_shapes=[pltpu.VMEM((TOP_K, tm, D), F32), pltpu.SemaphoreType.DMA((1,))],
        compiler_params=_cparams(("arbitrary",)),
        name="combine",
    )(dest, x2, route, gf, ys)


def _retention_tables(S, dk):
    pos = jnp.arange(S, dtype=F32)
    inv_freq = ROPE_BASE ** (-jnp.arange(0, dk, 2, dtype=F32) / dk)
    ang = pos[:, None] * inv_freq[None, :]
    cos, sin = jnp.cos(ang), jnp.sin(ang)
    cos2 = jnp.concatenate([cos, cos], axis=-1)
    sin2 = jnp.concatenate([-sin, sin], axis=-1)
    log_gamma = jnp.log1p(-jnp.exp2(-5.0 - jnp.arange(RET_HEADS, dtype=F32)))
    idx = jnp.arange(CHUNK, dtype=F32)
    diff = idx[:, None] - idx[None, :]
    decay = jnp.where(diff[None] >= 0, jnp.exp(log_gamma[:, None, None] * jnp.maximum(diff, 0.0)[None]), 0.0)
    xi = jnp.exp(log_gamma[:, None] * (idx[None, :] + 1.0))
    zeta = jnp.exp(log_gamma[:, None] * (CHUNK - 1.0 - idx[None, :]))
    cd = jnp.exp(log_gamma * CHUNK)
    xi_b = jnp.broadcast_to(xi[:, :, None], (RET_HEADS, CHUNK, LANES))
    zeta_b = jnp.broadcast_to(zeta[:, :, None], (RET_HEADS, CHUNK, dk))
    return cos2, sin2, decay, xi_b, zeta_b, cd


TM_INPROJ = 512
TM_MIXING = 256
TM_MERGE = 256
TM_EXPERT = 768
TF_EXPERT = 256
TM_COMBINE = 256


def kernel(x, norm1_g, w_in, gm_ln_g, gm_ln_b, gm_ws, gm_b, w_proj_a, w_proj_r, w_out, norm2_g, router_w, router_b,
           w_gate_up, b_gate_up, w_down, b_down, final_norm_g):
    B, S, D = x.shape
    assert B == 1 and norm1_g.shape[0] == 1, "single sequence, depth 1"
    gm_width = w_proj_a.shape[1]
    v_width = w_proj_r.shape[1]
    qk_width = (w_in.shape[2] - 2 * gm_width - 2 * v_width - 2 * D) // 2
    dk = qk_width // RET_HEADS
    assert dk == LANES and gm_ws.shape[2] == CHUNK
    G = gm_ws.shape[1]
    E = router_w.shape[2]
    d_ff = w_down.shape[2]
    x2d = x.reshape(S, D)

    cos2, sin2, decay, xi_b, zeta_b, cd = _retention_tables(S, dk)
    proj = _inproj(x2d, norm1_g[0][None], w_in[0].astype(BF16), gm_ln_g[0][None], gm_ln_b[0][None], cos2, sin2,
                   gm_width=gm_width, qk_width=qk_width, v_width=v_width, tm=min(TM_INPROJ, S))

    causal = jnp.tril(jnp.ones((CHUNK, CHUNK), dtype=bool))
    wm = jnp.where(causal[None], gm_ws[0], 0.0).astype(BF16)
    bs_b = jnp.broadcast_to(gm_b[0][:, :, None], (G, CHUNK, GM_GROUP_DIM))
    ga, gr = _mixing(proj, wm, bs_b, decay, xi_b, zeta_b, cd,
                     gm_width=gm_width, qk_width=qk_width, v_width=v_width, tm=min(TM_MIXING, S))

    rw = jnp.pad(router_w[0], ((0, 0), (0, LANES - E)))
    rw_hi = rw.astype(BF16)
    rw_lo = (rw - rw_hi.astype(F32)).astype(BF16)
    rb = jnp.pad(router_b[0], (0, LANES - E))[None]
    gates_off = 2 * gm_width + 2 * qk_width + 2 * v_width
    x2, h2, route, cnt = _merge(x2d, ga, gr, proj, w_proj_a[0].astype(BF16), w_proj_r[0].astype(BF16),
                                w_out[0].astype(BF16), norm2_g[0][None], rw_hi, rw_lo, rb,
                                gate_a_blk=gates_off // D, gate_r_blk=gates_off // D + 1, tm=min(TM_MERGE, S))

    tm_e = TM_EXPERT
    eidx = route[:, 0:TOP_K].astype(jnp.int32)
    rank = route[:, 2 * TOP_K:3 * TOP_K].astype(jnp.int32)
    counts = cnt[0, :E].astype(jnp.int32)
    padded = (counts + tm_e - 1) // tm_e * tm_e
    pad_end = jnp.cumsum(padded)
    pad_start = pad_end - padded
    dest = pad_start[eidx] + rank
    nb = (S * TOP_K + tm_e - 1) // tm_e + E
    nvb = (pad_end[-1] // tm_e).astype(jnp.int32)[None]
    block_start = jnp.arange(nb, dtype=jnp.int32) * tm_e
    block_e = jnp.minimum(jnp.sum(pad_end[None, :] <= block_start[:, None], axis=1), E - 1).astype(jnp.int32)
    tok = jnp.broadcast_to(jnp.arange(S, dtype=jnp.int32)[:, None], (S, TOP_K))
    row_tok = jnp.zeros((nb * tm_e,), jnp.int32).at[dest.reshape(-1)].set(tok.reshape(-1))

    ys = _experts(h2, row_tok.reshape(nb, 1, tm_e), block_e, nvb, w_gate_up[0],
                  b_gate_up[0][:, None, :], w_down[0], b_down[0][:, None, :],
                  tm=tm_e, tf=min(TF_EXPERT, d_ff))

    tm_c = min(TM_COMBINE, S)
    out = _combine(dest.reshape(S // tm_c, 1, tm_c * TOP_K), x2, route, final_norm_g[None], ys, tm=tm_c)
    return out.reshape(B, S, D)
```

```python
import functools

import jax
import jax.numpy as jnp
from jax import lax
from jax.experimental import pallas as pl
from jax.experimental.pallas import tpu as pltpu

F32 = jnp.float32
BF16 = jnp.bfloat16

CHUNK = 128
GM_GROUP_DIM = 128
RET_HEADS = 8
N_EXPERTS = 32
TOP_K = 4
SWIGLU_LIMIT = 7.0
SWIGLU_ALPHA = 1.702
ROPE_BASE = 10000.0
EPS = 1e-6
LANES = 128

VMEM_LIMIT_BYTES = 56 * 1024 * 1024


def _cparams(sem):
    return pltpu.CompilerParams(dimension_semantics=sem, vmem_limit_bytes=VMEM_LIMIT_BYTES)


def _gelu_exact(a):
    return 0.5 * a * (1.0 + lax.erf(a * (2.0 ** -0.5)))


def _inproj_body(x_ref, g1_ref, w_ref, lng_ref, lnb_ref, cos_ref, sin_ref, o_ref, h_ref, *, seg_tiles, k_scale):
    j = pl.program_id(1)

    @pl.when(j == 0)
    def _():
        xf = x_ref[...]
        ms = jnp.mean(xf * xf, axis=-1, keepdims=True)
        h_ref[...] = (xf * lax.rsqrt(ms + EPS) * g1_ref[...]).astype(BF16)

    acc = jnp.dot(h_ref[...], w_ref[...], preferred_element_type=F32)
    t_u, t_v, t_q, t_k, t_rv, t_rg = seg_tiles

    def rope(a):
        outs = []
        for hd in range(a.shape[1] // LANES):
            ah = a[:, hd * LANES:(hd + 1) * LANES]
            outs.append(ah * cos_ref[...] + pltpu.roll(ah, LANES // 2, axis=1) * sin_ref[...])
        return jnp.concatenate(outs, axis=1)

    @pl.when(j < t_u)
    def _():
        o_ref[...] = _gelu_exact(acc).astype(o_ref.dtype)

    @pl.when((j >= t_u) & (j < t_v))
    def _():
        vf = _gelu_exact(acc)
        mu = jnp.mean(vf, axis=-1, keepdims=True)
        vc = vf - mu
        var = jnp.mean(vc * vc, axis=-1, keepdims=True)
        o_ref[...] = (vc * lax.rsqrt(var + EPS) * lng_ref[...] + lnb_ref[...]).astype(o_ref.dtype)

    @pl.when((j >= t_v) & (j < t_q))
    def _():
        o_ref[...] = rope(acc).astype(o_ref.dtype)

    @pl.when((j >= t_q) & (j < t_k))
    def _():
        o_ref[...] = (rope(acc) * k_scale).astype(o_ref.dtype)

    @pl.when((j >= t_k) & (j < t_rv))
    def _():
        o_ref[...] = acc.astype(o_ref.dtype)

    @pl.when((j >= t_rv) & (j < t_rg))
    def _():
        o_ref[...] = (acc * jax.nn.sigmoid(acc)).astype(o_ref.dtype)

    @pl.when(j >= t_rg)
    def _():
        o_ref[...] = jax.nn.sigmoid(acc).astype(o_ref.dtype)


def _inproj(x2d, g1, w_in_bf, ln_g, ln_b, cos2, sin2, *, gm_width, qk_width, v_width, tm):
    S, D = x2d.shape
    N = w_in_bf.shape[1]
    tn = gm_width
    assert tn == qk_width and v_width % tn == 0 and S % tm == 0 and N % tn == 0
    bounds, acc = [], 0
    for width in (gm_width, gm_width, qk_width, qk_width, v_width, v_width):
        acc += width // tn
        bounds.append(acc)
    body = functools.partial(_inproj_body, seg_tiles=tuple(bounds), k_scale=float(LANES) ** -0.5)
    return pl.pallas_call(
        body,
        grid=(S // tm, N // tn),
        in_specs=[
            pl.BlockSpec((tm, D), lambda i, j: (i, 0)),
            pl.BlockSpec((1, D), lambda i, j: (0, 0)),
            pl.BlockSpec((D, tn), lambda i, j: (0, j)),
            pl.BlockSpec((1, tn), lambda i, j: (0, 0)),
            pl.BlockSpec((1, tn), lambda i, j: (0, 0)),
            pl.BlockSpec((tm, LANES), lambda i, j: (i, 0)),
            pl.BlockSpec((tm, LANES), lambda i, j: (i, 0)),
        ],
        out_specs=pl.BlockSpec((tm, tn), lambda i, j: (i, j)),
        out_shape=jax.ShapeDtypeStruct((S, N), BF16),
        scratch_shapes=[pltpu.VMEM((tm, D), BF16)],
        compiler_params=_cparams(("arbitrary", "arbitrary")),
        name="inproj",
    )(x2d, g1, w_in_bf, ln_g, ln_b, cos2, sin2)


def _mixing_body(cd_ref, u_ref, vn_ref, q_ref, k_ref, v_ref, sg_ref, wm_ref, bs_ref, decay_ref, xi_ref, zeta_ref,
                 ga_ref, gr_ref, state_ref, *, n_chunks, dk, dv):
    @pl.when(pl.program_id(0) == 0)
    def _():
        state_ref[...] = jnp.zeros_like(state_ref)

    def chunk(c, carry):
        rows = pl.ds(pl.multiple_of(c * CHUNK, CHUNK), CHUNK)
        for g in range(wm_ref.shape[0]):
            cols = slice(g * GM_GROUP_DIM, (g + 1) * GM_GROUP_DIM)
            mixed = jnp.dot(wm_ref[g], vn_ref[rows, cols], preferred_element_type=F32) + bs_ref[g]
            ga_ref[rows, cols] = (u_ref[rows, cols].astype(F32) * mixed).astype(ga_ref.dtype)
        for hd in range(RET_HEADS):
            qc = q_ref[rows, hd * dk:(hd + 1) * dk]
            kc = k_ref[rows, hd * dk:(hd + 1) * dk]
            vc = v_ref[rows, hd * dv:(hd + 1) * dv]
            st = state_ref[hd]
            scores = lax.dot_general(qc, kc, (((1,), (1,)), ((), ())), preferred_element_type=F32) * decay_ref[hd]
            inner = jnp.dot(scores.astype(BF16), vc, preferred_element_type=F32)
            cross = jnp.dot(qc, st.astype(BF16), preferred_element_type=F32)
            xi = xi_ref[hd]
            o = inner + cross * jnp.concatenate([xi] * (dv // LANES), axis=1)
            kz = (kc.astype(F32) * zeta_ref[hd]).astype(BF16)
            kv = lax.dot_general(kz, vc, (((0,), (0,)), ((), ())), preferred_element_type=F32)
            state_ref[hd] = st * cd_ref[hd] + kv
            mu = jnp.mean(o, axis=-1, keepdims=True)
            oc = o - mu
            var = jnp.mean(oc * oc, axis=-1, keepdims=True)
            on = oc * lax.rsqrt(var + EPS)
            gr_ref[rows, hd * dv:(hd + 1) * dv] = (on * sg_ref[rows, hd * dv:(hd + 1) * dv].astype(F32)).astype(
                gr_ref.dtype)
        return carry

    lax.fori_loop(0, n_chunks, chunk, 0)


def _mixing(proj, wm, bs_b, decay, xi_b, zeta_b, cd, *, gm_width, qk_width, v_width, tm):
    S = proj.shape[0]
    assert S % tm == 0 and tm % CHUNK == 0
    dk, dv = qk_width // RET_HEADS, v_width // RET_HEADS
    G = wm.shape[0]
    u_blk, vn_blk, q_blk, k_blk = 0, 1, (2 * gm_width) // qk_width, (2 * gm_width + qk_width) // qk_width
    v_blk = (2 * gm_width + 2 * qk_width) // v_width
    sg_blk = v_blk + 1
    body = functools.partial(_mixing_body, n_chunks=tm // CHUNK, dk=dk, dv=dv)
    const3 = lambda i: (0, 0, 0)
    return pl.pallas_call(
        body,
        grid=(S // tm,),
        in_specs=[
            pl.BlockSpec(memory_space=pltpu.SMEM),
            pl.BlockSpec((tm, gm_width), lambda i: (i, u_blk)),
            pl.BlockSpec((tm, gm_width), lambda i: (i, vn_blk)),
            pl.BlockSpec((tm, qk_width), lambda i: (i, q_blk)),
            pl.BlockSpec((tm, qk_width), lambda i: (i, k_blk)),
            pl.BlockSpec((tm, v_width), lambda i: (i, v_blk)),
            pl.BlockSpec((tm, v_width), lambda i: (i, sg_blk)),
            pl.BlockSpec((G, CHUNK, CHUNK), const3),
            pl.BlockSpec((G, CHUNK, GM_GROUP_DIM), const3),
            pl.BlockSpec((RET_HEADS, CHUNK, CHUNK), const3),
            pl.BlockSpec((RET_HEADS, CHUNK, LANES), const3),
            pl.BlockSpec((RET_HEADS, CHUNK, dk), const3),
        ],
        out_specs=[
            pl.BlockSpec((tm, gm_width), lambda i: (i, 0)),
            pl.BlockSpec((tm, v_width), lambda i: (i, 0)),
        ],
        out_shape=[
            jax.ShapeDtypeStruct((S, gm_width), BF16),
            jax.ShapeDtypeStruct((S, v_width), BF16),
        ],
        scratch_shapes=[pltpu.VMEM((RET_HEADS, dk, dv), F32)],
        compiler_params=_cparams(("arbitrary",)),
        name="mixing",
    )(cd, proj, proj, proj, proj, proj, proj, wm, bs_b, decay, xi_b, zeta_b)


def _merge_body(x_ref, ga_ref, gr_ref, sa_ref, sr_ref, wa_ref, wr_ref, wo_ref, g2_ref, rwh_ref, rwl_ref, rb_ref,
                x2_ref, h2_ref, route_ref, cnt_ref, run_ref):
    i = pl.program_id(0)
    tm = x_ref.shape[0]

    @pl.when(i == 0)
    def _():
        run_ref[...] = jnp.zeros_like(run_ref)

    y_a = jnp.dot(ga_ref[...], wa_ref[...], preferred_element_type=F32)
    y_r = jnp.dot(gr_ref[...], wr_ref[...], preferred_element_type=F32)
    merged = sa_ref[...].astype(F32) * y_a + sr_ref[...].astype(F32) * y_r
    x2 = x_ref[...] + jnp.dot(merged.astype(BF16), wo_ref[...], preferred_element_type=F32)
    x2_ref[...] = x2
    ms = jnp.mean(x2 * x2, axis=-1, keepdims=True)
    h2 = x2 * lax.rsqrt(ms + EPS) * g2_ref[...]
    h2_ref[...] = h2

    h_hi = h2.astype(BF16)
    h_lo = (h2 - h_hi.astype(F32)).astype(BF16)
    logits = (jnp.dot(h_hi, rwh_ref[...], preferred_element_type=F32)
              + jnp.dot(h_hi, rwl_ref[...], preferred_element_type=F32)
              + jnp.dot(h_lo, rwh_ref[...], preferred_element_type=F32)) + rb_ref[...]
    lane = lax.broadcasted_iota(jnp.int32, (tm, LANES), 1)
    neg_inf = jnp.float32(-jnp.inf)
    work = jnp.where(lane < N_EXPERTS, logits, neg_inf)

    vals, idxs, sels = [], [], []
    for _ in range(TOP_K):
        m = jnp.max(work, axis=-1, keepdims=True)
        idx = jnp.min(jnp.where(work == m, lane, LANES), axis=-1, keepdims=True)
        sel = lane == idx
        vals.append(m)
        idxs.append(idx)
        sels.append(sel)
        work = jnp.where(sel, neg_inf, work)
    exps = [jnp.exp(v - vals[0]) for v in vals]
    denom = exps[0] + exps[1] + exps[2] + exps[3]
    gates = [e / denom for e in exps]

    onehot = jnp.zeros((tm, LANES), F32)
    for sel in sels:
        onehot = onehot + sel.astype(F32)
    r_iota = lax.broadcasted_iota(jnp.int32, (tm, tm), 0)
    c_iota = lax.broadcasted_iota(jnp.int32, (tm, tm), 1)
    strict_lower = (c_iota < r_iota).astype(BF16)
    before = jnp.dot(strict_lower, onehot.astype(BF16), preferred_element_type=F32) + run_ref[0:1, :]
    ranks = [jnp.sum(jnp.where(sel, before, 0.0), axis=-1, keepdims=True) for sel in sels]
    run_new = run_ref[0:1, :] + jnp.sum(onehot, axis=0, keepdims=True)
    run_ref[...] = jnp.broadcast_to(run_new, run_ref.shape)
    cnt_ref[...] = jnp.broadcast_to(run_new, cnt_ref.shape)

    route = jnp.zeros((tm, LANES), F32)
    for k in range(TOP_K):
        route = jnp.where(lane == k, idxs[k].astype(F32), route)
        route = jnp.where(lane == TOP_K + k, gates[k], route)
        route = jnp.where(lane == 2 * TOP_K + k, ranks[k], route)
    route_ref[...] = route


def _merge(x2d, ga, gr, proj, wa, wr, wo, g2, rw_hi, rw_lo, rb, *, gate_a_blk, gate_r_blk, tm):
    S, D = x2d.shape
    assert S % tm == 0
    const = lambda i: (0, 0)
    resident = functools.partial(pl.BlockSpec, index_map=const, pipeline_mode=pl.Buffered(1))
    return pl.pallas_call(
        _merge_body,
        grid=(S // tm,),
        in_specs=[
            pl.BlockSpec((tm, D), lambda i: (i, 0)),
            pl.BlockSpec((tm, ga.shape[1]), lambda i: (i, 0)),
            pl.BlockSpec((tm, gr.shape[1]), lambda i: (i, 0)),
            pl.BlockSpec((tm, D), lambda i: (i, gate_a_blk)),
            pl.BlockSpec((tm, D), lambda i: (i, gate_r_blk)),
            resident(wa.shape),
            resident(wr.shape),
            resident(wo.shape),
            pl.BlockSpec((1, D), const),
            resident(rw_hi.shape),
            resident(rw_lo.shape),
            pl.BlockSpec((1, LANES), const),
        ],
        out_specs=[
            pl.BlockSpec((tm, D), lambda i: (i, 0)),
            pl.BlockSpec((tm, D), lambda i: (i, 0)),
            pl.BlockSpec((tm, LANES), lambda i: (i, 0)),
            pl.BlockSpec((8, LANES), const),
        ],
        out_shape=[
            jax.ShapeDtypeStruct((S, D), F32),
            jax.ShapeDtypeStruct((S, D), F32),
            jax.ShapeDtypeStruct((S, LANES), F32),
            jax.ShapeDtypeStruct((8, LANES), F32),
        ],
        scratch_shapes=[pltpu.VMEM((8, LANES), F32)],
        compiler_params=_cparams(("arbitrary",)),
        name="merge",
    )(x2d, ga, gr, proj, proj, wa, wr, wo, g2, rw_hi, rw_lo, rb)


def _dispatch_body(nvb_ref, tok_cur_ref, tok_nxt_ref, h2_hbm, xs_ref, rows_ref, sem, *, tm):
    i = pl.program_id(0)
    nvb = nvb_ref[0]
    slot = i % 2

    def start_rows(tok_ref, dst_slot):
        def issue(r, carry):
            t = tok_ref[0, 0, r]
            pltpu.make_async_copy(h2_hbm.at[pl.ds(t, 1)], rows_ref.at[dst_slot, pl.ds(r, 1)], sem.at[dst_slot]).start()
            return carry
        lax.fori_loop(0, tm, issue, 0, unroll=8)

    def wait_rows(dst_slot):
        pltpu.make_async_copy(h2_hbm.at[pl.ds(0, tm)], rows_ref.at[dst_slot], sem.at[dst_slot]).wait()

    @pl.when(i < nvb)
    def _():
        @pl.when(i == 0)
        def _():
            start_rows(tok_cur_ref, 0)

        wait_rows(slot)

        @pl.when(i + 1 < nvb)
        def _():
            start_rows(tok_nxt_ref, 1 - slot)

        xs_ref[...] = rows_ref[slot].astype(xs_ref.dtype)

    @pl.when(i >= nvb)
    def _():
        xs_ref[...] = jnp.zeros_like(xs_ref)


def _dispatch(h2, row_tok, nvb, *, tm):
    T, D = h2.shape
    nb = row_tok.shape[0]
    grid_spec = pltpu.PrefetchScalarGridSpec(
        num_scalar_prefetch=1,
        grid=(nb,),
        in_specs=[
            pl.BlockSpec((1, 1, tm), lambda i, nv: (jnp.minimum(i, nv[0] - 1), 0, 0), memory_space=pltpu.SMEM),
            pl.BlockSpec((1, 1, tm), lambda i, nv: (jnp.minimum(i + 1, nv[0] - 1), 0, 0), memory_space=pltpu.SMEM),
            pl.BlockSpec(memory_space=pl.ANY),
        ],
        out_specs=pl.BlockSpec((tm, D), lambda i, nv: (i, 0)),
        scratch_shapes=[pltpu.VMEM((2, tm, D), F32), pltpu.SemaphoreType.DMA((2,))],
    )
    return pl.pallas_call(
        functools.partial(_dispatch_body, tm=tm),
        grid_spec=grid_spec,
        out_shape=jax.ShapeDtypeStruct((nb * tm, D), BF16),
        compiler_params=_cparams(("arbitrary",)),
        name="dispatch",
    )(nvb, row_tok, row_tok, h2)


def _tile_schedule(nblk, blk_start, nvb, n_tiles, nb):
    n_steps = nb * n_tiles
    steps_per_e = n_tiles * nblk
    e_end = jnp.cumsum(steps_per_e)
    e_start = e_end - steps_per_e
    total = e_end[-1]
    s = jnp.arange(n_steps, dtype=jnp.int32)
    live = s < total
    sl = jnp.minimum(s, total - 1)
    e = jnp.sum(e_end[None, :] <= sl[:, None], axis=1).astype(jnp.int32)
    r = sl - e_start[e]
    nb_e = jnp.maximum(nblk[e], 1)
    tile = r // nb_e
    b = r - tile * nb_e
    blk = blk_start[e] + b
    first = (live & (b == 0)).astype(jnp.int32)
    dead = jnp.maximum(s - total, 0)
    out_blk = jnp.where(live, blk, nvb + dead // n_tiles)
    out_tile = jnp.where(live, tile, dead % n_tiles)
    as_i32 = lambda a: a.astype(jnp.int32)
    return as_i32(e), as_i32(tile), as_i32(blk), first, as_i32(out_blk), as_i32(out_tile), as_i32(total)[None]


def _gateup_body(se_ref, st_ref, sb_ref, sf_ref, ob_ref, ot_ref, nl_ref, xs_ref, wg_ref, wu_ref, bg_ref, bu_ref,
                 act_ref, wgb_ref, wub_ref):
    s = pl.program_id(0)
    live = s < nl_ref[0]

    @pl.when(sf_ref[s] == 1)
    def _():
        wgb_ref[...] = wg_ref[0].astype(BF16)
        wub_ref[...] = wu_ref[0].astype(BF16)

    @pl.when(live)
    def _():
        xb = xs_ref[...]
        gate = jnp.dot(xb, wgb_ref[...], preferred_element_type=F32) + bg_ref[0]
        up = jnp.dot(xb, wub_ref[...], preferred_element_type=F32) + bu_ref[0]
        gate = jnp.minimum(gate, SWIGLU_LIMIT)
        up = jnp.clip(up, -SWIGLU_LIMIT, SWIGLU_LIMIT)
        glu = gate * jax.nn.sigmoid(gate * SWIGLU_ALPHA)
        act_ref[...] = ((up + 1.0) * glu).astype(act_ref.dtype)

    @pl.when(jnp.logical_not(live))
    def _():
        act_ref[...] = jnp.zeros_like(act_ref)


def _gateup(xs, sched, wgu, bgu, *, tm, tf):
    R, D = xs.shape
    d_ff = wgu.shape[2] // 2
    nj = d_ff // tf
    nb = R // tm
    assert d_ff % tf == 0 and R % tm == 0
    grid_spec = pltpu.PrefetchScalarGridSpec(
        num_scalar_prefetch=7,
        grid=(nb * nj,),
        in_specs=[
            pl.BlockSpec((tm, D), lambda s, se, st, sb, sf, ob, ot, nl: (sb[s], 0)),
            pl.BlockSpec((1, D, tf), lambda s, se, st, sb, sf, ob, ot, nl: (se[s], 0, st[s])),
            pl.BlockSpec((1, D, tf), lambda s, se, st, sb, sf, ob, ot, nl: (se[s], 0, nj + st[s])),
            pl.BlockSpec((1, 1, tf), lambda s, se, st, sb, sf, ob, ot, nl: (se[s], 0, st[s])),
            pl.BlockSpec((1, 1, tf), lambda s, se, st, sb, sf, ob, ot, nl: (se[s], 0, nj + st[s])),
        ],
        out_specs=pl.BlockSpec((tm, tf), lambda s, se, st, sb, sf, ob, ot, nl: (ob[s], ot[s])),
        scratch_shapes=[pltpu.VMEM((D, tf), BF16), pltpu.VMEM((D, tf), BF16)],
    )
    return pl.pallas_call(
        _gateup_body,
        grid_spec=grid_spec,
        out_shape=jax.ShapeDtypeStruct((R, d_ff), BF16),
        compiler_params=_cparams(("arbitrary",)),
        name="gateup",
    )(*sched, xs, wgu, wgu, bgu, bgu)


def _down_body(se_ref, st_ref, sb_ref, sf_ref, ob_ref, ot_ref, nl_ref, act_ref, wd_ref, bd_ref, y_ref, wdb_ref):
    s = pl.program_id(0)
    live = s < nl_ref[0]

    @pl.when(sf_ref[s] == 1)
    def _():
        wdb_ref[...] = wd_ref[0].astype(BF16)

    @pl.when(live)
    def _():
        y_ref[...] = jnp.dot(act_ref[...], wdb_ref[...], preferred_element_type=F32) + bd_ref[0]

    @pl.when(jnp.logical_not(live))
    def _():
        y_ref[...] = jnp.zeros_like(y_ref)


def _down(act, sched, wd, bd, *, tm, tn):
    R, d_ff = act.shape
    D = wd.shape[2]
    nn = D // tn
    nb = R // tm
    assert D % tn == 0
    grid_spec = pltpu.PrefetchScalarGridSpec(
        num_scalar_prefetch=7,
        grid=(nb * nn,),
        in_specs=[
            pl.BlockSpec((tm, d_ff), lambda s, se, st, sb, sf, ob, ot, nl: (sb[s], 0)),
            pl.BlockSpec((1, d_ff, tn), lambda s, se, st, sb, sf, ob, ot, nl: (se[s], 0, st[s])),
            pl.BlockSpec((1, 1, tn), lambda s, se, st, sb, sf, ob, ot, nl: (se[s], 0, st[s])),
        ],
        out_specs=pl.BlockSpec((tm, tn), lambda s, se, st, sb, sf, ob, ot, nl: (ob[s], ot[s])),
        scratch_shapes=[pltpu.VMEM((d_ff, tn), BF16)],
    )
    return pl.pallas_call(
        _down_body,
        grid_spec=grid_spec,
        out_shape=jax.ShapeDtypeStruct((R, D), F32),
        compiler_params=_cparams(("arbitrary",)),
        name="down",
    )(*sched, act, wd, bd)


def _combine_body(dest_ref, x2_ref, route_ref, gf_ref, ys_hbm, o_ref, buf_ref, sem, *, tm):
    def issue(t, carry):
        for k in range(TOP_K):
            d = dest_ref[0, 0, t * TOP_K + k]
            pltpu.make_async_copy(ys_hbm.at[pl.ds(d, 1)], buf_ref.at[k, pl.ds(t, 1)], sem.at[0]).start()
        return carry

    lax.fori_loop(0, tm, issue, 0)
    for k in range(TOP_K):
        pltpu.make_async_copy(ys_hbm.at[pl.ds(0, tm)], buf_ref.at[k], sem.at[0]).wait()

    route = route_ref[...]
    x3 = x2_ref[...]
    for k in range(TOP_K):
        x3 = x3 + route[:, TOP_K + k:TOP_K + k + 1] * buf_ref[k]
    ms = jnp.mean(x3 * x3, axis=-1, keepdims=True)
    o_ref[...] = x3 * lax.rsqrt(ms + EPS) * gf_ref[...]


def _combine(dest, x2, route, gf, ys, *, tm):
    T, D = x2.shape
    assert T % tm == 0
    return pl.pallas_call(
        functools.partial(_combine_body, tm=tm),
        grid=(T // tm,),
        in_specs=[
            pl.BlockSpec((1, 1, tm * TOP_K), lambda i: (i, 0, 0), memory_space=pltpu.SMEM),
            pl.BlockSpec((tm, D), lambda i: (i, 0)),
            pl.BlockSpec((tm, LANES), lambda i: (i, 0)),
            pl.BlockSpec((1, D), lambda i: (0, 0)),
            pl.BlockSpec(memory_space=pl.ANY),
        ],
        out_specs=pl.BlockSpec((tm, D), lambda i: (i, 0)),
        out_shape=jax.ShapeDtypeStruct((T, D), F32),
        scratch_shapes=[pltpu.VMEM((TOP_K, tm, D), F32), pltpu.SemaphoreType.DMA((1,))],
        compiler_params=_cparams(("arbitrary",)),
        name="combine",
    )(dest, x2, route, gf, ys)


def _retention_tables(S, dk):
    pos = jnp.arange(S, dtype=F32)
    inv_freq = ROPE_BASE ** (-jnp.arange(0, dk, 2, dtype=F32) / dk)
    ang = pos[:, None] * inv_freq[None, :]
    cos, sin = jnp.cos(ang), jnp.sin(ang)
    cos2 = jnp.concatenate([cos, cos], axis=-1)
    sin2 = jnp.concatenate([-sin, sin], axis=-1)
    log_gamma = jnp.log1p(-jnp.exp2(-5.0 - jnp.arange(RET_HEADS, dtype=F32)))
    idx = jnp.arange(CHUNK, dtype=F32)
    diff = idx[:, None] - idx[None, :]
    decay = jnp.where(diff[None] >= 0, jnp.exp(log_gamma[:, None, None] * jnp.maximum(diff, 0.0)[None]), 0.0)
    xi = jnp.exp(log_gamma[:, None] * (idx[None, :] + 1.0))
    zeta = jnp.exp(log_gamma[:, None] * (CHUNK - 1.0 - idx[None, :]))
    cd = jnp.exp(log_gamma * CHUNK)
    xi_b = jnp.broadcast_to(xi[:, :, None], (RET_HEADS, CHUNK, LANES))
    zeta_b = jnp.broadcast_to(zeta[:, :, None], (RET_HEADS, CHUNK, dk))
    return cos2, sin2, decay, xi_b, zeta_b, cd


TM_INPROJ = 512
TM_MIXING = 256
TM_MERGE = 256
TM_EXPERT = 512
TF_EXPERT = 512
TN_EXPERT = 1024
TM_COMBINE = 256


def kernel(x, norm1_g, w_in, gm_ln_g, gm_ln_b, gm_ws, gm_b, w_proj_a, w_proj_r, w_out, norm2_g, router_w, router_b,
           w_gate_up, b_gate_up, w_down, b_down, final_norm_g):
    B, S, D = x.shape
    assert B == 1 and norm1_g.shape[0] == 1, "single sequence, depth 1"
    gm_width = w_proj_a.shape[1]
    v_width = w_proj_r.shape[1]
    qk_width = (w_in.shape[2] - 2 * gm_width - 2 * v_width - 2 * D) // 2
    dk = qk_width // RET_HEADS
    assert dk == LANES and gm_ws.shape[2] == CHUNK
    G = gm_ws.shape[1]
    E = router_w.shape[2]
    d_ff = w_down.shape[2]
    x2d = x.reshape(S, D)

    cos2, sin2, decay, xi_b, zeta_b, cd = _retention_tables(S, dk)
    proj = _inproj(x2d, norm1_g[0][None], w_in[0].astype(BF16), gm_ln_g[0][None], gm_ln_b[0][None], cos2, sin2,
                   gm_width=gm_width, qk_width=qk_width, v_width=v_width, tm=min(TM_INPROJ, S))

    causal = jnp.tril(jnp.ones((CHUNK, CHUNK), dtype=bool))
    wm = jnp.where(causal[None], gm_ws[0], 0.0).astype(BF16)
    bs_b = jnp.broadcast_to(gm_b[0][:, :, None], (G, CHUNK, GM_GROUP_DIM))
    ga, gr = _mixing(proj, wm, bs_b, decay, xi_b, zeta_b, cd,
                     gm_width=gm_width, qk_width=qk_width, v_width=v_width, tm=min(TM_MIXING, S))

    rw = jnp.pad(router_w[0], ((0, 0), (0, LANES - E)))
    rw_hi = rw.astype(BF16)
    rw_lo = (rw - rw_hi.astype(F32)).astype(BF16)
    rb = jnp.pad(router_b[0], (0, LANES - E))[None]
    gates_off = 2 * gm_width + 2 * qk_width + 2 * v_width
    x2, h2, route, cnt = _merge(x2d, ga, gr, proj, w_proj_a[0].astype(BF16), w_proj_r[0].astype(BF16),
                                w_out[0].astype(BF16), norm2_g[0][None], rw_hi, rw_lo, rb,
                                gate_a_blk=gates_off // D, gate_r_blk=gates_off // D + 1, tm=min(TM_MERGE, S))

    tm_e = TM_EXPERT
    eidx = route[:, 0:TOP_K].astype(jnp.int32)
    rank = route[:, 2 * TOP_K:3 * TOP_K].astype(jnp.int32)
    counts = cnt[0, :E].astype(jnp.int32)
    nblk = (counts + tm_e - 1) // tm_e
    blk_end = jnp.cumsum(nblk)
    blk_start = blk_end - nblk
    dest = (blk_start * tm_e)[eidx] + rank
    nb = (S * TOP_K + tm_e - 1) // tm_e + E
    nvb = blk_end[-1].astype(jnp.int32)
    tok = jnp.broadcast_to(jnp.arange(S, dtype=jnp.int32)[:, None], (S, TOP_K))
    row_tok = jnp.zeros((nb * tm_e,), jnp.int32).at[dest.reshape(-1)].set(tok.reshape(-1))

    xs = _dispatch(h2, row_tok.reshape(nb, 1, tm_e), nvb[None], tm=tm_e)
    tf, tn = min(TF_EXPERT, d_ff), min(TN_EXPERT, D)
    act = _gateup(xs, _tile_schedule(nblk, blk_start, nvb, d_ff // tf, nb), w_gate_up[0], b_gate_up[0][:, None, :],
                  tm=tm_e, tf=tf)
    ys = _down(act, _tile_schedule(nblk, blk_start, nvb, D // tn, nb), w_down[0], b_down[0][:, None, :],
               tm=tm_e, tn=tn)

    tm_c = min(TM_COMBINE, S)
    out = _combine(dest.reshape(S // tm_c, 1, tm_c * TOP_K), x2, route, final_norm_g[None], ys, tm=tm_c)
    return out.reshape(B, S, D)
```

```python
import functools

import jax
import jax.numpy as jnp
from jax import lax
from jax.experimental import pallas as pl
from jax.experimental.pallas import tpu as pltpu

F32 = jnp.float32
BF16 = jnp.bfloat16

CHUNK = 128
GM_GROUP_DIM = 128
RET_HEADS = 8
N_EXPERTS = 32
TOP_K = 4
SWIGLU_LIMIT = 7.0
SWIGLU_ALPHA = 1.702
ROPE_BASE = 10000.0
EPS = 1e-6
LANES = 128

VMEM_LIMIT_BYTES = 56 * 1024 * 1024


def _cparams(sem):
    return pltpu.CompilerParams(dimension_semantics=sem, vmem_limit_bytes=VMEM_LIMIT_BYTES)


SLAB_PITCH = 17


def _slab_store(ref, value):
    rows, width = value.shape
    n = width // LANES
    for c in range(n):
        ref[pl.ds(c, rows, stride=SLAB_PITCH), :] = value[:, c * LANES:(c + 1) * LANES]
    for c in range(n, SLAB_PITCH):
        ref[pl.ds(c, rows, stride=SLAB_PITCH), :] = jnp.zeros((rows, LANES), value.dtype)


def _slab_load(ref, rows, n):
    return jnp.concatenate([ref[pl.ds(c, rows, stride=SLAB_PITCH), :] for c in range(n)], axis=1)


def _gelu_exact(a):
    return 0.5 * a * (1.0 + lax.erf(a * (2.0 ** -0.5)))


def _inproj_body(x_ref, g1_ref, w_ref, lng_ref, lnb_ref, cos_ref, sin_ref, o_ref, h_ref, *, seg_tiles, k_scale):
    j = pl.program_id(1)

    @pl.when(j == 0)
    def _():
        xf = x_ref[...]
        ms = jnp.mean(xf * xf, axis=-1, keepdims=True)
        h_ref[...] = (xf * lax.rsqrt(ms + EPS) * g1_ref[...]).astype(BF16)

    acc = jnp.dot(h_ref[...], w_ref[...], preferred_element_type=F32)
    t_u, t_v, t_q, t_k, t_rv, t_rg = seg_tiles

    def rope(a):
        outs = []
        for hd in range(a.shape[1] // LANES):
            ah = a[:, hd * LANES:(hd + 1) * LANES]
            outs.append(ah * cos_ref[...] + pltpu.roll(ah, LANES // 2, axis=1) * sin_ref[...])
        return jnp.concatenate(outs, axis=1)

    @pl.when(j < t_u)
    def _():
        o_ref[...] = _gelu_exact(acc).astype(o_ref.dtype)

    @pl.when((j >= t_u) & (j < t_v))
    def _():
        vf = _gelu_exact(acc)
        mu = jnp.mean(vf, axis=-1, keepdims=True)
        vc = vf - mu
        var = jnp.mean(vc * vc, axis=-1, keepdims=True)
        o_ref[...] = (vc * lax.rsqrt(var + EPS) * lng_ref[...] + lnb_ref[...]).astype(o_ref.dtype)

    @pl.when((j >= t_v) & (j < t_q))
    def _():
        o_ref[...] = rope(acc).astype(o_ref.dtype)

    @pl.when((j >= t_q) & (j < t_k))
    def _():
        o_ref[...] = (rope(acc) * k_scale).astype(o_ref.dtype)

    @pl.when((j >= t_k) & (j < t_rv))
    def _():
        o_ref[...] = acc.astype(o_ref.dtype)

    @pl.when((j >= t_rv) & (j < t_rg))
    def _():
        o_ref[...] = (acc * jax.nn.sigmoid(acc)).astype(o_ref.dtype)

    @pl.when(j >= t_rg)
    def _():
        o_ref[...] = jax.nn.sigmoid(acc).astype(o_ref.dtype)


def _inproj(x2d, g1, w_in_bf, ln_g, ln_b, cos2, sin2, *, gm_width, qk_width, v_width, tm):
    S, D = x2d.shape
    N = w_in_bf.shape[1]
    tn = gm_width
    assert tn == qk_width and v_width % tn == 0 and S % tm == 0 and N % tn == 0
    bounds, acc = [], 0
    for width in (gm_width, gm_width, qk_width, qk_width, v_width, v_width):
        acc += width // tn
        bounds.append(acc)
    body = functools.partial(_inproj_body, seg_tiles=tuple(bounds), k_scale=float(LANES) ** -0.5)
    return pl.pallas_call(
        body,
        grid=(S // tm, N // tn),
        in_specs=[
            pl.BlockSpec((tm, D), lambda i, j: (i, 0)),
            pl.BlockSpec((1, D), lambda i, j: (0, 0)),
            pl.BlockSpec((D, tn), lambda i, j: (0, j)),
            pl.BlockSpec((1, tn), lambda i, j: (0, 0)),
            pl.BlockSpec((1, tn), lambda i, j: (0, 0)),
            pl.BlockSpec((tm, LANES), lambda i, j: (i, 0)),
            pl.BlockSpec((tm, LANES), lambda i, j: (i, 0)),
        ],
        out_specs=pl.BlockSpec((tm, tn), lambda i, j: (i, j)),
        out_shape=jax.ShapeDtypeStruct((S, N), BF16),
        scratch_shapes=[pltpu.VMEM((tm, D), BF16)],
        compiler_params=_cparams(("arbitrary", "arbitrary")),
        name="inproj",
    )(x2d, g1, w_in_bf, ln_g, ln_b, cos2, sin2)


def _mixing_body(cd_ref, u_ref, vn_ref, q_ref, k_ref, v_ref, sg_ref, wm_ref, bs_ref, decay_ref, xi_ref, zeta_ref,
                 ga_ref, gr_ref, state_ref, *, n_chunks, dk, dv):
    @pl.when(pl.program_id(0) == 0)
    def _():
        state_ref[...] = jnp.zeros_like(state_ref)

    def chunk(c, carry):
        rows = pl.ds(pl.multiple_of(c * CHUNK, CHUNK), CHUNK)
        for g in range(wm_ref.shape[0]):
            cols = slice(g * GM_GROUP_DIM, (g + 1) * GM_GROUP_DIM)
            mixed = jnp.dot(wm_ref[g], vn_ref[rows, cols], preferred_element_type=F32) + bs_ref[g]
            ga_ref[rows, cols] = (u_ref[rows, cols].astype(F32) * mixed).astype(ga_ref.dtype)
        for hd in range(RET_HEADS):
            qc = q_ref[rows, hd * dk:(hd + 1) * dk]
            kc = k_ref[rows, hd * dk:(hd + 1) * dk]
            vc = v_ref[rows, hd * dv:(hd + 1) * dv]
            st = state_ref[hd]
            scores = lax.dot_general(qc, kc, (((1,), (1,)), ((), ())), preferred_element_type=F32) * decay_ref[hd]
            inner = jnp.dot(scores.astype(BF16), vc, preferred_element_type=F32)
            cross = jnp.dot(qc, st.astype(BF16), preferred_element_type=F32)
            xi = xi_ref[hd]
            o = inner + cross * jnp.concatenate([xi] * (dv // LANES), axis=1)
            kz = (kc.astype(F32) * zeta_ref[hd]).astype(BF16)
            kv = lax.dot_general(kz, vc, (((0,), (0,)), ((), ())), preferred_element_type=F32)
            state_ref[hd] = st * cd_ref[hd] + kv
            mu = jnp.mean(o, axis=-1, keepdims=True)
            oc = o - mu
            var = jnp.mean(oc * oc, axis=-1, keepdims=True)
            on = oc * lax.rsqrt(var + EPS)
            gr_ref[rows, hd * dv:(hd + 1) * dv] = (on * sg_ref[rows, hd * dv:(hd + 1) * dv].astype(F32)).astype(
                gr_ref.dtype)
        return carry

    lax.fori_loop(0, n_chunks, chunk, 0)


def _mixing(proj, wm, bs_b, decay, xi_b, zeta_b, cd, *, gm_width, qk_width, v_width, tm):
    S = proj.shape[0]
    assert S % tm == 0 and tm % CHUNK == 0
    dk, dv = qk_width // RET_HEADS, v_width // RET_HEADS
    G = wm.shape[0]
    u_blk, vn_blk, q_blk, k_blk = 0, 1, (2 * gm_width) // qk_width, (2 * gm_width + qk_width) // qk_width
    v_blk = (2 * gm_width + 2 * qk_width) // v_width
    sg_blk = v_blk + 1
    body = functools.partial(_mixing_body, n_chunks=tm // CHUNK, dk=dk, dv=dv)
    const3 = lambda i: (0, 0, 0)
    return pl.pallas_call(
        body,
        grid=(S // tm,),
        in_specs=[
            pl.BlockSpec(memory_space=pltpu.SMEM),
            pl.BlockSpec((tm, gm_width), lambda i: (i, u_blk)),
            pl.BlockSpec((tm, gm_width), lambda i: (i, vn_blk)),
            pl.BlockSpec((tm, qk_width), lambda i: (i, q_blk)),
            pl.BlockSpec((tm, qk_width), lambda i: (i, k_blk)),
            pl.BlockSpec((tm, v_width), lambda i: (i, v_blk)),
            pl.BlockSpec((tm, v_width), lambda i: (i, sg_blk)),
            pl.BlockSpec((G, CHUNK, CHUNK), const3),
            pl.BlockSpec((G, CHUNK, GM_GROUP_DIM), const3),
            pl.BlockSpec((RET_HEADS, CHUNK, CHUNK), const3),
            pl.BlockSpec((RET_HEADS, CHUNK, LANES), const3),
            pl.BlockSpec((RET_HEADS, CHUNK, dk), const3),
        ],
        out_specs=[
            pl.BlockSpec((tm, gm_width), lambda i: (i, 0)),
            pl.BlockSpec((tm, v_width), lambda i: (i, 0)),
        ],
        out_shape=[
            jax.ShapeDtypeStruct((S, gm_width), BF16),
            jax.ShapeDtypeStruct((S, v_width), BF16),
        ],
        scratch_shapes=[pltpu.VMEM((RET_HEADS, dk, dv), F32)],
        compiler_params=_cparams(("arbitrary",)),
        name="mixing",
    )(cd, proj, proj, proj, proj, proj, proj, wm, bs_b, decay, xi_b, zeta_b)


def _merge_body(x_ref, ga_ref, gr_ref, sa_ref, sr_ref, wa_ref, wr_ref, wo_ref, g2_ref, rwh_ref, rwl_ref, rb_ref,
                x2_ref, h2_ref, route_ref, cnt_ref, run_ref):
    i = pl.program_id(0)
    tm = x_ref.shape[0]

    @pl.when(i == 0)
    def _():
        run_ref[...] = jnp.zeros_like(run_ref)

    y_a = jnp.dot(ga_ref[...], wa_ref[...], preferred_element_type=F32)
    y_r = jnp.dot(gr_ref[...], wr_ref[...], preferred_element_type=F32)
    merged = sa_ref[...].astype(F32) * y_a + sr_ref[...].astype(F32) * y_r
    x2 = x_ref[...] + jnp.dot(merged.astype(BF16), wo_ref[...], preferred_element_type=F32)
    x2_ref[...] = x2
    ms = jnp.mean(x2 * x2, axis=-1, keepdims=True)
    h2 = x2 * lax.rsqrt(ms + EPS) * g2_ref[...]
    _slab_store(h2_ref, h2)

    h_hi = h2.astype(BF16)
    h_lo = (h2 - h_hi.astype(F32)).astype(BF16)
    logits = (jnp.dot(h_hi, rwh_ref[...], preferred_element_type=F32)
              + jnp.dot(h_hi, rwl_ref[...], preferred_element_type=F32)
              + jnp.dot(h_lo, rwh_ref[...], preferred_element_type=F32)) + rb_ref[...]
    lane = lax.broadcasted_iota(jnp.int32, (tm, LANES), 1)
    neg_inf = jnp.float32(-jnp.inf)
    work = jnp.where(lane < N_EXPERTS, logits, neg_inf)

    vals, idxs, sels = [], [], []
    for _ in range(TOP_K):
        m = jnp.max(work, axis=-1, keepdims=True)
        idx = jnp.min(jnp.where(work == m, lane, LANES), axis=-1, keepdims=True)
        sel = lane == idx
        vals.append(m)
        idxs.append(idx)
        sels.append(sel)
        work = jnp.where(sel, neg_inf, work)
    exps = [jnp.exp(v - vals[0]) for v in vals]
    denom = exps[0] + exps[1] + exps[2] + exps[3]
    gates = [e / denom for e in exps]

    onehot = jnp.zeros((tm, LANES), F32)
    for sel in sels:
        onehot = onehot + sel.astype(F32)
    r_iota = lax.broadcasted_iota(jnp.int32, (tm, tm), 0)
    c_iota = lax.broadcasted_iota(jnp.int32, (tm, tm), 1)
    strict_lower = (c_iota < r_iota).astype(BF16)
    before = jnp.dot(strict_lower, onehot.astype(BF16), preferred_element_type=F32) + run_ref[0:1, :]
    ranks = [jnp.sum(jnp.where(sel, before, 0.0), axis=-1, keepdims=True) for sel in sels]
    run_new = run_ref[0:1, :] + jnp.sum(onehot, axis=0, keepdims=True)
    run_ref[...] = jnp.broadcast_to(run_new, run_ref.shape)
    cnt_ref[...] = jnp.broadcast_to(run_new, cnt_ref.shape)

    route = jnp.zeros((tm, LANES), F32)
    for k in range(TOP_K):
        route = jnp.where(lane == k, idxs[k].astype(F32), route)
        route = jnp.where(lane == TOP_K + k, gates[k], route)
        route = jnp.where(lane == 2 * TOP_K + k, ranks[k], route)
    route_ref[...] = route


def _merge(x2d, ga, gr, proj, wa, wr, wo, g2, rw_hi, rw_lo, rb, *, gate_a_blk, gate_r_blk, tm):
    S, D = x2d.shape
    assert S % tm == 0
    const = lambda i: (0, 0)
    resident = functools.partial(pl.BlockSpec, index_map=const, pipeline_mode=pl.Buffered(1))
    return pl.pallas_call(
        _merge_body,
        grid=(S // tm,),
        in_specs=[
            pl.BlockSpec((tm, D), lambda i: (i, 0)),
            pl.BlockSpec((tm, ga.shape[1]), lambda i: (i, 0)),
            pl.BlockSpec((tm, gr.shape[1]), lambda i: (i, 0)),
            pl.BlockSpec((tm, D), lambda i: (i, gate_a_blk)),
            pl.BlockSpec((tm, D), lambda i: (i, gate_r_blk)),
            resident(wa.shape),
            resident(wr.shape),
            resident(wo.shape),
            pl.BlockSpec((1, D), const),
            resident(rw_hi.shape),
            resident(rw_lo.shape),
            pl.BlockSpec((1, LANES), const),
        ],
        out_specs=[
            pl.BlockSpec((tm, D), lambda i: (i, 0)),
            pl.BlockSpec((tm * SLAB_PITCH, LANES), lambda i: (i, 0)),
            pl.BlockSpec((tm, LANES), lambda i: (i, 0)),
            pl.BlockSpec((8, LANES), const),
        ],
        out_shape=[
            jax.ShapeDtypeStruct((S, D), F32),
            jax.ShapeDtypeStruct((S * SLAB_PITCH, LANES), F32),
            jax.ShapeDtypeStruct((S, LANES), F32),
            jax.ShapeDtypeStruct((8, LANES), F32),
        ],
        scratch_shapes=[pltpu.VMEM((8, LANES), F32)],
        compiler_params=_cparams(("arbitrary",)),
        name="merge",
    )(x2d, ga, gr, proj, proj, wa, wr, wo, g2, rw_hi, rw_lo, rb)


def _dispatch_body(nvb_ref, tok_cur_ref, tok_nxt_ref, h2_hbm, xs_ref, rows_ref, sem, *, tm, n_col):
    i = pl.program_id(0)
    nvb = nvb_ref[0]
    slot = i % 2

    def start_rows(tok_ref, dst_slot):
        def issue(r, carry):
            t = tok_ref[0, 0, r]
            pltpu.make_async_copy(h2_hbm.at[pl.ds(t * SLAB_PITCH, n_col)],
                                  rows_ref.at[dst_slot, pl.ds(r * SLAB_PITCH, n_col)], sem.at[dst_slot]).start()
            return carry
        lax.fori_loop(0, tm, issue, 0, unroll=8)

    def wait_rows(dst_slot):
        pltpu.make_async_copy(h2_hbm.at[pl.ds(0, tm * n_col)], rows_ref.at[dst_slot, pl.ds(0, tm * n_col)],
                              sem.at[dst_slot]).wait()

    @pl.when(i < nvb)
    def _():
        @pl.when(i == 0)
        def _():
            start_rows(tok_cur_ref, 0)

        wait_rows(slot)

        @pl.when(i + 1 < nvb)
        def _():
            start_rows(tok_nxt_ref, 1 - slot)

        xs_ref[...] = _slab_load(rows_ref.at[slot], tm, n_col).astype(xs_ref.dtype)

    @pl.when(i >= nvb)
    def _():
        xs_ref[...] = jnp.zeros_like(xs_ref)


def _dispatch(h2_slab, row_tok, nvb, *, tm, D):
    n_col = D // LANES
    nb = row_tok.shape[0]
    grid_spec = pltpu.PrefetchScalarGridSpec(
        num_scalar_prefetch=1,
        grid=(nb,),
        in_specs=[
            pl.BlockSpec((1, 1, tm), lambda i, nv: (jnp.minimum(i, nv[0] - 1), 0, 0), memory_space=pltpu.SMEM),
            pl.BlockSpec((1, 1, tm), lambda i, nv: (jnp.minimum(i + 1, nv[0] - 1), 0, 0), memory_space=pltpu.SMEM),
            pl.BlockSpec(memory_space=pl.ANY),
        ],
        out_specs=pl.BlockSpec((tm, D), lambda i, nv: (i, 0)),
        scratch_shapes=[pltpu.VMEM((2, tm * SLAB_PITCH, LANES), F32), pltpu.SemaphoreType.DMA((2,))],
    )
    return pl.pallas_call(
        functools.partial(_dispatch_body, tm=tm, n_col=n_col),
        grid_spec=grid_spec,
        out_shape=jax.ShapeDtypeStruct((nb * tm, D), BF16),
        compiler_params=_cparams(("arbitrary",)),
        name="dispatch",
    )(nvb, row_tok, row_tok, h2_slab)


def _tile_schedule(nblk, blk_start, nvb, n_tiles, nb):
    n_steps = nb * n_tiles
    steps_per_e = n_tiles * nblk
    e_end = jnp.cumsum(steps_per_e)
    e_start = e_end - steps_per_e
    total = e_end[-1]
    s = jnp.arange(n_steps, dtype=jnp.int32)
    live = s < total
    sl = jnp.minimum(s, total - 1)
    e = jnp.sum(e_end[None, :] <= sl[:, None], axis=1).astype(jnp.int32)
    r = sl - e_start[e]
    nb_e = jnp.maximum(nblk[e], 1)
    tile = r // nb_e
    b = r - tile * nb_e
    blk = blk_start[e] + b
    first = (live & (b == 0)).astype(jnp.int32)
    dead = jnp.maximum(s - total, 0)
    out_blk = jnp.where(live, blk, nvb + dead // n_tiles)
    out_tile = jnp.where(live, tile, dead % n_tiles)
    as_i32 = lambda a: a.astype(jnp.int32)
    return as_i32(e), as_i32(tile), as_i32(blk), first, as_i32(out_blk), as_i32(out_tile), as_i32(total)[None]


def _gateup_body(se_ref, st_ref, sb_ref, sf_ref, ob_ref, ot_ref, nl_ref, xs_ref, wg_ref, wu_ref, bg_ref, bu_ref,
                 act_ref, wgb_ref, wub_ref):
    s = pl.program_id(0)
    live = s < nl_ref[0]

    @pl.when(sf_ref[s] == 1)
    def _():
        wgb_ref[...] = wg_ref[0].astype(BF16)
        wub_ref[...] = wu_ref[0].astype(BF16)

    @pl.when(live)
    def _():
        xb = xs_ref[...]
        gate = jnp.dot(xb, wgb_ref[...], preferred_element_type=F32) + bg_ref[0]
        up = jnp.dot(xb, wub_ref[...], preferred_element_type=F32) + bu_ref[0]
        gate = jnp.minimum(gate, SWIGLU_LIMIT)
        up = jnp.clip(up, -SWIGLU_LIMIT, SWIGLU_LIMIT)
        glu = gate * jax.nn.sigmoid(gate * SWIGLU_ALPHA)
        act_ref[...] = ((up + 1.0) * glu).astype(act_ref.dtype)

    @pl.when(jnp.logical_not(live))
    def _():
        act_ref[...] = jnp.zeros_like(act_ref)


def _gateup(xs, sched, wgu, bgu, *, tm, tf):
    R, D = xs.shape
    d_ff = wgu.shape[2] // 2
    nj = d_ff // tf
    nb = R // tm
    assert d_ff % tf == 0 and R % tm == 0
    grid_spec = pltpu.PrefetchScalarGridSpec(
        num_scalar_prefetch=7,
        grid=(nb * nj,),
        in_specs=[
            pl.BlockSpec((tm, D), lambda s, se, st, sb, sf, ob, ot, nl: (sb[s], 0)),
            pl.BlockSpec((1, D, tf), lambda s, se, st, sb, sf, ob, ot, nl: (se[s], 0, st[s])),
            pl.BlockSpec((1, D, tf), lambda s, se, st, sb, sf, ob, ot, nl: (se[s], 0, nj + st[s])),
            pl.BlockSpec((1, 1, tf), lambda s, se, st, sb, sf, ob, ot, nl: (se[s], 0, st[s])),
            pl.BlockSpec((1, 1, tf), lambda s, se, st, sb, sf, ob, ot, nl: (se[s], 0, nj + st[s])),
        ],
        out_specs=pl.BlockSpec((tm, tf), lambda s, se, st, sb, sf, ob, ot, nl: (ob[s], ot[s])),
        scratch_shapes=[pltpu.VMEM((D, tf), BF16), pltpu.VMEM((D, tf), BF16)],
    )
    return pl.pallas_call(
        _gateup_body,
        grid_spec=grid_spec,
        out_shape=jax.ShapeDtypeStruct((R, d_ff), BF16),
        compiler_params=_cparams(("arbitrary",)),
        name="gateup",
    )(*sched, xs, wgu, wgu, bgu, bgu)


_LOAD, _COMPUTE, _ZERO, _IDLE = 0, 1, 2, 3


def _down_schedule(nblk, blk_start, nvb, n_load, nb):
    n_steps = nb + n_load * nblk.shape[0]
    steps_per_e = jnp.where(nblk > 0, nblk + n_load, 0)
    e_end = jnp.cumsum(steps_per_e)
    e_start = e_end - steps_per_e
    total = e_end[-1]
    s = jnp.arange(n_steps, dtype=jnp.int32)
    live = s < total
    sl = jnp.minimum(s, total - 1)
    e = jnp.sum(e_end[None, :] <= sl[:, None], axis=1).astype(jnp.int32)
    r = sl - e_start[e]
    loading = r < n_load
    chunk = jnp.where(loading, r, n_load - 1)
    blk = blk_start[e] + jnp.where(loading, 0, r - n_load)
    dead_blk = nvb + (s - total)
    kind = jnp.where(live, jnp.where(loading, _LOAD, _COMPUTE), jnp.where(dead_blk < nb, _ZERO, _IDLE))
    out_blk = jnp.where(live, blk, jnp.minimum(dead_blk, nb - 1))
    as_i32 = lambda a: a.astype(jnp.int32)
    return as_i32(kind), as_i32(e), as_i32(chunk), as_i32(blk), as_i32(out_blk)


def _down_body(kind_ref, se_ref, sc_ref, sb_ref, ob_ref, act_ref, wd_ref, bd_ref, y_ref, wdb_ref, *, chunk_rows):
    s = pl.program_id(0)
    kind = kind_ref[s]

    @pl.when(kind == _LOAD)
    def _():
        row0 = pl.multiple_of(sc_ref[s] * chunk_rows, chunk_rows)
        wdb_ref[pl.ds(row0, chunk_rows), :] = wd_ref[0].astype(BF16)

    @pl.when(kind == _COMPUTE)
    def _():
        y = jnp.dot(act_ref[...], wdb_ref[...], preferred_element_type=F32) + bd_ref[0]
        _slab_store(y_ref, y)

    @pl.when(kind == _ZERO)
    def _():
        y_ref[...] = jnp.zeros_like(y_ref)


def _down(act, sched, wd, bd, *, tm, n_load):
    R, d_ff = act.shape
    D = wd.shape[2]
    nb = R // tm
    chunk_rows = d_ff // n_load
    assert d_ff % n_load == 0 and D // LANES < SLAB_PITCH
    grid_spec = pltpu.PrefetchScalarGridSpec(
        num_scalar_prefetch=5,
        grid=(sched[0].shape[0],),
        in_specs=[
            pl.BlockSpec((tm, d_ff), lambda s, kd, se, sc, sb, ob: (sb[s], 0)),
            pl.BlockSpec((1, chunk_rows, D), lambda s, kd, se, sc, sb, ob: (se[s], sc[s], 0)),
            pl.BlockSpec((1, 1, D), lambda s, kd, se, sc, sb, ob: (se[s], 0, 0)),
        ],
        out_specs=pl.BlockSpec((tm * SLAB_PITCH, LANES), lambda s, kd, se, sc, sb, ob: (ob[s], 0)),
        scratch_shapes=[pltpu.VMEM((d_ff, D), BF16)],
    )
    return pl.pallas_call(
        functools.partial(_down_body, chunk_rows=chunk_rows),
        grid_spec=grid_spec,
        out_shape=jax.ShapeDtypeStruct((R * SLAB_PITCH, LANES), F32),
        compiler_params=_cparams(("arbitrary",)),
        name="down",
    )(*sched, act, wd, bd)


def _combine_body(dest_cur_ref, dest_nxt_ref, x2_ref, route_ref, gf_ref, ys_hbm, o_ref, buf_ref, sem, *, tm, n_col):
    i = pl.program_id(0)
    slot = i % 2

    def start_rows(dest_ref, dst_slot):
        def issue(t, carry):
            for k in range(TOP_K):
                d = dest_ref[0, 0, t * TOP_K + k]
                pltpu.make_async_copy(ys_hbm.at[pl.ds(d * SLAB_PITCH, n_col)],
                                      buf_ref.at[dst_slot, k, pl.ds(t * SLAB_PITCH, n_col)], sem.at[dst_slot]).start()
            return carry
        lax.fori_loop(0, tm, issue, 0, unroll=2)

    def wait_rows(dst_slot):
        for k in range(TOP_K):
            pltpu.make_async_copy(ys_hbm.at[pl.ds(0, tm * n_col)], buf_ref.at[dst_slot, k, pl.ds(0, tm * n_col)],
                                  sem.at[dst_slot]).wait()

    @pl.when(i == 0)
    def _():
        start_rows(dest_cur_ref, 0)

    @pl.when(i + 1 < pl.num_programs(0))
    def _():
        start_rows(dest_nxt_ref, 1 - slot)

    wait_rows(slot)
    route = route_ref[...]
    x3 = x2_ref[...]
    for k in range(TOP_K):
        x3 = x3 + route[:, TOP_K + k:TOP_K + k + 1] * _slab_load(buf_ref.at[slot, k], tm, n_col)
    ms = jnp.mean(x3 * x3, axis=-1, keepdims=True)
    o_ref[...] = x3 * lax.rsqrt(ms + EPS) * gf_ref[...]


def _combine(dest, x2, route, gf, ys_slab, *, tm):
    T, D = x2.shape
    n = T // tm
    assert T % tm == 0
    return pl.pallas_call(
        functools.partial(_combine_body, tm=tm, n_col=D // LANES),
        grid=(n,),
        in_specs=[
            pl.BlockSpec((1, 1, tm * TOP_K), lambda i: (i, 0, 0), memory_space=pltpu.SMEM),
            pl.BlockSpec((1, 1, tm * TOP_K), lambda i: (jnp.minimum(i + 1, n - 1), 0, 0), memory_space=pltpu.SMEM),
            pl.BlockSpec((tm, D), lambda i: (i, 0)),
            pl.BlockSpec((tm, LANES), lambda i: (i, 0)),
            pl.BlockSpec((1, D), lambda i: (0, 0)),
            pl.BlockSpec(memory_space=pl.ANY),
        ],
        out_specs=pl.BlockSpec((tm, D), lambda i: (i, 0)),
        out_shape=jax.ShapeDtypeStruct((T, D), F32),
        scratch_shapes=[pltpu.VMEM((2, TOP_K, tm * SLAB_PITCH, LANES), F32), pltpu.SemaphoreType.DMA((2,))],
        compiler_params=_cparams(("arbitrary",)),
        name="combine",
    )(dest, dest, x2, route, gf, ys_slab)


def _retention_tables(S, dk):
    pos = jnp.arange(S, dtype=F32)
    inv_freq = ROPE_BASE ** (-jnp.arange(0, dk, 2, dtype=F32) / dk)
    ang = pos[:, None] * inv_freq[None, :]
    cos, sin = jnp.cos(ang), jnp.sin(ang)
    cos2 = jnp.concatenate([cos, cos], axis=-1)
    sin2 = jnp.concatenate([-sin, sin], axis=-1)
    log_gamma = jnp.log1p(-jnp.exp2(-5.0 - jnp.arange(RET_HEADS, dtype=F32)))
    idx = jnp.arange(CHUNK, dtype=F32)
    diff = idx[:, None] - idx[None, :]
    decay = jnp.where(diff[None] >= 0, jnp.exp(log_gamma[:, None, None] * jnp.maximum(diff, 0.0)[None]), 0.0)
    xi = jnp.exp(log_gamma[:, None] * (idx[None, :] + 1.0))
    zeta = jnp.exp(log_gamma[:, None] * (CHUNK - 1.0 - idx[None, :]))
    cd = jnp.exp(log_gamma * CHUNK)
    xi_b = jnp.broadcast_to(xi[:, :, None], (RET_HEADS, CHUNK, LANES))
    zeta_b = jnp.broadcast_to(zeta[:, :, None], (RET_HEADS, CHUNK, dk))
    return cos2, sin2, decay, xi_b, zeta_b, cd


TM_INPROJ = 512
TM_MIXING = 256
TM_MERGE = 256
TM_EXPERT = 512
TF_EXPERT = 512
DOWN_LOAD_STEPS = 2
TM_COMBINE = 256


def kernel(x, norm1_g, w_in, gm_ln_g, gm_ln_b, gm_ws, gm_b, w_proj_a, w_proj_r, w_out, norm2_g, router_w, router_b,
           w_gate_up, b_gate_up, w_down, b_down, final_norm_g):
    B, S, D = x.shape
    assert B == 1 and norm1_g.shape[0] == 1, "single sequence, depth 1"
    gm_width = w_proj_a.shape[1]
    v_width = w_proj_r.shape[1]
    qk_width = (w_in.shape[2] - 2 * gm_width - 2 * v_width - 2 * D) // 2
    dk = qk_width // RET_HEADS
    assert dk == LANES and gm_ws.shape[2] == CHUNK
    G = gm_ws.shape[1]
    E = router_w.shape[2]
    d_ff = w_down.shape[2]
    x2d = x.reshape(S, D)

    cos2, sin2, decay, xi_b, zeta_b, cd = _retention_tables(S, dk)
    proj = _inproj(x2d, norm1_g[0][None], w_in[0].astype(BF16), gm_ln_g[0][None], gm_ln_b[0][None], cos2, sin2,
                   gm_width=gm_width, qk_width=qk_width, v_width=v_width, tm=min(TM_INPROJ, S))

    causal = jnp.tril(jnp.ones((CHUNK, CHUNK), dtype=bool))
    wm = jnp.where(causal[None], gm_ws[0], 0.0).astype(BF16)
    bs_b = jnp.broadcast_to(gm_b[0][:, :, None], (G, CHUNK, GM_GROUP_DIM))
    ga, gr = _mixing(proj, wm, bs_b, decay, xi_b, zeta_b, cd,
                     gm_width=gm_width, qk_width=qk_width, v_width=v_width, tm=min(TM_MIXING, S))

    rw = jnp.pad(router_w[0], ((0, 0), (0, LANES - E)))
    rw_hi = rw.astype(BF16)
    rw_lo = (rw - rw_hi.astype(F32)).astype(BF16)
    rb = jnp.pad(router_b[0], (0, LANES - E))[None]
    gates_off = 2 * gm_width + 2 * qk_width + 2 * v_width
    x2, h2, route, cnt = _merge(x2d, ga, gr, proj, w_proj_a[0].astype(BF16), w_proj_r[0].astype(BF16),
                                w_out[0].astype(BF16), norm2_g[0][None], rw_hi, rw_lo, rb,
                                gate_a_blk=gates_off // D, gate_r_blk=gates_off // D + 1, tm=min(TM_MERGE, S))

    tm_e = TM_EXPERT
    eidx = route[:, 0:TOP_K].astype(jnp.int32)
    rank = route[:, 2 * TOP_K:3 * TOP_K].astype(jnp.int32)
    counts = cnt[0, :E].astype(jnp.int32)
    nblk = (counts + tm_e - 1) // tm_e
    blk_end = jnp.cumsum(nblk)
    blk_start = blk_end - nblk
    dest = (blk_start * tm_e)[eidx] + rank
    nb = (S * TOP_K + tm_e - 1) // tm_e + E
    nvb = blk_end[-1].astype(jnp.int32)
    tok = jnp.broadcast_to(jnp.arange(S, dtype=jnp.int32)[:, None], (S, TOP_K))
    row_tok = jnp.zeros((nb * tm_e,), jnp.int32).at[dest.reshape(-1)].set(
        tok.reshape(-1), unique_indices=True, mode="promise_in_bounds")

    xs = _dispatch(h2, row_tok.reshape(nb, 1, tm_e), nvb[None], tm=tm_e, D=D)
    tf = min(TF_EXPERT, d_ff)
    act = _gateup(xs, _tile_schedule(nblk, blk_start, nvb, d_ff // tf, nb), w_gate_up[0], b_gate_up[0][:, None, :],
                  tm=tm_e, tf=tf)
    ys = _down(act, _down_schedule(nblk, blk_start, nvb, DOWN_LOAD_STEPS, nb), w_down[0], b_down[0][:, None, :],
               tm=tm_e, n_load=DOWN_LOAD_STEPS)

    tm_c = min(TM_COMBINE, S)
    out = _combine(dest.reshape(S // tm_c, 1, tm_c * TOP_K), x2, route, final_norm_g[None], ys, tm=tm_c)
    return out.reshape(B, S, D)
```

```python
import functools

import jax
import jax.numpy as jnp
from jax import lax
from jax.experimental import pallas as pl
from jax.experimental.pallas import tpu as pltpu

F32 = jnp.float32
BF16 = jnp.bfloat16

CHUNK = 128
GM_GROUP_DIM = 128
RET_HEADS = 8
N_EXPERTS = 32
TOP_K = 4
SWIGLU_LIMIT = 7.0
SWIGLU_ALPHA = 1.702
ROPE_BASE = 10000.0
EPS = 1e-6
LANES = 128

VMEM_LIMIT_BYTES = 56 * 1024 * 1024


def _cparams(sem):
    return pltpu.CompilerParams(dimension_semantics=sem, vmem_limit_bytes=VMEM_LIMIT_BYTES)


SLAB_PITCH = 17
DMA_QUEUES = 2


def _slab_store(ref, value):
    rows, width = value.shape
    n = width // LANES
    for c in range(n):
        ref[pl.ds(c, rows, stride=SLAB_PITCH), :] = value[:, c * LANES:(c + 1) * LANES]
    for c in range(n, SLAB_PITCH):
        ref[pl.ds(c, rows, stride=SLAB_PITCH), :] = jnp.zeros((rows, LANES), value.dtype)


def _slab_load(ref, rows, n):
    return jnp.concatenate([ref[pl.ds(c, rows, stride=SLAB_PITCH), :] for c in range(n)], axis=1)


def _gelu_exact(a):
    return 0.5 * a * (1.0 + lax.erf(a * (2.0 ** -0.5)))


def _prenorm_body(x_ref, g_ref, h_ref):
    xf = x_ref[...]
    ms = jnp.mean(xf * xf, axis=-1, keepdims=True)
    h_ref[...] = (xf * lax.rsqrt(ms + EPS) * g_ref[...]).astype(h_ref.dtype)


def _prenorm(x2d, g, *, tm):
    S, D = x2d.shape
    return pl.pallas_call(
        _prenorm_body,
        grid=(S // tm,),
        in_specs=[pl.BlockSpec((tm, D), lambda i: (i, 0)), pl.BlockSpec((1, D), lambda i: (0, 0))],
        out_specs=pl.BlockSpec((tm, D), lambda i: (i, 0)),
        out_shape=jax.ShapeDtypeStruct((S, D), BF16),
        compiler_params=_cparams(("arbitrary",)),
        name="prenorm",
    )(x2d, g)


def _rope_heads(a, cos2, sin2):
    outs = []
    for hd in range(a.shape[1] // LANES):
        ah = a[:, hd * LANES:(hd + 1) * LANES]
        outs.append(ah * cos2 + pltpu.roll(ah, LANES // 2, axis=1) * sin2)
    return jnp.concatenate(outs, axis=1)


def _segment_body(h_ref, w_ref, *refs, kind, k_scale):
    o_ref = refs[-1]
    acc = jnp.dot(h_ref[...], w_ref[...], preferred_element_type=F32)
    if kind == "gelu":
        out = _gelu_exact(acc)
    elif kind == "gelu_layernorm":
        lng_ref, lnb_ref = refs[0], refs[1]
        vf = _gelu_exact(acc)
        mu = jnp.mean(vf, axis=-1, keepdims=True)
        vc = vf - mu
        var = jnp.mean(vc * vc, axis=-1, keepdims=True)
        out = vc * lax.rsqrt(var + EPS) * lng_ref[...] + lnb_ref[...]
    elif kind == "rope":
        out = _rope_heads(acc, refs[0][...], refs[1][...])
    elif kind == "rope_scaled":
        out = _rope_heads(acc, refs[0][...], refs[1][...]) * k_scale
    elif kind == "identity":
        out = acc
    elif kind == "silu":
        out = acc * jax.nn.sigmoid(acc)
    elif kind == "sigmoid":
        out = jax.nn.sigmoid(acc)
    else:
        raise ValueError(kind)
    o_ref[...] = out.astype(o_ref.dtype)


def _segment(h, w_in_bf, col0, width, kind, extras=(), *, tm, tn):
    S, D = h.shape
    assert width % tn == 0 and col0 % tn == 0 and S % tm == 0
    blk0 = col0 // tn
    if kind == "gelu_layernorm":
        assert width == tn
        extra_specs = [pl.BlockSpec((1, tn), lambda j, i: (0, 0))] * 2
    elif kind in ("rope", "rope_scaled"):
        extra_specs = [pl.BlockSpec((tm, LANES), lambda j, i: (i, 0))] * 2
    else:
        extra_specs = []
    return pl.pallas_call(
        functools.partial(_segment_body, kind=kind, k_scale=float(LANES) ** -0.5),
        grid=(width // tn, S // tm),
        in_specs=[
            pl.BlockSpec((tm, D), lambda j, i: (i, 0)),
            pl.BlockSpec((D, tn), lambda j, i: (0, blk0 + j)),
        ] + extra_specs,
        out_specs=pl.BlockSpec((tm, tn), lambda j, i: (i, j)),
        out_shape=jax.ShapeDtypeStruct((S, width), BF16),
        compiler_params=_cparams(("arbitrary", "arbitrary")),
        name="inproj_" + kind,
    )(h, w_in_bf, *extras)


def _mixing_body(cd_ref, u_ref, vn_ref, q_ref, k_ref, v_ref, sg_ref, wm_ref, bs_ref, decay_ref, xi_ref, zeta_ref,
                 ga_ref, gr_ref, state_ref, *, n_chunks, dk, dv):
    @pl.when(pl.program_id(0) == 0)
    def _():
        state_ref[...] = jnp.zeros_like(state_ref)

    def chunk(c, carry):
        rows = pl.ds(pl.multiple_of(c * CHUNK, CHUNK), CHUNK)
        for g in range(wm_ref.shape[0]):
            cols = slice(g * GM_GROUP_DIM, (g + 1) * GM_GROUP_DIM)
            mixed = jnp.dot(wm_ref[g], vn_ref[rows, cols], preferred_element_type=F32) + bs_ref[g]
            ga_ref[rows, cols] = (u_ref[rows, cols].astype(F32) * mixed).astype(ga_ref.dtype)
        for hd in range(RET_HEADS):
            qc = q_ref[rows, hd * dk:(hd + 1) * dk]
            kc = k_ref[rows, hd * dk:(hd + 1) * dk]
            vc = v_ref[rows, hd * dv:(hd + 1) * dv]
            st = state_ref[hd]
            scores = lax.dot_general(qc, kc, (((1,), (1,)), ((), ())), preferred_element_type=F32) * decay_ref[hd]
            inner = jnp.dot(scores.astype(BF16), vc, preferred_element_type=F32)
            cross = jnp.dot(qc, st.astype(BF16), preferred_element_type=F32)
            xi = xi_ref[hd]
            o = inner + cross * jnp.concatenate([xi] * (dv // LANES), axis=1)
            kz = (kc.astype(F32) * zeta_ref[hd]).astype(BF16)
            kv = lax.dot_general(kz, vc, (((0,), (0,)), ((), ())), preferred_element_type=F32)
            state_ref[hd] = st * cd_ref[hd] + kv
            mu = jnp.mean(o, axis=-1, keepdims=True)
            oc = o - mu
            var = jnp.mean(oc * oc, axis=-1, keepdims=True)
            on = oc * lax.rsqrt(var + EPS)
            gr_ref[rows, hd * dv:(hd + 1) * dv] = (on * sg_ref[rows, hd * dv:(hd + 1) * dv].astype(F32)).astype(
                gr_ref.dtype)
        return carry

    lax.fori_loop(0, n_chunks, chunk, 0)


def _mixing(u, vn, q, k, v, sg, wm, bs_b, decay, xi_b, zeta_b, cd, *, tm):
    S, gm_width = u.shape
    qk_width, v_width = q.shape[1], v.shape[1]
    assert S % tm == 0 and tm % CHUNK == 0
    dk, dv = qk_width // RET_HEADS, v_width // RET_HEADS
    G = wm.shape[0]
    body = functools.partial(_mixing_body, n_chunks=tm // CHUNK, dk=dk, dv=dv)
    const3 = lambda i: (0, 0, 0)
    rows = lambda i: (i, 0)
    return pl.pallas_call(
        body,
        grid=(S // tm,),
        in_specs=[
            pl.BlockSpec(memory_space=pltpu.SMEM),
            pl.BlockSpec((tm, gm_width), rows),
            pl.BlockSpec((tm, gm_width), rows),
            pl.BlockSpec((tm, qk_width), rows),
            pl.BlockSpec((tm, qk_width), rows),
            pl.BlockSpec((tm, v_width), rows),
            pl.BlockSpec((tm, v_width), rows),
            pl.BlockSpec((G, CHUNK, CHUNK), const3),
            pl.BlockSpec((G, CHUNK, GM_GROUP_DIM), const3),
            pl.BlockSpec((RET_HEADS, CHUNK, CHUNK), const3),
            pl.BlockSpec((RET_HEADS, CHUNK, LANES), const3),
            pl.BlockSpec((RET_HEADS, CHUNK, dk), const3),
        ],
        out_specs=[
            pl.BlockSpec((tm, gm_width), lambda i: (i, 0)),
            pl.BlockSpec((tm, v_width), lambda i: (i, 0)),
        ],
        out_shape=[
            jax.ShapeDtypeStruct((S, gm_width), BF16),
            jax.ShapeDtypeStruct((S, v_width), BF16),
        ],
        scratch_shapes=[pltpu.VMEM((RET_HEADS, dk, dv), F32)],
        compiler_params=_cparams(("arbitrary",)),
        name="mixing",
    )(cd, u, vn, q, k, v, sg, wm, bs_b, decay, xi_b, zeta_b)


def _merge_body(x_ref, ga_ref, gr_ref, sa_ref, sr_ref, wa_ref, wr_ref, wo_ref, g2_ref, rwh_ref, rwl_ref, rb_ref,
                x2_ref, h2_ref, route_ref, cnt_ref, run_ref):
    i = pl.program_id(0)
    tm = x_ref.shape[0]

    @pl.when(i == 0)
    def _():
        run_ref[...] = jnp.zeros_like(run_ref)

    y_a = jnp.dot(ga_ref[...], wa_ref[...], preferred_element_type=F32)
    y_r = jnp.dot(gr_ref[...], wr_ref[...], preferred_element_type=F32)
    merged = sa_ref[...].astype(F32) * y_a + sr_ref[...].astype(F32) * y_r
    x2 = x_ref[...] + jnp.dot(merged.astype(BF16), wo_ref[...], preferred_element_type=F32)
    x2_ref[...] = x2
    ms = jnp.mean(x2 * x2, axis=-1, keepdims=True)
    h2 = x2 * lax.rsqrt(ms + EPS) * g2_ref[...]
    _slab_store(h2_ref, h2)

    h_hi = h2.astype(BF16)
    h_lo = (h2 - h_hi.astype(F32)).astype(BF16)
    logits = (jnp.dot(h_hi, rwh_ref[...], preferred_element_type=F32)
              + jnp.dot(h_hi, rwl_ref[...], preferred_element_type=F32)
              + jnp.dot(h_lo, rwh_ref[...], preferred_element_type=F32)) + rb_ref[...]
    lane = lax.broadcasted_iota(jnp.int32, (tm, LANES), 1)
    neg_inf = jnp.float32(-jnp.inf)
    work = jnp.where(lane < N_EXPERTS, logits, neg_inf)

    vals, idxs, sels = [], [], []
    for _ in range(TOP_K):
        m = jnp.max(work, axis=-1, keepdims=True)
        idx = jnp.min(jnp.where(work == m, lane, LANES), axis=-1, keepdims=True)
        sel = lane == idx
        vals.append(m)
        idxs.append(idx)
        sels.append(sel)
        work = jnp.where(sel, neg_inf, work)
    exps = [jnp.exp(v - vals[0]) for v in vals]
    denom = exps[0] + exps[1] + exps[2] + exps[3]
    gates = [e / denom for e in exps]

    onehot = jnp.zeros((tm, LANES), F32)
    for sel in sels:
        onehot = onehot + sel.astype(F32)
    r_iota = lax.broadcasted_iota(jnp.int32, (tm, tm), 0)
    c_iota = lax.broadcasted_iota(jnp.int32, (tm, tm), 1)
    strict_lower = (c_iota < r_iota).astype(BF16)
    before = jnp.dot(strict_lower, onehot.astype(BF16), preferred_element_type=F32) + run_ref[0:1, :]
    ranks = [jnp.sum(jnp.where(sel, before, 0.0), axis=-1, keepdims=True) for sel in sels]
    run_new = run_ref[0:1, :] + jnp.sum(onehot, axis=0, keepdims=True)
    run_ref[...] = jnp.broadcast_to(run_new, run_ref.shape)
    cnt_ref[...] = jnp.broadcast_to(run_new, cnt_ref.shape)

    route = jnp.zeros((tm, LANES), F32)
    for k in range(TOP_K):
        route = jnp.where(lane == k, idxs[k].astype(F32), route)
        route = jnp.where(lane == TOP_K + k, gates[k], route)
        route = jnp.where(lane == 2 * TOP_K + k, ranks[k], route)
    route_ref[...] = route


def _merge(x2d, ga, gr, gates, wa, wr, wo, g2, rw_hi, rw_lo, rb, *, tm):
    S, D = x2d.shape
    assert S % tm == 0
    const = lambda i: (0, 0)
    resident = functools.partial(pl.BlockSpec, index_map=const, pipeline_mode=pl.Buffered(1))
    return pl.pallas_call(
        _merge_body,
        grid=(S // tm,),
        in_specs=[
            pl.BlockSpec((tm, D), lambda i: (i, 0)),
            pl.BlockSpec((tm, ga.shape[1]), lambda i: (i, 0)),
            pl.BlockSpec((tm, gr.shape[1]), lambda i: (i, 0)),
            pl.BlockSpec((tm, D), lambda i: (i, 0)),
            pl.BlockSpec((tm, D), lambda i: (i, 1)),
            resident(wa.shape),
            resident(wr.shape),
            resident(wo.shape),
            pl.BlockSpec((1, D), const),
            resident(rw_hi.shape),
            resident(rw_lo.shape),
            pl.BlockSpec((1, LANES), const),
        ],
        out_specs=[
            pl.BlockSpec((tm, D), lambda i: (i, 0)),
            pl.BlockSpec((tm * SLAB_PITCH, LANES), lambda i: (i, 0)),
            pl.BlockSpec((tm, LANES), lambda i: (i, 0)),
            pl.BlockSpec((8, LANES), const),
        ],
        out_shape=[
            jax.ShapeDtypeStruct((S, D), F32),
            jax.ShapeDtypeStruct((S * SLAB_PITCH, LANES), F32),
            jax.ShapeDtypeStruct((S, LANES), F32),
            jax.ShapeDtypeStruct((8, LANES), F32),
        ],
        scratch_shapes=[pltpu.VMEM((8, LANES), F32)],
        compiler_params=_cparams(("arbitrary",)),
        name="merge",
    )(x2d, ga, gr, gates, gates, wa, wr, wo, g2, rw_hi, rw_lo, rb)


def _dispatch_body(nvb_ref, tok_cur_ref, tok_nxt_ref, h2_hbm, xs_ref, rows_ref, sem, *, tm, n_col):
    i = pl.program_id(0)
    nvb = nvb_ref[0]
    slot = i % 2

    def start_rows(tok_ref, dst_slot):
        def issue(p, carry):
            for q in range(DMA_QUEUES):
                r = p * DMA_QUEUES + q
                t = tok_ref[0, 0, r]
                pltpu.make_async_copy(h2_hbm.at[pl.ds(t * SLAB_PITCH, n_col)],
                                      rows_ref.at[dst_slot, pl.ds(r * SLAB_PITCH, n_col)],
                                      sem.at[dst_slot]).start(priority=q)
            return carry
        lax.fori_loop(0, tm // DMA_QUEUES, issue, 0, unroll=4)

    def wait_rows(dst_slot):
        pltpu.make_async_copy(h2_hbm.at[pl.ds(0, tm * n_col)], rows_ref.at[dst_slot, pl.ds(0, tm * n_col)],
                              sem.at[dst_slot]).wait()

    @pl.when(i < nvb)
    def _():
        @pl.when(i == 0)
        def _():
            start_rows(tok_cur_ref, 0)

        @pl.when(i + 1 < nvb)
        def _():
            start_rows(tok_nxt_ref, 1 - slot)

        wait_rows(slot)
        xs_ref[...] = _slab_load(rows_ref.at[slot], tm, n_col).astype(xs_ref.dtype)

    @pl.when(i >= nvb)
    def _():
        xs_ref[...] = jnp.zeros_like(xs_ref)


def _dispatch(h2_slab, row_tok, nvb, *, tm, D):
    n_col = D // LANES
    nb = row_tok.shape[0]
    grid_spec = pltpu.PrefetchScalarGridSpec(
        num_scalar_prefetch=1,
        grid=(nb,),
        in_specs=[
            pl.BlockSpec((1, 1, tm), lambda i, nv: (jnp.minimum(i, nv[0] - 1), 0, 0), memory_space=pltpu.SMEM),
            pl.BlockSpec((1, 1, tm), lambda i, nv: (jnp.minimum(i + 1, nv[0] - 1), 0, 0), memory_space=pltpu.SMEM),
            pl.BlockSpec(memory_space=pl.ANY),
        ],
        out_specs=pl.BlockSpec((tm, D), lambda i, nv: (i, 0)),
        scratch_shapes=[pltpu.VMEM((2, tm * SLAB_PITCH, LANES), F32), pltpu.SemaphoreType.DMA((2,))],
    )
    return pl.pallas_call(
        functools.partial(_dispatch_body, tm=tm, n_col=n_col),
        grid_spec=grid_spec,
        out_shape=jax.ShapeDtypeStruct((nb * tm, D), BF16),
        compiler_params=_cparams(("arbitrary",)),
        name="dispatch",
    )(nvb, row_tok, row_tok, h2_slab)


def _tile_schedule(nblk, blk_start, nvb, n_tiles, nb):
    n_steps = nb * n_tiles
    steps_per_e = n_tiles * nblk
    e_end = jnp.cumsum(steps_per_e)
    e_start = e_end - steps_per_e
    total = e_end[-1]
    s = jnp.arange(n_steps, dtype=jnp.int32)
    live = s < total
    sl = jnp.minimum(s, total - 1)
    e = jnp.sum(e_end[None, :] <= sl[:, None], axis=1).astype(jnp.int32)
    r = sl - e_start[e]
    nb_e = jnp.maximum(nblk[e], 1)
    tile = r // nb_e
    b = r - tile * nb_e
    blk = blk_start[e] + b
    first = (live & (b == 0)).astype(jnp.int32)
    dead = jnp.maximum(s - total, 0)
    out_blk = jnp.where(live, blk, nvb + dead // n_tiles)
    out_tile = jnp.where(live, tile, dead % n_tiles)
    as_i32 = lambda a: a.astype(jnp.int32)
    return as_i32(e), as_i32(tile), as_i32(blk), first, as_i32(out_blk), as_i32(out_tile), as_i32(total)[None]


def _gateup_body(se_ref, st_ref, sb_ref, sf_ref, ob_ref, ot_ref, nl_ref, xs_ref, wg_ref, wu_ref, bg_ref, bu_ref,
                 act_ref, wgb_ref, wub_ref):
    s = pl.program_id(0)
    live = s < nl_ref[0]

    @pl.when(sf_ref[s] == 1)
    def _():
        wgb_ref[...] = wg_ref[0].astype(BF16)
        wub_ref[...] = wu_ref[0].astype(BF16)

    @pl.when(live)
    def _():
        xb = xs_ref[...]
        gate = jnp.dot(xb, wgb_ref[...], preferred_element_type=F32) + bg_ref[0]
        up = jnp.dot(xb, wub_ref[...], preferred_element_type=F32) + bu_ref[0]
        gate = jnp.minimum(gate, SWIGLU_LIMIT)
        up = jnp.clip(up, -SWIGLU_LIMIT, SWIGLU_LIMIT)
        glu = gate * jax.nn.sigmoid(gate * SWIGLU_ALPHA)
        act_ref[...] = ((up + 1.0) * glu).astype(act_ref.dtype)

    @pl.when(jnp.logical_not(live))
    def _():
        act_ref[...] = jnp.zeros_like(act_ref)


def _gateup(xs, sched, wgu, bgu, *, tm, tf):
    R, D = xs.shape
    d_ff = wgu.shape[2] // 2
    nj = d_ff // tf
    nb = R // tm
    assert d_ff % tf == 0 and R % tm == 0
    grid_spec = pltpu.PrefetchScalarGridSpec(
        num_scalar_prefetch=7,
        grid=(nb * nj,),
        in_specs=[
            pl.BlockSpec((tm, D), lambda s, se, st, sb, sf, ob, ot, nl: (sb[s], 0)),
            pl.BlockSpec((1, D, tf), lambda s, se, st, sb, sf, ob, ot, nl: (se[s], 0, st[s])),
            pl.BlockSpec((1, D, tf), lambda s, se, st, sb, sf, ob, ot, nl: (se[s], 0, nj + st[s])),
            pl.BlockSpec((1, 1, tf), lambda s, se, st, sb, sf, ob, ot, nl: (se[s], 0, st[s])),
            pl.BlockSpec((1, 1, tf), lambda s, se, st, sb, sf, ob, ot, nl: (se[s], 0, nj + st[s])),
        ],
        out_specs=pl.BlockSpec((tm, tf), lambda s, se, st, sb, sf, ob, ot, nl: (ob[s], ot[s])),
        scratch_shapes=[pltpu.VMEM((D, tf), BF16), pltpu.VMEM((D, tf), BF16)],
    )
    return pl.pallas_call(
        _gateup_body,
        grid_spec=grid_spec,
        out_shape=jax.ShapeDtypeStruct((R, d_ff), BF16),
        compiler_params=_cparams(("arbitrary",)),
        name="gateup",
    )(*sched, xs, wgu, wgu, bgu, bgu)


_LOAD, _COMPUTE, _ZERO, _IDLE = 0, 1, 2, 3


def _down_schedule(nblk, blk_start, nvb, n_load, nb):
    n_steps = nb + n_load * nblk.shape[0]
    steps_per_e = jnp.where(nblk > 0, nblk + n_load, 0)
    e_end = jnp.cumsum(steps_per_e)
    e_start = e_end - steps_per_e
    total = e_end[-1]
    s = jnp.arange(n_steps, dtype=jnp.int32)
    live = s < total
    sl = jnp.minimum(s, total - 1)
    e = jnp.sum(e_end[None, :] <= sl[:, None], axis=1).astype(jnp.int32)
    r = sl - e_start[e]
    loading = r < n_load
    chunk = jnp.where(loading, r, n_load - 1)
    blk = blk_start[e] + jnp.where(loading, 0, r - n_load)
    dead_blk = nvb + (s - total)
    kind = jnp.where(live, jnp.where(loading, _LOAD, _COMPUTE), jnp.where(dead_blk < nb, _ZERO, _IDLE))
    out_blk = jnp.where(live, blk, jnp.minimum(dead_blk, nb - 1))
    as_i32 = lambda a: a.astype(jnp.int32)
    return as_i32(kind), as_i32(e), as_i32(chunk), as_i32(blk), as_i32(out_blk)


def _down_body(kind_ref, se_ref, sc_ref, sb_ref, ob_ref, act_ref, wd_ref, bd_ref, y_ref, wdb_ref, *, chunk_rows):
    s = pl.program_id(0)
    kind = kind_ref[s]

    @pl.when(kind == _LOAD)
    def _():
        row0 = pl.multiple_of(sc_ref[s] * chunk_rows, chunk_rows)
        wdb_ref[pl.ds(row0, chunk_rows), :] = wd_ref[0].astype(BF16)

    @pl.when(kind == _COMPUTE)
    def _():
        y = jnp.dot(act_ref[...], wdb_ref[...], preferred_element_type=F32) + bd_ref[0]
        _slab_store(y_ref, y)

    @pl.when(kind == _ZERO)
    def _():
        y_ref[...] = jnp.zeros_like(y_ref)


def _down(act, sched, wd, bd, *, tm, n_load):
    R, d_ff = act.shape
    D = wd.shape[2]
    nb = R // tm
    chunk_rows = d_ff // n_load
    assert d_ff % n_load == 0 and D // LANES < SLAB_PITCH
    grid_spec = pltpu.PrefetchScalarGridSpec(
        num_scalar_prefetch=5,
        grid=(sched[0].shape[0],),
        in_specs=[
            pl.BlockSpec((tm, d_ff), lambda s, kd, se, sc, sb, ob: (sb[s], 0)),
            pl.BlockSpec((1, chunk_rows, D), lambda s, kd, se, sc, sb, ob: (se[s], sc[s], 0)),
            pl.BlockSpec((1, 1, D), lambda s, kd, se, sc, sb, ob: (se[s], 0, 0)),
        ],
        out_specs=pl.BlockSpec((tm * SLAB_PITCH, LANES), lambda s, kd, se, sc, sb, ob: (ob[s], 0)),
        scratch_shapes=[pltpu.VMEM((d_ff, D), BF16)],
    )
    return pl.pallas_call(
        functools.partial(_down_body, chunk_rows=chunk_rows),
        grid_spec=grid_spec,
        out_shape=jax.ShapeDtypeStruct((R * SLAB_PITCH, LANES), F32),
        compiler_params=_cparams(("arbitrary",)),
        name="down",
    )(*sched, act, wd, bd)


def _combine_body(dest_cur_ref, dest_nxt_ref, x2_ref, route_ref, gf_ref, ys_hbm, o_ref, buf_ref, sem, *, tm, n_col):
    i = pl.program_id(0)
    slot = i % 2

    def start_rows(dest_ref, dst_slot):
        def issue(t, carry):
            for k in range(TOP_K):
                d = dest_ref[0, 0, t * TOP_K + k]
                pltpu.make_async_copy(ys_hbm.at[pl.ds(d * SLAB_PITCH, n_col)],
                                      buf_ref.at[dst_slot, k, pl.ds(t * SLAB_PITCH, n_col)],
                                      sem.at[dst_slot]).start(priority=k % DMA_QUEUES)
            return carry
        lax.fori_loop(0, tm, issue, 0, unroll=2)

    def wait_rows(dst_slot):
        for k in range(TOP_K):
            pltpu.make_async_copy(ys_hbm.at[pl.ds(0, tm * n_col)], buf_ref.at[dst_slot, k, pl.ds(0, tm * n_col)],
                                  sem.at[dst_slot]).wait()

    @pl.when(i == 0)
    def _():
        start_rows(dest_cur_ref, 0)

    @pl.when(i + 1 < pl.num_programs(0))
    def _():
        start_rows(dest_nxt_ref, 1 - slot)

    wait_rows(slot)
    route = route_ref[...]
    x3 = x2_ref[...]
    for k in range(TOP_K):
        x3 = x3 + route[:, TOP_K + k:TOP_K + k + 1] * _slab_load(buf_ref.at[slot, k], tm, n_col)
    ms = jnp.mean(x3 * x3, axis=-1, keepdims=True)
    o_ref[...] = x3 * lax.rsqrt(ms + EPS) * gf_ref[...]


def _combine(dest, x2, route, gf, ys_slab, *, tm):
    T, D = x2.shape
    n = T // tm
    assert T % tm == 0
    return pl.pallas_call(
        functools.partial(_combine_body, tm=tm, n_col=D // LANES),
        grid=(n,),
        in_specs=[
            pl.BlockSpec((1, 1, tm * TOP_K), lambda i: (i, 0, 0), memory_space=pltpu.SMEM),
            pl.BlockSpec((1, 1, tm * TOP_K), lambda i: (jnp.minimum(i + 1, n - 1), 0, 0), memory_space=pltpu.SMEM),
            pl.BlockSpec((tm, D), lambda i: (i, 0)),
            pl.BlockSpec((tm, LANES), lambda i: (i, 0)),
            pl.BlockSpec((1, D), lambda i: (0, 0)),
            pl.BlockSpec(memory_space=pl.ANY),
        ],
        out_specs=pl.BlockSpec((tm, D), lambda i: (i, 0)),
        out_shape=jax.ShapeDtypeStruct((T, D), F32),
        scratch_shapes=[pltpu.VMEM((2, TOP_K, tm * SLAB_PITCH, LANES), F32), pltpu.SemaphoreType.DMA((2,))],
        compiler_params=_cparams(("arbitrary",)),
        name="combine",
    )(dest, dest, x2, route, gf, ys_slab)


def _retention_tables(S, dk):
    pos = jnp.arange(S, dtype=F32)
    inv_freq = ROPE_BASE ** (-jnp.arange(0, dk, 2, dtype=F32) / dk)
    ang = pos[:, None] * inv_freq[None, :]
    cos, sin = jnp.cos(ang), jnp.sin(ang)
    cos2 = jnp.concatenate([cos, cos], axis=-1)
    sin2 = jnp.concatenate([-sin, sin], axis=-1)
    log_gamma = jnp.log1p(-jnp.exp2(-5.0 - jnp.arange(RET_HEADS, dtype=F32)))
    idx = jnp.arange(CHUNK, dtype=F32)
    diff = idx[:, None] - idx[None, :]
    decay = jnp.where(diff[None] >= 0, jnp.exp(log_gamma[:, None, None] * jnp.maximum(diff, 0.0)[None]), 0.0)
    xi = jnp.exp(log_gamma[:, None] * (idx[None, :] + 1.0))
    zeta = jnp.exp(log_gamma[:, None] * (CHUNK - 1.0 - idx[None, :]))
    cd = jnp.exp(log_gamma * CHUNK)
    xi_b = jnp.broadcast_to(xi[:, :, None], (RET_HEADS, CHUNK, LANES))
    zeta_b = jnp.broadcast_to(zeta[:, :, None], (RET_HEADS, CHUNK, dk))
    return cos2, sin2, decay, xi_b, zeta_b, cd


TM_INPROJ = 1024
TN_INPROJ = 1024
TM_MIXING = 256
TM_MERGE = 256
TM_EXPERT = 512
TF_EXPERT = 512
DOWN_LOAD_STEPS = 2
TM_COMBINE = 256


def kernel(x, norm1_g, w_in, gm_ln_g, gm_ln_b, gm_ws, gm_b, w_proj_a, w_proj_r, w_out, norm2_g, router_w, router_b,
           w_gate_up, b_gate_up, w_down, b_down, final_norm_g):
    B, S, D = x.shape
    assert B == 1 and norm1_g.shape[0] == 1, "single sequence, depth 1"
    gm_width = w_proj_a.shape[1]
    v_width = w_proj_r.shape[1]
    qk_width = (w_in.shape[2] - 2 * gm_width - 2 * v_width - 2 * D) // 2
    dk = qk_width // RET_HEADS
    assert dk == LANES and gm_ws.shape[2] == CHUNK
    G = gm_ws.shape[1]
    E = router_w.shape[2]
    d_ff = w_down.shape[2]
    x2d = x.reshape(S, D)

    cos2, sin2, decay, xi_b, zeta_b, cd = _retention_tables(S, dk)
    tm_p = min(TM_INPROJ, S)
    h = _prenorm(x2d, norm1_g[0][None], tm=tm_p)
    w_in_bf = w_in[0].astype(BF16)
    segments = (
        (gm_width, "gelu", ()),
        (gm_width, "gelu_layernorm", (gm_ln_g[0][None], gm_ln_b[0][None])),
        (qk_width, "rope", (cos2, sin2)),
        (qk_width, "rope_scaled", (cos2, sin2)),
        (v_width, "identity", ()),
        (v_width, "silu", ()),
        (2 * D, "sigmoid", ()),
    )
    outs, col0 = [], 0
    for width, kind, extras in segments:
        outs.append(_segment(h, w_in_bf, col0, width, kind, extras, tm=tm_p, tn=TN_INPROJ))
        col0 += width
    a_u, a_vn, r_q, r_k, r_v, r_sg, gates = outs

    causal = jnp.tril(jnp.ones((CHUNK, CHUNK), dtype=bool))
    wm = jnp.where(causal[None], gm_ws[0], 0.0).astype(BF16)
    bs_b = jnp.broadcast_to(gm_b[0][:, :, None], (G, CHUNK, GM_GROUP_DIM))
    ga, gr = _mixing(a_u, a_vn, r_q, r_k, r_v, r_sg, wm, bs_b, decay, xi_b, zeta_b, cd, tm=min(TM_MIXING, S))

    rw = jnp.pad(router_w[0], ((0, 0), (0, LANES - E)))
    rw_hi = rw.astype(BF16)
    rw_lo = (rw - rw_hi.astype(F32)).astype(BF16)
    rb = jnp.pad(router_b[0], (0, LANES - E))[None]
    x2, h2, route, cnt = _merge(x2d, ga, gr, gates, w_proj_a[0].astype(BF16), w_proj_r[0].astype(BF16),
                                w_out[0].astype(BF16), norm2_g[0][None], rw_hi, rw_lo, rb, tm=min(TM_MERGE, S))

    tm_e = TM_EXPERT
    eidx = route[:, 0:TOP_K].astype(jnp.int32)
    rank = route[:, 2 * TOP_K:3 * TOP_K].astype(jnp.int32)
    counts = cnt[0, :E].astype(jnp.int32)
    nblk = (counts + tm_e - 1) // tm_e
    blk_end = jnp.cumsum(nblk)
    blk_start = blk_end - nblk
    dest = (blk_start * tm_e)[eidx] + rank
    nb = (S * TOP_K + tm_e - 1) // tm_e + E
    nvb = blk_end[-1].astype(jnp.int32)
    tok = jnp.broadcast_to(jnp.arange(S, dtype=jnp.int32)[:, None], (S, TOP_K))
    row_tok = jnp.zeros((nb * tm_e,), jnp.int32).at[dest.reshape(-1)].set(
        tok.reshape(-1), unique_indices=True, mode="promise_in_bounds")

    xs = _dispatch(h2, row_tok.reshape(nb, 1, tm_e), nvb[None], tm=tm_e, D=D)
    tf = min(TF_EXPERT, d_ff)
    act = _gateup(xs, _tile_schedule(nblk, blk_start, nvb, d_ff // tf, nb), w_gate_up[0], b_gate_up[0][:, None, :],
                  tm=tm_e, tf=tf)
    ys = _down(act, _down_schedule(nblk, blk_start, nvb, DOWN_LOAD_STEPS, nb), w_down[0], b_down[0][:, None, :],
               tm=tm_e, n_load=DOWN_LOAD_STEPS)

    tm_c = min(TM_COMBINE, S)
    out = _combine(dest.reshape(S // tm_c, 1, tm_c * TOP_K), x2, route, final_norm_g[None], ys, tm=tm_c)
    return out.reshape(B, S, D)
```

```python
import functools

import jax
import jax.numpy as jnp
from jax import lax
from jax.experimental import pallas as pl
from jax.experimental.pallas import tpu as pltpu

F32 = jnp.float32
BF16 = jnp.bfloat16
U32 = jnp.uint32

CHUNK = 128
GM_GROUP_DIM = 128
RET_HEADS = 8
N_EXPERTS = 32
TOP_K = 4
SWIGLU_LIMIT = 7.0
SWIGLU_ALPHA = 1.702
ROPE_BASE = 10000.0
EPS = 1e-6
LANES = 128

VMEM_LIMIT_BYTES = 56 * 1024 * 1024


def _cparams(sem):
    return pltpu.CompilerParams(dimension_semantics=sem, vmem_limit_bytes=VMEM_LIMIT_BYTES)


SLAB_ROWS = 8
SLAB_PITCH = 9
DMA_QUEUES = 2


def _slab_store(ref, value):
    rows, width = value.shape
    half = width // 2
    assert half == SLAB_ROWS * LANES
    lo = lax.bitcast_convert_type(value[:, :half].astype(BF16).astype(F32), U32)
    hi = lax.bitcast_convert_type(value[:, half:].astype(BF16).astype(F32), U32)
    words = (lo >> 16) | hi
    for c in range(SLAB_ROWS):
        ref[pl.ds(c, rows, stride=SLAB_PITCH), :] = words[:, c * LANES:(c + 1) * LANES]
    for c in range(SLAB_ROWS, SLAB_PITCH):
        ref[pl.ds(c, rows, stride=SLAB_PITCH), :] = jnp.zeros((rows, LANES), U32)


def _slab_load(ref, rows):
    words = jnp.concatenate([ref[pl.ds(c, rows, stride=SLAB_PITCH), :] for c in range(SLAB_ROWS)], axis=1)
    lo = lax.bitcast_convert_type(words << 16, F32)
    hi = lax.bitcast_convert_type(words & jnp.uint32(0xFFFF0000), F32)
    return jnp.concatenate([lo, hi], axis=1)


def _gelu_exact(a):
    return 0.5 * a * (1.0 + lax.erf(a * (2.0 ** -0.5)))


def _prenorm_body(x_ref, g_ref, h_ref):
    xf = x_ref[...]
    ms = jnp.mean(xf * xf, axis=-1, keepdims=True)
    h_ref[...] = (xf * lax.rsqrt(ms + EPS) * g_ref[...]).astype(h_ref.dtype)


def _prenorm(x2d, g, *, tm):
    S, D = x2d.shape
    return pl.pallas_call(
        _prenorm_body,
        grid=(S // tm,),
        in_specs=[pl.BlockSpec((tm, D), lambda i: (i, 0)), pl.BlockSpec((1, D), lambda i: (0, 0))],
        out_specs=pl.BlockSpec((tm, D), lambda i: (i, 0)),
        out_shape=jax.ShapeDtypeStruct((S, D), BF16),
        compiler_params=_cparams(("arbitrary",)),
        name="prenorm",
    )(x2d, g)


def _rope_heads(a, cos2, sin2):
    outs = []
    for hd in range(a.shape[1] // LANES):
        ah = a[:, hd * LANES:(hd + 1) * LANES]
        outs.append(ah * cos2 + pltpu.roll(ah, LANES // 2, axis=1) * sin2)
    return jnp.concatenate(outs, axis=1)


def _segment_body(h_ref, w_ref, *refs, kind, k_scale):
    o_ref = refs[-1]
    acc = jnp.dot(h_ref[...], w_ref[...], preferred_element_type=F32)
    if kind == "gelu":
        out = _gelu_exact(acc)
    elif kind == "gelu_layernorm":
        lng_ref, lnb_ref = refs[0], refs[1]
        vf = _gelu_exact(acc)
        mu = jnp.mean(vf, axis=-1, keepdims=True)
        vc = vf - mu
        var = jnp.mean(vc * vc, axis=-1, keepdims=True)
        out = vc * lax.rsqrt(var + EPS) * lng_ref[...] + lnb_ref[...]
    elif kind == "rope":
        out = _rope_heads(acc, refs[0][...], refs[1][...])
    elif kind == "rope_scaled":
        out = _rope_heads(acc, refs[0][...], refs[1][...]) * k_scale
    elif kind == "identity":
        out = acc
    elif kind == "silu":
        out = acc * jax.nn.sigmoid(acc)
    elif kind == "sigmoid":
        out = jax.nn.sigmoid(acc)
    else:
        raise ValueError(kind)
    o_ref[...] = out.astype(o_ref.dtype)


def _segment(h, w_in_bf, col0, width, kind, extras=(), *, tm, tn):
    S, D = h.shape
    assert width % tn == 0 and col0 % tn == 0 and S % tm == 0
    blk0 = col0 // tn
    if kind == "gelu_layernorm":
        assert width == tn
        extra_specs = [pl.BlockSpec((1, tn), lambda j, i: (0, 0))] * 2
    elif kind in ("rope", "rope_scaled"):
        extra_specs = [pl.BlockSpec((tm, LANES), lambda j, i: (i, 0))] * 2
    else:
        extra_specs = []
    return pl.pallas_call(
        functools.partial(_segment_body, kind=kind, k_scale=float(LANES) ** -0.5),
        grid=(width // tn, S // tm),
        in_specs=[
            pl.BlockSpec((tm, D), lambda j, i: (i, 0)),
            pl.BlockSpec((D, tn), lambda j, i: (0, blk0 + j)),
        ] + extra_specs,
        out_specs=pl.BlockSpec((tm, tn), lambda j, i: (i, j)),
        out_shape=jax.ShapeDtypeStruct((S, width), BF16),
        compiler_params=_cparams(("arbitrary", "arbitrary")),
        name="inproj_" + kind,
    )(h, w_in_bf, *extras)


def _mixing_body(cd_ref, u_ref, vn_ref, q_ref, k_ref, v_ref, sg_ref, wm_ref, bs_ref, decay_ref, xi_ref, zeta_ref,
                 ga_ref, gr_ref, state_ref, *, n_chunks, dk, dv):
    @pl.when(pl.program_id(0) == 0)
    def _():
        state_ref[...] = jnp.zeros_like(state_ref)

    def chunk(c, carry):
        rows = pl.ds(pl.multiple_of(c * CHUNK, CHUNK), CHUNK)
        for g in range(wm_ref.shape[0]):
            cols = slice(g * GM_GROUP_DIM, (g + 1) * GM_GROUP_DIM)
            mixed = jnp.dot(wm_ref[g], vn_ref[rows, cols], preferred_element_type=F32) + bs_ref[g]
            ga_ref[rows, cols] = (u_ref[rows, cols].astype(F32) * mixed).astype(ga_ref.dtype)
        for hd in range(RET_HEADS):
            qc = q_ref[rows, hd * dk:(hd + 1) * dk]
            kc = k_ref[rows, hd * dk:(hd + 1) * dk]
            vc = v_ref[rows, hd * dv:(hd + 1) * dv]
            st = state_ref[hd]
            scores = lax.dot_general(qc, kc, (((1,), (1,)), ((), ())), preferred_element_type=F32) * decay_ref[hd]
            inner = jnp.dot(scores.astype(BF16), vc, preferred_element_type=F32)
            cross = jnp.dot(qc, st.astype(BF16), preferred_element_type=F32)
            xi = xi_ref[hd]
            o = inner + cross * jnp.concatenate([xi] * (dv // LANES), axis=1)
            kz = (kc.astype(F32) * zeta_ref[hd]).astype(BF16)
            kv = lax.dot_general(kz, vc, (((0,), (0,)), ((), ())), preferred_element_type=F32)
            state_ref[hd] = st * cd_ref[hd] + kv
            mu = jnp.mean(o, axis=-1, keepdims=True)
            oc = o - mu
            var = jnp.mean(oc * oc, axis=-1, keepdims=True)
            on = oc * lax.rsqrt(var + EPS)
            gr_ref[rows, hd * dv:(hd + 1) * dv] = (on * sg_ref[rows, hd * dv:(hd + 1) * dv].astype(F32)).astype(
                gr_ref.dtype)
        return carry

    lax.fori_loop(0, n_chunks, chunk, 0)


def _mixing(u, vn, q, k, v, sg, wm, bs_b, decay, xi_b, zeta_b, cd, *, tm):
    S, gm_width = u.shape
    qk_width, v_width = q.shape[1], v.shape[1]
    assert S % tm == 0 and tm % CHUNK == 0
    dk, dv = qk_width // RET_HEADS, v_width // RET_HEADS
    G = wm.shape[0]
    body = functools.partial(_mixing_body, n_chunks=tm // CHUNK, dk=dk, dv=dv)
    const3 = lambda i: (0, 0, 0)
    rows = lambda i: (i, 0)
    return pl.pallas_call(
        body,
        grid=(S // tm,),
        in_specs=[
            pl.BlockSpec(memory_space=pltpu.SMEM),
            pl.BlockSpec((tm, gm_width), rows),
            pl.BlockSpec((tm, gm_width), rows),
            pl.BlockSpec((tm, qk_width), rows),
            pl.BlockSpec((tm, qk_width), rows),
            pl.BlockSpec((tm, v_width), rows),
            pl.BlockSpec((tm, v_width), rows),
            pl.BlockSpec((G, CHUNK, CHUNK), const3),
            pl.BlockSpec((G, CHUNK, GM_GROUP_DIM), const3),
            pl.BlockSpec((RET_HEADS, CHUNK, CHUNK), const3),
            pl.BlockSpec((RET_HEADS, CHUNK, LANES), const3),
            pl.BlockSpec((RET_HEADS, CHUNK, dk), const3),
        ],
        out_specs=[
            pl.BlockSpec((tm, gm_width), lambda i: (i, 0)),
            pl.BlockSpec((tm, v_width), lambda i: (i, 0)),
        ],
        out_shape=[
            jax.ShapeDtypeStruct((S, gm_width), BF16),
            jax.ShapeDtypeStruct((S, v_width), BF16),
        ],
        scratch_shapes=[pltpu.VMEM((RET_HEADS, dk, dv), F32)],
        compiler_params=_cparams(("arbitrary",)),
        name="mixing",
    )(cd, u, vn, q, k, v, sg, wm, bs_b, decay, xi_b, zeta_b)


def _merge_body(x_ref, ga_ref, gr_ref, sa_ref, sr_ref, wa_ref, wr_ref, wo_ref, g2_ref, rwh_ref, rwl_ref, rb_ref,
                x2_ref, h2_ref, route_ref, cnt_ref, run_ref):
    i = pl.program_id(0)
    tm = x_ref.shape[0]

    @pl.when(i == 0)
    def _():
        run_ref[...] = jnp.zeros_like(run_ref)

    y_a = jnp.dot(ga_ref[...], wa_ref[...], preferred_element_type=F32)
    y_r = jnp.dot(gr_ref[...], wr_ref[...], preferred_element_type=F32)
    merged = sa_ref[...].astype(F32) * y_a + sr_ref[...].astype(F32) * y_r
    x2 = x_ref[...] + jnp.dot(merged.astype(BF16), wo_ref[...], preferred_element_type=F32)
    x2_ref[...] = x2
    ms = jnp.mean(x2 * x2, axis=-1, keepdims=True)
    h2 = x2 * lax.rsqrt(ms + EPS) * g2_ref[...]
    _slab_store(h2_ref, h2)

    h_hi = h2.astype(BF16)
    h_lo = (h2 - h_hi.astype(F32)).astype(BF16)
    logits = (jnp.dot(h_hi, rwh_ref[...], preferred_element_type=F32)
              + jnp.dot(h_hi, rwl_ref[...], preferred_element_type=F32)
              + jnp.dot(h_lo, rwh_ref[...], preferred_element_type=F32)) + rb_ref[...]
    lane = lax.broadcasted_iota(jnp.int32, (tm, LANES), 1)
    neg_inf = jnp.float32(-jnp.inf)
    work = jnp.where(lane < N_EXPERTS, logits, neg_inf)

    vals, idxs, sels = [], [], []
    for _ in range(TOP_K):
        m = jnp.max(work, axis=-1, keepdims=True)
        idx = jnp.min(jnp.where(work == m, lane, LANES), axis=-1, keepdims=True)
        sel = lane == idx
        vals.append(m)
        idxs.append(idx)
        sels.append(sel)
        work = jnp.where(sel, neg_inf, work)
    exps = [jnp.exp(v - vals[0]) for v in vals]
    denom = exps[0] + exps[1] + exps[2] + exps[3]
    gates = [e / denom for e in exps]

    onehot = jnp.zeros((tm, LANES), F32)
    for sel in sels:
        onehot = onehot + sel.astype(F32)
    r_iota = lax.broadcasted_iota(jnp.int32, (tm, tm), 0)
    c_iota = lax.broadcasted_iota(jnp.int32, (tm, tm), 1)
    strict_lower = (c_iota < r_iota).astype(BF16)
    before = jnp.dot(strict_lower, onehot.astype(BF16), preferred_element_type=F32) + run_ref[0:1, :]
    ranks = [jnp.sum(jnp.where(sel, before, 0.0), axis=-1, keepdims=True) for sel in sels]
    run_new = run_ref[0:1, :] + jnp.sum(onehot, axis=0, keepdims=True)
    run_ref[...] = jnp.broadcast_to(run_new, run_ref.shape)
    cnt_ref[...] = jnp.broadcast_to(run_new, cnt_ref.shape)

    route = jnp.zeros((tm, LANES), F32)
    for k in range(TOP_K):
        route = jnp.where(lane == k, idxs[k].astype(F32), route)
        route = jnp.where(lane == TOP_K + k, gates[k], route)
        route = jnp.where(lane == 2 * TOP_K + k, ranks[k], route)
    route_ref[...] = route


def _merge(x2d, ga, gr, gates, wa, wr, wo, g2, rw_hi, rw_lo, rb, *, tm):
    S, D = x2d.shape
    assert S % tm == 0
    const = lambda i: (0, 0)
    resident = functools.partial(pl.BlockSpec, index_map=const, pipeline_mode=pl.Buffered(1))
    return pl.pallas_call(
        _merge_body,
        grid=(S // tm,),
        in_specs=[
            pl.BlockSpec((tm, D), lambda i: (i, 0)),
            pl.BlockSpec((tm, ga.shape[1]), lambda i: (i, 0)),
            pl.BlockSpec((tm, gr.shape[1]), lambda i: (i, 0)),
            pl.BlockSpec((tm, D), lambda i: (i, 0)),
            pl.BlockSpec((tm, D), lambda i: (i, 1)),
            resident(wa.shape),
            resident(wr.shape),
            resident(wo.shape),
            pl.BlockSpec((1, D), const),
            resident(rw_hi.shape),
            resident(rw_lo.shape),
            pl.BlockSpec((1, LANES), const),
        ],
        out_specs=[
            pl.BlockSpec((tm, D), lambda i: (i, 0)),
            pl.BlockSpec((tm * SLAB_PITCH, LANES), lambda i: (i, 0)),
            pl.BlockSpec((tm, LANES), lambda i: (i, 0)),
            pl.BlockSpec((8, LANES), const),
        ],
        out_shape=[
            jax.ShapeDtypeStruct((S, D), F32),
            jax.ShapeDtypeStruct((S * SLAB_PITCH, LANES), U32),
            jax.ShapeDtypeStruct((S, LANES), F32),
            jax.ShapeDtypeStruct((8, LANES), F32),
        ],
        scratch_shapes=[pltpu.VMEM((8, LANES), F32)],
        compiler_params=_cparams(("arbitrary",)),
        name="merge",
    )(x2d, ga, gr, gates, gates, wa, wr, wo, g2, rw_hi, rw_lo, rb)


def _dispatch_body(nvb_ref, tok_cur_ref, tok_nxt_ref, h2_hbm, xs_ref, rows_ref, sem, *, tm):
    i = pl.program_id(0)
    nvb = nvb_ref[0]
    slot = i % 2

    def start_rows(tok_ref, dst_slot):
        def issue(p, carry):
            for q in range(DMA_QUEUES):
                r = p * DMA_QUEUES + q
                t = tok_ref[0, 0, r]
                pltpu.make_async_copy(h2_hbm.at[pl.ds(t * SLAB_PITCH, SLAB_ROWS)],
                                      rows_ref.at[dst_slot, pl.ds(r * SLAB_PITCH, SLAB_ROWS)],
                                      sem.at[dst_slot]).start(priority=q)
            return carry
        lax.fori_loop(0, tm // DMA_QUEUES, issue, 0, unroll=4)

    def wait_rows(dst_slot):
        pltpu.make_async_copy(h2_hbm.at[pl.ds(0, tm * SLAB_ROWS)], rows_ref.at[dst_slot, pl.ds(0, tm * SLAB_ROWS)],
                              sem.at[dst_slot]).wait()

    @pl.when(i < nvb)
    def _():
        @pl.when(i == 0)
        def _():
            start_rows(tok_cur_ref, 0)

        @pl.when(i + 1 < nvb)
        def _():
            start_rows(tok_nxt_ref, 1 - slot)

        wait_rows(slot)
        xs_ref[...] = _slab_load(rows_ref.at[slot], tm).astype(xs_ref.dtype)

    @pl.when(i >= nvb)
    def _():
        xs_ref[...] = jnp.zeros_like(xs_ref)


def _dispatch(h2_slab, row_tok, nvb, *, tm):
    D = 2 * SLAB_ROWS * LANES
    nb = row_tok.shape[0]
    grid_spec = pltpu.PrefetchScalarGridSpec(
        num_scalar_prefetch=1,
        grid=(nb,),
        in_specs=[
            pl.BlockSpec((1, 1, tm), lambda i, nv: (jnp.minimum(i, nv[0] - 1), 0, 0), memory_space=pltpu.SMEM),
            pl.BlockSpec((1, 1, tm), lambda i, nv: (jnp.minimum(i + 1, nv[0] - 1), 0, 0), memory_space=pltpu.SMEM),
            pl.BlockSpec(memory_space=pl.ANY),
        ],
        out_specs=pl.BlockSpec((tm, D), lambda i, nv: (i, 0)),
        scratch_shapes=[pltpu.VMEM((2, tm * SLAB_PITCH, LANES), U32), pltpu.SemaphoreType.DMA((2,))],
    )
    return pl.pallas_call(
        functools.partial(_dispatch_body, tm=tm),
        grid_spec=grid_spec,
        out_shape=jax.ShapeDtypeStruct((nb * tm, D), BF16),
        compiler_params=_cparams(("arbitrary",)),
        name="dispatch",
    )(nvb, row_tok, row_tok, h2_slab)


def _tile_schedule(nblk, blk_start, nvb, n_tiles, nb):
    n_steps = nb * n_tiles
    steps_per_e = n_tiles * nblk
    e_end = jnp.cumsum(steps_per_e)
    e_start = e_end - steps_per_e
    total = e_end[-1]
    s = jnp.arange(n_steps, dtype=jnp.int32)
    live = s < total
    sl = jnp.minimum(s, total - 1)
    e = jnp.sum(e_end[None, :] <= sl[:, None], axis=1).astype(jnp.int32)
    r = sl - e_start[e]
    nb_e = jnp.maximum(nblk[e], 1)
    tile = r // nb_e
    b = r - tile * nb_e
    blk = blk_start[e] + b
    first = (live & (b == 0)).astype(jnp.int32)
    dead = jnp.maximum(s - total, 0)
    out_blk = jnp.where(live, blk, nvb + dead // n_tiles)
    out_tile = jnp.where(live, tile, dead % n_tiles)
    as_i32 = lambda a: a.astype(jnp.int32)
    return as_i32(e), as_i32(tile), as_i32(blk), first, as_i32(out_blk), as_i32(out_tile), as_i32(total)[None]


def _gateup_body(se_ref, st_ref, sb_ref, sf_ref, ob_ref, ot_ref, nl_ref, xs_ref, wg_ref, wu_ref, bg_ref, bu_ref,
                 act_ref, wgb_ref, wub_ref):
    s = pl.program_id(0)
    live = s < nl_ref[0]

    @pl.when(sf_ref[s] == 1)
    def _():
        wgb_ref[...] = wg_ref[0].astype(BF16)
        wub_ref[...] = wu_ref[0].astype(BF16)

    @pl.when(live)
    def _():
        xb = xs_ref[...]
        gate = jnp.dot(xb, wgb_ref[...], preferred_element_type=F32) + bg_ref[0]
        up = jnp.dot(xb, wub_ref[...], preferred_element_type=F32) + bu_ref[0]
        gate = jnp.minimum(gate, SWIGLU_LIMIT)
        up = jnp.clip(up, -SWIGLU_LIMIT, SWIGLU_LIMIT)
        glu = gate * jax.nn.sigmoid(gate * SWIGLU_ALPHA)
        act_ref[...] = ((up + 1.0) * glu).astype(act_ref.dtype)

    @pl.when(jnp.logical_not(live))
    def _():
        act_ref[...] = jnp.zeros_like(act_ref)


def _gateup(xs, sched, wgu, bgu, *, tm, tf):
    R, D = xs.shape
    d_ff = wgu.shape[2] // 2
    nj = d_ff // tf
    nb = R // tm
    assert d_ff % tf == 0 and R % tm == 0
    grid_spec = pltpu.PrefetchScalarGridSpec(
        num_scalar_prefetch=7,
        grid=(nb * nj,),
        in_specs=[
            pl.BlockSpec((tm, D), lambda s, se, st, sb, sf, ob, ot, nl: (sb[s], 0)),
            pl.BlockSpec((1, D, tf), lambda s, se, st, sb, sf, ob, ot, nl: (se[s], 0, st[s])),
            pl.BlockSpec((1, D, tf), lambda s, se, st, sb, sf, ob, ot, nl: (se[s], 0, nj + st[s])),
            pl.BlockSpec((1, 1, tf), lambda s, se, st, sb, sf, ob, ot, nl: (se[s], 0, st[s])),
            pl.BlockSpec((1, 1, tf), lambda s, se, st, sb, sf, ob, ot, nl: (se[s], 0, nj + st[s])),
        ],
        out_specs=pl.BlockSpec((tm, tf), lambda s, se, st, sb, sf, ob, ot, nl: (ob[s], ot[s])),
        scratch_shapes=[pltpu.VMEM((D, tf), BF16), pltpu.VMEM((D, tf), BF16)],
    )
    return pl.pallas_call(
        _gateup_body,
        grid_spec=grid_spec,
        out_shape=jax.ShapeDtypeStruct((R, d_ff), BF16),
        compiler_params=_cparams(("arbitrary",)),
        name="gateup",
    )(*sched, xs, wgu, wgu, bgu, bgu)


_LOAD, _COMPUTE, _ZERO, _IDLE = 0, 1, 2, 3


def _down_schedule(nblk, blk_start, nvb, n_load, nb):
    n_steps = nb + n_load * nblk.shape[0]
    steps_per_e = jnp.where(nblk > 0, nblk + n_load, 0)
    e_end = jnp.cumsum(steps_per_e)
    e_start = e_end - steps_per_e
    total = e_end[-1]
    s = jnp.arange(n_steps, dtype=jnp.int32)
    live = s < total
    sl = jnp.minimum(s, total - 1)
    e = jnp.sum(e_end[None, :] <= sl[:, None], axis=1).astype(jnp.int32)
    r = sl - e_start[e]
    loading = r < n_load
    chunk = jnp.where(loading, r, n_load - 1)
    blk = blk_start[e] + jnp.where(loading, 0, r - n_load)
    dead_blk = nvb + (s - total)
    kind = jnp.where(live, jnp.where(loading, _LOAD, _COMPUTE), jnp.where(dead_blk < nb, _ZERO, _IDLE))
    out_blk = jnp.where(live, blk, jnp.minimum(dead_blk, nb - 1))
    as_i32 = lambda a: a.astype(jnp.int32)
    return as_i32(kind), as_i32(e), as_i32(chunk), as_i32(blk), as_i32(out_blk)


def _down_body(kind_ref, se_ref, sc_ref, sb_ref, ob_ref, act_ref, wd_ref, bd_ref, y_ref, wdb_ref, *, chunk_rows):
    s = pl.program_id(0)
    kind = kind_ref[s]

    @pl.when(kind == _LOAD)
    def _():
        row0 = pl.multiple_of(sc_ref[s] * chunk_rows, chunk_rows)
        wdb_ref[pl.ds(row0, chunk_rows), :] = wd_ref[0].astype(BF16)

    @pl.when(kind == _COMPUTE)
    def _():
        y = jnp.dot(act_ref[...], wdb_ref[...], preferred_element_type=F32) + bd_ref[0]
        _slab_store(y_ref, y)

    @pl.when(kind == _ZERO)
    def _():
        y_ref[...] = jnp.zeros_like(y_ref)


def _down(act, sched, wd, bd, *, tm, n_load):
    R, d_ff = act.shape
    D = wd.shape[2]
    nb = R // tm
    chunk_rows = d_ff // n_load
    assert d_ff % n_load == 0
    grid_spec = pltpu.PrefetchScalarGridSpec(
        num_scalar_prefetch=5,
        grid=(sched[0].shape[0],),
        in_specs=[
            pl.BlockSpec((tm, d_ff), lambda s, kd, se, sc, sb, ob: (sb[s], 0)),
            pl.BlockSpec((1, chunk_rows, D), lambda s, kd, se, sc, sb, ob: (se[s], sc[s], 0)),
            pl.BlockSpec((1, 1, D), lambda s, kd, se, sc, sb, ob: (se[s], 0, 0)),
        ],
        out_specs=pl.BlockSpec((tm * SLAB_PITCH, LANES), lambda s, kd, se, sc, sb, ob: (ob[s], 0)),
        scratch_shapes=[pltpu.VMEM((d_ff, D), BF16)],
    )
    return pl.pallas_call(
        functools.partial(_down_body, chunk_rows=chunk_rows),
        grid_spec=grid_spec,
        out_shape=jax.ShapeDtypeStruct((R * SLAB_PITCH, LANES), U32),
        compiler_params=_cparams(("arbitrary",)),
        name="down",
    )(*sched, act, wd, bd)


def _combine_body(dest_cur_ref, dest_nxt_ref, x2_ref, route_ref, gf_ref, ys_hbm, o_ref, buf_ref, sem, *, tm):
    i = pl.program_id(0)
    slot = i % 2

    def start_rows(dest_ref, dst_slot):
        def issue(t, carry):
            for k in range(TOP_K):
                d = dest_ref[0, 0, t * TOP_K + k]
                pltpu.make_async_copy(ys_hbm.at[pl.ds(d * SLAB_PITCH, SLAB_ROWS)],
                                      buf_ref.at[dst_slot, k, pl.ds(t * SLAB_PITCH, SLAB_ROWS)],
                                      sem.at[dst_slot]).start(priority=k % DMA_QUEUES)
            return carry
        lax.fori_loop(0, tm, issue, 0, unroll=2)

    def wait_rows(dst_slot):
        for k in range(TOP_K):
            pltpu.make_async_copy(ys_hbm.at[pl.ds(0, tm * SLAB_ROWS)],
                                  buf_ref.at[dst_slot, k, pl.ds(0, tm * SLAB_ROWS)], sem.at[dst_slot]).wait()

    @pl.when(i == 0)
    def _():
        start_rows(dest_cur_ref, 0)

    @pl.when(i + 1 < pl.num_programs(0))
    def _():
        start_rows(dest_nxt_ref, 1 - slot)

    wait_rows(slot)
    route = route_ref[...]
    x3 = x2_ref[...]
    for k in range(TOP_K):
        x3 = x3 + route[:, TOP_K + k:TOP_K + k + 1] * _slab_load(buf_ref.at[slot, k], tm)
    ms = jnp.mean(x3 * x3, axis=-1, keepdims=True)
    o_ref[...] = x3 * lax.rsqrt(ms + EPS) * gf_ref[...]


def _combine(dest, x2, route, gf, ys_slab, *, tm):
    T, D = x2.shape
    n = T // tm
    assert T % tm == 0
    return pl.pallas_call(
        functools.partial(_combine_body, tm=tm),
        grid=(n,),
        in_specs=[
            pl.BlockSpec((1, 1, tm * TOP_K), lambda i: (i, 0, 0), memory_space=pltpu.SMEM),
            pl.BlockSpec((1, 1, tm * TOP_K), lambda i: (jnp.minimum(i + 1, n - 1), 0, 0), memory_space=pltpu.SMEM),
            pl.BlockSpec((tm, D), lambda i: (i, 0)),
            pl.BlockSpec((tm, LANES), lambda i: (i, 0)),
            pl.BlockSpec((1, D), lambda i: (0, 0)),
            pl.BlockSpec(memory_space=pl.ANY),
        ],
        out_specs=pl.BlockSpec((tm, D), lambda i: (i, 0)),
        out_shape=jax.ShapeDtypeStruct((T, D), F32),
        scratch_shapes=[pltpu.VMEM((2, TOP_K, tm * SLAB_PITCH, LANES), U32), pltpu.SemaphoreType.DMA((2,))],
        compiler_params=_cparams(("arbitrary",)),
        name="combine",
    )(dest, dest, x2, route, gf, ys_slab)


def _retention_tables(S, dk):
    pos = jnp.arange(S, dtype=F32)
    inv_freq = ROPE_BASE ** (-jnp.arange(0, dk, 2, dtype=F32) / dk)
    ang = pos[:, None] * inv_freq[None, :]
    cos, sin = jnp.cos(ang), jnp.sin(ang)
    cos2 = jnp.concatenate([cos, cos], axis=-1)
    sin2 = jnp.concatenate([-sin, sin], axis=-1)
    log_gamma = jnp.log1p(-jnp.exp2(-5.0 - jnp.arange(RET_HEADS, dtype=F32)))
    idx = jnp.arange(CHUNK, dtype=F32)
    diff = idx[:, None] - idx[None, :]
    decay = jnp.where(diff[None] >= 0, jnp.exp(log_gamma[:, None, None] * jnp.maximum(diff, 0.0)[None]), 0.0)
    xi = jnp.exp(log_gamma[:, None] * (idx[None, :] + 1.0))
    zeta = jnp.exp(log_gamma[:, None] * (CHUNK - 1.0 - idx[None, :]))
    cd = jnp.exp(log_gamma * CHUNK)
    xi_b = jnp.broadcast_to(xi[:, :, None], (RET_HEADS, CHUNK, LANES))
    zeta_b = jnp.broadcast_to(zeta[:, :, None], (RET_HEADS, CHUNK, dk))
    return cos2, sin2, decay, xi_b, zeta_b, cd


TM_INPROJ = 1024
TN_INPROJ = 1024
TM_MIXING = 256
TM_MERGE = 256
TM_EXPERT = 512
TF_EXPERT = 512
DOWN_LOAD_STEPS = 2
TM_COMBINE = 256


def kernel(x, norm1_g, w_in, gm_ln_g, gm_ln_b, gm_ws, gm_b, w_proj_a, w_proj_r, w_out, norm2_g, router_w, router_b,
           w_gate_up, b_gate_up, w_down, b_down, final_norm_g):
    B, S, D = x.shape
    assert B == 1 and norm1_g.shape[0] == 1, "single sequence, depth 1"
    gm_width = w_proj_a.shape[1]
    v_width = w_proj_r.shape[1]
    qk_width = (w_in.shape[2] - 2 * gm_width - 2 * v_width - 2 * D) // 2
    dk = qk_width // RET_HEADS
    assert dk == LANES and gm_ws.shape[2] == CHUNK and D == 2 * SLAB_ROWS * LANES
    G = gm_ws.shape[1]
    E = router_w.shape[2]
    d_ff = w_down.shape[2]
    x2d = x.reshape(S, D)

    cos2, sin2, decay, xi_b, zeta_b, cd = _retention_tables(S, dk)
    tm_p = min(TM_INPROJ, S)
    h = _prenorm(x2d, norm1_g[0][None], tm=tm_p)
    w_in_bf = w_in[0].astype(BF16)
    segments = (
        (gm_width, "gelu", ()),
        (gm_width, "gelu_layernorm", (gm_ln_g[0][None], gm_ln_b[0][None])),
        (qk_width, "rope", (cos2, sin2)),
        (qk_width, "rope_scaled", (cos2, sin2)),
        (v_width, "identity", ()),
        (v_width, "silu", ()),
        (2 * D, "sigmoid", ()),
    )
    outs, col0 = [], 0
    for width, kind, extras in segments:
        outs.append(_segment(h, w_in_bf, col0, width, kind, extras, tm=tm_p, tn=TN_INPROJ))
        col0 += width
    a_u, a_vn, r_q, r_k, r_v, r_sg, gates = outs

    causal = jnp.tril(jnp.ones((CHUNK, CHUNK), dtype=bool))
    wm = jnp.where(causal[None], gm_ws[0], 0.0).astype(BF16)
    bs_b = jnp.broadcast_to(gm_b[0][:, :, None], (G, CHUNK, GM_GROUP_DIM))
    ga, gr = _mixing(a_u, a_vn, r_q, r_k, r_v, r_sg, wm, bs_b, decay, xi_b, zeta_b, cd, tm=min(TM_MIXING, S))

    rw = jnp.pad(router_w[0], ((0, 0), (0, LANES - E)))
    rw_hi = rw.astype(BF16)
    rw_lo = (rw - rw_hi.astype(F32)).astype(BF16)
    rb = jnp.pad(router_b[0], (0, LANES - E))[None]
    x2, h2, route, cnt = _merge(x2d, ga, gr, gates, w_proj_a[0].astype(BF16), w_proj_r[0].astype(BF16),
                                w_out[0].astype(BF16), norm2_g[0][None], rw_hi, rw_lo, rb, tm=min(TM_MERGE, S))

    tm_e = TM_EXPERT
    eidx = route[:, 0:TOP_K].astype(jnp.int32)
    rank = route[:, 2 * TOP_K:3 * TOP_K].astype(jnp.int32)
    counts = cnt[0, :E].astype(jnp.int32)
    nblk = (counts + tm_e - 1) // tm_e
    blk_end = jnp.cumsum(nblk)
    blk_start = blk_end - nblk
    dest = (blk_start * tm_e)[eidx] + rank
    nb = (S * TOP_K + tm_e - 1) // tm_e + E
    nvb = blk_end[-1].astype(jnp.int32)
    tok = jnp.broadcast_to(jnp.arange(S, dtype=jnp.int32)[:, None], (S, TOP_K))
    row_tok = jnp.zeros((nb * tm_e,), jnp.int32).at[dest.reshape(-1)].set(
        tok.reshape(-1), unique_indices=True, mode="promise_in_bounds")

    xs = _dispatch(h2, row_tok.reshape(nb, 1, tm_e), nvb[None], tm=tm_e)
    tf = min(TF_EXPERT, d_ff)
    act = _gateup(xs, _tile_schedule(nblk, blk_start, nvb, d_ff // tf, nb), w_gate_up[0], b_gate_up[0][:, None, :],
                  tm=tm_e, tf=tf)
    ys = _down(act, _down_schedule(nblk, blk_start, nvb, DOWN_LOAD_STEPS, nb), w_down[0], b_down[0][:, None, :],
               tm=tm_e, n_load=DOWN_LOAD_STEPS)

    tm_c = min(TM_COMBINE, S)
    out = _combine(dest.reshape(S // tm_c, 1, tm_c * TOP_K), x2, route, final_norm_g[None], ys, tm=tm_c)
    return out.reshape(B, S, D)
```

```python
import functools

import jax
import jax.numpy as jnp
from jax import lax
from jax.experimental import pallas as pl
from jax.experimental.pallas import tpu as pltpu

F32 = jnp.float32
BF16 = jnp.bfloat16
U32 = jnp.uint32

CHUNK = 128
GM_GROUP_DIM = 128
RET_HEADS = 8
N_EXPERTS = 32
TOP_K = 4
SWIGLU_LIMIT = 7.0
SWIGLU_ALPHA = 1.702
ROPE_BASE = 10000.0
EPS = 1e-6
LANES = 128

VMEM_LIMIT_BYTES = 56 * 1024 * 1024


def _cparams(sem):
    return pltpu.CompilerParams(dimension_semantics=sem, vmem_limit_bytes=VMEM_LIMIT_BYTES)


SLAB_ROWS = 8
SLAB_PITCH = 9
DMA_QUEUES = 2


def _slab_store(ref, value):
    rows, width = value.shape
    half = width // 2
    assert half == SLAB_ROWS * LANES
    lo = lax.bitcast_convert_type(value[:, :half].astype(BF16).astype(F32), U32)
    hi = lax.bitcast_convert_type(value[:, half:].astype(BF16).astype(F32), U32)
    words = (lo >> 16) | hi
    for c in range(SLAB_ROWS):
        ref[pl.ds(c, rows, stride=SLAB_PITCH), :] = words[:, c * LANES:(c + 1) * LANES]
    for c in range(SLAB_ROWS, SLAB_PITCH):
        ref[pl.ds(c, rows, stride=SLAB_PITCH), :] = jnp.zeros((rows, LANES), U32)


def _slab_load(ref, rows):
    words = jnp.concatenate([ref[pl.ds(c, rows, stride=SLAB_PITCH), :] for c in range(SLAB_ROWS)], axis=1)
    lo = lax.bitcast_convert_type(words << 16, F32)
    hi = lax.bitcast_convert_type(words & jnp.uint32(0xFFFF0000), F32)
    return jnp.concatenate([lo, hi], axis=1)


def _gelu_exact(a):
    return 0.5 * a * (1.0 + lax.erf(a * (2.0 ** -0.5)))


def _prenorm_body(x_ref, g_ref, h_ref):
    xf = x_ref[...]
    ms = jnp.mean(xf * xf, axis=-1, keepdims=True)
    h_ref[...] = (xf * lax.rsqrt(ms + EPS) * g_ref[...]).astype(h_ref.dtype)


def _prenorm(x2d, g, *, tm):
    S, D = x2d.shape
    return pl.pallas_call(
        _prenorm_body,
        grid=(S // tm,),
        in_specs=[pl.BlockSpec((tm, D), lambda i: (i, 0)), pl.BlockSpec((1, D), lambda i: (0, 0))],
        out_specs=pl.BlockSpec((tm, D), lambda i: (i, 0)),
        out_shape=jax.ShapeDtypeStruct((S, D), BF16),
        compiler_params=_cparams(("arbitrary",)),
        name="prenorm",
    )(x2d, g)


def _rope_heads(a, cos2, sin2):
    outs = []
    for hd in range(a.shape[1] // LANES):
        ah = a[:, hd * LANES:(hd + 1) * LANES]
        outs.append(ah * cos2 + pltpu.roll(ah, LANES // 2, axis=1) * sin2)
    return jnp.concatenate(outs, axis=1)


def _segment_body(h_ref, w_ref, *refs, kind, k_scale):
    o_ref = refs[-1]
    acc = jnp.dot(h_ref[...], w_ref[...], preferred_element_type=F32)
    if kind == "gelu":
        out = _gelu_exact(acc)
    elif kind == "gelu_layernorm":
        lng_ref, lnb_ref = refs[0], refs[1]
        vf = _gelu_exact(acc)
        mu = jnp.mean(vf, axis=-1, keepdims=True)
        vc = vf - mu
        var = jnp.mean(vc * vc, axis=-1, keepdims=True)
        out = vc * lax.rsqrt(var + EPS) * lng_ref[...] + lnb_ref[...]
    elif kind == "rope":
        out = _rope_heads(acc, refs[0][...], refs[1][...])
    elif kind == "rope_scaled":
        out = _rope_heads(acc, refs[0][...], refs[1][...]) * k_scale
    elif kind == "identity":
        out = acc
    elif kind == "silu":
        out = acc * jax.nn.sigmoid(acc)
    elif kind == "sigmoid":
        out = jax.nn.sigmoid(acc)
    else:
        raise ValueError(kind)
    o_ref[...] = out.astype(o_ref.dtype)


def _segment(h, w_in_bf, col0, width, kind, extras=(), *, tm, tn):
    S, D = h.shape
    assert width % tn == 0 and col0 % tn == 0 and S % tm == 0
    blk0 = col0 // tn
    if kind == "gelu_layernorm":
        assert width == tn
        extra_specs = [pl.BlockSpec((1, tn), lambda j, i: (0, 0))] * 2
    elif kind in ("rope", "rope_scaled"):
        extra_specs = [pl.BlockSpec((tm, LANES), lambda j, i: (i, 0))] * 2
    else:
        extra_specs = []
    return pl.pallas_call(
        functools.partial(_segment_body, kind=kind, k_scale=float(LANES) ** -0.5),
        grid=(width // tn, S // tm),
        in_specs=[
            pl.BlockSpec((tm, D), lambda j, i: (i, 0)),
            pl.BlockSpec((D, tn), lambda j, i: (0, blk0 + j)),
        ] + extra_specs,
        out_specs=pl.BlockSpec((tm, tn), lambda j, i: (i, j)),
        out_shape=jax.ShapeDtypeStruct((S, width), BF16),
        compiler_params=_cparams(("arbitrary", "arbitrary")),
        name="inproj_" + kind,
    )(h, w_in_bf, *extras)


def _mixing_body(cd_ref, u_ref, vn_ref, q_ref, k_ref, v_ref, sg_ref, wm_ref, bs_ref, decay_ref, xi_ref, zeta_ref,
                 ga_ref, gr_ref, state_ref, *, n_chunks, dk, dv):
    @pl.when(pl.program_id(0) == 0)
    def _():
        state_ref[...] = jnp.zeros_like(state_ref)

    def chunk(c, carry):
        rows = pl.ds(pl.multiple_of(c * CHUNK, CHUNK), CHUNK)
        for g in range(wm_ref.shape[0]):
            cols = slice(g * GM_GROUP_DIM, (g + 1) * GM_GROUP_DIM)
            mixed = jnp.dot(wm_ref[g], vn_ref[rows, cols], preferred_element_type=F32) + bs_ref[g]
            ga_ref[rows, cols] = (u_ref[rows, cols].astype(F32) * mixed).astype(ga_ref.dtype)
        for hd in range(RET_HEADS):
            qc = q_ref[rows, hd * dk:(hd + 1) * dk]
            kc = k_ref[rows, hd * dk:(hd + 1) * dk]
            vc = v_ref[rows, hd * dv:(hd + 1) * dv]
            st = state_ref[hd]
            scores = lax.dot_general(qc, kc, (((1,), (1,)), ((), ())), preferred_element_type=F32) * decay_ref[hd]
            inner = jnp.dot(scores.astype(BF16), vc, preferred_element_type=F32)
            cross = jnp.dot(qc, st.astype(BF16), preferred_element_type=F32)
            xi = xi_ref[hd]
            o = inner + cross * jnp.concatenate([xi] * (dv // LANES), axis=1)
            kz = (kc.astype(F32) * zeta_ref[hd]).astype(BF16)
            kv = lax.dot_general(kz, vc, (((0,), (0,)), ((), ())), preferred_element_type=F32)
            state_ref[hd] = st * cd_ref[hd] + kv
            mu = jnp.mean(o, axis=-1, keepdims=True)
            oc = o - mu
            var = jnp.mean(oc * oc, axis=-1, keepdims=True)
            on = oc * lax.rsqrt(var + EPS)
            gr_ref[rows, hd * dv:(hd + 1) * dv] = (on * sg_ref[rows, hd * dv:(hd + 1) * dv].astype(F32)).astype(
                gr_ref.dtype)
        return carry

    lax.fori_loop(0, n_chunks, chunk, 0)


def _mixing(u, vn, q, k, v, sg, wm, bs_b, decay, xi_b, zeta_b, cd, *, tm):
    S, gm_width = u.shape
    qk_width, v_width = q.shape[1], v.shape[1]
    assert S % tm == 0 and tm % CHUNK == 0
    dk, dv = qk_width // RET_HEADS, v_width // RET_HEADS
    G = wm.shape[0]
    body = functools.partial(_mixing_body, n_chunks=tm // CHUNK, dk=dk, dv=dv)
    const3 = lambda i: (0, 0, 0)
    rows = lambda i: (i, 0)
    return pl.pallas_call(
        body,
        grid=(S // tm,),
        in_specs=[
            pl.BlockSpec(memory_space=pltpu.SMEM),
            pl.BlockSpec((tm, gm_width), rows),
            pl.BlockSpec((tm, gm_width), rows),
            pl.BlockSpec((tm, qk_width), rows),
            pl.BlockSpec((tm, qk_width), rows),
            pl.BlockSpec((tm, v_width), rows),
            pl.BlockSpec((tm, v_width), rows),
            pl.BlockSpec((G, CHUNK, CHUNK), const3),
            pl.BlockSpec((G, CHUNK, GM_GROUP_DIM), const3),
            pl.BlockSpec((RET_HEADS, CHUNK, CHUNK), const3),
            pl.BlockSpec((RET_HEADS, CHUNK, LANES), const3),
            pl.BlockSpec((RET_HEADS, CHUNK, dk), const3),
        ],
        out_specs=[
            pl.BlockSpec((tm, gm_width), lambda i: (i, 0)),
            pl.BlockSpec((tm, v_width), lambda i: (i, 0)),
        ],
        out_shape=[
            jax.ShapeDtypeStruct((S, gm_width), BF16),
            jax.ShapeDtypeStruct((S, v_width), BF16),
        ],
        scratch_shapes=[pltpu.VMEM((RET_HEADS, dk, dv), F32)],
        compiler_params=_cparams(("arbitrary",)),
        name="mixing",
    )(cd, u, vn, q, k, v, sg, wm, bs_b, decay, xi_b, zeta_b)


def _merge_body(x_ref, ga_ref, gr_ref, sa_ref, sr_ref, wa_ref, wr_ref, wo_ref, g2_ref, rwh_ref, rwl_ref, rb_ref,
                x2_ref, h2_ref, route_ref, cnt_ref, run_ref):
    i = pl.program_id(0)
    tm = x_ref.shape[0]

    @pl.when(i == 0)
    def _():
        run_ref[...] = jnp.zeros_like(run_ref)

    y_a = jnp.dot(ga_ref[...], wa_ref[...], preferred_element_type=F32)
    y_r = jnp.dot(gr_ref[...], wr_ref[...], preferred_element_type=F32)
    merged = sa_ref[...].astype(F32) * y_a + sr_ref[...].astype(F32) * y_r
    x2 = x_ref[...] + jnp.dot(merged.astype(BF16), wo_ref[...], preferred_element_type=F32)
    x2_ref[...] = x2
    ms = jnp.mean(x2 * x2, axis=-1, keepdims=True)
    h2 = x2 * lax.rsqrt(ms + EPS) * g2_ref[...]
    _slab_store(h2_ref, h2)

    h_hi = h2.astype(BF16)
    h_lo = (h2 - h_hi.astype(F32)).astype(BF16)
    logits = (jnp.dot(h_hi, rwh_ref[...], preferred_element_type=F32)
              + jnp.dot(h_hi, rwl_ref[...], preferred_element_type=F32)
              + jnp.dot(h_lo, rwh_ref[...], preferred_element_type=F32)) + rb_ref[...]
    lane = lax.broadcasted_iota(jnp.int32, (tm, LANES), 1)
    neg_inf = jnp.float32(-jnp.inf)
    work = jnp.where(lane < N_EXPERTS, logits, neg_inf)

    vals, idxs, sels = [], [], []
    for _ in range(TOP_K):
        m = jnp.max(work, axis=-1, keepdims=True)
        idx = jnp.min(jnp.where(work == m, lane, LANES), axis=-1, keepdims=True)
        sel = lane == idx
        vals.append(m)
        idxs.append(idx)
        sels.append(sel)
        work = jnp.where(sel, neg_inf, work)
    exps = [jnp.exp(v - vals[0]) for v in vals]
    denom = exps[0] + exps[1] + exps[2] + exps[3]
    gates = [e / denom for e in exps]

    onehot = jnp.zeros((tm, LANES), F32)
    for sel in sels:
        onehot = onehot + sel.astype(F32)
    r_iota = lax.broadcasted_iota(jnp.int32, (tm, tm), 0)
    c_iota = lax.broadcasted_iota(jnp.int32, (tm, tm), 1)
    strict_lower = (c_iota < r_iota).astype(BF16)
    before = jnp.dot(strict_lower, onehot.astype(BF16), preferred_element_type=F32) + run_ref[0:1, :]
    ranks = [jnp.sum(jnp.where(sel, before, 0.0), axis=-1, keepdims=True) for sel in sels]
    run_new = run_ref[0:1, :] + jnp.sum(onehot, axis=0, keepdims=True)
    run_ref[...] = jnp.broadcast_to(run_new, run_ref.shape)
    cnt_ref[...] = jnp.broadcast_to(run_new, cnt_ref.shape)

    route = jnp.zeros((tm, LANES), F32)
    for k in range(TOP_K):
        route = jnp.where(lane == k, idxs[k].astype(F32), route)
        route = jnp.where(lane == TOP_K + k, gates[k], route)
        route = jnp.where(lane == 2 * TOP_K + k, ranks[k], route)
    route_ref[...] = route


def _merge(x2d, ga, gr, gates, wa, wr, wo, g2, rw_hi, rw_lo, rb, *, tm):
    S, D = x2d.shape
    assert S % tm == 0
    const = lambda i: (0, 0)
    resident = functools.partial(pl.BlockSpec, index_map=const, pipeline_mode=pl.Buffered(1))
    return pl.pallas_call(
        _merge_body,
        grid=(S // tm,),
        in_specs=[
            pl.BlockSpec((tm, D), lambda i: (i, 0)),
            pl.BlockSpec((tm, ga.shape[1]), lambda i: (i, 0)),
            pl.BlockSpec((tm, gr.shape[1]), lambda i: (i, 0)),
            pl.BlockSpec((tm, D), lambda i: (i, 0)),
            pl.BlockSpec((tm, D), lambda i: (i, 1)),
            resident(wa.shape),
            resident(wr.shape),
            resident(wo.shape),
            pl.BlockSpec((1, D), const),
            resident(rw_hi.shape),
            resident(rw_lo.shape),
            pl.BlockSpec((1, LANES), const),
        ],
        out_specs=[
            pl.BlockSpec((tm, D), lambda i: (i, 0)),
            pl.BlockSpec((tm * SLAB_PITCH, LANES), lambda i: (i, 0)),
            pl.BlockSpec((tm, LANES), lambda i: (i, 0)),
            pl.BlockSpec((8, LANES), const),
        ],
        out_shape=[
            jax.ShapeDtypeStruct((S, D), F32),
            jax.ShapeDtypeStruct((S * SLAB_PITCH, LANES), U32),
            jax.ShapeDtypeStruct((S, LANES), F32),
            jax.ShapeDtypeStruct((8, LANES), F32),
        ],
        scratch_shapes=[pltpu.VMEM((8, LANES), F32)],
        compiler_params=_cparams(("arbitrary",)),
        name="merge",
    )(x2d, ga, gr, gates, gates, wa, wr, wo, g2, rw_hi, rw_lo, rb)


def _dispatch_body(clear_ref, dest_ref, h2_ref, xs_hbm, zeros_ref, sem, clear_sem, *, tm, blk_rows, nb):
    i = pl.program_id(0)

    def clear_copy(b):
        return pltpu.make_async_copy(zeros_ref, xs_hbm.at[pl.ds(pl.multiple_of(b * blk_rows, blk_rows), blk_rows)],
                                     clear_sem.at[0])

    @pl.when(i == 0)
    def _():
        zeros_ref[...] = jnp.zeros_like(zeros_ref)

        def start(b, carry):
            @pl.when(clear_ref[b] == 1)
            def _():
                clear_copy(b).start()
            return carry

        def drain(b, carry):
            @pl.when(clear_ref[b] == 1)
            def _():
                clear_copy(b).wait()
            return carry

        lax.fori_loop(0, nb, start, 0)
        lax.fori_loop(0, nb, drain, 0)

    def issue(t, carry):
        for k in range(TOP_K):
            d = dest_ref[0, 0, t * TOP_K + k]
            pltpu.make_async_copy(h2_ref.at[pl.ds(t * SLAB_PITCH, SLAB_PITCH)],
                                  xs_hbm.at[pl.ds(d * SLAB_PITCH, SLAB_PITCH)],
                                  sem.at[0]).start(priority=k % DMA_QUEUES)
        return carry

    lax.fori_loop(0, tm, issue, 0, unroll=2)
    for k in range(TOP_K):
        pltpu.make_async_copy(h2_ref, xs_hbm.at[pl.ds(0, tm * SLAB_PITCH)], sem.at[0]).wait()


def _dispatch(h2_slab, dest, clear_flag, *, tm, tm_e):
    nb = clear_flag.shape[0]
    blk_rows = tm_e * SLAB_PITCH
    n = h2_slab.shape[0] // (tm * SLAB_PITCH)
    grid_spec = pltpu.PrefetchScalarGridSpec(
        num_scalar_prefetch=1,
        grid=(n,),
        in_specs=[
            pl.BlockSpec((1, 1, tm * TOP_K), lambda i, cf: (i, 0, 0), memory_space=pltpu.SMEM),
            pl.BlockSpec((tm * SLAB_PITCH, LANES), lambda i, cf: (i, 0)),
        ],
        out_specs=pl.BlockSpec(memory_space=pl.ANY),
        scratch_shapes=[pltpu.VMEM((blk_rows, LANES), U32), pltpu.SemaphoreType.DMA((1,)),
                        pltpu.SemaphoreType.DMA((1,))],
    )
    return pl.pallas_call(
        functools.partial(_dispatch_body, tm=tm, blk_rows=blk_rows, nb=nb),
        grid_spec=grid_spec,
        out_shape=jax.ShapeDtypeStruct((nb * blk_rows, LANES), U32),
        compiler_params=_cparams(("arbitrary",)),
        name="dispatch",
    )(clear_flag, dest, h2_slab)


def _tile_schedule(nblk, blk_start, nvb, n_tiles, nb):
    n_steps = nb * n_tiles
    steps_per_e = n_tiles * nblk
    e_end = jnp.cumsum(steps_per_e)
    e_start = e_end - steps_per_e
    total = e_end[-1]
    s = jnp.arange(n_steps, dtype=jnp.int32)
    live = s < total
    sl = jnp.minimum(s, total - 1)
    e = jnp.sum(e_end[None, :] <= sl[:, None], axis=1).astype(jnp.int32)
    r = sl - e_start[e]
    nb_e = jnp.maximum(nblk[e], 1)
    tile = r // nb_e
    b = r - tile * nb_e
    blk = blk_start[e] + b
    first = (live & (b == 0)).astype(jnp.int32)
    dead = jnp.maximum(s - total, 0)
    out_blk = jnp.where(live, blk, nvb + dead // n_tiles)
    out_tile = jnp.where(live, tile, dead % n_tiles)
    as_i32 = lambda a: a.astype(jnp.int32)
    return as_i32(e), as_i32(tile), as_i32(blk), first, as_i32(out_blk), as_i32(out_tile), as_i32(total)[None]


def _gateup_body(se_ref, st_ref, sb_ref, sf_ref, ob_ref, ot_ref, nl_ref, xs_ref, wg_ref, wu_ref, bg_ref, bu_ref,
                 act_ref, wgb_ref, wub_ref):
    s = pl.program_id(0)
    live = s < nl_ref[0]

    @pl.when(sf_ref[s] == 1)
    def _():
        wgb_ref[...] = wg_ref[0].astype(BF16)
        wub_ref[...] = wu_ref[0].astype(BF16)

    @pl.when(live)
    def _():
        xb = _slab_load(xs_ref, act_ref.shape[0]).astype(BF16)
        gate = jnp.dot(xb, wgb_ref[...], preferred_element_type=F32) + bg_ref[0]
        up = jnp.dot(xb, wub_ref[...], preferred_element_type=F32) + bu_ref[0]
        gate = jnp.minimum(gate, SWIGLU_LIMIT)
        up = jnp.clip(up, -SWIGLU_LIMIT, SWIGLU_LIMIT)
        glu = gate * jax.nn.sigmoid(gate * SWIGLU_ALPHA)
        act_ref[...] = ((up + 1.0) * glu).astype(act_ref.dtype)

    @pl.when(jnp.logical_not(live))
    def _():
        act_ref[...] = jnp.zeros_like(act_ref)


def _gateup(xs, sched, wgu, bgu, *, tm, tf):
    D = wgu.shape[1]
    R = xs.shape[0] // SLAB_PITCH
    d_ff = wgu.shape[2] // 2
    nj = d_ff // tf
    nb = R // tm
    assert d_ff % tf == 0 and R % tm == 0
    grid_spec = pltpu.PrefetchScalarGridSpec(
        num_scalar_prefetch=7,
        grid=(nb * nj,),
        in_specs=[
            pl.BlockSpec((tm * SLAB_PITCH, LANES), lambda s, se, st, sb, sf, ob, ot, nl: (sb[s], 0)),
            pl.BlockSpec((1, D, tf), lambda s, se, st, sb, sf, ob, ot, nl: (se[s], 0, st[s])),
            pl.BlockSpec((1, D, tf), lambda s, se, st, sb, sf, ob, ot, nl: (se[s], 0, nj + st[s])),
            pl.BlockSpec((1, 1, tf), lambda s, se, st, sb, sf, ob, ot, nl: (se[s], 0, st[s])),
            pl.BlockSpec((1, 1, tf), lambda s, se, st, sb, sf, ob, ot, nl: (se[s], 0, nj + st[s])),
        ],
        out_specs=pl.BlockSpec((tm, tf), lambda s, se, st, sb, sf, ob, ot, nl: (ob[s], ot[s])),
        scratch_shapes=[pltpu.VMEM((D, tf), BF16), pltpu.VMEM((D, tf), BF16)],
    )
    return pl.pallas_call(
        _gateup_body,
        grid_spec=grid_spec,
        out_shape=jax.ShapeDtypeStruct((R, d_ff), BF16),
        compiler_params=_cparams(("arbitrary",)),
        name="gateup",
    )(*sched, xs, wgu, wgu, bgu, bgu)


_LOAD, _COMPUTE, _ZERO, _IDLE = 0, 1, 2, 3


def _down_schedule(nblk, blk_start, nvb, n_load, nb):
    n_steps = nb + n_load * nblk.shape[0]
    steps_per_e = jnp.where(nblk > 0, nblk + n_load, 0)
    e_end = jnp.cumsum(steps_per_e)
    e_start = e_end - steps_per_e
    total = e_end[-1]
    s = jnp.arange(n_steps, dtype=jnp.int32)
    live = s < total
    sl = jnp.minimum(s, total - 1)
    e = jnp.sum(e_end[None, :] <= sl[:, None], axis=1).astype(jnp.int32)
    r = sl - e_start[e]
    loading = r < n_load
    chunk = jnp.where(loading, r, n_load - 1)
    blk = blk_start[e] + jnp.where(loading, 0, r - n_load)
    dead_blk = nvb + (s - total)
    kind = jnp.where(live, jnp.where(loading, _LOAD, _COMPUTE), jnp.where(dead_blk < nb, _ZERO, _IDLE))
    out_blk = jnp.where(live, blk, jnp.minimum(dead_blk, nb - 1))
    as_i32 = lambda a: a.astype(jnp.int32)
    return as_i32(kind), as_i32(e), as_i32(chunk), as_i32(blk), as_i32(out_blk)


def _down_body(kind_ref, se_ref, sc_ref, sb_ref, ob_ref, act_ref, wd_ref, bd_ref, y_ref, wdb_ref, *, chunk_rows):
    s = pl.program_id(0)
    kind = kind_ref[s]

    @pl.when(kind == _LOAD)
    def _():
        row0 = pl.multiple_of(sc_ref[s] * chunk_rows, chunk_rows)
        wdb_ref[pl.ds(row0, chunk_rows), :] = wd_ref[0].astype(BF16)

    @pl.when(kind == _COMPUTE)
    def _():
        y = jnp.dot(act_ref[...], wdb_ref[...], preferred_element_type=F32) + bd_ref[0]
        _slab_store(y_ref, y)

    @pl.when(kind == _ZERO)
    def _():
        y_ref[...] = jnp.zeros_like(y_ref)


def _down(act, sched, wd, bd, *, tm, n_load):
    R, d_ff = act.shape
    D = wd.shape[2]
    nb = R // tm
    chunk_rows = d_ff // n_load
    assert d_ff % n_load == 0
    grid_spec = pltpu.PrefetchScalarGridSpec(
        num_scalar_prefetch=5,
        grid=(sched[0].shape[0],),
        in_specs=[
            pl.BlockSpec((tm, d_ff), lambda s, kd, se, sc, sb, ob: (sb[s], 0)),
            pl.BlockSpec((1, chunk_rows, D), lambda s, kd, se, sc, sb, ob: (se[s], sc[s], 0)),
            pl.BlockSpec((1, 1, D), lambda s, kd, se, sc, sb, ob: (se[s], 0, 0)),
        ],
        out_specs=pl.BlockSpec((tm * SLAB_PITCH, LANES), lambda s, kd, se, sc, sb, ob: (ob[s], 0)),
        scratch_shapes=[pltpu.VMEM((d_ff, D), BF16)],
    )
    return pl.pallas_call(
        functools.partial(_down_body, chunk_rows=chunk_rows),
        grid_spec=grid_spec,
        out_shape=jax.ShapeDtypeStruct((R * SLAB_PITCH, LANES), U32),
        compiler_params=_cparams(("arbitrary",)),
        name="down",
    )(*sched, act, wd, bd)


def _combine_body(dest_cur_ref, dest_nxt_ref, x2_ref, route_ref, gf_ref, ys_hbm, o_ref, buf_ref, sem, *, tm):
    i = pl.program_id(0)
    slot = i % 2

    def start_rows(dest_ref, dst_slot):
        def issue(t, carry):
            for k in range(TOP_K):
                d = dest_ref[0, 0, t * TOP_K + k]
                pltpu.make_async_copy(ys_hbm.at[pl.ds(d * SLAB_PITCH, SLAB_ROWS)],
                                      buf_ref.at[dst_slot, k, pl.ds(t * SLAB_PITCH, SLAB_ROWS)],
                                      sem.at[dst_slot]).start(priority=k % DMA_QUEUES)
            return carry
        lax.fori_loop(0, tm, issue, 0, unroll=2)

    def wait_rows(dst_slot):
        for k in range(TOP_K):
            pltpu.make_async_copy(ys_hbm.at[pl.ds(0, tm * SLAB_ROWS)],
                                  buf_ref.at[dst_slot, k, pl.ds(0, tm * SLAB_ROWS)], sem.at[dst_slot]).wait()

    @pl.when(i == 0)
    def _():
        start_rows(dest_cur_ref, 0)

    @pl.when(i + 1 < pl.num_programs(0))
    def _():
        start_rows(dest_nxt_ref, 1 - slot)

    wait_rows(slot)
    route = route_ref[...]
    x3 = x2_ref[...]
    for k in range(TOP_K):
        x3 = x3 + route[:, TOP_K + k:TOP_K + k + 1] * _slab_load(buf_ref.at[slot, k], tm)
    ms = jnp.mean(x3 * x3, axis=-1, keepdims=True)
    o_ref[...] = x3 * lax.rsqrt(ms + EPS) * gf_ref[...]


def _combine(dest, x2, route, gf, ys_slab, *, tm):
    T, D = x2.shape
    n = T // tm
    assert T % tm == 0
    return pl.pallas_call(
        functools.partial(_combine_body, tm=tm),
        grid=(n,),
        in_specs=[
            pl.BlockSpec((1, 1, tm * TOP_K), lambda i: (i, 0, 0), memory_space=pltpu.SMEM),
            pl.BlockSpec((1, 1, tm * TOP_K), lambda i: (jnp.minimum(i + 1, n - 1), 0, 0), memory_space=pltpu.SMEM),
            pl.BlockSpec((tm, D), lambda i: (i, 0)),
            pl.BlockSpec((tm, LANES), lambda i: (i, 0)),
            pl.BlockSpec((1, D), lambda i: (0, 0)),
            pl.BlockSpec(memory_space=pl.ANY),
        ],
        out_specs=pl.BlockSpec((tm, D), lambda i: (i, 0)),
        out_shape=jax.ShapeDtypeStruct((T, D), F32),
        scratch_shapes=[pltpu.VMEM((2, TOP_K, tm * SLAB_PITCH, LANES), U32), pltpu.SemaphoreType.DMA((2,))],
        compiler_params=_cparams(("arbitrary",)),
        name="combine",
    )(dest, dest, x2, route, gf, ys_slab)


def _retention_tables(S, dk):
    pos = jnp.arange(S, dtype=F32)
    inv_freq = ROPE_BASE ** (-jnp.arange(0, dk, 2, dtype=F32) / dk)
    ang = pos[:, None] * inv_freq[None, :]
    cos, sin = jnp.cos(ang), jnp.sin(ang)
    cos2 = jnp.concatenate([cos, cos], axis=-1)
    sin2 = jnp.concatenate([-sin, sin], axis=-1)
    log_gamma = jnp.log1p(-jnp.exp2(-5.0 - jnp.arange(RET_HEADS, dtype=F32)))
    idx = jnp.arange(CHUNK, dtype=F32)
    diff = idx[:, None] - idx[None, :]
    decay = jnp.where(diff[None] >= 0, jnp.exp(log_gamma[:, None, None] * jnp.maximum(diff, 0.0)[None]), 0.0)
    xi = jnp.exp(log_gamma[:, None] * (idx[None, :] + 1.0))
    zeta = jnp.exp(log_gamma[:, None] * (CHUNK - 1.0 - idx[None, :]))
    cd = jnp.exp(log_gamma * CHUNK)
    xi_b = jnp.broadcast_to(xi[:, :, None], (RET_HEADS, CHUNK, LANES))
    zeta_b = jnp.broadcast_to(zeta[:, :, None], (RET_HEADS, CHUNK, dk))
    return cos2, sin2, decay, xi_b, zeta_b, cd


TM_INPROJ = 1024
TN_INPROJ = 1024
TM_MIXING = 256
TM_MERGE = 256
TM_EXPERT = 512
TF_EXPERT = 512
DOWN_LOAD_STEPS = 2
TM_COMBINE = 256


def kernel(x, norm1_g, w_in, gm_ln_g, gm_ln_b, gm_ws, gm_b, w_proj_a, w_proj_r, w_out, norm2_g, router_w, router_b,
           w_gate_up, b_gate_up, w_down, b_down, final_norm_g):
    B, S, D = x.shape
    assert B == 1 and norm1_g.shape[0] == 1, "single sequence, depth 1"
    gm_width = w_proj_a.shape[1]
    v_width = w_proj_r.shape[1]
    qk_width = (w_in.shape[2] - 2 * gm_width - 2 * v_width - 2 * D) // 2
    dk = qk_width // RET_HEADS
    assert dk == LANES and gm_ws.shape[2] == CHUNK and D == 2 * SLAB_ROWS * LANES
    G = gm_ws.shape[1]
    E = router_w.shape[2]
    d_ff = w_down.shape[2]
    x2d = x.reshape(S, D)

    cos2, sin2, decay, xi_b, zeta_b, cd = _retention_tables(S, dk)
    tm_p = min(TM_INPROJ, S)
    h = _prenorm(x2d, norm1_g[0][None], tm=tm_p)
    w_in_bf = w_in[0].astype(BF16)
    segments = (
        (gm_width, "gelu", ()),
        (gm_width, "gelu_layernorm", (gm_ln_g[0][None], gm_ln_b[0][None])),
        (qk_width, "rope", (cos2, sin2)),
        (qk_width, "rope_scaled", (cos2, sin2)),
        (v_width, "identity", ()),
        (v_width, "silu", ()),
        (2 * D, "sigmoid", ()),
    )
    outs, col0 = [], 0
    for width, kind, extras in segments:
        outs.append(_segment(h, w_in_bf, col0, width, kind, extras, tm=tm_p, tn=TN_INPROJ))
        col0 += width
    a_u, a_vn, r_q, r_k, r_v, r_sg, gates = outs

    causal = jnp.tril(jnp.ones((CHUNK, CHUNK), dtype=bool))
    wm = jnp.where(causal[None], gm_ws[0], 0.0).astype(BF16)
    bs_b = jnp.broadcast_to(gm_b[0][:, :, None], (G, CHUNK, GM_GROUP_DIM))
    ga, gr = _mixing(a_u, a_vn, r_q, r_k, r_v, r_sg, wm, bs_b, decay, xi_b, zeta_b, cd, tm=min(TM_MIXING, S))

    rw = jnp.pad(router_w[0], ((0, 0), (0, LANES - E)))
    rw_hi = rw.astype(BF16)
    rw_lo = (rw - rw_hi.astype(F32)).astype(BF16)
    rb = jnp.pad(router_b[0], (0, LANES - E))[None]
    x2, h2, route, cnt = _merge(x2d, ga, gr, gates, w_proj_a[0].astype(BF16), w_proj_r[0].astype(BF16),
                                w_out[0].astype(BF16), norm2_g[0][None], rw_hi, rw_lo, rb, tm=min(TM_MERGE, S))

    tm_e = TM_EXPERT
    eidx = route[:, 0:TOP_K].astype(jnp.int32)
    rank = route[:, 2 * TOP_K:3 * TOP_K].astype(jnp.int32)
    counts = cnt[0, :E].astype(jnp.int32)
    nblk = (counts + tm_e - 1) // tm_e
    blk_end = jnp.cumsum(nblk)
    blk_start = blk_end - nblk
    dest = (blk_start * tm_e)[eidx] + rank
    nb = (S * TOP_K + tm_e - 1) // tm_e + E
    nvb = blk_end[-1].astype(jnp.int32)
    blk_ids = jnp.arange(nb, dtype=jnp.int32)
    is_last = jnp.any((blk_ids[:, None] == blk_end[None, :] - 1) & (nblk[None, :] > 0), axis=1)
    clear_flag = (is_last | (blk_ids >= nvb)).astype(jnp.int32)

    tm_c = min(TM_COMBINE, S)
    dest_blocks = dest.reshape(S // tm_c, 1, tm_c * TOP_K)
    xs = _dispatch(h2, dest_blocks, clear_flag, tm=tm_c, tm_e=tm_e)
    tf = min(TF_EXPERT, d_ff)
    act = _gateup(xs, _tile_schedule(nblk, blk_start, nvb, d_ff // tf, nb), w_gate_up[0], b_gate_up[0][:, None, :],
                  tm=tm_e, tf=tf)
    ys = _down(act, _down_schedule(nblk, blk_start, nvb, DOWN_LOAD_STEPS, nb), w_down[0], b_down[0][:, None, :],
               tm=tm_e, n_load=DOWN_LOAD_STEPS)

    out = _combine(dest_blocks, x2, route, final_norm_g[None], ys, tm=tm_c)
    return out.reshape(B, S, D)
```

```python
import functools

import jax
import jax.numpy as jnp
from jax import lax
from jax.experimental import pallas as pl
from jax.experimental.pallas import tpu as pltpu

F32 = jnp.float32
BF16 = jnp.bfloat16
U32 = jnp.uint32

CHUNK = 128
GM_GROUP_DIM = 128
RET_HEADS = 8
N_EXPERTS = 32
TOP_K = 4
SWIGLU_LIMIT = 7.0
SWIGLU_ALPHA = 1.702
ROPE_BASE = 10000.0
EPS = 1e-6
LANES = 128

VMEM_LIMIT_BYTES = 56 * 1024 * 1024


def _cparams(sem):
    return pltpu.CompilerParams(dimension_semantics=sem, vmem_limit_bytes=VMEM_LIMIT_BYTES)


SLAB_ROWS = 8
SLAB_PITCH = 9
DMA_QUEUES = 2


def _slab_store(ref, value):
    rows, width = value.shape
    half = width // 2
    assert half == SLAB_ROWS * LANES
    lo = lax.bitcast_convert_type(value[:, :half].astype(BF16).astype(F32), U32)
    hi = lax.bitcast_convert_type(value[:, half:].astype(BF16).astype(F32), U32)
    words = (lo >> 16) | hi
    for c in range(SLAB_ROWS):
        ref[pl.ds(c, rows, stride=SLAB_PITCH), :] = words[:, c * LANES:(c + 1) * LANES]
    for c in range(SLAB_ROWS, SLAB_PITCH):
        ref[pl.ds(c, rows, stride=SLAB_PITCH), :] = jnp.zeros((rows, LANES), U32)


def _slab_load(ref, rows):
    words = jnp.concatenate([ref[pl.ds(c, rows, stride=SLAB_PITCH), :] for c in range(SLAB_ROWS)], axis=1)
    lo = lax.bitcast_convert_type(words << 16, F32)
    hi = lax.bitcast_convert_type(words & jnp.uint32(0xFFFF0000), F32)
    return jnp.concatenate([lo, hi], axis=1)


def _gelu_exact(a):
    return 0.5 * a * (1.0 + lax.erf(a * (2.0 ** -0.5)))


def _prenorm_body(x_ref, g_ref, h_ref):
    xf = x_ref[...]
    ms = jnp.mean(xf * xf, axis=-1, keepdims=True)
    h_ref[...] = (xf * lax.rsqrt(ms + EPS) * g_ref[...]).astype(h_ref.dtype)


def _prenorm(x2d, g, *, tm):
    S, D = x2d.shape
    return pl.pallas_call(
        _prenorm_body,
        grid=(S // tm,),
        in_specs=[pl.BlockSpec((tm, D), lambda i: (i, 0)), pl.BlockSpec((1, D), lambda i: (0, 0))],
        out_specs=pl.BlockSpec((tm, D), lambda i: (i, 0)),
        out_shape=jax.ShapeDtypeStruct((S, D), BF16),
        compiler_params=_cparams(("arbitrary",)),
        name="prenorm",
    )(x2d, g)


def _rope_heads(a, cos2, sin2):
    outs = []
    for hd in range(a.shape[1] // LANES):
        ah = a[:, hd * LANES:(hd + 1) * LANES]
        outs.append(ah * cos2 + pltpu.roll(ah, LANES // 2, axis=1) * sin2)
    return jnp.concatenate(outs, axis=1)


def _segment_body(h_ref, w_ref, *refs, kind, k_scale):
    o_ref = refs[-1]
    acc = jnp.dot(h_ref[...], w_ref[...], preferred_element_type=F32)
    if kind == "gelu":
        out = _gelu_exact(acc)
    elif kind == "gelu_layernorm":
        lng_ref, lnb_ref = refs[0], refs[1]
        vf = _gelu_exact(acc)
        mu = jnp.mean(vf, axis=-1, keepdims=True)
        vc = vf - mu
        var = jnp.mean(vc * vc, axis=-1, keepdims=True)
        out = vc * lax.rsqrt(var + EPS) * lng_ref[...] + lnb_ref[...]
    elif kind == "rope":
        out = _rope_heads(acc, refs[0][...], refs[1][...])
    elif kind == "rope_scaled":
        out = _rope_heads(acc, refs[0][...], refs[1][...]) * k_scale
    elif kind == "identity":
        out = acc
    elif kind == "silu":
        out = acc * jax.nn.sigmoid(acc)
    elif kind == "sigmoid":
        out = jax.nn.sigmoid(acc)
    else:
        raise ValueError(kind)
    o_ref[...] = out.astype(o_ref.dtype)


def _segment(h, w_in_bf, col0, width, kind, extras=(), *, tm, tn):
    S, D = h.shape
    assert width % tn == 0 and col0 % tn == 0 and S % tm == 0
    blk0 = col0 // tn
    if kind == "gelu_layernorm":
        assert width == tn
        extra_specs = [pl.BlockSpec((1, tn), lambda j, i: (0, 0))] * 2
    elif kind in ("rope", "rope_scaled"):
        extra_specs = [pl.BlockSpec((tm, LANES), lambda j, i: (i, 0))] * 2
    else:
        extra_specs = []
    return pl.pallas_call(
        functools.partial(_segment_body, kind=kind, k_scale=float(LANES) ** -0.5),
        grid=(width // tn, S // tm),
        in_specs=[
            pl.BlockSpec((tm, D), lambda j, i: (i, 0)),
            pl.BlockSpec((D, tn), lambda j, i: (0, blk0 + j)),
        ] + extra_specs,
        out_specs=pl.BlockSpec((tm, tn), lambda j, i: (i, j)),
        out_shape=jax.ShapeDtypeStruct((S, width), BF16),
        compiler_params=_cparams(("arbitrary", "arbitrary")),
        name="inproj_" + kind,
    )(h, w_in_bf, *extras)


def _mixing_body(cd_ref, u_ref, vn_ref, q_ref, k_ref, v_ref, sg_ref, wm_ref, bs_ref, decay_ref, xi_ref, zeta_ref,
                 ga_ref, gr_ref, state_ref, *, n_chunks, dk, dv):
    @pl.when(pl.program_id(0) == 0)
    def _():
        state_ref[...] = jnp.zeros_like(state_ref)

    def chunk(c, carry):
        rows = pl.ds(pl.multiple_of(c * CHUNK, CHUNK), CHUNK)
        for g in range(wm_ref.shape[0]):
            cols = slice(g * GM_GROUP_DIM, (g + 1) * GM_GROUP_DIM)
            mixed = jnp.dot(wm_ref[g], vn_ref[rows, cols], preferred_element_type=F32) + bs_ref[g]
            ga_ref[rows, cols] = (u_ref[rows, cols].astype(F32) * mixed).astype(ga_ref.dtype)
        for hd in range(RET_HEADS):
            qc = q_ref[rows, hd * dk:(hd + 1) * dk]
            kc = k_ref[rows, hd * dk:(hd + 1) * dk]
            vc = v_ref[rows, hd * dv:(hd + 1) * dv]
            st = state_ref[hd]
            scores = lax.dot_general(qc, kc, (((1,), (1,)), ((), ())), preferred_element_type=F32) * decay_ref[hd]
            inner = jnp.dot(scores.astype(BF16), vc, preferred_element_type=F32)
            cross = jnp.dot(qc, st.astype(BF16), preferred_element_type=F32)
            xi = xi_ref[hd]
            o = inner + cross * jnp.concatenate([xi] * (dv // LANES), axis=1)
            kz = (kc.astype(F32) * zeta_ref[hd]).astype(BF16)
            kv = lax.dot_general(kz, vc, (((0,), (0,)), ((), ())), preferred_element_type=F32)
            state_ref[hd] = st * cd_ref[hd] + kv
            mu = jnp.mean(o, axis=-1, keepdims=True)
            oc = o - mu
            var = jnp.mean(oc * oc, axis=-1, keepdims=True)
            on = oc * lax.rsqrt(var + EPS)
            gr_ref[rows, hd * dv:(hd + 1) * dv] = (on * sg_ref[rows, hd * dv:(hd + 1) * dv].astype(F32)).astype(
                gr_ref.dtype)
        return carry

    lax.fori_loop(0, n_chunks, chunk, 0)


def _mixing(u, vn, q, k, v, sg, wm, bs_b, decay, xi_b, zeta_b, cd, *, tm):
    S, gm_width = u.shape
    qk_width, v_width = q.shape[1], v.shape[1]
    assert S % tm == 0 and tm % CHUNK == 0
    dk, dv = qk_width // RET_HEADS, v_width // RET_HEADS
    G = wm.shape[0]
    body = functools.partial(_mixing_body, n_chunks=tm // CHUNK, dk=dk, dv=dv)
    const3 = lambda i: (0, 0, 0)
    rows = lambda i: (i, 0)
    return pl.pallas_call(
        body,
        grid=(S // tm,),
        in_specs=[
            pl.BlockSpec(memory_space=pltpu.SMEM),
            pl.BlockSpec((tm, gm_width), rows),
            pl.BlockSpec((tm, gm_width), rows),
            pl.BlockSpec((tm, qk_width), rows),
            pl.BlockSpec((tm, qk_width), rows),
            pl.BlockSpec((tm, v_width), rows),
            pl.BlockSpec((tm, v_width), rows),
            pl.BlockSpec((G, CHUNK, CHUNK), const3),
            pl.BlockSpec((G, CHUNK, GM_GROUP_DIM), const3),
            pl.BlockSpec((RET_HEADS, CHUNK, CHUNK), const3),
            pl.BlockSpec((RET_HEADS, CHUNK, LANES), const3),
            pl.BlockSpec((RET_HEADS, CHUNK, dk), const3),
        ],
        out_specs=[
            pl.BlockSpec((tm, gm_width), lambda i: (i, 0)),
            pl.BlockSpec((tm, v_width), lambda i: (i, 0)),
        ],
        out_shape=[
            jax.ShapeDtypeStruct((S, gm_width), BF16),
            jax.ShapeDtypeStruct((S, v_width), BF16),
        ],
        scratch_shapes=[pltpu.VMEM((RET_HEADS, dk, dv), F32)],
        compiler_params=_cparams(("arbitrary",)),
        name="mixing",
    )(cd, u, vn, q, k, v, sg, wm, bs_b, decay, xi_b, zeta_b)


def _merge_body(x_ref, ga_ref, gr_ref, sa_ref, sr_ref, wa_ref, wr_ref, wo_ref, g2_ref, rwh_ref, rwl_ref, rb_ref,
                x2_ref, h2_ref, route_ref, cnt_ref, run_ref):
    i = pl.program_id(0)
    tm = x_ref.shape[0]

    @pl.when(i == 0)
    def _():
        run_ref[...] = jnp.zeros_like(run_ref)

    y_a = jnp.dot(ga_ref[...], wa_ref[...], preferred_element_type=F32)
    y_r = jnp.dot(gr_ref[...], wr_ref[...], preferred_element_type=F32)
    merged = sa_ref[...].astype(F32) * y_a + sr_ref[...].astype(F32) * y_r
    x2 = x_ref[...] + jnp.dot(merged.astype(BF16), wo_ref[...], preferred_element_type=F32)
    x2_ref[...] = x2
    ms = jnp.mean(x2 * x2, axis=-1, keepdims=True)
    h2 = x2 * lax.rsqrt(ms + EPS) * g2_ref[...]
    _slab_store(h2_ref, h2)

    h_hi = h2.astype(BF16)
    h_lo = (h2 - h_hi.astype(F32)).astype(BF16)
    logits = (jnp.dot(h_hi, rwh_ref[...], preferred_element_type=F32)
              + jnp.dot(h_hi, rwl_ref[...], preferred_element_type=F32)
              + jnp.dot(h_lo, rwh_ref[...], preferred_element_type=F32)) + rb_ref[...]
    lane = lax.broadcasted_iota(jnp.int32, (tm, LANES), 1)
    neg_inf = jnp.float32(-jnp.inf)
    work = jnp.where(lane < N_EXPERTS, logits, neg_inf)

    vals, idxs, sels = [], [], []
    for _ in range(TOP_K):
        m = jnp.max(work, axis=-1, keepdims=True)
        idx = jnp.min(jnp.where(work == m, lane, LANES), axis=-1, keepdims=True)
        sel = lane == idx
        vals.append(m)
        idxs.append(idx)
        sels.append(sel)
        work = jnp.where(sel, neg_inf, work)
    exps = [jnp.exp(v - vals[0]) for v in vals]
    denom = exps[0] + exps[1] + exps[2] + exps[3]
    gates = [e / denom for e in exps]

    onehot = jnp.zeros((tm, LANES), F32)
    for sel in sels:
        onehot = onehot + sel.astype(F32)
    r_iota = lax.broadcasted_iota(jnp.int32, (tm, tm), 0)
    c_iota = lax.broadcasted_iota(jnp.int32, (tm, tm), 1)
    strict_lower = (c_iota < r_iota).astype(BF16)
    before = jnp.dot(strict_lower, onehot.astype(BF16), preferred_element_type=F32) + run_ref[0:1, :]
    ranks = [jnp.sum(jnp.where(sel, before, 0.0), axis=-1, keepdims=True) for sel in sels]
    run_new = run_ref[0:1, :] + jnp.sum(onehot, axis=0, keepdims=True)
    run_ref[...] = jnp.broadcast_to(run_new, run_ref.shape)
    cnt_ref[...] = jnp.broadcast_to(run_new, cnt_ref.shape)

    route = jnp.zeros((tm, LANES), F32)
    for k in range(TOP_K):
        route = jnp.where(lane == k, idxs[k].astype(F32), route)
        route = jnp.where(lane == TOP_K + k, gates[k], route)
        route = jnp.where(lane == 2 * TOP_K + k, ranks[k], route)
    route_ref[...] = route


def _merge(x2d, ga, gr, gates, wa, wr, wo, g2, rw_hi, rw_lo, rb, *, tm):
    S, D = x2d.shape
    assert S % tm == 0
    const = lambda i: (0, 0)
    resident = functools.partial(pl.BlockSpec, index_map=const, pipeline_mode=pl.Buffered(1))
    return pl.pallas_call(
        _merge_body,
        grid=(S // tm,),
        in_specs=[
            pl.BlockSpec((tm, D), lambda i: (i, 0)),
            pl.BlockSpec((tm, ga.shape[1]), lambda i: (i, 0)),
            pl.BlockSpec((tm, gr.shape[1]), lambda i: (i, 0)),
            pl.BlockSpec((tm, D), lambda i: (i, 0)),
            pl.BlockSpec((tm, D), lambda i: (i, 1)),
            resident(wa.shape),
            resident(wr.shape),
            resident(wo.shape),
            pl.BlockSpec((1, D), const),
            resident(rw_hi.shape),
            resident(rw_lo.shape),
            pl.BlockSpec((1, LANES), const),
        ],
        out_specs=[
            pl.BlockSpec((tm, D), lambda i: (i, 0)),
            pl.BlockSpec((tm * SLAB_PITCH, LANES), lambda i: (i, 0)),
            pl.BlockSpec((tm, LANES), lambda i: (i, 0)),
            pl.BlockSpec((8, LANES), const),
        ],
        out_shape=[
            jax.ShapeDtypeStruct((S, D), F32),
            jax.ShapeDtypeStruct((S * SLAB_PITCH, LANES), U32),
            jax.ShapeDtypeStruct((S, LANES), F32),
            jax.ShapeDtypeStruct((8, LANES), F32),
        ],
        scratch_shapes=[pltpu.VMEM((8, LANES), F32)],
        compiler_params=_cparams(("arbitrary",)),
        name="merge",
    )(x2d, ga, gr, gates, gates, wa, wr, wo, g2, rw_hi, rw_lo, rb)


def _dispatch_body(clear_ref, dest_ref, h2_ref, xs_hbm, zeros_ref, sem, clear_sem, *, tm, blk_rows, nb):
    i = pl.program_id(0)

    def clear_copy(b):
        return pltpu.make_async_copy(zeros_ref, xs_hbm.at[pl.ds(pl.multiple_of(b * blk_rows, blk_rows), blk_rows)],
                                     clear_sem.at[0])

    @pl.when(i == 0)
    def _():
        zeros_ref[...] = jnp.zeros_like(zeros_ref)

        def start(b, carry):
            @pl.when(clear_ref[b] == 1)
            def _():
                clear_copy(b).start()
            return carry

        def drain(b, carry):
            @pl.when(clear_ref[b] == 1)
            def _():
                clear_copy(b).wait()
            return carry

        lax.fori_loop(0, nb, start, 0)
        lax.fori_loop(0, nb, drain, 0)

    def issue(t, carry):
        for k in range(TOP_K):
            d = dest_ref[0, 0, t * TOP_K + k]
            pltpu.make_async_copy(h2_ref.at[pl.ds(t * SLAB_PITCH, SLAB_PITCH)],
                                  xs_hbm.at[pl.ds(d * SLAB_PITCH, SLAB_PITCH)],
                                  sem.at[0]).start(priority=k % DMA_QUEUES)
        return carry

    lax.fori_loop(0, tm, issue, 0, unroll=2)
    for k in range(TOP_K):
        pltpu.make_async_copy(h2_ref, xs_hbm.at[pl.ds(0, tm * SLAB_PITCH)], sem.at[0]).wait()


def _dispatch(h2_slab, dest, clear_flag, *, tm, tm_e):
    nb = clear_flag.shape[0]
    blk_rows = tm_e * SLAB_PITCH
    n = h2_slab.shape[0] // (tm * SLAB_PITCH)
    grid_spec = pltpu.PrefetchScalarGridSpec(
        num_scalar_prefetch=1,
        grid=(n,),
        in_specs=[
            pl.BlockSpec((1, 1, tm * TOP_K), lambda i, cf: (i, 0, 0), memory_space=pltpu.SMEM),
            pl.BlockSpec((tm * SLAB_PITCH, LANES), lambda i, cf: (i, 0)),
        ],
        out_specs=pl.BlockSpec(memory_space=pl.ANY),
        scratch_shapes=[pltpu.VMEM((blk_rows, LANES), U32), pltpu.SemaphoreType.DMA((1,)),
                        pltpu.SemaphoreType.DMA((1,))],
    )
    return pl.pallas_call(
        functools.partial(_dispatch_body, tm=tm, blk_rows=blk_rows, nb=nb),
        grid_spec=grid_spec,
        out_shape=jax.ShapeDtypeStruct((nb * blk_rows, LANES), U32),
        compiler_params=_cparams(("arbitrary",)),
        name="dispatch",
    )(clear_flag, dest, h2_slab)


def _tile_schedule(nblk, blk_start, nvb, n_tiles, nb):
    n_steps = nb * n_tiles
    steps_per_e = n_tiles * nblk
    e_end = jnp.cumsum(steps_per_e)
    e_start = e_end - steps_per_e
    total = e_end[-1]
    s = jnp.arange(n_steps, dtype=jnp.int32)
    live = s < total
    sl = jnp.minimum(s, total - 1)
    e = jnp.sum(e_end[None, :] <= sl[:, None], axis=1).astype(jnp.int32)
    r = sl - e_start[e]
    nb_e = jnp.maximum(nblk[e], 1)
    tile = r // nb_e
    b = r - tile * nb_e
    blk = blk_start[e] + b
    first = (live & (b == 0)).astype(jnp.int32)
    dead = jnp.maximum(s - total, 0)
    out_blk = jnp.where(live, blk, nvb + dead // n_tiles)
    out_tile = jnp.where(live, tile, dead % n_tiles)
    nxt = jnp.minimum(s - b + nb_e, n_steps - 1)
    has_next = (live & (s - b + nb_e < total)).astype(jnp.int32)
    as_i32 = lambda a: a.astype(jnp.int32)
    return (as_i32(e), as_i32(tile), as_i32(blk), first, as_i32(out_blk), as_i32(out_tile), as_i32(total)[None],
            as_i32(e[nxt]), as_i32(tile[nxt]), has_next)


def _gateup_body(se_ref, st_ref, sb_ref, sf_ref, ob_ref, ot_ref, nl_ref, ne_ref, nt_ref, hn_ref,
                 xs_ref, wgu_hbm, bg_ref, bu_ref, act_ref, stage_ref, wgb_ref, wub_ref, sem, *, tf, d_ff):
    s = pl.program_id(0)
    live = s < nl_ref[0]
    D = stage_ref.shape[1]

    def tile_copies(e, t):
        return [pltpu.make_async_copy(
            wgu_hbm.at[e, pl.ds(0, D), pl.ds(pl.multiple_of(half * d_ff + t * tf, tf), tf)],
            stage_ref.at[half], sem.at[half]) for half in range(2)]

    @pl.when(sf_ref[s] == 1)
    def _():
        @pl.when(s == 0)
        def _():
            for c in tile_copies(se_ref[0], st_ref[0]):
                c.start()

        for c in tile_copies(se_ref[s], st_ref[s]):
            c.wait()
        wgb_ref[...] = stage_ref[0].astype(BF16)
        wub_ref[...] = stage_ref[1].astype(BF16)

        @pl.when(hn_ref[s] == 1)
        def _():
            for c in tile_copies(ne_ref[s], nt_ref[s]):
                c.start()

    @pl.when(live)
    def _():
        xb = _slab_load(xs_ref, act_ref.shape[0]).astype(BF16)
        gate = jnp.dot(xb, wgb_ref[...], preferred_element_type=F32) + bg_ref[0]
        up = jnp.dot(xb, wub_ref[...], preferred_element_type=F32) + bu_ref[0]
        gate = jnp.minimum(gate, SWIGLU_LIMIT)
        up = jnp.clip(up, -SWIGLU_LIMIT, SWIGLU_LIMIT)
        glu = gate * jax.nn.sigmoid(gate * SWIGLU_ALPHA)
        act_ref[...] = ((up + 1.0) * glu).astype(act_ref.dtype)

    @pl.when(jnp.logical_not(live))
    def _():
        act_ref[...] = jnp.zeros_like(act_ref)


def _gateup(xs, sched, wgu, bgu, *, tm, tf):
    D = wgu.shape[1]
    R = xs.shape[0] // SLAB_PITCH
    d_ff = wgu.shape[2] // 2
    nj = d_ff // tf
    nb = R // tm
    assert d_ff % tf == 0 and R % tm == 0
    grid_spec = pltpu.PrefetchScalarGridSpec(
        num_scalar_prefetch=10,
        grid=(nb * nj,),
        in_specs=[
            pl.BlockSpec((tm * SLAB_PITCH, LANES), lambda s, se, st, sb, *_: (sb[s], 0)),
            pl.BlockSpec(memory_space=pl.ANY),
            pl.BlockSpec((1, 1, tf), lambda s, se, st, *_: (se[s], 0, st[s])),
            pl.BlockSpec((1, 1, tf), lambda s, se, st, *_: (se[s], 0, nj + st[s])),
        ],
        out_specs=pl.BlockSpec((tm, tf), lambda s, se, st, sb, sf, ob, ot, *_: (ob[s], ot[s])),
        scratch_shapes=[pltpu.VMEM((2, D, tf), F32), pltpu.VMEM((D, tf), BF16), pltpu.VMEM((D, tf), BF16),
                        pltpu.SemaphoreType.DMA((2,))],
    )
    return pl.pallas_call(
        functools.partial(_gateup_body, tf=tf, d_ff=d_ff),
        grid_spec=grid_spec,
        out_shape=jax.ShapeDtypeStruct((R, d_ff), BF16),
        compiler_params=_cparams(("arbitrary",)),
        name="gateup",
    )(*sched, xs, wgu, bgu, bgu)


_LOAD, _COMPUTE, _ZERO, _IDLE = 0, 1, 2, 3


def _down_schedule(nblk, blk_start, nvb, n_load, nb):
    n_steps = nb + n_load * nblk.shape[0]
    steps_per_e = jnp.where(nblk > 0, nblk + n_load, 0)
    e_end = jnp.cumsum(steps_per_e)
    e_start = e_end - steps_per_e
    total = e_end[-1]
    s = jnp.arange(n_steps, dtype=jnp.int32)
    live = s < total
    sl = jnp.minimum(s, total - 1)
    e = jnp.sum(e_end[None, :] <= sl[:, None], axis=1).astype(jnp.int32)
    r = sl - e_start[e]
    loading = r < n_load
    chunk = jnp.where(loading, r, n_load - 1)
    blk = blk_start[e] + jnp.where(loading, 0, r - n_load)
    dead_blk = nvb + (s - total)
    kind = jnp.where(live, jnp.where(loading, _LOAD, _COMPUTE), jnp.where(dead_blk < nb, _ZERO, _IDLE))
    out_blk = jnp.where(live, blk, jnp.minimum(dead_blk, nb - 1))
    as_i32 = lambda a: a.astype(jnp.int32)
    return as_i32(kind), as_i32(e), as_i32(chunk), as_i32(blk), as_i32(out_blk)


def _down_body(kind_ref, se_ref, sc_ref, sb_ref, ob_ref, act_ref, wd_ref, bd_ref, y_ref, wdb_ref, *, chunk_rows):
    s = pl.program_id(0)
    kind = kind_ref[s]

    @pl.when(kind == _LOAD)
    def _():
        row0 = pl.multiple_of(sc_ref[s] * chunk_rows, chunk_rows)
        wdb_ref[pl.ds(row0, chunk_rows), :] = wd_ref[0].astype(BF16)

    @pl.when(kind == _COMPUTE)
    def _():
        y = jnp.dot(act_ref[...], wdb_ref[...], preferred_element_type=F32) + bd_ref[0]
        _slab_store(y_ref, y)

    @pl.when(kind == _ZERO)
    def _():
        y_ref[...] = jnp.zeros_like(y_ref)


def _down(act, sched, wd, bd, *, tm, n_load):
    R, d_ff = act.shape
    D = wd.shape[2]
    nb = R // tm
    chunk_rows = d_ff // n_load
    assert d_ff % n_load == 0
    grid_spec = pltpu.PrefetchScalarGridSpec(
        num_scalar_prefetch=5,
        grid=(sched[0].shape[0],),
        in_specs=[
            pl.BlockSpec((tm, d_ff), lambda s, kd, se, sc, sb, ob: (sb[s], 0)),
            pl.BlockSpec((1, chunk_rows, D), lambda s, kd, se, sc, sb, ob: (se[s], sc[s], 0)),
            pl.BlockSpec((1, 1, D), lambda s, kd, se, sc, sb, ob: (se[s], 0, 0)),
        ],
        out_specs=pl.BlockSpec((tm * SLAB_PITCH, LANES), lambda s, kd, se, sc, sb, ob: (ob[s], 0)),
        scratch_shapes=[pltpu.VMEM((d_ff, D), BF16)],
    )
    return pl.pallas_call(
        functools.partial(_down_body, chunk_rows=chunk_rows),
        grid_spec=grid_spec,
        out_shape=jax.ShapeDtypeStruct((R * SLAB_PITCH, LANES), U32),
        compiler_params=_cparams(("arbitrary",)),
        name="down",
    )(*sched, act, wd, bd)


def _combine_body(dest_cur_ref, dest_nxt_ref, x2_ref, route_ref, gf_ref, ys_hbm, o_ref, buf_ref, sem, *, tm):
    i = pl.program_id(0)
    slot = i % 2

    def start_rows(dest_ref, dst_slot):
        def issue(t, carry):
            for k in range(TOP_K):
                d = dest_ref[0, 0, t * TOP_K + k]
                pltpu.make_async_copy(ys_hbm.at[pl.ds(d * SLAB_PITCH, SLAB_ROWS)],
                                      buf_ref.at[dst_slot, k, pl.ds(t * SLAB_PITCH, SLAB_ROWS)],
                                      sem.at[dst_slot]).start(priority=k % DMA_QUEUES)
            return carry
        lax.fori_loop(0, tm, issue, 0, unroll=2)

    def wait_rows(dst_slot):
        for k in range(TOP_K):
            pltpu.make_async_copy(ys_hbm.at[pl.ds(0, tm * SLAB_ROWS)],
                                  buf_ref.at[dst_slot, k, pl.ds(0, tm * SLAB_ROWS)], sem.at[dst_slot]).wait()

    @pl.when(i == 0)
    def _():
        start_rows(dest_cur_ref, 0)

    @pl.when(i + 1 < pl.num_programs(0))
    def _():
        start_rows(dest_nxt_ref, 1 - slot)

    wait_rows(slot)
    route = route_ref[...]
    x3 = x2_ref[...]
    for k in range(TOP_K):
        x3 = x3 + route[:, TOP_K + k:TOP_K + k + 1] * _slab_load(buf_ref.at[slot, k], tm)
    ms = jnp.mean(x3 * x3, axis=-1, keepdims=True)
    o_ref[...] = x3 * lax.rsqrt(ms + EPS) * gf_ref[...]


def _combine(dest, x2, route, gf, ys_slab, *, tm):
    T, D = x2.shape
    n = T // tm
    assert T % tm == 0
    return pl.pallas_call(
        functools.partial(_combine_body, tm=tm),
        grid=(n,),
        in_specs=[
            pl.BlockSpec((1, 1, tm * TOP_K), lambda i: (i, 0, 0), memory_space=pltpu.SMEM),
            pl.BlockSpec((1, 1, tm * TOP_K), lambda i: (jnp.minimum(i + 1, n - 1), 0, 0), memory_space=pltpu.SMEM),
            pl.BlockSpec((tm, D), lambda i: (i, 0)),
            pl.BlockSpec((tm, LANES), lambda i: (i, 0)),
            pl.BlockSpec((1, D), lambda i: (0, 0)),
            pl.BlockSpec(memory_space=pl.ANY),
        ],
        out_specs=pl.BlockSpec((tm, D), lambda i: (i, 0)),
        out_shape=jax.ShapeDtypeStruct((T, D), F32),
        scratch_shapes=[pltpu.VMEM((2, TOP_K, tm * SLAB_PITCH, LANES), U32), pltpu.SemaphoreType.DMA((2,))],
        compiler_params=_cparams(("arbitrary",)),
        name="combine",
    )(dest, dest, x2, route, gf, ys_slab)


def _retention_tables(S, dk):
    pos = jnp.arange(S, dtype=F32)
    inv_freq = ROPE_BASE ** (-jnp.arange(0, dk, 2, dtype=F32) / dk)
    ang = pos[:, None] * inv_freq[None, :]
    cos, sin = jnp.cos(ang), jnp.sin(ang)
    cos2 = jnp.concatenate([cos, cos], axis=-1)
    sin2 = jnp.concatenate([-sin, sin], axis=-1)
    log_gamma = jnp.log1p(-jnp.exp2(-5.0 - jnp.arange(RET_HEADS, dtype=F32)))
    idx = jnp.arange(CHUNK, dtype=F32)
    diff = idx[:, None] - idx[None, :]
    decay = jnp.where(diff[None] >= 0, jnp.exp(log_gamma[:, None, None] * jnp.maximum(diff, 0.0)[None]), 0.0)
    xi = jnp.exp(log_gamma[:, None] * (idx[None, :] + 1.0))
    zeta = jnp.exp(log_gamma[:, None] * (CHUNK - 1.0 - idx[None, :]))
    cd = jnp.exp(log_gamma * CHUNK)
    xi_b = jnp.broadcast_to(xi[:, :, None], (RET_HEADS, CHUNK, LANES))
    zeta_b = jnp.broadcast_to(zeta[:, :, None], (RET_HEADS, CHUNK, dk))
    return cos2, sin2, decay, xi_b, zeta_b, cd


TM_INPROJ = 1024
TN_INPROJ = 1024
TM_MIXING = 256
TM_MERGE = 256
TM_EXPERT = 512
TF_EXPERT = 512
DOWN_LOAD_STEPS = 2
TM_COMBINE = 256


def kernel(x, norm1_g, w_in, gm_ln_g, gm_ln_b, gm_ws, gm_b, w_proj_a, w_proj_r, w_out, norm2_g, router_w, router_b,
           w_gate_up, b_gate_up, w_down, b_down, final_norm_g):
    B, S, D = x.shape
    assert B == 1 and norm1_g.shape[0] == 1, "single sequence, depth 1"
    gm_width = w_proj_a.shape[1]
    v_width = w_proj_r.shape[1]
    qk_width = (w_in.shape[2] - 2 * gm_width - 2 * v_width - 2 * D) // 2
    dk = qk_width // RET_HEADS
    assert dk == LANES and gm_ws.shape[2] == CHUNK and D == 2 * SLAB_ROWS * LANES
    G = gm_ws.shape[1]
    E = router_w.shape[2]
    d_ff = w_down.shape[2]
    x2d = x.reshape(S, D)

    cos2, sin2, decay, xi_b, zeta_b, cd = _retention_tables(S, dk)
    tm_p = min(TM_INPROJ, S)
    h = _prenorm(x2d, norm1_g[0][None], tm=tm_p)
    w_in_bf = w_in[0].astype(BF16)
    segments = (
        (gm_width, "gelu", ()),
        (gm_width, "gelu_layernorm", (gm_ln_g[0][None], gm_ln_b[0][None])),
        (qk_width, "rope", (cos2, sin2)),
        (qk_width, "rope_scaled", (cos2, sin2)),
        (v_width, "identity", ()),
        (v_width, "silu", ()),
        (2 * D, "sigmoid", ()),
    )
    outs, col0 = [], 0
    for width, kind, extras in segments:
        outs.append(_segment(h, w_in_bf, col0, width, kind, extras, tm=tm_p, tn=TN_INPROJ))
        col0 += width
    a_u, a_vn, r_q, r_k, r_v, r_sg, gates = outs

    causal = jnp.tril(jnp.ones((CHUNK, CHUNK), dtype=bool))
    wm = jnp.where(causal[None], gm_ws[0], 0.0).astype(BF16)
    bs_b = jnp.broadcast_to(gm_b[0][:, :, None], (G, CHUNK, GM_GROUP_DIM))
    ga, gr = _mixing(a_u, a_vn, r_q, r_k, r_v, r_sg, wm, bs_b, decay, xi_b, zeta_b, cd, tm=min(TM_MIXING, S))

    rw = jnp.pad(router_w[0], ((0, 0), (0, LANES - E)))
    rw_hi = rw.astype(BF16)
    rw_lo = (rw - rw_hi.astype(F32)).astype(BF16)
    rb = jnp.pad(router_b[0], (0, LANES - E))[None]
    x2, h2, route, cnt = _merge(x2d, ga, gr, gates, w_proj_a[0].astype(BF16), w_proj_r[0].astype(BF16),
                                w_out[0].astype(BF16), norm2_g[0][None], rw_hi, rw_lo, rb, tm=min(TM_MERGE, S))

    tm_e = TM_EXPERT
    eidx = route[:, 0:TOP_K].astype(jnp.int32)
    rank = route[:, 2 * TOP_K:3 * TOP_K].astype(jnp.int32)
    counts = cnt[0, :E].astype(jnp.int32)
    nblk = (counts + tm_e - 1) // tm_e
    blk_end = jnp.cumsum(nblk)
    blk_start = blk_end - nblk
    dest = (blk_start * tm_e)[eidx] + rank
    nb = (S * TOP_K + tm_e - 1) // tm_e + E
    nvb = blk_end[-1].astype(jnp.int32)
    blk_ids = jnp.arange(nb, dtype=jnp.int32)
    is_last = jnp.any((blk_ids[:, None] == blk_end[None, :] - 1) & (nblk[None, :] > 0), axis=1)
    clear_flag = (is_last | (blk_ids >= nvb)).astype(jnp.int32)

    tm_c = min(TM_COMBINE, S)
    dest_blocks = dest.reshape(S // tm_c, 1, tm_c * TOP_K)
    xs = _dispatch(h2, dest_blocks, clear_flag, tm=tm_c, tm_e=tm_e)
    tf = min(TF_EXPERT, d_ff)
    act = _gateup(xs, _tile_schedule(nblk, blk_start, nvb, d_ff // tf, nb), w_gate_up[0], b_gate_up[0][:, None, :],
                  tm=tm_e, tf=tf)
    ys = _down(act, _down_schedule(nblk, blk_start, nvb, DOWN_LOAD_STEPS, nb), w_down[0], b_down[0][:, None, :],
               tm=tm_e, n_load=DOWN_LOAD_STEPS)

    out = _combine(dest_blocks, x2, route, final_norm_g[None], ys, tm=tm_c)
    return out.reshape(B, S, D)
```

```python
import functools

import jax
import jax.numpy as jnp
from jax import lax
from jax.experimental import pallas as pl
from jax.experimental.pallas import tpu as pltpu

F32 = jnp.float32
BF16 = jnp.bfloat16
U32 = jnp.uint32

CHUNK = 128
GM_GROUP_DIM = 128
RET_HEADS = 8
N_EXPERTS = 32
TOP_K = 4
SWIGLU_LIMIT = 7.0
SWIGLU_ALPHA = 1.702
ROPE_BASE = 10000.0
EPS = 1e-6
LANES = 128

VMEM_LIMIT_BYTES = 56 * 1024 * 1024


def _cparams(sem):
    return pltpu.CompilerParams(dimension_semantics=sem, vmem_limit_bytes=VMEM_LIMIT_BYTES)


SLAB_ROWS = 8
SLAB_PITCH = 9
DMA_QUEUES = 2


def _slab_store(ref, value, row0=0):
    rows, width = value.shape
    half = width // 2
    assert half == SLAB_ROWS * LANES
    lo = lax.bitcast_convert_type(value[:, :half].astype(BF16).astype(F32), U32)
    hi = lax.bitcast_convert_type(value[:, half:].astype(BF16).astype(F32), U32)
    words = (lo >> 16) | hi
    base = row0 * SLAB_PITCH
    for c in range(SLAB_ROWS):
        ref[pl.ds(base + c, rows, stride=SLAB_PITCH), :] = words[:, c * LANES:(c + 1) * LANES]
    for c in range(SLAB_ROWS, SLAB_PITCH):
        ref[pl.ds(base + c, rows, stride=SLAB_PITCH), :] = jnp.zeros((rows, LANES), U32)


def _slab_load(ref, rows):
    words = jnp.concatenate([ref[pl.ds(c, rows, stride=SLAB_PITCH), :] for c in range(SLAB_ROWS)], axis=1)
    lo = lax.bitcast_convert_type(words << 16, F32)
    hi = lax.bitcast_convert_type(words & jnp.uint32(0xFFFF0000), F32)
    return jnp.concatenate([lo, hi], axis=1)


def _gelu_exact(a):
    return 0.5 * a * (1.0 + lax.erf(a * (2.0 ** -0.5)))


def _prenorm_body(x_ref, g_ref, h_ref):
    xf = x_ref[...]
    ms = jnp.mean(xf * xf, axis=-1, keepdims=True)
    h_ref[...] = (xf * lax.rsqrt(ms + EPS) * g_ref[...]).astype(h_ref.dtype)


def _prenorm(x2d, g, *, tm):
    S, D = x2d.shape
    return pl.pallas_call(
        _prenorm_body,
        grid=(S // tm,),
        in_specs=[pl.BlockSpec((tm, D), lambda i: (i, 0)), pl.BlockSpec((1, D), lambda i: (0, 0))],
        out_specs=pl.BlockSpec((tm, D), lambda i: (i, 0)),
        out_shape=jax.ShapeDtypeStruct((S, D), BF16),
        compiler_params=_cparams(("arbitrary",)),
        name="prenorm",
    )(x2d, g)


def _rope_heads(a, cos2, sin2):
    outs = []
    for hd in range(a.shape[1] // LANES):
        ah = a[:, hd * LANES:(hd + 1) * LANES]
        outs.append(ah * cos2 + pltpu.roll(ah, LANES // 2, axis=1) * sin2)
    return jnp.concatenate(outs, axis=1)


def _segment_body(h_ref, w_ref, *refs, kind, k_scale):
    o_ref = refs[-1]
    acc = jnp.dot(h_ref[...], w_ref[...], preferred_element_type=F32)
    if kind == "gelu":
        out = _gelu_exact(acc)
    elif kind == "gelu_layernorm":
        lng_ref, lnb_ref = refs[0], refs[1]
        vf = _gelu_exact(acc)
        mu = jnp.mean(vf, axis=-1, keepdims=True)
        vc = vf - mu
        var = jnp.mean(vc * vc, axis=-1, keepdims=True)
        out = vc * lax.rsqrt(var + EPS) * lng_ref[...] + lnb_ref[...]
    elif kind == "rope":
        out = _rope_heads(acc, refs[0][...], refs[1][...])
    elif kind == "rope_scaled":
        out = _rope_heads(acc, refs[0][...], refs[1][...]) * k_scale
    elif kind == "identity":
        out = acc
    elif kind == "silu":
        out = acc * jax.nn.sigmoid(acc)
    elif kind == "sigmoid":
        out = jax.nn.sigmoid(acc)
    else:
        raise ValueError(kind)
    o_ref[...] = out.astype(o_ref.dtype)


def _segment(h, w_in_bf, col0, width, kind, extras=(), *, tm, tn):
    S, D = h.shape
    assert width % tn == 0 and col0 % tn == 0 and S % tm == 0
    blk0 = col0 // tn
    if kind == "gelu_layernorm":
        assert width == tn
        extra_specs = [pl.BlockSpec((1, tn), lambda j, i: (0, 0))] * 2
    elif kind in ("rope", "rope_scaled"):
        extra_specs = [pl.BlockSpec((tm, LANES), lambda j, i: (i, 0))] * 2
    else:
        extra_specs = []
    return pl.pallas_call(
        functools.partial(_segment_body, kind=kind, k_scale=float(LANES) ** -0.5),
        grid=(width // tn, S // tm),
        in_specs=[
            pl.BlockSpec((tm, D), lambda j, i: (i, 0)),
            pl.BlockSpec((D, tn), lambda j, i: (0, blk0 + j)),
        ] + extra_specs,
        out_specs=pl.BlockSpec((tm, tn), lambda j, i: (i, j)),
        out_shape=jax.ShapeDtypeStruct((S, width), BF16),
        compiler_params=_cparams(("arbitrary", "arbitrary")),
        name="inproj_" + kind,
    )(h, w_in_bf, *extras)


def _mixing_body(cd_ref, u_ref, vn_ref, q_ref, k_ref, v_ref, sg_ref, wm_ref, bs_ref, decay_ref, xi_ref, zeta_ref,
                 ga_ref, gr_ref, state_ref, *, n_chunks, dk, dv):
    @pl.when(pl.program_id(0) == 0)
    def _():
        state_ref[...] = jnp.zeros_like(state_ref)

    def chunk(c, carry):
        rows = pl.ds(pl.multiple_of(c * CHUNK, CHUNK), CHUNK)
        for g in range(wm_ref.shape[0]):
            cols = slice(g * GM_GROUP_DIM, (g + 1) * GM_GROUP_DIM)
            mixed = jnp.dot(wm_ref[g], vn_ref[rows, cols], preferred_element_type=F32) + bs_ref[g]
            ga_ref[rows, cols] = (u_ref[rows, cols].astype(F32) * mixed).astype(ga_ref.dtype)
        for hd in range(RET_HEADS):
            qc = q_ref[rows, hd * dk:(hd + 1) * dk]
            kc = k_ref[rows, hd * dk:(hd + 1) * dk]
            vc = v_ref[rows, hd * dv:(hd + 1) * dv]
            st = state_ref[hd]
            scores = lax.dot_general(qc, kc, (((1,), (1,)), ((), ())), preferred_element_type=F32) * decay_ref[hd]
            inner = jnp.dot(scores.astype(BF16), vc, preferred_element_type=F32)
            cross = jnp.dot(qc, st.astype(BF16), preferred_element_type=F32)
            xi = xi_ref[hd]
            o = inner + cross * jnp.concatenate([xi] * (dv // LANES), axis=1)
            kz = (kc.astype(F32) * zeta_ref[hd]).astype(BF16)
            kv = lax.dot_general(kz, vc, (((0,), (0,)), ((), ())), preferred_element_type=F32)
            state_ref[hd] = st * cd_ref[hd] + kv
            mu = jnp.mean(o, axis=-1, keepdims=True)
            oc = o - mu
            var = jnp.mean(oc * oc, axis=-1, keepdims=True)
            on = oc * lax.rsqrt(var + EPS)
            gr_ref[rows, hd * dv:(hd + 1) * dv] = (on * sg_ref[rows, hd * dv:(hd + 1) * dv].astype(F32)).astype(
                gr_ref.dtype)
        return carry

    lax.fori_loop(0, n_chunks, chunk, 0)


def _mixing(u, vn, q, k, v, sg, wm, bs_b, decay, xi_b, zeta_b, cd, *, tm):
    S, gm_width = u.shape
    qk_width, v_width = q.shape[1], v.shape[1]
    assert S % tm == 0 and tm % CHUNK == 0
    dk, dv = qk_width // RET_HEADS, v_width // RET_HEADS
    G = wm.shape[0]
    body = functools.partial(_mixing_body, n_chunks=tm // CHUNK, dk=dk, dv=dv)
    const3 = lambda i: (0, 0, 0)
    rows = lambda i: (i, 0)
    return pl.pallas_call(
        body,
        grid=(S // tm,),
        in_specs=[
            pl.BlockSpec(memory_space=pltpu.SMEM),
            pl.BlockSpec((tm, gm_width), rows),
            pl.BlockSpec((tm, gm_width), rows),
            pl.BlockSpec((tm, qk_width), rows),
            pl.BlockSpec((tm, qk_width), rows),
            pl.BlockSpec((tm, v_width), rows),
            pl.BlockSpec((tm, v_width), rows),
            pl.BlockSpec((G, CHUNK, CHUNK), const3),
            pl.BlockSpec((G, CHUNK, GM_GROUP_DIM), const3),
            pl.BlockSpec((RET_HEADS, CHUNK, CHUNK), const3),
            pl.BlockSpec((RET_HEADS, CHUNK, LANES), const3),
            pl.BlockSpec((RET_HEADS, CHUNK, dk), const3),
        ],
        out_specs=[
            pl.BlockSpec((tm, gm_width), lambda i: (i, 0)),
            pl.BlockSpec((tm, v_width), lambda i: (i, 0)),
        ],
        out_shape=[
            jax.ShapeDtypeStruct((S, gm_width), BF16),
            jax.ShapeDtypeStruct((S, v_width), BF16),
        ],
        scratch_shapes=[pltpu.VMEM((RET_HEADS, dk, dv), F32)],
        compiler_params=_cparams(("arbitrary",)),
        name="mixing",
    )(cd, u, vn, q, k, v, sg, wm, bs_b, decay, xi_b, zeta_b)


def _branches_body(ga_ref, gr_ref, sa_ref, sr_ref, wa_ref, wr_ref, o_ref):
    y_a = jnp.dot(ga_ref[...], wa_ref[...], preferred_element_type=F32)
    y_r = jnp.dot(gr_ref[...], wr_ref[...], preferred_element_type=F32)
    o_ref[...] = (sa_ref[...].astype(F32) * y_a + sr_ref[...].astype(F32) * y_r).astype(o_ref.dtype)


def _branches(ga, gr, gates, wa, wr, *, tm):
    S = ga.shape[0]
    D = wa.shape[1]
    const = lambda i: (0, 0)
    resident = functools.partial(pl.BlockSpec, index_map=const, pipeline_mode=pl.Buffered(1))
    return pl.pallas_call(
        _branches_body,
        grid=(S // tm,),
        in_specs=[
            pl.BlockSpec((tm, ga.shape[1]), lambda i: (i, 0)),
            pl.BlockSpec((tm, gr.shape[1]), lambda i: (i, 0)),
            pl.BlockSpec((tm, D), lambda i: (i, 0)),
            pl.BlockSpec((tm, D), lambda i: (i, 1)),
            resident(wa.shape),
            resident(wr.shape),
        ],
        out_specs=pl.BlockSpec((tm, D), lambda i: (i, 0)),
        out_shape=jax.ShapeDtypeStruct((S, D), BF16),
        compiler_params=_cparams(("arbitrary",)),
        name="branches",
    )(ga, gr, gates, gates, wa, wr)


def _merge_body(x_ref, m_ref, wo_ref, g2_ref, rwh_ref, rwl_ref, rb_ref,
                x2_ref, h2_ref, route_ref, cnt_ref, run_ref):
    i = pl.program_id(0)
    tm = x_ref.shape[0]

    @pl.when(i == 0)
    def _():
        run_ref[...] = jnp.zeros_like(run_ref)

    sub = min(tm, MERGE_SUB_ROWS)
    lane = lax.broadcasted_iota(jnp.int32, (sub, LANES), 1)
    neg_inf = jnp.float32(-jnp.inf)
    r_iota = lax.broadcasted_iota(jnp.int32, (sub, sub), 0)
    c_iota = lax.broadcasted_iota(jnp.int32, (sub, sub), 1)
    strict_lower = (c_iota < r_iota).astype(BF16)
    run = run_ref[0:1, :]
    for j in range(tm // sub):
        rows = slice(j * sub, (j + 1) * sub)
        x2 = x_ref[rows, :] + jnp.dot(m_ref[rows, :], wo_ref[...], preferred_element_type=F32)
        x2_ref[rows, :] = x2
        ms = jnp.mean(x2 * x2, axis=-1, keepdims=True)
        h2 = x2 * lax.rsqrt(ms + EPS) * g2_ref[...]
        _slab_store(h2_ref, h2, row0=j * sub)

        h_hi = h2.astype(BF16)
        h_lo = (h2 - h_hi.astype(F32)).astype(BF16)
        logits = (jnp.dot(h_hi, rwh_ref[...], preferred_element_type=F32)
                  + jnp.dot(h_hi, rwl_ref[...], preferred_element_type=F32)
                  + jnp.dot(h_lo, rwh_ref[...], preferred_element_type=F32)) + rb_ref[...]
        work = jnp.where(lane < N_EXPERTS, logits, neg_inf)

        vals, idxs, sels = [], [], []
        for _ in range(TOP_K):
            m = jnp.max(work, axis=-1, keepdims=True)
            idx = jnp.min(jnp.where(work == m, lane, LANES), axis=-1, keepdims=True)
            sel = lane == idx
            vals.append(m)
            idxs.append(idx)
            sels.append(sel)
            work = jnp.where(sel, neg_inf, work)
        exps = [jnp.exp(v - vals[0]) for v in vals]
        denom = exps[0] + exps[1] + exps[2] + exps[3]
        gates = [e / denom for e in exps]

        onehot = jnp.zeros((sub, LANES), F32)
        for sel in sels:
            onehot = onehot + sel.astype(F32)
        before = jnp.dot(strict_lower, onehot.astype(BF16), preferred_element_type=F32) + run
        ranks = [jnp.sum(jnp.where(sel, before, 0.0), axis=-1, keepdims=True) for sel in sels]
        run = run + jnp.sum(onehot, axis=0, keepdims=True)

        route = jnp.zeros((sub, LANES), F32)
        for k in range(TOP_K):
            route = jnp.where(lane == k, idxs[k].astype(F32), route)
            route = jnp.where(lane == TOP_K + k, gates[k], route)
            route = jnp.where(lane == 2 * TOP_K + k, ranks[k], route)
        route_ref[rows, :] = route

    run_ref[...] = jnp.broadcast_to(run, run_ref.shape)
    cnt_ref[...] = jnp.broadcast_to(run, cnt_ref.shape)


def _merge(x2d, merged, wo, g2, rw_hi, rw_lo, rb, *, tm):
    S, D = x2d.shape
    assert S % tm == 0
    const = lambda i: (0, 0)
    resident = functools.partial(pl.BlockSpec, index_map=const, pipeline_mode=pl.Buffered(1))
    return pl.pallas_call(
        _merge_body,
        grid=(S // tm,),
        in_specs=[
            pl.BlockSpec((tm, D), lambda i: (i, 0)),
            pl.BlockSpec((tm, D), lambda i: (i, 0)),
            resident(wo.shape),
            pl.BlockSpec((1, D), const),
            resident(rw_hi.shape),
            resident(rw_lo.shape),
            pl.BlockSpec((1, LANES), const),
        ],
        out_specs=[
            pl.BlockSpec((tm, D), lambda i: (i, 0)),
            pl.BlockSpec((tm * SLAB_PITCH, LANES), lambda i: (i, 0)),
            pl.BlockSpec((tm, LANES), lambda i: (i, 0)),
            pl.BlockSpec((8, LANES), const),
        ],
        out_shape=[
            jax.ShapeDtypeStruct((S, D), F32),
            jax.ShapeDtypeStruct((S * SLAB_PITCH, LANES), U32),
            jax.ShapeDtypeStruct((S, LANES), F32),
            jax.ShapeDtypeStruct((8, LANES), F32),
        ],
        scratch_shapes=[pltpu.VMEM((8, LANES), F32)],
        compiler_params=_cparams(("arbitrary",)),
        name="merge",
    )(x2d, merged, wo, g2, rw_hi, rw_lo, rb)


def _dispatch_body(clear_ref, dest_ref, h2_ref, xs_hbm, zeros_ref, sem, clear_sem, *, tm, blk_rows, nb):
    i = pl.program_id(0)

    def clear_copy(b):
        return pltpu.make_async_copy(zeros_ref, xs_hbm.at[pl.ds(pl.multiple_of(b * blk_rows, blk_rows), blk_rows)],
                                     clear_sem.at[0])

    @pl.when(i == 0)
    def _():
        zeros_ref[...] = jnp.zeros_like(zeros_ref)

        def start(b, carry):
            @pl.when(clear_ref[b] == 1)
            def _():
                clear_copy(b).start()
            return carry

        def drain(b, carry):
            @pl.when(clear_ref[b] == 1)
            def _():
                clear_copy(b).wait()
            return carry

        lax.fori_loop(0, nb, start, 0)
        lax.fori_loop(0, nb, drain, 0)

    def issue(t, carry):
        for k in range(TOP_K):
            d = dest_ref[0, 0, t * TOP_K + k]
            pltpu.make_async_copy(h2_ref.at[pl.ds(t * SLAB_PITCH, SLAB_PITCH)],
                                  xs_hbm.at[pl.ds(d * SLAB_PITCH, SLAB_PITCH)],
                                  sem.at[0]).start(priority=k % DMA_QUEUES)
        return carry

    lax.fori_loop(0, tm, issue, 0, unroll=2)
    for k in range(TOP_K):
        pltpu.make_async_copy(h2_ref, xs_hbm.at[pl.ds(0, tm * SLAB_PITCH)], sem.at[0]).wait()


def _dispatch(h2_slab, dest, clear_flag, *, tm, tm_e):
    nb = clear_flag.shape[0]
    blk_rows = tm_e * SLAB_PITCH
    n = h2_slab.shape[0] // (tm * SLAB_PITCH)
    grid_spec = pltpu.PrefetchScalarGridSpec(
        num_scalar_prefetch=1,
        grid=(n,),
        in_specs=[
            pl.BlockSpec((1, 1, tm * TOP_K), lambda i, cf: (i, 0, 0), memory_space=pltpu.SMEM),
            pl.BlockSpec((tm * SLAB_PITCH, LANES), lambda i, cf: (i, 0)),
        ],
        out_specs=pl.BlockSpec(memory_space=pl.ANY),
        scratch_shapes=[pltpu.VMEM((blk_rows, LANES), U32), pltpu.SemaphoreType.DMA((1,)),
                        pltpu.SemaphoreType.DMA((1,))],
    )
    return pl.pallas_call(
        functools.partial(_dispatch_body, tm=tm, blk_rows=blk_rows, nb=nb),
        grid_spec=grid_spec,
        out_shape=jax.ShapeDtypeStruct((nb * blk_rows, LANES), U32),
        compiler_params=_cparams(("arbitrary",)),
        name="dispatch",
    )(clear_flag, dest, h2_slab)


def _tile_schedule(nblk, blk_start, nvb, n_tiles, nb):
    n_steps = nb * n_tiles
    steps_per_e = n_tiles * nblk
    e_end = jnp.cumsum(steps_per_e)
    e_start = e_end - steps_per_e
    total = e_end[-1]
    s = jnp.arange(n_steps, dtype=jnp.int32)
    live = s < total
    sl = jnp.minimum(s, total - 1)
    e = jnp.sum(e_end[None, :] <= sl[:, None], axis=1).astype(jnp.int32)
    r = sl - e_start[e]
    nb_e = jnp.maximum(nblk[e], 1)
    tile = r // nb_e
    b = r - tile * nb_e
    blk = blk_start[e] + b
    first = (live & (b == 0)).astype(jnp.int32)
    dead = jnp.maximum(s - total, 0)
    out_blk = jnp.where(live, blk, nvb + dead // n_tiles)
    out_tile = jnp.where(live, tile, dead % n_tiles)
    nxt = jnp.minimum(s - b + nb_e, n_steps - 1)
    has_next = (live & (s - b + nb_e < total)).astype(jnp.int32)
    as_i32 = lambda a: a.astype(jnp.int32)
    return (as_i32(e), as_i32(tile), as_i32(blk), first, as_i32(out_blk), as_i32(out_tile), as_i32(total)[None],
            as_i32(e[nxt]), as_i32(tile[nxt]), has_next)


def _gateup_body(se_ref, st_ref, sb_ref, sf_ref, ob_ref, ot_ref, nl_ref, ne_ref, nt_ref, hn_ref,
                 xs_ref, wgu_hbm, bg_ref, bu_ref, act_ref, stage_ref, wgb_ref, wub_ref, sem, *, tf, d_ff):
    s = pl.program_id(0)
    live = s < nl_ref[0]
    D = stage_ref.shape[1]

    def tile_copies(e, t):
        return [pltpu.make_async_copy(
            wgu_hbm.at[e, pl.ds(0, D), pl.ds(pl.multiple_of(half * d_ff + t * tf, tf), tf)],
            stage_ref.at[half], sem.at[half]) for half in range(2)]

    @pl.when(sf_ref[s] == 1)
    def _():
        @pl.when(s == 0)
        def _():
            for c in tile_copies(se_ref[0], st_ref[0]):
                c.start()

        for c in tile_copies(se_ref[s], st_ref[s]):
            c.wait()
        wgb_ref[...] = stage_ref[0].astype(BF16)
        wub_ref[...] = stage_ref[1].astype(BF16)

        @pl.when(hn_ref[s] == 1)
        def _():
            for c in tile_copies(ne_ref[s], nt_ref[s]):
                c.start()

    @pl.when(live)
    def _():
        xb = _slab_load(xs_ref, act_ref.shape[0]).astype(BF16)
        gate = jnp.dot(xb, wgb_ref[...], preferred_element_type=F32) + bg_ref[0]
        up = jnp.dot(xb, wub_ref[...], preferred_element_type=F32) + bu_ref[0]
        gate = jnp.minimum(gate, SWIGLU_LIMIT)
        up = jnp.clip(up, -SWIGLU_LIMIT, SWIGLU_LIMIT)
        glu = gate * jax.nn.sigmoid(gate * SWIGLU_ALPHA)
        act_ref[...] = ((up + 1.0) * glu).astype(act_ref.dtype)

    @pl.when(jnp.logical_not(live))
    def _():
        act_ref[...] = jnp.zeros_like(act_ref)


def _gateup(xs, sched, wgu, bgu, *, tm, tf):
    D = wgu.shape[1]
    R = xs.shape[0] // SLAB_PITCH
    d_ff = wgu.shape[2] // 2
    nj = d_ff // tf
    nb = R // tm
    assert d_ff % tf == 0 and R % tm == 0
    grid_spec = pltpu.PrefetchScalarGridSpec(
        num_scalar_prefetch=10,
        grid=(nb * nj,),
        in_specs=[
            pl.BlockSpec((tm * SLAB_PITCH, LANES), lambda s, se, st, sb, *_: (sb[s], 0)),
            pl.BlockSpec(memory_space=pl.ANY),
            pl.BlockSpec((1, 1, tf), lambda s, se, st, *_: (se[s], 0, st[s])),
            pl.BlockSpec((1, 1, tf), lambda s, se, st, *_: (se[s], 0, nj + st[s])),
        ],
        out_specs=pl.BlockSpec((tm, tf), lambda s, se, st, sb, sf, ob, ot, *_: (ob[s], ot[s])),
        scratch_shapes=[pltpu.VMEM((2, D, tf), F32), pltpu.VMEM((D, tf), BF16), pltpu.VMEM((D, tf), BF16),
                        pltpu.SemaphoreType.DMA((2,))],
    )
    return pl.pallas_call(
        functools.partial(_gateup_body, tf=tf, d_ff=d_ff),
        grid_spec=grid_spec,
        out_shape=jax.ShapeDtypeStruct((R, d_ff), BF16),
        compiler_params=_cparams(("arbitrary",)),
        name="gateup",
    )(*sched, xs, wgu, bgu, bgu)


def _down_body(se_ref, st_ref, sb_ref, sf_ref, ob_ref, ot_ref, nl_ref, ne_ref, nt_ref, hn_ref,
               act_ref, wd_hbm, bd_ref, y_ref, stage_ref, wdb_ref, sem):
    s = pl.program_id(0)
    live = s < nl_ref[0]

    def weight_copy(e):
        return pltpu.make_async_copy(wd_hbm.at[e], stage_ref, sem.at[0])

    @pl.when(sf_ref[s] == 1)
    def _():
        @pl.when(s == 0)
        def _():
            weight_copy(se_ref[0]).start()

        weight_copy(se_ref[s]).wait()
        wdb_ref[...] = stage_ref[...].astype(BF16)

        @pl.when(hn_ref[s] == 1)
        def _():
            weight_copy(ne_ref[s]).start()

    @pl.when(live)
    def _():
        y = jnp.dot(act_ref[...], wdb_ref[...], preferred_element_type=F32) + bd_ref[0]
        _slab_store(y_ref, y)

    @pl.when(jnp.logical_not(live))
    def _():
        y_ref[...] = jnp.zeros_like(y_ref)


def _down(act, sched, wd, bd, *, tm):
    R, d_ff = act.shape
    D = wd.shape[2]
    nb = R // tm
    grid_spec = pltpu.PrefetchScalarGridSpec(
        num_scalar_prefetch=10,
        grid=(nb,),
        in_specs=[
            pl.BlockSpec((tm, d_ff), lambda s, se, st, sb, *_: (sb[s], 0)),
            pl.BlockSpec(memory_space=pl.ANY),
            pl.BlockSpec((1, 1, D), lambda s, se, *_: (se[s], 0, 0)),
        ],
        out_specs=pl.BlockSpec((tm * SLAB_PITCH, LANES), lambda s, se, st, sb, sf, ob, *_: (ob[s], 0)),
        scratch_shapes=[pltpu.VMEM((d_ff, D), F32), pltpu.VMEM((d_ff, D), BF16), pltpu.SemaphoreType.DMA((1,))],
    )
    return pl.pallas_call(
        _down_body,
        grid_spec=grid_spec,
        out_shape=jax.ShapeDtypeStruct((R * SLAB_PITCH, LANES), U32),
        compiler_params=_cparams(("arbitrary",)),
        name="down",
    )(*sched, act, wd, bd)


def _combine_body(dest_cur_ref, dest_nxt_ref, x2_ref, route_ref, gf_ref, ys_hbm, o_ref, buf_ref, sem, *, tm):
    i = pl.program_id(0)
    slot = i % 2

    def start_rows(dest_ref, dst_slot):
        def issue(t, carry):
            for k in range(TOP_K):
                d = dest_ref[0, 0, t * TOP_K + k]
                pltpu.make_async_copy(ys_hbm.at[pl.ds(d * SLAB_PITCH, SLAB_ROWS)],
                                      buf_ref.at[dst_slot, k, pl.ds(t * SLAB_PITCH, SLAB_ROWS)],
                                      sem.at[dst_slot]).start(priority=k % DMA_QUEUES)
            return carry
        lax.fori_loop(0, tm, issue, 0, unroll=2)

    def wait_rows(dst_slot):
        for k in range(TOP_K):
            pltpu.make_async_copy(ys_hbm.at[pl.ds(0, tm * SLAB_ROWS)],
                                  buf_ref.at[dst_slot, k, pl.ds(0, tm * SLAB_ROWS)], sem.at[dst_slot]).wait()

    @pl.when(i == 0)
    def _():
        start_rows(dest_cur_ref, 0)

    @pl.when(i + 1 < pl.num_programs(0))
    def _():
        start_rows(dest_nxt_ref, 1 - slot)

    wait_rows(slot)
    route = route_ref[...]
    x3 = x2_ref[...]
    for k in range(TOP_K):
        x3 = x3 + route[:, TOP_K + k:TOP_K + k + 1] * _slab_load(buf_ref.at[slot, k], tm)
    ms = jnp.mean(x3 * x3, axis=-1, keepdims=True)
    o_ref[...] = x3 * lax.rsqrt(ms + EPS) * gf_ref[...]


def _combine(dest, x2, route, gf, ys_slab, *, tm):
    T, D = x2.shape
    n = T // tm
    assert T % tm == 0
    return pl.pallas_call(
        functools.partial(_combine_body, tm=tm),
        grid=(n,),
        in_specs=[
            pl.BlockSpec((1, 1, tm * TOP_K), lambda i: (i, 0, 0), memory_space=pltpu.SMEM),
            pl.BlockSpec((1, 1, tm * TOP_K), lambda i: (jnp.minimum(i + 1, n - 1), 0, 0), memory_space=pltpu.SMEM),
            pl.BlockSpec((tm, D), lambda i: (i, 0)),
            pl.BlockSpec((tm, LANES), lambda i: (i, 0)),
            pl.BlockSpec((1, D), lambda i: (0, 0)),
            pl.BlockSpec(memory_space=pl.ANY),
        ],
        out_specs=pl.BlockSpec((tm, D), lambda i: (i, 0)),
        out_shape=jax.ShapeDtypeStruct((T, D), F32),
        scratch_shapes=[pltpu.VMEM((2, TOP_K, tm * SLAB_PITCH, LANES), U32), pltpu.SemaphoreType.DMA((2,))],
        compiler_params=_cparams(("arbitrary",)),
        name="combine",
    )(dest, dest, x2, route, gf, ys_slab)


def _retention_tables(S, dk):
    pos = jnp.arange(S, dtype=F32)
    inv_freq = ROPE_BASE ** (-jnp.arange(0, dk, 2, dtype=F32) / dk)
    ang = pos[:, None] * inv_freq[None, :]
    cos, sin = jnp.cos(ang), jnp.sin(ang)
    cos2 = jnp.concatenate([cos, cos], axis=-1)
    sin2 = jnp.concatenate([-sin, sin], axis=-1)
    log_gamma = jnp.log1p(-jnp.exp2(-5.0 - jnp.arange(RET_HEADS, dtype=F32)))
    idx = jnp.arange(CHUNK, dtype=F32)
    diff = idx[:, None] - idx[None, :]
    decay = jnp.where(diff[None] >= 0, jnp.exp(log_gamma[:, None, None] * jnp.maximum(diff, 0.0)[None]), 0.0)
    xi = jnp.exp(log_gamma[:, None] * (idx[None, :] + 1.0))
    zeta = jnp.exp(log_gamma[:, None] * (CHUNK - 1.0 - idx[None, :]))
    cd = jnp.exp(log_gamma * CHUNK)
    xi_b = jnp.broadcast_to(xi[:, :, None], (RET_HEADS, CHUNK, LANES))
    zeta_b = jnp.broadcast_to(zeta[:, :, None], (RET_HEADS, CHUNK, dk))
    return cos2, sin2, decay, xi_b, zeta_b, cd


TM_INPROJ = 1024
TN_INPROJ = 1024
TM_MIXING = 256
TM_MERGE = 512
MERGE_SUB_ROWS = 512
TM_EXPERT = 512
TF_EXPERT = 512
TM_COMBINE = 256


def kernel(x, norm1_g, w_in, gm_ln_g, gm_ln_b, gm_ws, gm_b, w_proj_a, w_proj_r, w_out, norm2_g, router_w, router_b,
           w_gate_up, b_gate_up, w_down, b_down, final_norm_g):
    B, S, D = x.shape
    assert B == 1 and norm1_g.shape[0] == 1, "single sequence, depth 1"
    gm_width = w_proj_a.shape[1]
    v_width = w_proj_r.shape[1]
    qk_width = (w_in.shape[2] - 2 * gm_width - 2 * v_width - 2 * D) // 2
    dk = qk_width // RET_HEADS
    assert dk == LANES and gm_ws.shape[2] == CHUNK and D == 2 * SLAB_ROWS * LANES
    G = gm_ws.shape[1]
    E = router_w.shape[2]
    d_ff = w_down.shape[2]
    x2d = x.reshape(S, D)

    cos2, sin2, decay, xi_b, zeta_b, cd = _retention_tables(S, dk)
    tm_p = min(TM_INPROJ, S)
    h = _prenorm(x2d, norm1_g[0][None], tm=tm_p)
    w_in_bf = w_in[0].astype(BF16)
    segments = (
        (gm_width, "gelu", ()),
        (gm_width, "gelu_layernorm", (gm_ln_g[0][None], gm_ln_b[0][None])),
        (qk_width, "rope", (cos2, sin2)),
        (qk_width, "rope_scaled", (cos2, sin2)),
        (v_width, "identity", ()),
        (v_width, "silu", ()),
        (2 * D, "sigmoid", ()),
    )
    outs, col0 = [], 0
    for width, kind, extras in segments:
        outs.append(_segment(h, w_in_bf, col0, width, kind, extras, tm=tm_p, tn=TN_INPROJ))
        col0 += width
    a_u, a_vn, r_q, r_k, r_v, r_sg, gates = outs

    causal = jnp.tril(jnp.ones((CHUNK, CHUNK), dtype=bool))
    wm = jnp.where(causal[None], gm_ws[0], 0.0).astype(BF16)
    bs_b = jnp.broadcast_to(gm_b[0][:, :, None], (G, CHUNK, GM_GROUP_DIM))
    ga, gr = _mixing(a_u, a_vn, r_q, r_k, r_v, r_sg, wm, bs_b, decay, xi_b, zeta_b, cd, tm=min(TM_MIXING, S))

    rw = jnp.pad(router_w[0], ((0, 0), (0, LANES - E)))
    rw_hi = rw.astype(BF16)
    rw_lo = (rw - rw_hi.astype(F32)).astype(BF16)
    rb = jnp.pad(router_b[0], (0, LANES - E))[None]
    tm_m = min(TM_MERGE, S)
    merged = _branches(ga, gr, gates, w_proj_a[0].astype(BF16), w_proj_r[0].astype(BF16), tm=tm_m)
    x2, h2, route, cnt = _merge(x2d, merged, w_out[0].astype(BF16), norm2_g[0][None], rw_hi, rw_lo, rb, tm=tm_m)

    tm_e = TM_EXPERT
    eidx = route[:, 0:TOP_K].astype(jnp.int32)
    rank = route[:, 2 * TOP_K:3 * TOP_K].astype(jnp.int32)
    counts = cnt[0, :E].astype(jnp.int32)
    nblk = (counts + tm_e - 1) // tm_e
    blk_end = jnp.cumsum(nblk)
    blk_start = blk_end - nblk
    dest = (blk_start * tm_e)[eidx] + rank
    nb = (S * TOP_K + tm_e - 1) // tm_e + E
    nvb = blk_end[-1].astype(jnp.int32)
    blk_ids = jnp.arange(nb, dtype=jnp.int32)
    is_last = jnp.any((blk_ids[:, None] == blk_end[None, :] - 1) & (nblk[None, :] > 0), axis=1)
    clear_flag = (is_last | (blk_ids >= nvb)).astype(jnp.int32)

    tm_c = min(TM_COMBINE, S)
    dest_blocks = dest.reshape(S // tm_c, 1, tm_c * TOP_K)
    xs = _dispatch(h2, dest_blocks, clear_flag, tm=tm_c, tm_e=tm_e)
    tf = min(TF_EXPERT, d_ff)
    act = _gateup(xs, _tile_schedule(nblk, blk_start, nvb, d_ff // tf, nb), w_gate_up[0], b_gate_up[0][:, None, :],
                  tm=tm_e, tf=tf)
    ys = _down(act, _tile_schedule(nblk, blk_start, nvb, 1, nb), w_down[0], b_down[0][:, None, :], tm=tm_e)

    out = _combine(dest_blocks, x2, route, final_norm_g[None], ys, tm=tm_c)
    return out.reshape(B, S, D)
```

```python
import functools

import jax
import jax.numpy as jnp
from jax import lax
from jax.experimental import pallas as pl
from jax.experimental.pallas import tpu as pltpu

F32 = jnp.float32
BF16 = jnp.bfloat16
U32 = jnp.uint32

CHUNK = 128
GM_GROUP_DIM = 128
RET_HEADS = 8
N_EXPERTS = 32
TOP_K = 4
SWIGLU_LIMIT = 7.0
SWIGLU_ALPHA = 1.702
ROPE_BASE = 10000.0
EPS = 1e-6
LANES = 128

VMEM_LIMIT_BYTES = 56 * 1024 * 1024


def _cparams(sem):
    return pltpu.CompilerParams(dimension_semantics=sem, vmem_limit_bytes=VMEM_LIMIT_BYTES)


SLAB_ROWS = 8
SLAB_PITCH = 9
DMA_QUEUES = 2


def _slab_store(ref, value, row0=0):
    rows, width = value.shape
    half = width // 2
    assert half == SLAB_ROWS * LANES
    lo = lax.bitcast_convert_type(value[:, :half].astype(BF16).astype(F32), U32)
    hi = lax.bitcast_convert_type(value[:, half:].astype(BF16).astype(F32), U32)
    words = (lo >> 16) | hi
    base = row0 * SLAB_PITCH
    for c in range(SLAB_ROWS):
        ref[pl.ds(base + c, rows, stride=SLAB_PITCH), :] = words[:, c * LANES:(c + 1) * LANES]
    for c in range(SLAB_ROWS, SLAB_PITCH):
        ref[pl.ds(base + c, rows, stride=SLAB_PITCH), :] = jnp.zeros((rows, LANES), U32)


def _slab_load(ref, rows):
    words = jnp.concatenate([ref[pl.ds(c, rows, stride=SLAB_PITCH), :] for c in range(SLAB_ROWS)], axis=1)
    lo = lax.bitcast_convert_type(words << 16, F32)
    hi = lax.bitcast_convert_type(words & jnp.uint32(0xFFFF0000), F32)
    return jnp.concatenate([lo, hi], axis=1)


def _gelu_exact(a):
    return 0.5 * a * (1.0 + lax.erf(a * (2.0 ** -0.5)))


def _prenorm_body(x_ref, g_ref, h_ref):
    xf = x_ref[...]
    ms = jnp.mean(xf * xf, axis=-1, keepdims=True)
    h_ref[...] = (xf * lax.rsqrt(ms + EPS) * g_ref[...]).astype(h_ref.dtype)


def _prenorm(x2d, g, *, tm):
    S, D = x2d.shape
    return pl.pallas_call(
        _prenorm_body,
        grid=(S // tm,),
        in_specs=[pl.BlockSpec((tm, D), lambda i: (i, 0)), pl.BlockSpec((1, D), lambda i: (0, 0))],
        out_specs=pl.BlockSpec((tm, D), lambda i: (i, 0)),
        out_shape=jax.ShapeDtypeStruct((S, D), BF16),
        compiler_params=_cparams(("arbitrary",)),
        name="prenorm",
    )(x2d, g)


def _rope_heads(a, cos2, sin2):
    outs = []
    for hd in range(a.shape[1] // LANES):
        ah = a[:, hd * LANES:(hd + 1) * LANES]
        outs.append(ah * cos2 + pltpu.roll(ah, LANES // 2, axis=1) * sin2)
    return jnp.concatenate(outs, axis=1)


def _segment_body(h_ref, w_ref, *refs, kind, k_scale):
    o_ref = refs[-1]
    acc = jnp.dot(h_ref[...], w_ref[...], preferred_element_type=F32)
    if kind == "gelu":
        out = _gelu_exact(acc)
    elif kind == "gelu_layernorm":
        lng_ref, lnb_ref = refs[0], refs[1]
        vf = _gelu_exact(acc)
        mu = jnp.mean(vf, axis=-1, keepdims=True)
        vc = vf - mu
        var = jnp.mean(vc * vc, axis=-1, keepdims=True)
        out = vc * lax.rsqrt(var + EPS) * lng_ref[...] + lnb_ref[...]
    elif kind == "rope":
        out = _rope_heads(acc, refs[0][...], refs[1][...])
    elif kind == "rope_scaled":
        out = _rope_heads(acc, refs[0][...], refs[1][...]) * k_scale
    elif kind == "identity":
        out = acc
    elif kind == "silu":
        out = acc * jax.nn.sigmoid(acc)
    elif kind == "sigmoid":
        out = jax.nn.sigmoid(acc)
    else:
        raise ValueError(kind)
    o_ref[...] = out.astype(o_ref.dtype)


def _segment(h, w_in_bf, col0, width, kind, extras=(), *, tm, tn):
    S, D = h.shape
    assert width % tn == 0 and col0 % tn == 0 and S % tm == 0
    blk0 = col0 // tn
    if kind == "gelu_layernorm":
        assert width == tn
        extra_specs = [pl.BlockSpec((1, tn), lambda j, i: (0, 0))] * 2
    elif kind in ("rope", "rope_scaled"):
        extra_specs = [pl.BlockSpec((tm, LANES), lambda j, i: (i, 0))] * 2
    else:
        extra_specs = []
    return pl.pallas_call(
        functools.partial(_segment_body, kind=kind, k_scale=float(LANES) ** -0.5),
        grid=(width // tn, S // tm),
        in_specs=[
            pl.BlockSpec((tm, D), lambda j, i: (i, 0)),
            pl.BlockSpec((D, tn), lambda j, i: (0, blk0 + j)),
        ] + extra_specs,
        out_specs=pl.BlockSpec((tm, tn), lambda j, i: (i, j)),
        out_shape=jax.ShapeDtypeStruct((S, width), BF16),
        compiler_params=_cparams(("arbitrary", "arbitrary")),
        name="inproj_" + kind,
    )(h, w_in_bf, *extras)


def _mixing_body(cd_ref, u_ref, vn_ref, q_ref, k_ref, v_ref, sg_ref, wm_ref, bs_ref, decay_ref, xi_ref, zeta_ref,
                 ga_ref, gr_ref, state_ref, *, n_chunks, dk, dv):
    @pl.when(pl.program_id(0) == 0)
    def _():
        state_ref[...] = jnp.zeros_like(state_ref)

    def chunk(c, carry):
        rows = pl.ds(pl.multiple_of(c * CHUNK, CHUNK), CHUNK)
        for g in range(wm_ref.shape[0]):
            cols = slice(g * GM_GROUP_DIM, (g + 1) * GM_GROUP_DIM)
            mixed = jnp.dot(wm_ref[g], vn_ref[rows, cols], preferred_element_type=F32) + bs_ref[g]
            ga_ref[rows, cols] = (u_ref[rows, cols].astype(F32) * mixed).astype(ga_ref.dtype)
        for hd in range(RET_HEADS):
            qc = q_ref[rows, hd * dk:(hd + 1) * dk]
            kc = k_ref[rows, hd * dk:(hd + 1) * dk]
            vc = v_ref[rows, hd * dv:(hd + 1) * dv]
            st = state_ref[hd]
            scores = lax.dot_general(qc, kc, (((1,), (1,)), ((), ())), preferred_element_type=F32) * decay_ref[hd]
            inner = jnp.dot(scores.astype(BF16), vc, preferred_element_type=F32)
            cross = jnp.dot(qc, st.astype(BF16), preferred_element_type=F32)
            xi = xi_ref[hd]
            o = inner + cross * jnp.concatenate([xi] * (dv // LANES), axis=1)
            kz = (kc.astype(F32) * zeta_ref[hd]).astype(BF16)
            kv = lax.dot_general(kz, vc, (((0,), (0,)), ((), ())), preferred_element_type=F32)
            state_ref[hd] = st * cd_ref[hd] + kv
            mu = jnp.mean(o, axis=-1, keepdims=True)
            oc = o - mu
            var = jnp.mean(oc * oc, axis=-1, keepdims=True)
            on = oc * lax.rsqrt(var + EPS)
            gr_ref[rows, hd * dv:(hd + 1) * dv] = (on * sg_ref[rows, hd * dv:(hd + 1) * dv].astype(F32)).astype(
                gr_ref.dtype)
        return carry

    lax.fori_loop(0, n_chunks, chunk, 0, unroll=True)


def _mixing(u, vn, q, k, v, sg, wm, bs_b, decay, xi_b, zeta_b, cd, *, tm):
    S, gm_width = u.shape
    qk_width, v_width = q.shape[1], v.shape[1]
    assert S % tm == 0 and tm % CHUNK == 0
    dk, dv = qk_width // RET_HEADS, v_width // RET_HEADS
    G = wm.shape[0]
    body = functools.partial(_mixing_body, n_chunks=tm // CHUNK, dk=dk, dv=dv)
    const3 = lambda i: (0, 0, 0)
    rows = lambda i: (i, 0)
    return pl.pallas_call(
        body,
        grid=(S // tm,),
        in_specs=[
            pl.BlockSpec(memory_space=pltpu.SMEM),
            pl.BlockSpec((tm, gm_width), rows),
            pl.BlockSpec((tm, gm_width), rows),
            pl.BlockSpec((tm, qk_width), rows),
            pl.BlockSpec((tm, qk_width), rows),
            pl.BlockSpec((tm, v_width), rows),
            pl.BlockSpec((tm, v_width), rows),
            pl.BlockSpec((G, CHUNK, CHUNK), const3),
            pl.BlockSpec((G, CHUNK, GM_GROUP_DIM), const3),
            pl.BlockSpec((RET_HEADS, CHUNK, CHUNK), const3),
            pl.BlockSpec((RET_HEADS, CHUNK, LANES), const3),
            pl.BlockSpec((RET_HEADS, CHUNK, dk), const3),
        ],
        out_specs=[
            pl.BlockSpec((tm, gm_width), lambda i: (i, 0)),
            pl.BlockSpec((tm, v_width), lambda i: (i, 0)),
        ],
        out_shape=[
            jax.ShapeDtypeStruct((S, gm_width), BF16),
            jax.ShapeDtypeStruct((S, v_width), BF16),
        ],
        scratch_shapes=[pltpu.VMEM((RET_HEADS, dk, dv), F32)],
        compiler_params=_cparams(("arbitrary",)),
        name="mixing",
    )(cd, u, vn, q, k, v, sg, wm, bs_b, decay, xi_b, zeta_b)


def _branches_body(ga_ref, gr_ref, sa_ref, sr_ref, wa_ref, wr_ref, o_ref):
    y_a = jnp.dot(ga_ref[...], wa_ref[...], preferred_element_type=F32)
    y_r = jnp.dot(gr_ref[...], wr_ref[...], preferred_element_type=F32)
    o_ref[...] = (sa_ref[...].astype(F32) * y_a + sr_ref[...].astype(F32) * y_r).astype(o_ref.dtype)


def _branches(ga, gr, gates, wa, wr, *, tm):
    S = ga.shape[0]
    D = wa.shape[1]
    const = lambda i: (0, 0)
    resident = functools.partial(pl.BlockSpec, index_map=const, pipeline_mode=pl.Buffered(1))
    return pl.pallas_call(
        _branches_body,
        grid=(S // tm,),
        in_specs=[
            pl.BlockSpec((tm, ga.shape[1]), lambda i: (i, 0)),
            pl.BlockSpec((tm, gr.shape[1]), lambda i: (i, 0)),
            pl.BlockSpec((tm, D), lambda i: (i, 0)),
            pl.BlockSpec((tm, D), lambda i: (i, 1)),
            resident(wa.shape),
            resident(wr.shape),
        ],
        out_specs=pl.BlockSpec((tm, D), lambda i: (i, 0)),
        out_shape=jax.ShapeDtypeStruct((S, D), BF16),
        compiler_params=_cparams(("arbitrary",)),
        name="branches",
    )(ga, gr, gates, gates, wa, wr)


def _merge_body(x_ref, m_ref, wo_ref, g2_ref, rwh_ref, rwl_ref, rb_ref,
                x2_ref, h2_ref, route_ref, cnt_ref, run_ref):
    i = pl.program_id(0)
    tm = x_ref.shape[0]

    @pl.when(i == 0)
    def _():
        run_ref[...] = jnp.zeros_like(run_ref)

    sub = min(tm, MERGE_SUB_ROWS)
    lane = lax.broadcasted_iota(jnp.int32, (sub, LANES), 1)
    neg_inf = jnp.float32(-jnp.inf)
    r_iota = lax.broadcasted_iota(jnp.int32, (sub, sub), 0)
    c_iota = lax.broadcasted_iota(jnp.int32, (sub, sub), 1)
    strict_lower = (c_iota < r_iota).astype(BF16)
    run = run_ref[0:1, :]
    for j in range(tm // sub):
        rows = slice(j * sub, (j + 1) * sub)
        x2 = x_ref[rows, :] + jnp.dot(m_ref[rows, :], wo_ref[...], preferred_element_type=F32)
        x2_ref[rows, :] = x2
        ms = jnp.mean(x2 * x2, axis=-1, keepdims=True)
        h2 = x2 * lax.rsqrt(ms + EPS) * g2_ref[...]
        _slab_store(h2_ref, h2, row0=j * sub)

        h_hi = h2.astype(BF16)
        h_lo = (h2 - h_hi.astype(F32)).astype(BF16)
        logits = (jnp.dot(h_hi, rwh_ref[...], preferred_element_type=F32)
                  + jnp.dot(h_hi, rwl_ref[...], preferred_element_type=F32)
                  + jnp.dot(h_lo, rwh_ref[...], preferred_element_type=F32)) + rb_ref[...]
        work = jnp.where(lane < N_EXPERTS, logits, neg_inf)

        vals, idxs, sels = [], [], []
        for _ in range(TOP_K):
            m = jnp.max(work, axis=-1, keepdims=True)
            idx = jnp.min(jnp.where(work == m, lane, LANES), axis=-1, keepdims=True)
            sel = lane == idx
            vals.append(m)
            idxs.append(idx)
            sels.append(sel)
            work = jnp.where(sel, neg_inf, work)
        exps = [jnp.exp(v - vals[0]) for v in vals]
        denom = exps[0] + exps[1] + exps[2] + exps[3]
        gates = [e / denom for e in exps]

        onehot = jnp.zeros((sub, LANES), F32)
        for sel in sels:
            onehot = onehot + sel.astype(F32)
        before = jnp.dot(strict_lower, onehot.astype(BF16), preferred_element_type=F32) + run
        ranks = [jnp.sum(jnp.where(sel, before, 0.0), axis=-1, keepdims=True) for sel in sels]
        run = run + jnp.sum(onehot, axis=0, keepdims=True)

        route = jnp.zeros((sub, LANES), F32)
        for k in range(TOP_K):
            route = jnp.where(lane == k, idxs[k].astype(F32), route)
            route = jnp.where(lane == TOP_K + k, gates[k], route)
            route = jnp.where(lane == 2 * TOP_K + k, ranks[k], route)
        route_ref[rows, :] = route

    run_ref[...] = jnp.broadcast_to(run, run_ref.shape)
    cnt_ref[...] = jnp.broadcast_to(run, cnt_ref.shape)


def _merge(x2d, merged, wo, g2, rw_hi, rw_lo, rb, *, tm):
    S, D = x2d.shape
    assert S % tm == 0
    const = lambda i: (0, 0)
    resident = functools.partial(pl.BlockSpec, index_map=const, pipeline_mode=pl.Buffered(1))
    return pl.pallas_call(
        _merge_body,
        grid=(S // tm,),
        in_specs=[
            pl.BlockSpec((tm, D), lambda i: (i, 0)),
            pl.BlockSpec((tm, D), lambda i: (i, 0)),
            resident(wo.shape),
            pl.BlockSpec((1, D), const),
            resident(rw_hi.shape),
            resident(rw_lo.shape),
            pl.BlockSpec((1, LANES), const),
        ],
        out_specs=[
            pl.BlockSpec((tm, D), lambda i: (i, 0)),
            pl.BlockSpec((tm * SLAB_PITCH, LANES), lambda i: (i, 0)),
            pl.BlockSpec((tm, LANES), lambda i: (i, 0)),
            pl.BlockSpec((8, LANES), const),
        ],
        out_shape=[
            jax.ShapeDtypeStruct((S, D), F32),
            jax.ShapeDtypeStruct((S * SLAB_PITCH, LANES), U32),
            jax.ShapeDtypeStruct((S, LANES), F32),
            jax.ShapeDtypeStruct((8, LANES), F32),
        ],
        scratch_shapes=[pltpu.VMEM((8, LANES), F32)],
        compiler_params=_cparams(("arbitrary",)),
        name="merge",
    )(x2d, merged, wo, g2, rw_hi, rw_lo, rb)


def _dispatch_body(clear_ref, dest_ref, h2_ref, xs_hbm, zeros_ref, sem, clear_sem, *, tm, blk_rows, nb):
    i = pl.program_id(0)

    def clear_copy(b):
        return pltpu.make_async_copy(zeros_ref, xs_hbm.at[pl.ds(pl.multiple_of(b * blk_rows, blk_rows), blk_rows)],
                                     clear_sem.at[0])

    @pl.when(i == 0)
    def _():
        zeros_ref[...] = jnp.zeros_like(zeros_ref)

        def start(b, carry):
            @pl.when(clear_ref[b] == 1)
            def _():
                clear_copy(b).start()
            return carry

        def drain(b, carry):
            @pl.when(clear_ref[b] == 1)
            def _():
                clear_copy(b).wait()
            return carry

        lax.fori_loop(0, nb, start, 0)
        lax.fori_loop(0, nb, drain, 0)

    def issue(t, carry):
        for k in range(TOP_K):
            d = dest_ref[0, 0, t * TOP_K + k]
            pltpu.make_async_copy(h2_ref.at[pl.ds(t * SLAB_PITCH, SLAB_PITCH)],
                                  xs_hbm.at[pl.ds(d * SLAB_PITCH, SLAB_PITCH)],
                                  sem.at[0]).start(priority=k % DMA_QUEUES)
        return carry

    lax.fori_loop(0, tm, issue, 0, unroll=4)
    for k in range(TOP_K):
        pltpu.make_async_copy(h2_ref, xs_hbm.at[pl.ds(0, tm * SLAB_PITCH)], sem.at[0]).wait()


def _dispatch(h2_slab, dest, clear_flag, *, tm, tm_e):
    nb = clear_flag.shape[0]
    blk_rows = tm_e * SLAB_PITCH
    n = h2_slab.shape[0] // (tm * SLAB_PITCH)
    grid_spec = pltpu.PrefetchScalarGridSpec(
        num_scalar_prefetch=1,
        grid=(n,),
        in_specs=[
            pl.BlockSpec((1, 1, tm * TOP_K), lambda i, cf: (i, 0, 0), memory_space=pltpu.SMEM),
            pl.BlockSpec((tm * SLAB_PITCH, LANES), lambda i, cf: (i, 0)),
        ],
        out_specs=pl.BlockSpec(memory_space=pl.ANY),
        scratch_shapes=[pltpu.VMEM((blk_rows, LANES), U32), pltpu.SemaphoreType.DMA((1,)),
                        pltpu.SemaphoreType.DMA((1,))],
    )
    return pl.pallas_call(
        functools.partial(_dispatch_body, tm=tm, blk_rows=blk_rows, nb=nb),
        grid_spec=grid_spec,
        out_shape=jax.ShapeDtypeStruct((nb * blk_rows, LANES), U32),
        compiler_params=_cparams(("arbitrary",)),
        name="dispatch",
    )(clear_flag, dest, h2_slab)


def _tile_schedule(nblk, blk_start, nvb, n_tiles, nb):
    n_steps = nb * n_tiles
    steps_per_e = n_tiles * nblk
    e_end = jnp.cumsum(steps_per_e)
    e_start = e_end - steps_per_e
    total = e_end[-1]
    s = jnp.arange(n_steps, dtype=jnp.int32)
    live = s < total
    sl = jnp.minimum(s, total - 1)
    e = jnp.sum(e_end[None, :] <= sl[:, None], axis=1).astype(jnp.int32)
    r = sl - e_start[e]
    nb_e = jnp.maximum(nblk[e], 1)
    tile = r // nb_e
    b = r - tile * nb_e
    blk = blk_start[e] + b
    first = (live & (b == 0)).astype(jnp.int32)
    dead = jnp.maximum(s - total, 0)
    out_blk = jnp.where(live, blk, nvb + dead // n_tiles)
    out_tile = jnp.where(live, tile, dead % n_tiles)
    nxt = jnp.minimum(s - b + nb_e, n_steps - 1)
    has_next = (live & (s - b + nb_e < total)).astype(jnp.int32)
    as_i32 = lambda a: a.astype(jnp.int32)
    return (as_i32(e), as_i32(tile), as_i32(blk), first, as_i32(out_blk), as_i32(out_tile), as_i32(total)[None],
            as_i32(e[nxt]), as_i32(tile[nxt]), has_next)


def _gateup_body(se_ref, st_ref, sb_ref, sf_ref, ob_ref, ot_ref, nl_ref, ne_ref, nt_ref, hn_ref,
                 xs_ref, wgu_hbm, bg_ref, bu_ref, act_ref, stage_ref, wgb_ref, wub_ref, sem, *, tf, d_ff):
    s = pl.program_id(0)
    live = s < nl_ref[0]
    D = stage_ref.shape[1]

    def tile_copies(e, t):
        return [pltpu.make_async_copy(
            wgu_hbm.at[e, pl.ds(0, D), pl.ds(pl.multiple_of(half * d_ff + t * tf, tf), tf)],
            stage_ref.at[half], sem.at[half]) for half in range(2)]

    @pl.when(sf_ref[s] == 1)
    def _():
        @pl.when(s == 0)
        def _():
            for c in tile_copies(se_ref[0], st_ref[0]):
                c.start()

        for c in tile_copies(se_ref[s], st_ref[s]):
            c.wait()
        wgb_ref[...] = stage_ref[0].astype(BF16)
        wub_ref[...] = stage_ref[1].astype(BF16)

        @pl.when(hn_ref[s] == 1)
        def _():
            for c in tile_copies(ne_ref[s], nt_ref[s]):
                c.start()

    @pl.when(live)
    def _():
        xb = _slab_load(xs_ref, act_ref.shape[0]).astype(BF16)
        gate = jnp.dot(xb, wgb_ref[...], preferred_element_type=F32) + bg_ref[0]
        up = jnp.dot(xb, wub_ref[...], preferred_element_type=F32) + bu_ref[0]
        gate = jnp.minimum(gate, SWIGLU_LIMIT)
        up = jnp.clip(up, -SWIGLU_LIMIT, SWIGLU_LIMIT)
        glu = gate * jax.nn.sigmoid(gate * SWIGLU_ALPHA)
        act_ref[...] = ((up + 1.0) * glu).astype(act_ref.dtype)

    @pl.when(jnp.logical_not(live))
    def _():
        act_ref[...] = jnp.zeros_like(act_ref)


def _gateup(xs, sched, wgu, bgu, *, tm, tf):
    D = wgu.shape[1]
    R = xs.shape[0] // SLAB_PITCH
    d_ff = wgu.shape[2] // 2
    nj = d_ff // tf
    nb = R // tm
    assert d_ff % tf == 0 and R % tm == 0
    grid_spec = pltpu.PrefetchScalarGridSpec(
        num_scalar_prefetch=10,
        grid=(nb * nj,),
        in_specs=[
            pl.BlockSpec((tm * SLAB_PITCH, LANES), lambda s, se, st, sb, *_: (sb[s], 0)),
            pl.BlockSpec(memory_space=pl.ANY),
            pl.BlockSpec((1, 1, tf), lambda s, se, st, *_: (se[s], 0, st[s])),
            pl.BlockSpec((1, 1, tf), lambda s, se, st, *_: (se[s], 0, nj + st[s])),
        ],
        out_specs=pl.BlockSpec((tm, tf), lambda s, se, st, sb, sf, ob, ot, *_: (ob[s], ot[s])),
        scratch_shapes=[pltpu.VMEM((2, D, tf), F32), pltpu.VMEM((D, tf), BF16), pltpu.VMEM((D, tf), BF16),
                        pltpu.SemaphoreType.DMA((2,))],
    )
    return pl.pallas_call(
        functools.partial(_gateup_body, tf=tf, d_ff=d_ff),
        grid_spec=grid_spec,
        out_shape=jax.ShapeDtypeStruct((R, d_ff), BF16),
        compiler_params=_cparams(("arbitrary",)),
        name="gateup",
    )(*sched, xs, wgu, bgu, bgu)


def _down_body(se_ref, st_ref, sb_ref, sf_ref, ob_ref, ot_ref, nl_ref, ne_ref, nt_ref, hn_ref,
               act_ref, wd_hbm, bd_ref, y_ref, stage_ref, wdb_ref, sem):
    s = pl.program_id(0)
    live = s < nl_ref[0]

    def weight_copy(e):
        return pltpu.make_async_copy(wd_hbm.at[e], stage_ref, sem.at[0])

    @pl.when(sf_ref[s] == 1)
    def _():
        @pl.when(s == 0)
        def _():
            weight_copy(se_ref[0]).start()

        weight_copy(se_ref[s]).wait()
        wdb_ref[...] = stage_ref[...].astype(BF16)

        @pl.when(hn_ref[s] == 1)
        def _():
            weight_copy(ne_ref[s]).start()

    @pl.when(live)
    def _():
        y = jnp.dot(act_ref[...], wdb_ref[...], preferred_element_type=F32) + bd_ref[0]
        _slab_store(y_ref, y)

    @pl.when(jnp.logical_not(live))
    def _():
        y_ref[...] = jnp.zeros_like(y_ref)


def _down(act, sched, wd, bd, *, tm):
    R, d_ff = act.shape
    D = wd.shape[2]
    nb = R // tm
    grid_spec = pltpu.PrefetchScalarGridSpec(
        num_scalar_prefetch=10,
        grid=(nb,),
        in_specs=[
            pl.BlockSpec((tm, d_ff), lambda s, se, st, sb, *_: (sb[s], 0)),
            pl.BlockSpec(memory_space=pl.ANY),
            pl.BlockSpec((1, 1, D), lambda s, se, *_: (se[s], 0, 0)),
        ],
        out_specs=pl.BlockSpec((tm * SLAB_PITCH, LANES), lambda s, se, st, sb, sf, ob, *_: (ob[s], 0)),
        scratch_shapes=[pltpu.VMEM((d_ff, D), F32), pltpu.VMEM((d_ff, D), BF16), pltpu.SemaphoreType.DMA((1,))],
    )
    return pl.pallas_call(
        _down_body,
        grid_spec=grid_spec,
        out_shape=jax.ShapeDtypeStruct((R * SLAB_PITCH, LANES), U32),
        compiler_params=_cparams(("arbitrary",)),
        name="down",
    )(*sched, act, wd, bd)


def _combine_body(dest_cur_ref, dest_nxt_ref, x2_ref, route_ref, gf_ref, ys_hbm, o_ref, buf_ref, sem, *, tm):
    i = pl.program_id(0)
    slot = i % 2

    def start_rows(dest_ref, dst_slot):
        def issue(t, carry):
            for k in range(TOP_K):
                d = dest_ref[0, 0, t * TOP_K + k]
                pltpu.make_async_copy(ys_hbm.at[pl.ds(d * SLAB_PITCH, SLAB_ROWS)],
                                      buf_ref.at[dst_slot, k, pl.ds(t * SLAB_PITCH, SLAB_ROWS)],
                                      sem.at[dst_slot]).start(priority=k % DMA_QUEUES)
            return carry
        lax.fori_loop(0, tm, issue, 0, unroll=4)

    def wait_rows(dst_slot):
        for k in range(TOP_K):
            pltpu.make_async_copy(ys_hbm.at[pl.ds(0, tm * SLAB_ROWS)],
                                  buf_ref.at[dst_slot, k, pl.ds(0, tm * SLAB_ROWS)], sem.at[dst_slot]).wait()

    @pl.when(i == 0)
    def _():
        start_rows(dest_cur_ref, 0)

    @pl.when(i + 1 < pl.num_programs(0))
    def _():
        start_rows(dest_nxt_ref, 1 - slot)

    wait_rows(slot)
    route = route_ref[...]
    x3 = x2_ref[...]
    for k in range(TOP_K):
        x3 = x3 + route[:, TOP_K + k:TOP_K + k + 1] * _slab_load(buf_ref.at[slot, k], tm)
    ms = jnp.mean(x3 * x3, axis=-1, keepdims=True)
    o_ref[...] = x3 * lax.rsqrt(ms + EPS) * gf_ref[...]


def _combine(dest, x2, route, gf, ys_slab, *, tm):
    T, D = x2.shape
    n = T // tm
    assert T % tm == 0
    return pl.pallas_call(
        functools.partial(_combine_body, tm=tm),
        grid=(n,),
        in_specs=[
            pl.BlockSpec((1, 1, tm * TOP_K), lambda i: (i, 0, 0), memory_space=pltpu.SMEM),
            pl.BlockSpec((1, 1, tm * TOP_K), lambda i: (jnp.minimum(i + 1, n - 1), 0, 0), memory_space=pltpu.SMEM),
            pl.BlockSpec((tm, D), lambda i: (i, 0)),
            pl.BlockSpec((tm, LANES), lambda i: (i, 0)),
            pl.BlockSpec((1, D), lambda i: (0, 0)),
            pl.BlockSpec(memory_space=pl.ANY),
        ],
        out_specs=pl.BlockSpec((tm, D), lambda i: (i, 0)),
        out_shape=jax.ShapeDtypeStruct((T, D), F32),
        scratch_shapes=[pltpu.VMEM((2, TOP_K, tm * SLAB_PITCH, LANES), U32), pltpu.SemaphoreType.DMA((2,))],
        compiler_params=_cparams(("arbitrary",)),
        name="combine",
    )(dest, dest, x2, route, gf, ys_slab)


def _retention_tables(S, dk):
    pos = jnp.arange(S, dtype=F32)
    inv_freq = ROPE_BASE ** (-jnp.arange(0, dk, 2, dtype=F32) / dk)
    ang = pos[:, None] * inv_freq[None, :]
    cos, sin = jnp.cos(ang), jnp.sin(ang)
    cos2 = jnp.concatenate([cos, cos], axis=-1)
    sin2 = jnp.concatenate([-sin, sin], axis=-1)
    log_gamma = jnp.log1p(-jnp.exp2(-5.0 - jnp.arange(RET_HEADS, dtype=F32)))
    idx = jnp.arange(CHUNK, dtype=F32)
    diff = idx[:, None] - idx[None, :]
    decay = jnp.where(diff[None] >= 0, jnp.exp(log_gamma[:, None, None] * jnp.maximum(diff, 0.0)[None]), 0.0)
    xi = jnp.exp(log_gamma[:, None] * (idx[None, :] + 1.0))
    zeta = jnp.exp(log_gamma[:, None] * (CHUNK - 1.0 - idx[None, :]))
    cd = jnp.exp(log_gamma * CHUNK)
    xi_b = jnp.broadcast_to(xi[:, :, None], (RET_HEADS, CHUNK, LANES))
    zeta_b = jnp.broadcast_to(zeta[:, :, None], (RET_HEADS, CHUNK, dk))
    return cos2, sin2, decay, xi_b, zeta_b, cd


TM_INPROJ = 1024
TN_INPROJ = 1024
TM_MIXING = 256
TM_MERGE = 512
MERGE_SUB_ROWS = 512
TM_EXPERT = 512
TF_EXPERT = 1024
TM_COMBINE = 256


def kernel(x, norm1_g, w_in, gm_ln_g, gm_ln_b, gm_ws, gm_b, w_proj_a, w_proj_r, w_out, norm2_g, router_w, router_b,
           w_gate_up, b_gate_up, w_down, b_down, final_norm_g):
    B, S, D = x.shape
    assert B == 1 and norm1_g.shape[0] == 1, "single sequence, depth 1"
    gm_width = w_proj_a.shape[1]
    v_width = w_proj_r.shape[1]
    qk_width = (w_in.shape[2] - 2 * gm_width - 2 * v_width - 2 * D) // 2
    dk = qk_width // RET_HEADS
    assert dk == LANES and gm_ws.shape[2] == CHUNK and D == 2 * SLAB_ROWS * LANES
    G = gm_ws.shape[1]
    E = router_w.shape[2]
    d_ff = w_down.shape[2]
    x2d = x.reshape(S, D)

    cos2, sin2, decay, xi_b, zeta_b, cd = _retention_tables(S, dk)
    tm_p = min(TM_INPROJ, S)
    h = _prenorm(x2d, norm1_g[0][None], tm=tm_p)
    w_in_bf = w_in[0].astype(BF16)
    segments = (
        (gm_width, "gelu", ()),
        (gm_width, "gelu_layernorm", (gm_ln_g[0][None], gm_ln_b[0][None])),
        (qk_width, "rope", (cos2, sin2)),
        (qk_width, "rope_scaled", (cos2, sin2)),
        (v_width, "identity", ()),
        (v_width, "silu", ()),
        (2 * D, "sigmoid", ()),
    )
    outs, col0 = [], 0
    for width, kind, extras in segments:
        outs.append(_segment(h, w_in_bf, col0, width, kind, extras, tm=tm_p, tn=TN_INPROJ))
        col0 += width
    a_u, a_vn, r_q, r_k, r_v, r_sg, gates = outs

    causal = jnp.tril(jnp.ones((CHUNK, CHUNK), dtype=bool))
    wm = jnp.where(causal[None], gm_ws[0], 0.0).astype(BF16)
    bs_b = jnp.broadcast_to(gm_b[0][:, :, None], (G, CHUNK, GM_GROUP_DIM))
    ga, gr = _mixing(a_u, a_vn, r_q, r_k, r_v, r_sg, wm, bs_b, decay, xi_b, zeta_b, cd, tm=min(TM_MIXING, S))

    rw = jnp.pad(router_w[0], ((0, 0), (0, LANES - E)))
    rw_hi = rw.astype(BF16)
    rw_lo = (rw - rw_hi.astype(F32)).astype(BF16)
    rb = jnp.pad(router_b[0], (0, LANES - E))[None]
    tm_m = min(TM_MERGE, S)
    merged = _branches(ga, gr, gates, w_proj_a[0].astype(BF16), w_proj_r[0].astype(BF16), tm=tm_m)
    x2, h2, route, cnt = _merge(x2d, merged, w_out[0].astype(BF16), norm2_g[0][None], rw_hi, rw_lo, rb, tm=tm_m)

    tm_e = TM_EXPERT
    eidx = route[:, 0:TOP_K].astype(jnp.int32)
    rank = route[:, 2 * TOP_K:3 * TOP_K].astype(jnp.int32)
    counts = cnt[0, :E].astype(jnp.int32)
    nblk = (counts + tm_e - 1) // tm_e
    blk_end = jnp.cumsum(nblk)
    blk_start = blk_end - nblk
    dest = (blk_start * tm_e)[eidx] + rank
    nb = (S * TOP_K + tm_e - 1) // tm_e + E
    nvb = blk_end[-1].astype(jnp.int32)
    blk_ids = jnp.arange(nb, dtype=jnp.int32)
    is_last = jnp.any((blk_ids[:, None] == blk_end[None, :] - 1) & (nblk[None, :] > 0), axis=1)
    clear_flag = (is_last | (blk_ids >= nvb)).astype(jnp.int32)

    tm_c = min(TM_COMBINE, S)
    dest_blocks = dest.reshape(S // tm_c, 1, tm_c * TOP_K)
    xs = _dispatch(h2, dest_blocks, clear_flag, tm=tm_c, tm_e=tm_e)
    tf = min(TF_EXPERT, d_ff)
    act = _gateup(xs, _tile_schedule(nblk, blk_start, nvb, d_ff // tf, nb), w_gate_up[0], b_gate_up[0][:, None, :],
                  tm=tm_e, tf=tf)
    ys = _down(act, _tile_schedule(nblk, blk_start, nvb, 1, nb), w_down[0], b_down[0][:, None, :], tm=tm_e)

    out = _combine(dest_blocks, x2, route, final_norm_g[None], ys, tm=tm_c)
    return out.reshape(B, S, D)
```

```python
import functools

import jax
import jax.numpy as jnp
from jax import lax
from jax.experimental import pallas as pl
from jax.experimental.pallas import tpu as pltpu

F32 = jnp.float32
BF16 = jnp.bfloat16
U32 = jnp.uint32

CHUNK = 128
GM_GROUP_DIM = 128
RET_HEADS = 8
N_EXPERTS = 32
TOP_K = 4
SWIGLU_LIMIT = 7.0
SWIGLU_ALPHA = 1.702
ROPE_BASE = 10000.0
EPS = 1e-6
LANES = 128

VMEM_LIMIT_BYTES = 56 * 1024 * 1024


def _cparams(sem):
    return pltpu.CompilerParams(dimension_semantics=sem, vmem_limit_bytes=VMEM_LIMIT_BYTES)


SLAB_ROWS = 8
SLAB_PITCH = 9
DMA_QUEUES = 2


def _slab_store(ref, value, row0=0):
    rows, width = value.shape
    half = width // 2
    assert half == SLAB_ROWS * LANES
    lo = lax.bitcast_convert_type(value[:, :half].astype(BF16).astype(F32), U32)
    hi = lax.bitcast_convert_type(value[:, half:].astype(BF16).astype(F32), U32)
    words = (lo >> 16) | hi
    base = row0 * SLAB_PITCH
    for c in range(SLAB_ROWS):
        ref[pl.ds(base + c, rows, stride=SLAB_PITCH), :] = words[:, c * LANES:(c + 1) * LANES]
    for c in range(SLAB_ROWS, SLAB_PITCH):
        ref[pl.ds(base + c, rows, stride=SLAB_PITCH), :] = jnp.zeros((rows, LANES), U32)


def _slab_load(ref, rows):
    words = jnp.concatenate([ref[pl.ds(c, rows, stride=SLAB_PITCH), :] for c in range(SLAB_ROWS)], axis=1)
    lo = lax.bitcast_convert_type(words << 16, F32)
    hi = lax.bitcast_convert_type(words & jnp.uint32(0xFFFF0000), F32)
    return jnp.concatenate([lo, hi], axis=1)


def _gelu_exact(a):
    return 0.5 * a * (1.0 + lax.erf(a * (2.0 ** -0.5)))


def _prenorm_body(x_ref, g_ref, h_ref):
    xf = x_ref[...]
    ms = jnp.mean(xf * xf, axis=-1, keepdims=True)
    h_ref[...] = (xf * lax.rsqrt(ms + EPS) * g_ref[...]).astype(h_ref.dtype)


def _prenorm(x2d, g, *, tm):
    S, D = x2d.shape
    return pl.pallas_call(
        _prenorm_body,
        grid=(S // tm,),
        in_specs=[pl.BlockSpec((tm, D), lambda i: (i, 0)), pl.BlockSpec((1, D), lambda i: (0, 0))],
        out_specs=pl.BlockSpec((tm, D), lambda i: (i, 0)),
        out_shape=jax.ShapeDtypeStruct((S, D), BF16),
        compiler_params=_cparams(("arbitrary",)),
        name="prenorm",
    )(x2d, g)


def _rope_heads(a, cos2, sin2):
    outs = []
    for hd in range(a.shape[1] // LANES):
        ah = a[:, hd * LANES:(hd + 1) * LANES]
        outs.append(ah * cos2 + pltpu.roll(ah, LANES // 2, axis=1) * sin2)
    return jnp.concatenate(outs, axis=1)


def _segment_body(h_ref, w_ref, *refs, kind, k_scale):
    o_ref = refs[-1]
    acc = jnp.dot(h_ref[...], w_ref[...], preferred_element_type=F32)
    if kind == "gelu":
        out = _gelu_exact(acc)
    elif kind == "gelu_layernorm":
        lng_ref, lnb_ref = refs[0], refs[1]
        vf = _gelu_exact(acc)
        mu = jnp.mean(vf, axis=-1, keepdims=True)
        vc = vf - mu
        var = jnp.mean(vc * vc, axis=-1, keepdims=True)
        out = vc * lax.rsqrt(var + EPS) * lng_ref[...] + lnb_ref[...]
    elif kind == "rope":
        out = _rope_heads(acc, refs[0][...], refs[1][...])
    elif kind == "rope_scaled":
        out = _rope_heads(acc, refs[0][...], refs[1][...]) * k_scale
    elif kind == "identity":
        out = acc
    elif kind == "silu":
        out = acc * jax.nn.sigmoid(acc)
    elif kind == "sigmoid":
        out = jax.nn.sigmoid(acc)
    else:
        raise ValueError(kind)
    o_ref[...] = out.astype(o_ref.dtype)


def _segment(h, w_in_bf, col0, width, kind, extras=(), *, tm, tn):
    S, D = h.shape
    assert width % tn == 0 and col0 % tn == 0 and S % tm == 0
    blk0 = col0 // tn
    if kind == "gelu_layernorm":
        assert width == tn
        extra_specs = [pl.BlockSpec((1, tn), lambda j, i: (0, 0))] * 2
    elif kind in ("rope", "rope_scaled"):
        extra_specs = [pl.BlockSpec((tm, LANES), lambda j, i: (i, 0))] * 2
    else:
        extra_specs = []
    return pl.pallas_call(
        functools.partial(_segment_body, kind=kind, k_scale=float(LANES) ** -0.5),
        grid=(width // tn, S // tm),
        in_specs=[
            pl.BlockSpec((tm, D), lambda j, i: (i, 0)),
            pl.BlockSpec((D, tn), lambda j, i: (0, blk0 + j)),
        ] + extra_specs,
        out_specs=pl.BlockSpec((tm, tn), lambda j, i: (i, j)),
        out_shape=jax.ShapeDtypeStruct((S, width), BF16),
        compiler_params=_cparams(("arbitrary", "arbitrary")),
        name="inproj_" + kind,
    )(h, w_in_bf, *extras)


def _mixing_body(cd_ref, u_ref, vn_ref, q_ref, k_ref, v_ref, sg_ref, wm_ref, bs_ref, decay_ref, xi_ref, zeta_ref,
                 ga_ref, gr_ref, state_ref, *, n_chunks, dk, dv):
    @pl.when(pl.program_id(0) == 0)
    def _():
        state_ref[...] = jnp.zeros_like(state_ref)

    def chunk(c, carry):
        rows = pl.ds(pl.multiple_of(c * CHUNK, CHUNK), CHUNK)
        for g in range(wm_ref.shape[0]):
            cols = slice(g * GM_GROUP_DIM, (g + 1) * GM_GROUP_DIM)
            mixed = jnp.dot(wm_ref[g], vn_ref[rows, cols], preferred_element_type=F32) + bs_ref[g]
            ga_ref[rows, cols] = (u_ref[rows, cols].astype(F32) * mixed).astype(ga_ref.dtype)
        for hd in range(RET_HEADS):
            qc = q_ref[rows, hd * dk:(hd + 1) * dk]
            kc = k_ref[rows, hd * dk:(hd + 1) * dk]
            vc = v_ref[rows, hd * dv:(hd + 1) * dv]
            st = state_ref[hd]
            scores = lax.dot_general(qc, kc, (((1,), (1,)), ((), ())), preferred_element_type=F32) * decay_ref[hd]
            inner = jnp.dot(scores.astype(BF16), vc, preferred_element_type=F32)
            cross = jnp.dot(qc, st.astype(BF16), preferred_element_type=F32)
            xi = xi_ref[hd]
            o = inner + cross * jnp.concatenate([xi] * (dv // LANES), axis=1)
            kz = (kc.astype(F32) * zeta_ref[hd]).astype(BF16)
            kv = lax.dot_general(kz, vc, (((0,), (0,)), ((), ())), preferred_element_type=F32)
            state_ref[hd] = st * cd_ref[hd] + kv
            mu = jnp.mean(o, axis=-1, keepdims=True)
            oc = o - mu
            var = jnp.mean(oc * oc, axis=-1, keepdims=True)
            on = oc * lax.rsqrt(var + EPS)
            gr_ref[rows, hd * dv:(hd + 1) * dv] = (on * sg_ref[rows, hd * dv:(hd + 1) * dv].astype(F32)).astype(
                gr_ref.dtype)
        return carry

    lax.fori_loop(0, n_chunks, chunk, 0, unroll=True)


def _mixing(u, vn, q, k, v, sg, wm, bs_b, decay, xi_b, zeta_b, cd, *, tm):
    S, gm_width = u.shape
    qk_width, v_width = q.shape[1], v.shape[1]
    assert S % tm == 0 and tm % CHUNK == 0
    dk, dv = qk_width // RET_HEADS, v_width // RET_HEADS
    G = wm.shape[0]
    body = functools.partial(_mixing_body, n_chunks=tm // CHUNK, dk=dk, dv=dv)
    const3 = lambda i: (0, 0, 0)
    rows = lambda i: (i, 0)
    return pl.pallas_call(
        body,
        grid=(S // tm,),
        in_specs=[
            pl.BlockSpec(memory_space=pltpu.SMEM),
            pl.BlockSpec((tm, gm_width), rows),
            pl.BlockSpec((tm, gm_width), rows),
            pl.BlockSpec((tm, qk_width), rows),
            pl.BlockSpec((tm, qk_width), rows),
            pl.BlockSpec((tm, v_width), rows),
            pl.BlockSpec((tm, v_width), rows),
            pl.BlockSpec((G, CHUNK, CHUNK), const3),
            pl.BlockSpec((G, CHUNK, GM_GROUP_DIM), const3),
            pl.BlockSpec((RET_HEADS, CHUNK, CHUNK), const3),
            pl.BlockSpec((RET_HEADS, CHUNK, LANES), const3),
            pl.BlockSpec((RET_HEADS, CHUNK, dk), const3),
        ],
        out_specs=[
            pl.BlockSpec((tm, gm_width), lambda i: (i, 0)),
            pl.BlockSpec((tm, v_width), lambda i: (i, 0)),
        ],
        out_shape=[
            jax.ShapeDtypeStruct((S, gm_width), BF16),
            jax.ShapeDtypeStruct((S, v_width), BF16),
        ],
        scratch_shapes=[pltpu.VMEM((RET_HEADS, dk, dv), F32)],
        compiler_params=_cparams(("arbitrary",)),
        name="mixing",
    )(cd, u, vn, q, k, v, sg, wm, bs_b, decay, xi_b, zeta_b)


def _branches_body(ga_ref, gr_ref, sa_ref, sr_ref, wa_ref, wr_ref, o_ref):
    y_a = jnp.dot(ga_ref[...], wa_ref[...], preferred_element_type=F32)
    y_r = jnp.dot(gr_ref[...], wr_ref[...], preferred_element_type=F32)
    o_ref[...] = (sa_ref[...].astype(F32) * y_a + sr_ref[...].astype(F32) * y_r).astype(o_ref.dtype)


def _branches(ga, gr, gates, wa, wr, *, tm):
    S = ga.shape[0]
    D = wa.shape[1]
    const = lambda i: (0, 0)
    resident = functools.partial(pl.BlockSpec, index_map=const, pipeline_mode=pl.Buffered(1))
    return pl.pallas_call(
        _branches_body,
        grid=(S // tm,),
        in_specs=[
            pl.BlockSpec((tm, ga.shape[1]), lambda i: (i, 0)),
            pl.BlockSpec((tm, gr.shape[1]), lambda i: (i, 0)),
            pl.BlockSpec((tm, D), lambda i: (i, 0)),
            pl.BlockSpec((tm, D), lambda i: (i, 1)),
            resident(wa.shape),
            resident(wr.shape),
        ],
        out_specs=pl.BlockSpec((tm, D), lambda i: (i, 0)),
        out_shape=jax.ShapeDtypeStruct((S, D), BF16),
        compiler_params=_cparams(("arbitrary",)),
        name="branches",
    )(ga, gr, gates, gates, wa, wr)


def _merge_body(x_ref, m_ref, wo_ref, g2_ref, rwh_ref, rwl_ref, rb_ref,
                x2_ref, h2_ref, route_ref, cnt_ref, run_ref):
    i = pl.program_id(0)
    tm = x_ref.shape[0]

    @pl.when(i == 0)
    def _():
        run_ref[...] = jnp.zeros_like(run_ref)

    sub = min(tm, MERGE_SUB_ROWS)
    lane = lax.broadcasted_iota(jnp.int32, (sub, LANES), 1)
    neg_inf = jnp.float32(-jnp.inf)
    r_iota = lax.broadcasted_iota(jnp.int32, (sub, sub), 0)
    c_iota = lax.broadcasted_iota(jnp.int32, (sub, sub), 1)
    strict_lower = (c_iota < r_iota).astype(BF16)
    run = run_ref[0:1, :]
    for j in range(tm // sub):
        rows = slice(j * sub, (j + 1) * sub)
        x2 = x_ref[rows, :] + jnp.dot(m_ref[rows, :], wo_ref[...], preferred_element_type=F32)
        x2_ref[rows, :] = x2
        ms = jnp.mean(x2 * x2, axis=-1, keepdims=True)
        h2 = x2 * lax.rsqrt(ms + EPS) * g2_ref[...]
        _slab_store(h2_ref, h2, row0=j * sub)

        h_hi = h2.astype(BF16)
        h_lo = (h2 - h_hi.astype(F32)).astype(BF16)
        logits = (jnp.dot(h_hi, rwh_ref[...], preferred_element_type=F32)
                  + jnp.dot(h_hi, rwl_ref[...], preferred_element_type=F32)
                  + jnp.dot(h_lo, rwh_ref[...], preferred_element_type=F32)) + rb_ref[...]
        work = jnp.where(lane < N_EXPERTS, logits, neg_inf)

        vals, idxs, sels = [], [], []
        for _ in range(TOP_K):
            m = jnp.max(work, axis=-1, keepdims=True)
            idx = jnp.min(jnp.where(work == m, lane, LANES), axis=-1, keepdims=True)
            sel = lane == idx
            vals.append(m)
            idxs.append(idx)
            sels.append(sel)
            work = jnp.where(sel, neg_inf, work)
        exps = [jnp.exp(v - vals[0]) for v in vals]
        denom = exps[0] + exps[1] + exps[2] + exps[3]
        gates = [e / denom for e in exps]

        onehot = jnp.zeros((sub, LANES), F32)
        for sel in sels:
            onehot = onehot + sel.astype(F32)
        before = jnp.dot(strict_lower, onehot.astype(BF16), preferred_element_type=F32) + run
        ranks = [jnp.sum(jnp.where(sel, before, 0.0), axis=-1, keepdims=True) for sel in sels]
        run = run + jnp.sum(onehot, axis=0, keepdims=True)

        route = jnp.zeros((sub, LANES), F32)
        for k in range(TOP_K):
            route = jnp.where(lane == k, idxs[k].astype(F32), route)
            route = jnp.where(lane == TOP_K + k, gates[k], route)
            route = jnp.where(lane == 2 * TOP_K + k, ranks[k], route)
        route_ref[rows, :] = route

    run_ref[...] = jnp.broadcast_to(run, run_ref.shape)
    cnt_ref[...] = jnp.broadcast_to(run, cnt_ref.shape)


def _merge(x2d, merged, wo, g2, rw_hi, rw_lo, rb, *, tm):
    S, D = x2d.shape
    assert S % tm == 0
    const = lambda i: (0, 0)
    resident = functools.partial(pl.BlockSpec, index_map=const, pipeline_mode=pl.Buffered(1))
    return pl.pallas_call(
        _merge_body,
        grid=(S // tm,),
        in_specs=[
            pl.BlockSpec((tm, D), lambda i: (i, 0)),
            pl.BlockSpec((tm, D), lambda i: (i, 0)),
            resident(wo.shape),
            pl.BlockSpec((1, D), const),
            resident(rw_hi.shape),
            resident(rw_lo.shape),
            pl.BlockSpec((1, LANES), const),
        ],
        out_specs=[
            pl.BlockSpec((tm, D), lambda i: (i, 0)),
            pl.BlockSpec((tm * SLAB_PITCH, LANES), lambda i: (i, 0)),
            pl.BlockSpec((tm, LANES), lambda i: (i, 0)),
            pl.BlockSpec((8, LANES), const),
        ],
        out_shape=[
            jax.ShapeDtypeStruct((S, D), F32),
            jax.ShapeDtypeStruct((S * SLAB_PITCH, LANES), U32),
            jax.ShapeDtypeStruct((S, LANES), F32),
            jax.ShapeDtypeStruct((8, LANES), F32),
        ],
        scratch_shapes=[pltpu.VMEM((8, LANES), F32)],
        compiler_params=_cparams(("arbitrary",)),
        name="merge",
    )(x2d, merged, wo, g2, rw_hi, rw_lo, rb)


def _dispatch_body(clear_ref, dest_ref, h2_ref, xs_hbm, zeros_ref, sem, clear_sem, *, tm, blk_rows, nb):
    i = pl.program_id(0)

    def clear_copy(b):
        return pltpu.make_async_copy(zeros_ref, xs_hbm.at[pl.ds(pl.multiple_of(b * blk_rows, blk_rows), blk_rows)],
                                     clear_sem.at[0])

    @pl.when(i == 0)
    def _():
        zeros_ref[...] = jnp.zeros_like(zeros_ref)

        def start(b, carry):
            @pl.when(clear_ref[b] == 1)
            def _():
                clear_copy(b).start()
            return carry

        def drain(b, carry):
            @pl.when(clear_ref[b] == 1)
            def _():
                clear_copy(b).wait()
            return carry

        lax.fori_loop(0, nb, start, 0)
        lax.fori_loop(0, nb, drain, 0)

    def issue(t, carry):
        for k in range(TOP_K):
            d = dest_ref[0, 0, t * TOP_K + k]
            pltpu.make_async_copy(h2_ref.at[pl.ds(t * SLAB_PITCH, SLAB_PITCH)],
                                  xs_hbm.at[pl.ds(d * SLAB_PITCH, SLAB_PITCH)],
                                  sem.at[0]).start(priority=k % DMA_QUEUES)
        return carry

    lax.fori_loop(0, tm, issue, 0, unroll=4)
    for k in range(TOP_K):
        pltpu.make_async_copy(h2_ref, xs_hbm.at[pl.ds(0, tm * SLAB_PITCH)], sem.at[0]).wait()


def _dispatch(h2_slab, dest, clear_flag, *, tm, tm_e):
    nb = clear_flag.shape[0]
    blk_rows = tm_e * SLAB_PITCH
    n = h2_slab.shape[0] // (tm * SLAB_PITCH)
    grid_spec = pltpu.PrefetchScalarGridSpec(
        num_scalar_prefetch=1,
        grid=(n,),
        in_specs=[
            pl.BlockSpec((1, 1, tm * TOP_K), lambda i, cf: (i, 0, 0), memory_space=pltpu.SMEM),
            pl.BlockSpec((tm * SLAB_PITCH, LANES), lambda i, cf: (i, 0)),
        ],
        out_specs=pl.BlockSpec(memory_space=pl.ANY),
        scratch_shapes=[pltpu.VMEM((blk_rows, LANES), U32), pltpu.SemaphoreType.DMA((1,)),
                        pltpu.SemaphoreType.DMA((1,))],
    )
    return pl.pallas_call(
        functools.partial(_dispatch_body, tm=tm, blk_rows=blk_rows, nb=nb),
        grid_spec=grid_spec,
        out_shape=jax.ShapeDtypeStruct((nb * blk_rows, LANES), U32),
        compiler_params=_cparams(("arbitrary",)),
        name="dispatch",
    )(clear_flag, dest, h2_slab)


def _tile_schedule(counts, nblk, blk_start, nvb, n_tiles, nb, tm):
    E = nblk.shape[0]
    n_steps = nb * n_tiles
    steps_per_e = n_tiles * nblk
    e_end = jnp.cumsum(steps_per_e)
    e_start = e_end - steps_per_e
    total = e_end[-1]
    e_ids = jnp.arange(E, dtype=jnp.int32)

    def locate(idx):
        inside = (e_start[None, :] <= idx[:, None]) & (idx[:, None] < e_end[None, :])
        pick = lambda v: jnp.sum(jnp.where(inside, v[None, :], 0), axis=1)
        r = idx - pick(e_start)
        nb_e = jnp.maximum(pick(nblk), 1)
        tile = sum((r >= t * nb_e).astype(jnp.int32) for t in range(1, n_tiles)) if n_tiles > 1 else 0 * r
        return pick(e_ids), tile, r - tile * nb_e, nb_e, pick(blk_start), pick(counts)

    s = jnp.arange(n_steps, dtype=jnp.int32)
    live = s < total
    sl = jnp.minimum(s, total - 1)
    e, tile, b, nb_e, bs_e, cnt_e = locate(sl)
    blk = bs_e + b
    first = (live & (b == 0)).astype(jnp.int32)
    dead = jnp.maximum(s - total, 0)
    out_blk = jnp.where(live, blk, nvb + dead // n_tiles)
    out_tile = jnp.where(live, tile, dead % n_tiles)
    pieces = (jnp.clip(cnt_e - b * tm, 0, tm) + EXPERT_ROW_STEP - 1) // EXPERT_ROW_STEP
    pieces = jnp.where(live, pieces, 0)
    nxt = sl - b + nb_e
    has_next = (live & (nxt < total)).astype(jnp.int32)
    ne, nt, _, _, _, _ = locate(jnp.minimum(nxt, total - 1))
    as_i32 = lambda a: a.astype(jnp.int32)
    return (as_i32(e), as_i32(tile), as_i32(blk), first, as_i32(out_blk), as_i32(out_tile), as_i32(total)[None],
            as_i32(ne), as_i32(nt), has_next, as_i32(pieces))


def _gateup_body(se_ref, st_ref, sb_ref, sf_ref, ob_ref, ot_ref, nl_ref, ne_ref, nt_ref, hn_ref, np_ref,
                 xs_ref, wgu_hbm, bg_ref, bu_ref, act_ref, stage_ref, wgb_ref, wub_ref, sem, *, tf, d_ff):
    s = pl.program_id(0)
    D = stage_ref.shape[1]
    tm = act_ref.shape[0]

    def tile_copies(e, t):
        return [pltpu.make_async_copy(
            wgu_hbm.at[e, pl.ds(0, D), pl.ds(pl.multiple_of(half * d_ff + t * tf, tf), tf)],
            stage_ref.at[half], sem.at[half]) for half in range(2)]

    @pl.when(sf_ref[s] == 1)
    def _():
        @pl.when(s == 0)
        def _():
            for c in tile_copies(se_ref[0], st_ref[0]):
                c.start()

        for c in tile_copies(se_ref[s], st_ref[s]):
            c.wait()
        wgb_ref[...] = stage_ref[0].astype(BF16)
        wub_ref[...] = stage_ref[1].astype(BF16)

        @pl.when(hn_ref[s] == 1)
        def _():
            for c in tile_copies(ne_ref[s], nt_ref[s]):
                c.start()

    for pieces in range(1, tm // EXPERT_ROW_STEP + 1):
        @pl.when(np_ref[s] == pieces)
        def _(m=pieces * EXPERT_ROW_STEP):
            xb = _slab_load(xs_ref, m).astype(BF16)
            gate = jnp.dot(xb, wgb_ref[...], preferred_element_type=F32) + bg_ref[0]
            up = jnp.dot(xb, wub_ref[...], preferred_element_type=F32) + bu_ref[0]
            gate = jnp.minimum(gate, SWIGLU_LIMIT)
            up = jnp.clip(up, -SWIGLU_LIMIT, SWIGLU_LIMIT)
            glu = gate * jax.nn.sigmoid(gate * SWIGLU_ALPHA)
            act_ref[0:m, :] = ((up + 1.0) * glu).astype(act_ref.dtype)
            if m < tm:
                act_ref[m:tm, :] = jnp.zeros((tm - m, act_ref.shape[1]), act_ref.dtype)

    @pl.when(np_ref[s] == 0)
    def _():
        act_ref[...] = jnp.zeros_like(act_ref)


def _gateup(xs, sched, wgu, bgu, *, tm, tf):
    D = wgu.shape[1]
    R = xs.shape[0] // SLAB_PITCH
    d_ff = wgu.shape[2] // 2
    nj = d_ff // tf
    nb = R // tm
    assert d_ff % tf == 0 and R % tm == 0
    grid_spec = pltpu.PrefetchScalarGridSpec(
        num_scalar_prefetch=11,
        grid=(nb * nj,),
        in_specs=[
            pl.BlockSpec((tm * SLAB_PITCH, LANES), lambda s, se, st, sb, *_: (sb[s], 0)),
            pl.BlockSpec(memory_space=pl.ANY),
            pl.BlockSpec((1, 1, tf), lambda s, se, st, *_: (se[s], 0, st[s])),
            pl.BlockSpec((1, 1, tf), lambda s, se, st, *_: (se[s], 0, nj + st[s])),
        ],
        out_specs=pl.BlockSpec((tm, tf), lambda s, se, st, sb, sf, ob, ot, *_: (ob[s], ot[s])),
        scratch_shapes=[pltpu.VMEM((2, D, tf), F32), pltpu.VMEM((D, tf), BF16), pltpu.VMEM((D, tf), BF16),
                        pltpu.SemaphoreType.DMA((2,))],
    )
    return pl.pallas_call(
        functools.partial(_gateup_body, tf=tf, d_ff=d_ff),
        grid_spec=grid_spec,
        out_shape=jax.ShapeDtypeStruct((R, d_ff), BF16),
        compiler_params=_cparams(("arbitrary",)),
        name="gateup",
    )(*sched, xs, wgu, bgu, bgu)


def _down_body(se_ref, st_ref, sb_ref, sf_ref, ob_ref, ot_ref, nl_ref, ne_ref, nt_ref, hn_ref, np_ref,
               act_ref, wd_hbm, bd_ref, y_ref, stage_ref, wdb_ref, sem):
    s = pl.program_id(0)
    tm = act_ref.shape[0]

    def weight_copy(e):
        return pltpu.make_async_copy(wd_hbm.at[e], stage_ref, sem.at[0])

    @pl.when(sf_ref[s] == 1)
    def _():
        @pl.when(s == 0)
        def _():
            weight_copy(se_ref[0]).start()

        weight_copy(se_ref[s]).wait()
        wdb_ref[...] = stage_ref[...].astype(BF16)

        @pl.when(hn_ref[s] == 1)
        def _():
            weight_copy(ne_ref[s]).start()

    for pieces in range(1, tm // EXPERT_ROW_STEP + 1):
        @pl.when(np_ref[s] == pieces)
        def _(m=pieces * EXPERT_ROW_STEP):
            y = jnp.dot(act_ref[0:m, :], wdb_ref[...], preferred_element_type=F32) + bd_ref[0]
            _slab_store(y_ref, y)
            if m < tm:
                y_ref[m * SLAB_PITCH:tm * SLAB_PITCH, :] = jnp.zeros(((tm - m) * SLAB_PITCH, LANES), U32)

    @pl.when(np_ref[s] == 0)
    def _():
        y_ref[...] = jnp.zeros_like(y_ref)


def _down(act, sched, wd, bd, *, tm):
    R, d_ff = act.shape
    D = wd.shape[2]
    nb = R // tm
    grid_spec = pltpu.PrefetchScalarGridSpec(
        num_scalar_prefetch=11,
        grid=(nb,),
        in_specs=[
            pl.BlockSpec((tm, d_ff), lambda s, se, st, sb, *_: (sb[s], 0)),
            pl.BlockSpec(memory_space=pl.ANY),
            pl.BlockSpec((1, 1, D), lambda s, se, *_: (se[s], 0, 0)),
        ],
        out_specs=pl.BlockSpec((tm * SLAB_PITCH, LANES), lambda s, se, st, sb, sf, ob, *_: (ob[s], 0)),
        scratch_shapes=[pltpu.VMEM((d_ff, D), F32), pltpu.VMEM((d_ff, D), BF16), pltpu.SemaphoreType.DMA((1,))],
    )
    return pl.pallas_call(
        _down_body,
        grid_spec=grid_spec,
        out_shape=jax.ShapeDtypeStruct((R * SLAB_PITCH, LANES), U32),
        compiler_params=_cparams(("arbitrary",)),
        name="down",
    )(*sched, act, wd, bd)


def _combine_body(dest_cur_ref, dest_nxt_ref, x2_ref, route_ref, gf_ref, ys_hbm, o_ref, buf_ref, sem, *, tm):
    i = pl.program_id(0)
    slot = i % 2

    def start_rows(dest_ref, dst_slot):
        def issue(t, carry):
            for k in range(TOP_K):
                d = dest_ref[0, 0, t * TOP_K + k]
                pltpu.make_async_copy(ys_hbm.at[pl.ds(d * SLAB_PITCH, SLAB_ROWS)],
                                      buf_ref.at[dst_slot, k, pl.ds(t * SLAB_PITCH, SLAB_ROWS)],
                                      sem.at[dst_slot]).start(priority=k % DMA_QUEUES)
            return carry
        lax.fori_loop(0, tm, issue, 0, unroll=4)

    def wait_rows(dst_slot):
        for k in range(TOP_K):
            pltpu.make_async_copy(ys_hbm.at[pl.ds(0, tm * SLAB_ROWS)],
                                  buf_ref.at[dst_slot, k, pl.ds(0, tm * SLAB_ROWS)], sem.at[dst_slot]).wait()

    @pl.when(i == 0)
    def _():
        start_rows(dest_cur_ref, 0)

    @pl.when(i + 1 < pl.num_programs(0))
    def _():
        start_rows(dest_nxt_ref, 1 - slot)

    wait_rows(slot)
    route = route_ref[...]
    x3 = x2_ref[...]
    for k in range(TOP_K):
        x3 = x3 + route[:, TOP_K + k:TOP_K + k + 1] * _slab_load(buf_ref.at[slot, k], tm)
    ms = jnp.mean(x3 * x3, axis=-1, keepdims=True)
    o_ref[...] = x3 * lax.rsqrt(ms + EPS) * gf_ref[...]


def _combine(dest, x2, route, gf, ys_slab, *, tm):
    T, D = x2.shape
    n = T // tm
    assert T % tm == 0
    return pl.pallas_call(
        functools.partial(_combine_body, tm=tm),
        grid=(n,),
        in_specs=[
            pl.BlockSpec((1, 1, tm * TOP_K), lambda i: (i, 0, 0), memory_space=pltpu.SMEM),
            pl.BlockSpec((1, 1, tm * TOP_K), lambda i: (jnp.minimum(i + 1, n - 1), 0, 0), memory_space=pltpu.SMEM),
            pl.BlockSpec((tm, D), lambda i: (i, 0)),
            pl.BlockSpec((tm, LANES), lambda i: (i, 0)),
            pl.BlockSpec((1, D), lambda i: (0, 0)),
            pl.BlockSpec(memory_space=pl.ANY),
        ],
        out_specs=pl.BlockSpec((tm, D), lambda i: (i, 0)),
        out_shape=jax.ShapeDtypeStruct((T, D), F32),
        scratch_shapes=[pltpu.VMEM((2, TOP_K, tm * SLAB_PITCH, LANES), U32), pltpu.SemaphoreType.DMA((2,))],
        compiler_params=_cparams(("arbitrary",)),
        name="combine",
    )(dest, dest, x2, route, gf, ys_slab)


def _retention_tables(S, dk):
    pos = jnp.arange(S, dtype=F32)
    inv_freq = ROPE_BASE ** (-jnp.arange(0, dk, 2, dtype=F32) / dk)
    ang = pos[:, None] * inv_freq[None, :]
    cos, sin = jnp.cos(ang), jnp.sin(ang)
    cos2 = jnp.concatenate([cos, cos], axis=-1)
    sin2 = jnp.concatenate([-sin, sin], axis=-1)
    log_gamma = jnp.log1p(-jnp.exp2(-5.0 - jnp.arange(RET_HEADS, dtype=F32)))
    idx = jnp.arange(CHUNK, dtype=F32)
    diff = idx[:, None] - idx[None, :]
    decay = jnp.where(diff[None] >= 0, jnp.exp(log_gamma[:, None, None] * jnp.maximum(diff, 0.0)[None]), 0.0)
    xi = jnp.exp(log_gamma[:, None] * (idx[None, :] + 1.0))
    zeta = jnp.exp(log_gamma[:, None] * (CHUNK - 1.0 - idx[None, :]))
    cd = jnp.exp(log_gamma * CHUNK)
    xi_b = jnp.broadcast_to(xi[:, :, None], (RET_HEADS, CHUNK, LANES))
    zeta_b = jnp.broadcast_to(zeta[:, :, None], (RET_HEADS, CHUNK, dk))
    return cos2, sin2, decay, xi_b, zeta_b, cd


TM_INPROJ = 1024
TN_INPROJ = 1024
TM_MIXING = 256
TM_MERGE = 512
MERGE_SUB_ROWS = 512
TM_EXPERT = 512
EXPERT_ROW_STEP = 128
TF_EXPERT = 1024
TM_COMBINE = 256


def kernel(x, norm1_g, w_in, gm_ln_g, gm_ln_b, gm_ws, gm_b, w_proj_a, w_proj_r, w_out, norm2_g, router_w, router_b,
           w_gate_up, b_gate_up, w_down, b_down, final_norm_g):
    B, S, D = x.shape
    assert B == 1 and norm1_g.shape[0] == 1, "single sequence, depth 1"
    gm_width = w_proj_a.shape[1]
    v_width = w_proj_r.shape[1]
    qk_width = (w_in.shape[2] - 2 * gm_width - 2 * v_width - 2 * D) // 2
    dk = qk_width // RET_HEADS
    assert dk == LANES and gm_ws.shape[2] == CHUNK and D == 2 * SLAB_ROWS * LANES
    G = gm_ws.shape[1]
    E = router_w.shape[2]
    d_ff = w_down.shape[2]
    x2d = x.reshape(S, D)

    cos2, sin2, decay, xi_b, zeta_b, cd = _retention_tables(S, dk)
    tm_p = min(TM_INPROJ, S)
    h = _prenorm(x2d, norm1_g[0][None], tm=tm_p)
    w_in_bf = w_in[0].astype(BF16)
    segments = (
        (gm_width, "gelu", ()),
        (gm_width, "gelu_layernorm", (gm_ln_g[0][None], gm_ln_b[0][None])),
        (qk_width, "rope", (cos2, sin2)),
        (qk_width, "rope_scaled", (cos2, sin2)),
        (v_width, "identity", ()),
        (v_width, "silu", ()),
        (2 * D, "sigmoid", ()),
    )
    outs, col0 = [], 0
    for width, kind, extras in segments:
        outs.append(_segment(h, w_in_bf, col0, width, kind, extras, tm=tm_p, tn=TN_INPROJ))
        col0 += width
    a_u, a_vn, r_q, r_k, r_v, r_sg, gates = outs

    causal = jnp.tril(jnp.ones((CHUNK, CHUNK), dtype=bool))
    wm = jnp.where(causal[None], gm_ws[0], 0.0).astype(BF16)
    bs_b = jnp.broadcast_to(gm_b[0][:, :, None], (G, CHUNK, GM_GROUP_DIM))
    ga, gr = _mixing(a_u, a_vn, r_q, r_k, r_v, r_sg, wm, bs_b, decay, xi_b, zeta_b, cd, tm=min(TM_MIXING, S))

    rw = jnp.pad(router_w[0], ((0, 0), (0, LANES - E)))
    rw_hi = rw.astype(BF16)
    rw_lo = (rw - rw_hi.astype(F32)).astype(BF16)
    rb = jnp.pad(router_b[0], (0, LANES - E))[None]
    tm_m = min(TM_MERGE, S)
    merged = _branches(ga, gr, gates, w_proj_a[0].astype(BF16), w_proj_r[0].astype(BF16), tm=tm_m)
    x2, h2, route, cnt = _merge(x2d, merged, w_out[0].astype(BF16), norm2_g[0][None], rw_hi, rw_lo, rb, tm=tm_m)

    tm_e = TM_EXPERT
    eidx = route[:, 0:TOP_K].astype(jnp.int32)
    rank = route[:, 2 * TOP_K:3 * TOP_K].astype(jnp.int32)
    counts = cnt[0, :E].astype(jnp.int32)
    nblk = (counts + tm_e - 1) // tm_e
    blk_end = jnp.cumsum(nblk)
    blk_start = blk_end - nblk
    dest = (blk_start * tm_e)[eidx] + rank
    nb = (S * TOP_K + tm_e - 1) // tm_e + E
    nvb = blk_end[-1].astype(jnp.int32)
    blk_ids = jnp.arange(nb, dtype=jnp.int32)
    is_last = jnp.any((blk_ids[:, None] == blk_end[None, :] - 1) & (nblk[None, :] > 0), axis=1)
    clear_flag = (is_last | (blk_ids >= nvb)).astype(jnp.int32)

    tm_c = min(TM_COMBINE, S)
    dest_blocks = dest.reshape(S // tm_c, 1, tm_c * TOP_K)
    xs = _dispatch(h2, dest_blocks, clear_flag, tm=tm_c, tm_e=tm_e)
    tf = min(TF_EXPERT, d_ff)
    act = _gateup(xs, _tile_schedule(counts, nblk, blk_start, nvb, d_ff // tf, nb, tm_e), w_gate_up[0],
                  b_gate_up[0][:, None, :],
                  tm=tm_e, tf=tf)
    ys = _down(act, _tile_schedule(counts, nblk, blk_start, nvb, 1, nb, tm_e), w_down[0], b_down[0][:, None, :],
               tm=tm_e)

    out = _combine(dest_blocks, x2, route, final_norm_g[None], ys, tm=tm_c)
    return out.reshape(B, S, D)
```

```python
import functools

import jax
import jax.numpy as jnp
from jax import lax
from jax.experimental import pallas as pl
from jax.experimental.pallas import tpu as pltpu

F32 = jnp.float32
BF16 = jnp.bfloat16
U32 = jnp.uint32

CHUNK = 128
GM_GROUP_DIM = 128
RET_HEADS = 8
N_EXPERTS = 32
TOP_K = 4
SWIGLU_LIMIT = 7.0
SWIGLU_ALPHA = 1.702
ROPE_BASE = 10000.0
EPS = 1e-6
LANES = 128

VMEM_LIMIT_BYTES = 56 * 1024 * 1024


def _cparams(sem):
    return pltpu.CompilerParams(dimension_semantics=sem, vmem_limit_bytes=VMEM_LIMIT_BYTES)


SLAB_ROWS = 8
SLAB_PITCH = 9
DMA_QUEUES = 2


def _slab_store(ref, value, row0=0):
    rows, width = value.shape
    half = width // 2
    assert half == SLAB_ROWS * LANES
    lo = lax.bitcast_convert_type(value[:, :half].astype(BF16).astype(F32), U32)
    hi = lax.bitcast_convert_type(value[:, half:].astype(BF16).astype(F32), U32)
    words = (lo >> 16) | hi
    base = row0 * SLAB_PITCH
    for c in range(SLAB_ROWS):
        ref[pl.ds(base + c, rows, stride=SLAB_PITCH), :] = words[:, c * LANES:(c + 1) * LANES]
    for c in range(SLAB_ROWS, SLAB_PITCH):
        ref[pl.ds(base + c, rows, stride=SLAB_PITCH), :] = jnp.zeros((rows, LANES), U32)


def _slab_load(ref, rows):
    words = jnp.concatenate([ref[pl.ds(c, rows, stride=SLAB_PITCH), :] for c in range(SLAB_ROWS)], axis=1)
    lo = lax.bitcast_convert_type(words << 16, F32)
    hi = lax.bitcast_convert_type(words & jnp.uint32(0xFFFF0000), F32)
    return jnp.concatenate([lo, hi], axis=1)


def _gelu_exact(a):
    return 0.5 * a * (1.0 + lax.erf(a * (2.0 ** -0.5)))


def _prenorm_body(x_ref, g_ref, h_ref):
    xf = x_ref[...]
    ms = jnp.mean(xf * xf, axis=-1, keepdims=True)
    h_ref[...] = (xf * lax.rsqrt(ms + EPS) * g_ref[...]).astype(h_ref.dtype)


def _prenorm(x2d, g, *, tm):
    S, D = x2d.shape
    return pl.pallas_call(
        _prenorm_body,
        grid=(S // tm,),
        in_specs=[pl.BlockSpec((tm, D), lambda i: (i, 0)), pl.BlockSpec((1, D), lambda i: (0, 0))],
        out_specs=pl.BlockSpec((tm, D), lambda i: (i, 0)),
        out_shape=jax.ShapeDtypeStruct((S, D), BF16),
        compiler_params=_cparams(("arbitrary",)),
        name="prenorm",
    )(x2d, g)


def _rope_heads(a, cos2, sin2):
    outs = []
    for hd in range(a.shape[1] // LANES):
        ah = a[:, hd * LANES:(hd + 1) * LANES]
        outs.append(ah * cos2 + pltpu.roll(ah, LANES // 2, axis=1) * sin2)
    return jnp.concatenate(outs, axis=1)


def _segment_body(h_ref, w_ref, *refs, kind, k_scale):
    o_ref, wb_ref = refs[-2], refs[-1]

    @pl.when(pl.program_id(1) == 0)
    def _():
        def cast_rows(r, carry):
            rows = pl.ds(pl.multiple_of(r * CAST_ROWS, CAST_ROWS), CAST_ROWS)
            wb_ref[rows, :] = w_ref[rows, :].astype(BF16)
            return carry

        lax.fori_loop(0, w_ref.shape[0] // CAST_ROWS, cast_rows, 0)

    acc = jnp.dot(h_ref[...], wb_ref[...], preferred_element_type=F32)
    if kind == "gelu":
        out = _gelu_exact(acc)
    elif kind == "gelu_layernorm":
        lng_ref, lnb_ref = refs[0], refs[1]
        vf = _gelu_exact(acc)
        mu = jnp.mean(vf, axis=-1, keepdims=True)
        vc = vf - mu
        var = jnp.mean(vc * vc, axis=-1, keepdims=True)
        out = vc * lax.rsqrt(var + EPS) * lng_ref[...] + lnb_ref[...]
    elif kind == "rope":
        out = _rope_heads(acc, refs[0][...], refs[1][...])
    elif kind == "rope_scaled":
        out = _rope_heads(acc, refs[0][...], refs[1][...]) * k_scale
    elif kind == "identity":
        out = acc
    elif kind == "silu":
        out = acc * jax.nn.sigmoid(acc)
    elif kind == "sigmoid":
        out = jax.nn.sigmoid(acc)
    else:
        raise ValueError(kind)
    o_ref[...] = out.astype(o_ref.dtype)


def _segment(h, w_in, col0, width, kind, extras=(), *, tm, tn):
    S, D = h.shape
    assert width % tn == 0 and col0 % tn == 0 and S % tm == 0
    blk0 = col0 // tn
    if kind == "gelu_layernorm":
        assert width == tn
        extra_specs = [pl.BlockSpec((1, tn), lambda j, i: (0, 0))] * 2
    elif kind in ("rope", "rope_scaled"):
        extra_specs = [pl.BlockSpec((tm, LANES), lambda j, i: (i, 0))] * 2
    else:
        extra_specs = []
    return pl.pallas_call(
        functools.partial(_segment_body, kind=kind, k_scale=float(LANES) ** -0.5),
        grid=(width // tn, S // tm),
        in_specs=[
            pl.BlockSpec((tm, D), lambda j, i: (i, 0)),
            pl.BlockSpec((D, tn), lambda j, i: (0, blk0 + j)),
        ] + extra_specs,
        out_specs=pl.BlockSpec((tm, tn), lambda j, i: (i, j)),
        out_shape=jax.ShapeDtypeStruct((S, width), BF16),
        scratch_shapes=[pltpu.VMEM((D, tn), BF16)],
        compiler_params=_cparams(("arbitrary", "arbitrary")),
        name="inproj_" + kind,
    )(h, w_in, *extras)


def _mixing_body(cd_ref, u_ref, vn_ref, q_ref, k_ref, v_ref, sg_ref, wm_ref, bs_ref, decay_ref, xi_ref, zeta_ref,
                 ga_ref, gr_ref, state_ref, *, n_chunks, dk, dv):
    @pl.when(pl.program_id(0) == 0)
    def _():
        state_ref[...] = jnp.zeros_like(state_ref)

    def chunk(c, carry):
        rows = pl.ds(pl.multiple_of(c * CHUNK, CHUNK), CHUNK)
        for g in range(wm_ref.shape[0]):
            cols = slice(g * GM_GROUP_DIM, (g + 1) * GM_GROUP_DIM)
            mixed = jnp.dot(wm_ref[g], vn_ref[rows, cols], preferred_element_type=F32) + bs_ref[g]
            ga_ref[rows, cols] = (u_ref[rows, cols].astype(F32) * mixed).astype(ga_ref.dtype)
        for hd in range(RET_HEADS):
            qc = q_ref[rows, hd * dk:(hd + 1) * dk]
            kc = k_ref[rows, hd * dk:(hd + 1) * dk]
            vc = v_ref[rows, hd * dv:(hd + 1) * dv]
            st = state_ref[hd]
            scores = lax.dot_general(qc, kc, (((1,), (1,)), ((), ())), preferred_element_type=F32) * decay_ref[hd]
            inner = jnp.dot(scores.astype(BF16), vc, preferred_element_type=F32)
            cross = jnp.dot(qc, st.astype(BF16), preferred_element_type=F32)
            xi = xi_ref[hd]
            o = inner + cross * jnp.concatenate([xi] * (dv // LANES), axis=1)
            kz = (kc.astype(F32) * zeta_ref[hd]).astype(BF16)
            kv = lax.dot_general(kz, vc, (((0,), (0,)), ((), ())), preferred_element_type=F32)
            state_ref[hd] = st * cd_ref[hd] + kv
            mu = jnp.mean(o, axis=-1, keepdims=True)
            oc = o - mu
            var = jnp.mean(oc * oc, axis=-1, keepdims=True)
            on = oc * lax.rsqrt(var + EPS)
            gr_ref[rows, hd * dv:(hd + 1) * dv] = (on * sg_ref[rows, hd * dv:(hd + 1) * dv].astype(F32)).astype(
                gr_ref.dtype)
        return carry

    lax.fori_loop(0, n_chunks, chunk, 0, unroll=True)


def _mixing(u, vn, q, k, v, sg, wm, bs_b, decay, xi_b, zeta_b, cd, *, tm):
    S, gm_width = u.shape
    qk_width, v_width = q.shape[1], v.shape[1]
    assert S % tm == 0 and tm % CHUNK == 0
    dk, dv = qk_width // RET_HEADS, v_width // RET_HEADS
    G = wm.shape[0]
    body = functools.partial(_mixing_body, n_chunks=tm // CHUNK, dk=dk, dv=dv)
    const3 = lambda i: (0, 0, 0)
    rows = lambda i: (i, 0)
    return pl.pallas_call(
        body,
        grid=(S // tm,),
        in_specs=[
            pl.BlockSpec(memory_space=pltpu.SMEM),
            pl.BlockSpec((tm, gm_width), rows),
            pl.BlockSpec((tm, gm_width), rows),
            pl.BlockSpec((tm, qk_width), rows),
            pl.BlockSpec((tm, qk_width), rows),
            pl.BlockSpec((tm, v_width), rows),
            pl.BlockSpec((tm, v_width), rows),
            pl.BlockSpec((G, CHUNK, CHUNK), const3),
            pl.BlockSpec((G, CHUNK, GM_GROUP_DIM), const3),
            pl.BlockSpec((RET_HEADS, CHUNK, CHUNK), const3),
            pl.BlockSpec((RET_HEADS, CHUNK, LANES), const3),
            pl.BlockSpec((RET_HEADS, CHUNK, dk), const3),
        ],
        out_specs=[
            pl.BlockSpec((tm, gm_width), lambda i: (i, 0)),
            pl.BlockSpec((tm, v_width), lambda i: (i, 0)),
        ],
        out_shape=[
            jax.ShapeDtypeStruct((S, gm_width), BF16),
            jax.ShapeDtypeStruct((S, v_width), BF16),
        ],
        scratch_shapes=[pltpu.VMEM((RET_HEADS, dk, dv), F32)],
        compiler_params=_cparams(("arbitrary",)),
        name="mixing",
    )(cd, u, vn, q, k, v, sg, wm, bs_b, decay, xi_b, zeta_b)


def _branches_body(ga_ref, gr_ref, sa_ref, sr_ref, wa_ref, wr_ref, o_ref):
    y_a = jnp.dot(ga_ref[...], wa_ref[...], preferred_element_type=F32)
    y_r = jnp.dot(gr_ref[...], wr_ref[...], preferred_element_type=F32)
    o_ref[...] = (sa_ref[...].astype(F32) * y_a + sr_ref[...].astype(F32) * y_r).astype(o_ref.dtype)


def _branches(ga, gr, gates, wa, wr, *, tm):
    S = ga.shape[0]
    D = wa.shape[1]
    const = lambda i: (0, 0)
    resident = functools.partial(pl.BlockSpec, index_map=const, pipeline_mode=pl.Buffered(1))
    return pl.pallas_call(
        _branches_body,
        grid=(S // tm,),
        in_specs=[
            pl.BlockSpec((tm, ga.shape[1]), lambda i: (i, 0)),
            pl.BlockSpec((tm, gr.shape[1]), lambda i: (i, 0)),
            pl.BlockSpec((tm, D), lambda i: (i, 0)),
            pl.BlockSpec((tm, D), lambda i: (i, 1)),
            resident(wa.shape),
            resident(wr.shape),
        ],
        out_specs=pl.BlockSpec((tm, D), lambda i: (i, 0)),
        out_shape=jax.ShapeDtypeStruct((S, D), BF16),
        compiler_params=_cparams(("arbitrary",)),
        name="branches",
    )(ga, gr, gates, gates, wa, wr)


def _merge_body(x_ref, m_ref, wo_ref, g2_ref, rwh_ref, rwl_ref, rb_ref,
                x2_ref, h2_ref, route_ref, cnt_ref, run_ref):
    i = pl.program_id(0)
    tm = x_ref.shape[0]

    @pl.when(i == 0)
    def _():
        run_ref[...] = jnp.zeros_like(run_ref)

    sub = min(tm, MERGE_SUB_ROWS)
    lane = lax.broadcasted_iota(jnp.int32, (sub, LANES), 1)
    neg_inf = jnp.float32(-jnp.inf)
    r_iota = lax.broadcasted_iota(jnp.int32, (sub, sub), 0)
    c_iota = lax.broadcasted_iota(jnp.int32, (sub, sub), 1)
    strict_lower = (c_iota < r_iota).astype(BF16)
    run = run_ref[0:1, :]
    for j in range(tm // sub):
        rows = slice(j * sub, (j + 1) * sub)
        x2 = x_ref[rows, :] + jnp.dot(m_ref[rows, :], wo_ref[...], preferred_element_type=F32)
        x2_ref[rows, :] = x2
        ms = jnp.mean(x2 * x2, axis=-1, keepdims=True)
        h2 = x2 * lax.rsqrt(ms + EPS) * g2_ref[...]
        _slab_store(h2_ref, h2, row0=j * sub)

        h_hi = h2.astype(BF16)
        h_lo = (h2 - h_hi.astype(F32)).astype(BF16)
        logits = (jnp.dot(h_hi, rwh_ref[...], preferred_element_type=F32)
                  + jnp.dot(h_hi, rwl_ref[...], preferred_element_type=F32)
                  + jnp.dot(h_lo, rwh_ref[...], preferred_element_type=F32)) + rb_ref[...]
        work = jnp.where(lane < N_EXPERTS, logits, neg_inf)

        vals, idxs, sels = [], [], []
        for _ in range(TOP_K):
            m = jnp.max(work, axis=-1, keepdims=True)
            idx = jnp.min(jnp.where(work == m, lane, LANES), axis=-1, keepdims=True)
            sel = lane == idx
            vals.append(m)
            idxs.append(idx)
            sels.append(sel)
            work = jnp.where(sel, neg_inf, work)
        exps = [jnp.exp(v - vals[0]) for v in vals]
        denom = exps[0] + exps[1] + exps[2] + exps[3]
        gates = [e / denom for e in exps]

        onehot = jnp.zeros((sub, LANES), F32)
        for sel in sels:
            onehot = onehot + sel.astype(F32)
        before = jnp.dot(strict_lower, onehot.astype(BF16), preferred_element_type=F32) + run
        ranks = [jnp.sum(jnp.where(sel, before, 0.0), axis=-1, keepdims=True) for sel in sels]
        run = run + jnp.sum(onehot, axis=0, keepdims=True)

        route = jnp.zeros((sub, LANES), F32)
        for k in range(TOP_K):
            route = jnp.where(lane == k, idxs[k].astype(F32), route)
            route = jnp.where(lane == TOP_K + k, gates[k], route)
            route = jnp.where(lane == 2 * TOP_K + k, ranks[k], route)
        route_ref[rows, :] = route

    run_ref[...] = jnp.broadcast_to(run, run_ref.shape)
    cnt_ref[...] = jnp.broadcast_to(run, cnt_ref.shape)


def _merge(x2d, merged, wo, g2, rw_hi, rw_lo, rb, *, tm):
    S, D = x2d.shape
    assert S % tm == 0
    const = lambda i: (0, 0)
    resident = functools.partial(pl.BlockSpec, index_map=const, pipeline_mode=pl.Buffered(1))
    return pl.pallas_call(
        _merge_body,
        grid=(S // tm,),
        in_specs=[
            pl.BlockSpec((tm, D), lambda i: (i, 0)),
            pl.BlockSpec((tm, D), lambda i: (i, 0)),
            resident(wo.shape),
            pl.BlockSpec((1, D), const),
            resident(rw_hi.shape),
            resident(rw_lo.shape),
            pl.BlockSpec((1, LANES), const),
        ],
        out_specs=[
            pl.BlockSpec((tm, D), lambda i: (i, 0)),
            pl.BlockSpec((tm * SLAB_PITCH, LANES), lambda i: (i, 0)),
            pl.BlockSpec((tm, LANES), lambda i: (i, 0)),
            pl.BlockSpec((8, LANES), const),
        ],
        out_shape=[
            jax.ShapeDtypeStruct((S, D), F32),
            jax.ShapeDtypeStruct((S * SLAB_PITCH, LANES), U32),
            jax.ShapeDtypeStruct((S, LANES), F32),
            jax.ShapeDtypeStruct((8, LANES), F32),
        ],
        scratch_shapes=[pltpu.VMEM((8, LANES), F32)],
        compiler_params=_cparams(("arbitrary",)),
        name="merge",
    )(x2d, merged, wo, g2, rw_hi, rw_lo, rb)


def _dispatch_body(clear_ref, dest_ref, h2_ref, xs_hbm, zeros_ref, sem, clear_sem, *, tm, blk_rows, nb):
    i = pl.program_id(0)

    def clear_copy(b):
        return pltpu.make_async_copy(zeros_ref, xs_hbm.at[pl.ds(pl.multiple_of(b * blk_rows, blk_rows), blk_rows)],
                                     clear_sem.at[0])

    @pl.when(i == 0)
    def _():
        zeros_ref[...] = jnp.zeros_like(zeros_ref)

        def start(b, carry):
            @pl.when(clear_ref[b] == 1)
            def _():
                clear_copy(b).start()
            return carry

        def drain(b, carry):
            @pl.when(clear_ref[b] == 1)
            def _():
                clear_copy(b).wait()
            return carry

        lax.fori_loop(0, nb, start, 0)
        lax.fori_loop(0, nb, drain, 0)

    def issue(t, carry):
        for k in range(TOP_K):
            d = dest_ref[0, 0, t * TOP_K + k]
            pltpu.make_async_copy(h2_ref.at[pl.ds(t * SLAB_PITCH, SLAB_PITCH)],
                                  xs_hbm.at[pl.ds(d * SLAB_PITCH, SLAB_PITCH)],
                                  sem.at[0]).start(priority=k % DMA_QUEUES)
        return carry

    lax.fori_loop(0, tm, issue, 0, unroll=4)
    for k in range(TOP_K):
        pltpu.make_async_copy(h2_ref, xs_hbm.at[pl.ds(0, tm * SLAB_PITCH)], sem.at[0]).wait()


def _dispatch(h2_slab, dest, clear_flag, *, tm, tm_e):
    nb = clear_flag.shape[0]
    blk_rows = tm_e * SLAB_PITCH
    n = h2_slab.shape[0] // (tm * SLAB_PITCH)
    grid_spec = pltpu.PrefetchScalarGridSpec(
        num_scalar_prefetch=1,
        grid=(n,),
        in_specs=[
            pl.BlockSpec((1, 1, tm * TOP_K), lambda i, cf: (i, 0, 0), memory_space=pltpu.SMEM),
            pl.BlockSpec((tm * SLAB_PITCH, LANES), lambda i, cf: (i, 0)),
        ],
        out_specs=pl.BlockSpec(memory_space=pl.ANY),
        scratch_shapes=[pltpu.VMEM((blk_rows, LANES), U32), pltpu.SemaphoreType.DMA((1,)),
                        pltpu.SemaphoreType.DMA((1,))],
    )
    return pl.pallas_call(
        functools.partial(_dispatch_body, tm=tm, blk_rows=blk_rows, nb=nb),
        grid_spec=grid_spec,
        out_shape=jax.ShapeDtypeStruct((nb * blk_rows, LANES), U32),
        compiler_params=_cparams(("arbitrary",)),
        name="dispatch",
    )(clear_flag, dest, h2_slab)


def _tile_schedule(counts, nblk, blk_start, nvb, n_tiles, nb, tm):
    E = nblk.shape[0]
    n_steps = nb * n_tiles
    steps_per_e = n_tiles * nblk
    e_end = jnp.cumsum(steps_per_e)
    e_start = e_end - steps_per_e
    total = e_end[-1]
    e_ids = jnp.arange(E, dtype=jnp.int32)

    def locate(idx):
        inside = (e_start[None, :] <= idx[:, None]) & (idx[:, None] < e_end[None, :])
        pick = lambda v: jnp.sum(jnp.where(inside, v[None, :], 0), axis=1)
        r = idx - pick(e_start)
        nb_e = jnp.maximum(pick(nblk), 1)
        tile = sum((r >= t * nb_e).astype(jnp.int32) for t in range(1, n_tiles)) if n_tiles > 1 else 0 * r
        return pick(e_ids), tile, r - tile * nb_e, nb_e, pick(blk_start), pick(counts)

    s = jnp.arange(n_steps, dtype=jnp.int32)
    live = s < total
    sl = jnp.minimum(s, total - 1)
    e, tile, b, nb_e, bs_e, cnt_e = locate(sl)
    blk = bs_e + b
    first = (live & (b == 0)).astype(jnp.int32)
    dead = jnp.maximum(s - total, 0)
    out_blk = jnp.where(live, blk, nvb + dead // n_tiles)
    out_tile = jnp.where(live, tile, dead % n_tiles)
    pieces = (jnp.clip(cnt_e - b * tm, 0, tm) + EXPERT_ROW_STEP - 1) // EXPERT_ROW_STEP
    pieces = jnp.where(live, pieces, 0)
    nxt = sl - b + nb_e
    has_next = (live & (nxt < total)).astype(jnp.int32)
    ne, nt, _, _, _, _ = locate(jnp.minimum(nxt, total - 1))
    as_i32 = lambda a: a.astype(jnp.int32)
    return (as_i32(e), as_i32(tile), as_i32(blk), first, as_i32(out_blk), as_i32(out_tile), as_i32(total)[None],
            as_i32(ne), as_i32(nt), has_next, as_i32(pieces))


def _gateup_body(se_ref, st_ref, sb_ref, sf_ref, ob_ref, ot_ref, nl_ref, ne_ref, nt_ref, hn_ref, np_ref,
                 xs_ref, wgu_hbm, bg_ref, bu_ref, act_ref, stage_ref, wgb_ref, wub_ref, sem, *, tf, d_ff):
    s = pl.program_id(0)
    D = stage_ref.shape[1]
    tm = act_ref.shape[0]

    def tile_copies(e, t):
        return [pltpu.make_async_copy(
            wgu_hbm.at[e, pl.ds(0, D), pl.ds(pl.multiple_of(half * d_ff + t * tf, tf), tf)],
            stage_ref.at[half], sem.at[half]) for half in range(2)]

    @pl.when(sf_ref[s] == 1)
    def _():
        @pl.when(s == 0)
        def _():
            for c in tile_copies(se_ref[0], st_ref[0]):
                c.start()

        for c in tile_copies(se_ref[s], st_ref[s]):
            c.wait()

        def cast_rows(r, carry):
            rows = pl.ds(pl.multiple_of(r * CAST_ROWS, CAST_ROWS), CAST_ROWS)
            wgb_ref[rows, :] = stage_ref[0, rows, :].astype(BF16)
            wub_ref[rows, :] = stage_ref[1, rows, :].astype(BF16)
            return carry

        lax.fori_loop(0, D // CAST_ROWS, cast_rows, 0)

        @pl.when(hn_ref[s] == 1)
        def _():
            for c in tile_copies(ne_ref[s], nt_ref[s]):
                c.start()

    for pieces in range(1, tm // EXPERT_ROW_STEP + 1):
        @pl.when(np_ref[s] == pieces)
        def _(m=pieces * EXPERT_ROW_STEP):
            xb = _slab_load(xs_ref, m).astype(BF16)
            gate = jnp.dot(xb, wgb_ref[...], preferred_element_type=F32) + bg_ref[0]
            up = jnp.dot(xb, wub_ref[...], preferred_element_type=F32) + bu_ref[0]
            gate = jnp.minimum(gate, SWIGLU_LIMIT)
            up = jnp.clip(up, -SWIGLU_LIMIT, SWIGLU_LIMIT)
            glu = gate * jax.nn.sigmoid(gate * SWIGLU_ALPHA)
            act_ref[0:m, :] = ((up + 1.0) * glu).astype(act_ref.dtype)
            if m < tm:
                act_ref[m:tm, :] = jnp.zeros((tm - m, act_ref.shape[1]), act_ref.dtype)

    @pl.when(np_ref[s] == 0)
    def _():
        act_ref[...] = jnp.zeros_like(act_ref)


def _gateup(xs, sched, wgu, bgu, *, tm, tf):
    D = wgu.shape[1]
    R = xs.shape[0] // SLAB_PITCH
    d_ff = wgu.shape[2] // 2
    nj = d_ff // tf
    nb = R // tm
    assert d_ff % tf == 0 and R % tm == 0
    grid_spec = pltpu.PrefetchScalarGridSpec(
        num_scalar_prefetch=11,
        grid=(nb * nj,),
        in_specs=[
            pl.BlockSpec((tm * SLAB_PITCH, LANES), lambda s, se, st, sb, *_: (sb[s], 0)),
            pl.BlockSpec(memory_space=pl.ANY),
            pl.BlockSpec((1, 1, tf), lambda s, se, st, *_: (se[s], 0, st[s])),
            pl.BlockSpec((1, 1, tf), lambda s, se, st, *_: (se[s], 0, nj + st[s])),
        ],
        out_specs=pl.BlockSpec((tm, tf), lambda s, se, st, sb, sf, ob, ot, *_: (ob[s], ot[s])),
        scratch_shapes=[pltpu.VMEM((2, D, tf), F32), pltpu.VMEM((D, tf), BF16), pltpu.VMEM((D, tf), BF16),
                        pltpu.SemaphoreType.DMA((2,))],
    )
    return pl.pallas_call(
        functools.partial(_gateup_body, tf=tf, d_ff=d_ff),
        grid_spec=grid_spec,
        out_shape=jax.ShapeDtypeStruct((R, d_ff), BF16),
        compiler_params=_cparams(("arbitrary",)),
        name="gateup",
    )(*sched, xs, wgu, bgu, bgu)


def _down_body(se_ref, st_ref, sb_ref, sf_ref, ob_ref, ot_ref, nl_ref, ne_ref, nt_ref, hn_ref, np_ref,
               act_ref, wd_hbm, bd_ref, y_ref, stage_ref, wdb_ref, sem):
    s = pl.program_id(0)
    tm = act_ref.shape[0]

    def weight_copy(e):
        return pltpu.make_async_copy(wd_hbm.at[e], stage_ref, sem.at[0])

    @pl.when(sf_ref[s] == 1)
    def _():
        @pl.when(s == 0)
        def _():
            weight_copy(se_ref[0]).start()

        weight_copy(se_ref[s]).wait()
        wdb_ref[...] = stage_ref[...].astype(BF16)

        @pl.when(hn_ref[s] == 1)
        def _():
            weight_copy(ne_ref[s]).start()

    for pieces in range(1, tm // EXPERT_ROW_STEP + 1):
        @pl.when(np_ref[s] == pieces)
        def _(m=pieces * EXPERT_ROW_STEP):
            y = jnp.dot(act_ref[0:m, :], wdb_ref[...], preferred_element_type=F32) + bd_ref[0]
            _slab_store(y_ref, y)
            if m < tm:
                y_ref[m * SLAB_PITCH:tm * SLAB_PITCH, :] = jnp.zeros(((tm - m) * SLAB_PITCH, LANES), U32)

    @pl.when(np_ref[s] == 0)
    def _():
        y_ref[...] = jnp.zeros_like(y_ref)


def _down(act, sched, wd, bd, *, tm):
    R, d_ff = act.shape
    D = wd.shape[2]
    nb = R // tm
    grid_spec = pltpu.PrefetchScalarGridSpec(
        num_scalar_prefetch=11,
        grid=(nb,),
        in_specs=[
            pl.BlockSpec((tm, d_ff), lambda s, se, st, sb, *_: (sb[s], 0)),
            pl.BlockSpec(memory_space=pl.ANY),
            pl.BlockSpec((1, 1, D), lambda s, se, *_: (se[s], 0, 0)),
        ],
        out_specs=pl.BlockSpec((tm * SLAB_PITCH, LANES), lambda s, se, st, sb, sf, ob, *_: (ob[s], 0)),
        scratch_shapes=[pltpu.VMEM((d_ff, D), F32), pltpu.VMEM((d_ff, D), BF16), pltpu.SemaphoreType.DMA((1,))],
    )
    return pl.pallas_call(
        _down_body,
        grid_spec=grid_spec,
        out_shape=jax.ShapeDtypeStruct((R * SLAB_PITCH, LANES), U32),
        compiler_params=_cparams(("arbitrary",)),
        name="down",
    )(*sched, act, wd, bd)


def _combine_body(dest_cur_ref, dest_nxt_ref, x2_ref, route_ref, gf_ref, ys_hbm, o_ref, buf_ref, sem, *, tm):
    i = pl.program_id(0)
    slot = i % 2

    def start_rows(dest_ref, dst_slot):
        def issue(t, carry):
            for k in range(TOP_K):
                d = dest_ref[0, 0, t * TOP_K + k]
                pltpu.make_async_copy(ys_hbm.at[pl.ds(d * SLAB_PITCH, SLAB_ROWS)],
                                      buf_ref.at[dst_slot, k, pl.ds(t * SLAB_PITCH, SLAB_ROWS)],
                                      sem.at[dst_slot]).start(priority=k % DMA_QUEUES)
            return carry
        lax.fori_loop(0, tm, issue, 0, unroll=4)

    def wait_rows(dst_slot):
        for k in range(TOP_K):
            pltpu.make_async_copy(ys_hbm.at[pl.ds(0, tm * SLAB_ROWS)],
                                  buf_ref.at[dst_slot, k, pl.ds(0, tm * SLAB_ROWS)], sem.at[dst_slot]).wait()

    @pl.when(i == 0)
    def _():
        start_rows(dest_cur_ref, 0)

    @pl.when(i + 1 < pl.num_programs(0))
    def _():
        start_rows(dest_nxt_ref, 1 - slot)

    wait_rows(slot)
    route = route_ref[...]
    x3 = x2_ref[...]
    for k in range(TOP_K):
        x3 = x3 + route[:, TOP_K + k:TOP_K + k + 1] * _slab_load(buf_ref.at[slot, k], tm)
    ms = jnp.mean(x3 * x3, axis=-1, keepdims=True)
    o_ref[...] = x3 * lax.rsqrt(ms + EPS) * gf_ref[...]


def _combine(dest, x2, route, gf, ys_slab, *, tm):
    T, D = x2.shape
    n = T // tm
    assert T % tm == 0
    return pl.pallas_call(
        functools.partial(_combine_body, tm=tm),
        grid=(n,),
        in_specs=[
            pl.BlockSpec((1, 1, tm * TOP_K), lambda i: (i, 0, 0), memory_space=pltpu.SMEM),
            pl.BlockSpec((1, 1, tm * TOP_K), lambda i: (jnp.minimum(i + 1, n - 1), 0, 0), memory_space=pltpu.SMEM),
            pl.BlockSpec((tm, D), lambda i: (i, 0)),
            pl.BlockSpec((tm, LANES), lambda i: (i, 0)),
            pl.BlockSpec((1, D), lambda i: (0, 0)),
            pl.BlockSpec(memory_space=pl.ANY),
        ],
        out_specs=pl.BlockSpec((tm, D), lambda i: (i, 0)),
        out_shape=jax.ShapeDtypeStruct((T, D), F32),
        scratch_shapes=[pltpu.VMEM((2, TOP_K, tm * SLAB_PITCH, LANES), U32), pltpu.SemaphoreType.DMA((2,))],
        compiler_params=_cparams(("arbitrary",)),
        name="combine",
    )(dest, dest, x2, route, gf, ys_slab)


def _retention_tables(S, dk):
    pos = jnp.arange(S, dtype=F32)
    inv_freq = ROPE_BASE ** (-jnp.arange(0, dk, 2, dtype=F32) / dk)
    ang = pos[:, None] * inv_freq[None, :]
    cos, sin = jnp.cos(ang), jnp.sin(ang)
    cos2 = jnp.concatenate([cos, cos], axis=-1)
    sin2 = jnp.concatenate([-sin, sin], axis=-1)
    log_gamma = jnp.log1p(-jnp.exp2(-5.0 - jnp.arange(RET_HEADS, dtype=F32)))
    idx = jnp.arange(CHUNK, dtype=F32)
    diff = idx[:, None] - idx[None, :]
    decay = jnp.where(diff[None] >= 0, jnp.exp(log_gamma[:, None, None] * jnp.maximum(diff, 0.0)[None]), 0.0)
    xi = jnp.exp(log_gamma[:, None] * (idx[None, :] + 1.0))
    zeta = jnp.exp(log_gamma[:, None] * (CHUNK - 1.0 - idx[None, :]))
    cd = jnp.exp(log_gamma * CHUNK)
    xi_b = jnp.broadcast_to(xi[:, :, None], (RET_HEADS, CHUNK, LANES))
    zeta_b = jnp.broadcast_to(zeta[:, :, None], (RET_HEADS, CHUNK, dk))
    return cos2, sin2, decay, xi_b, zeta_b, cd


TM_INPROJ = 1024
TN_INPROJ = 1024
TM_MIXING = 256
TM_MERGE = 512
MERGE_SUB_ROWS = 512
TM_EXPERT = 512
EXPERT_ROW_STEP = 128
CAST_ROWS = 128
TF_EXPERT = 1024
TM_COMBINE = 256


def kernel(x, norm1_g, w_in, gm_ln_g, gm_ln_b, gm_ws, gm_b, w_proj_a, w_proj_r, w_out, norm2_g, router_w, router_b,
           w_gate_up, b_gate_up, w_down, b_down, final_norm_g):
    B, S, D = x.shape
    assert B == 1 and norm1_g.shape[0] == 1, "single sequence, depth 1"
    gm_width = w_proj_a.shape[1]
    v_width = w_proj_r.shape[1]
    qk_width = (w_in.shape[2] - 2 * gm_width - 2 * v_width - 2 * D) // 2
    dk = qk_width // RET_HEADS
    assert dk == LANES and gm_ws.shape[2] == CHUNK and D == 2 * SLAB_ROWS * LANES
    G = gm_ws.shape[1]
    E = router_w.shape[2]
    d_ff = w_down.shape[2]
    x2d = x.reshape(S, D)

    cos2, sin2, decay, xi_b, zeta_b, cd = _retention_tables(S, dk)
    tm_p = min(TM_INPROJ, S)
    h = _prenorm(x2d, norm1_g[0][None], tm=tm_p)
    segments = (
        (gm_width, "gelu", ()),
        (gm_width, "gelu_layernorm", (gm_ln_g[0][None], gm_ln_b[0][None])),
        (qk_width, "rope", (cos2, sin2)),
        (qk_width, "rope_scaled", (cos2, sin2)),
        (v_width, "identity", ()),
        (v_width, "silu", ()),
        (2 * D, "sigmoid", ()),
    )
    outs, col0 = [], 0
    for width, kind, extras in segments:
        outs.append(_segment(h, w_in[0], col0, width, kind, extras, tm=tm_p, tn=TN_INPROJ))
        col0 += width
    a_u, a_vn, r_q, r_k, r_v, r_sg, gates = outs

    causal = jnp.tril(jnp.ones((CHUNK, CHUNK), dtype=bool))
    wm = jnp.where(causal[None], gm_ws[0], 0.0).astype(BF16)
    bs_b = jnp.broadcast_to(gm_b[0][:, :, None], (G, CHUNK, GM_GROUP_DIM))
    ga, gr = _mixing(a_u, a_vn, r_q, r_k, r_v, r_sg, wm, bs_b, decay, xi_b, zeta_b, cd, tm=min(TM_MIXING, S))

    rw = jnp.pad(router_w[0], ((0, 0), (0, LANES - E)))
    rw_hi = rw.astype(BF16)
    rw_lo = (rw - rw_hi.astype(F32)).astype(BF16)
    rb = jnp.pad(router_b[0], (0, LANES - E))[None]
    tm_m = min(TM_MERGE, S)
    merged = _branches(ga, gr, gates, w_proj_a[0].astype(BF16), w_proj_r[0].astype(BF16), tm=tm_m)
    x2, h2, route, cnt = _merge(x2d, merged, w_out[0].astype(BF16), norm2_g[0][None], rw_hi, rw_lo, rb, tm=tm_m)

    tm_e = TM_EXPERT
    eidx = route[:, 0:TOP_K].astype(jnp.int32)
    rank = route[:, 2 * TOP_K:3 * TOP_K].astype(jnp.int32)
    counts = cnt[0, :E].astype(jnp.int32)
    nblk = (counts + tm_e - 1) // tm_e
    blk_end = jnp.cumsum(nblk)
    blk_start = blk_end - nblk
    dest = (blk_start * tm_e)[eidx] + rank
    nb = (S * TOP_K + tm_e - 1) // tm_e + E
    nvb = blk_end[-1].astype(jnp.int32)
    blk_ids = jnp.arange(nb, dtype=jnp.int32)
    is_last = jnp.any((blk_ids[:, None] == blk_end[None, :] - 1) & (nblk[None, :] > 0), axis=1)
    clear_flag = (is_last | (blk_ids >= nvb)).astype(jnp.int32)

    tm_c = min(TM_COMBINE, S)
    dest_blocks = dest.reshape(S // tm_c, 1, tm_c * TOP_K)
    xs = _dispatch(h2, dest_blocks, clear_flag, tm=tm_c, tm_e=tm_e)
    tf = min(TF_EXPERT, d_ff)
    act = _gateup(xs, _tile_schedule(counts, nblk, blk_start, nvb, d_ff // tf, nb, tm_e), w_gate_up[0],
                  b_gate_up[0][:, None, :],
                  tm=tm_e, tf=tf)
    ys = _down(act, _tile_schedule(counts, nblk, blk_start, nvb, 1, nb, tm_e), w_down[0], b_down[0][:, None, :],
               tm=tm_e)

    out = _combine(dest_blocks, x2, route, final_norm_g[None], ys, tm=tm_c)
    return out.reshape(B, S, D)
```

```python
import functools

import jax
import jax.numpy as jnp
from jax import lax
from jax.experimental import pallas as pl
from jax.experimental.pallas import tpu as pltpu

F32 = jnp.float32
BF16 = jnp.bfloat16
U32 = jnp.uint32

CHUNK = 128
GM_GROUP_DIM = 128
RET_HEADS = 8
N_EXPERTS = 32
TOP_K = 4
SWIGLU_LIMIT = 7.0
SWIGLU_ALPHA = 1.702
ROPE_BASE = 10000.0
EPS = 1e-6
LANES = 128

VMEM_LIMIT_BYTES = 56 * 1024 * 1024


def _cparams(sem):
    return pltpu.CompilerParams(dimension_semantics=sem, vmem_limit_bytes=VMEM_LIMIT_BYTES)


SLAB_ROWS = 8
SLAB_PITCH = 9
DMA_QUEUES = 2


def _slab_store(ref, value, row0=0):
    rows, width = value.shape
    half = width // 2
    assert half == SLAB_ROWS * LANES
    lo = lax.bitcast_convert_type(value[:, :half].astype(BF16).astype(F32), U32)
    hi = lax.bitcast_convert_type(value[:, half:].astype(BF16).astype(F32), U32)
    words = (lo >> 16) | hi
    base = row0 * SLAB_PITCH
    for c in range(SLAB_ROWS):
        ref[pl.ds(base + c, rows, stride=SLAB_PITCH), :] = words[:, c * LANES:(c + 1) * LANES]
    for c in range(SLAB_ROWS, SLAB_PITCH):
        ref[pl.ds(base + c, rows, stride=SLAB_PITCH), :] = jnp.zeros((rows, LANES), U32)


def _slab_load(ref, rows, row0=0):
    base = row0 * SLAB_PITCH
    words = jnp.concatenate([ref[pl.ds(base + c, rows, stride=SLAB_PITCH), :] for c in range(SLAB_ROWS)], axis=1)
    lo = lax.bitcast_convert_type(words << 16, F32)
    hi = lax.bitcast_convert_type(words & jnp.uint32(0xFFFF0000), F32)
    return jnp.concatenate([lo, hi], axis=1)


def _gelu_exact(a):
    return 0.5 * a * (1.0 + lax.erf(a * (2.0 ** -0.5)))


def _prenorm_body(x_ref, g_ref, h_ref):
    xf = x_ref[...]
    ms = jnp.mean(xf * xf, axis=-1, keepdims=True)
    h_ref[...] = (xf * lax.rsqrt(ms + EPS) * g_ref[...]).astype(h_ref.dtype)


def _prenorm(x2d, g, *, tm):
    S, D = x2d.shape
    return pl.pallas_call(
        _prenorm_body,
        grid=(S // tm,),
        in_specs=[pl.BlockSpec((tm, D), lambda i: (i, 0)), pl.BlockSpec((1, D), lambda i: (0, 0))],
        out_specs=pl.BlockSpec((tm, D), lambda i: (i, 0)),
        out_shape=jax.ShapeDtypeStruct((S, D), BF16),
        compiler_params=_cparams(("arbitrary",)),
        name="prenorm",
    )(x2d, g)


def _rope_heads(a, cos2, sin2):
    outs = []
    for hd in range(a.shape[1] // LANES):
        ah = a[:, hd * LANES:(hd + 1) * LANES]
        outs.append(ah * cos2 + pltpu.roll(ah, LANES // 2, axis=1) * sin2)
    return jnp.concatenate(outs, axis=1)


def _segment_body(h_ref, w_ref, *refs, kind, k_scale):
    o_ref, wb_ref = refs[-2], refs[-1]

    @pl.when(pl.program_id(1) == 0)
    def _():
        def cast_rows(r, carry):
            rows = pl.ds(pl.multiple_of(r * CAST_ROWS, CAST_ROWS), CAST_ROWS)
            wb_ref[rows, :] = w_ref[rows, :].astype(BF16)
            return carry

        lax.fori_loop(0, w_ref.shape[0] // CAST_ROWS, cast_rows, 0)

    acc = jnp.dot(h_ref[...], wb_ref[...], preferred_element_type=F32)
    if kind == "gelu":
        out = _gelu_exact(acc)
    elif kind == "gelu_layernorm":
        lng_ref, lnb_ref = refs[0], refs[1]
        vf = _gelu_exact(acc)
        mu = jnp.mean(vf, axis=-1, keepdims=True)
        vc = vf - mu
        var = jnp.mean(vc * vc, axis=-1, keepdims=True)
        out = vc * lax.rsqrt(var + EPS) * lng_ref[...] + lnb_ref[...]
    elif kind == "rope":
        out = _rope_heads(acc, refs[0][...], refs[1][...])
    elif kind == "rope_scaled":
        out = _rope_heads(acc, refs[0][...], refs[1][...]) * k_scale
    elif kind == "identity":
        out = acc
    elif kind == "silu":
        out = acc * jax.nn.sigmoid(acc)
    elif kind == "sigmoid":
        out = jax.nn.sigmoid(acc)
    else:
        raise ValueError(kind)
    o_ref[...] = out.astype(o_ref.dtype)


def _segment(h, w_in, col0, width, kind, extras=(), *, tm, tn):
    S, D = h.shape
    assert width % tn == 0 and col0 % tn == 0 and S % tm == 0
    blk0 = col0 // tn
    if kind == "gelu_layernorm":
        assert width == tn
        extra_specs = [pl.BlockSpec((1, tn), lambda j, i: (0, 0))] * 2
    elif kind in ("rope", "rope_scaled"):
        extra_specs = [pl.BlockSpec((tm, LANES), lambda j, i: (i, 0))] * 2
    else:
        extra_specs = []
    return pl.pallas_call(
        functools.partial(_segment_body, kind=kind, k_scale=float(LANES) ** -0.5),
        grid=(width // tn, S // tm),
        in_specs=[
            pl.BlockSpec((tm, D), lambda j, i: (i, 0)),
            pl.BlockSpec((D, tn), lambda j, i: (0, blk0 + j)),
        ] + extra_specs,
        out_specs=pl.BlockSpec((tm, tn), lambda j, i: (i, j)),
        out_shape=jax.ShapeDtypeStruct((S, width), BF16),
        scratch_shapes=[pltpu.VMEM((D, tn), BF16)],
        compiler_params=_cparams(("arbitrary", "arbitrary")),
        name="inproj_" + kind,
    )(h, w_in, *extras)


def _mixing_body(cd_ref, u_ref, vn_ref, q_ref, k_ref, v_ref, sg_ref, wm_ref, bs_ref, decay_ref, xi_ref, zeta_ref,
                 ga_ref, gr_ref, state_ref, *, n_chunks, dk, dv):
    @pl.when(pl.program_id(0) == 0)
    def _():
        state_ref[...] = jnp.zeros_like(state_ref)

    def chunk(c, carry):
        rows = pl.ds(pl.multiple_of(c * CHUNK, CHUNK), CHUNK)
        for g in range(wm_ref.shape[0]):
            cols = slice(g * GM_GROUP_DIM, (g + 1) * GM_GROUP_DIM)
            mixed = jnp.dot(wm_ref[g], vn_ref[rows, cols], preferred_element_type=F32) + bs_ref[g]
            ga_ref[rows, cols] = (u_ref[rows, cols].astype(F32) * mixed).astype(ga_ref.dtype)
        for hd in range(RET_HEADS):
            qc = q_ref[rows, hd * dk:(hd + 1) * dk]
            kc = k_ref[rows, hd * dk:(hd + 1) * dk]
            vc = v_ref[rows, hd * dv:(hd + 1) * dv]
            st = state_ref[hd]
            scores = lax.dot_general(qc, kc, (((1,), (1,)), ((), ())), preferred_element_type=F32) * decay_ref[hd]
            inner = jnp.dot(scores.astype(BF16), vc, preferred_element_type=F32)
            cross = jnp.dot(qc, st.astype(BF16), preferred_element_type=F32)
            xi = xi_ref[hd]
            o = inner + cross * jnp.concatenate([xi] * (dv // LANES), axis=1)
            kz = (kc.astype(F32) * zeta_ref[hd]).astype(BF16)
            kv = lax.dot_general(kz, vc, (((0,), (0,)), ((), ())), preferred_element_type=F32)
            state_ref[hd] = st * cd_ref[hd] + kv
            mu = jnp.mean(o, axis=-1, keepdims=True)
            oc = o - mu
            var = jnp.mean(oc * oc, axis=-1, keepdims=True)
            on = oc * lax.rsqrt(var + EPS)
            gr_ref[rows, hd * dv:(hd + 1) * dv] = (on * sg_ref[rows, hd * dv:(hd + 1) * dv].astype(F32)).astype(
                gr_ref.dtype)
        return carry

    lax.fori_loop(0, n_chunks, chunk, 0, unroll=True)


def _mixing(u, vn, q, k, v, sg, wm, bs_b, decay, xi_b, zeta_b, cd, *, tm):
    S, gm_width = u.shape
    qk_width, v_width = q.shape[1], v.shape[1]
    assert S % tm == 0 and tm % CHUNK == 0
    dk, dv = qk_width // RET_HEADS, v_width // RET_HEADS
    G = wm.shape[0]
    body = functools.partial(_mixing_body, n_chunks=tm // CHUNK, dk=dk, dv=dv)
    const3 = lambda i: (0, 0, 0)
    rows = lambda i: (i, 0)
    return pl.pallas_call(
        body,
        grid=(S // tm,),
        in_specs=[
            pl.BlockSpec(memory_space=pltpu.SMEM),
            pl.BlockSpec((tm, gm_width), rows),
            pl.BlockSpec((tm, gm_width), rows),
            pl.BlockSpec((tm, qk_width), rows),
            pl.BlockSpec((tm, qk_width), rows),
            pl.BlockSpec((tm, v_width), rows),
            pl.BlockSpec((tm, v_width), rows),
            pl.BlockSpec((G, CHUNK, CHUNK), const3),
            pl.BlockSpec((G, CHUNK, GM_GROUP_DIM), const3),
            pl.BlockSpec((RET_HEADS, CHUNK, CHUNK), const3),
            pl.BlockSpec((RET_HEADS, CHUNK, LANES), const3),
            pl.BlockSpec((RET_HEADS, CHUNK, dk), const3),
        ],
        out_specs=[
            pl.BlockSpec((tm, gm_width), lambda i: (i, 0)),
            pl.BlockSpec((tm, v_width), lambda i: (i, 0)),
        ],
        out_shape=[
            jax.ShapeDtypeStruct((S, gm_width), BF16),
            jax.ShapeDtypeStruct((S, v_width), BF16),
        ],
        scratch_shapes=[pltpu.VMEM((RET_HEADS, dk, dv), F32)],
        compiler_params=_cparams(("arbitrary",)),
        name="mixing",
    )(cd, u, vn, q, k, v, sg, wm, bs_b, decay, xi_b, zeta_b)


def _branches_body(ga_ref, gr_ref, sa_ref, sr_ref, wa_ref, wr_ref, o_ref):
    y_a = jnp.dot(ga_ref[...], wa_ref[...], preferred_element_type=F32)
    y_r = jnp.dot(gr_ref[...], wr_ref[...], preferred_element_type=F32)
    o_ref[...] = (sa_ref[...].astype(F32) * y_a + sr_ref[...].astype(F32) * y_r).astype(o_ref.dtype)


def _branches(ga, gr, gates, wa, wr, *, tm):
    S = ga.shape[0]
    D = wa.shape[1]
    const = lambda i: (0, 0)
    resident = functools.partial(pl.BlockSpec, index_map=const, pipeline_mode=pl.Buffered(1))
    return pl.pallas_call(
        _branches_body,
        grid=(S // tm,),
        in_specs=[
            pl.BlockSpec((tm, ga.shape[1]), lambda i: (i, 0)),
            pl.BlockSpec((tm, gr.shape[1]), lambda i: (i, 0)),
            pl.BlockSpec((tm, D), lambda i: (i, 0)),
            pl.BlockSpec((tm, D), lambda i: (i, 1)),
            resident(wa.shape),
            resident(wr.shape),
        ],
        out_specs=pl.BlockSpec((tm, D), lambda i: (i, 0)),
        out_shape=jax.ShapeDtypeStruct((S, D), BF16),
        compiler_params=_cparams(("arbitrary",)),
        name="branches",
    )(ga, gr, gates, gates, wa, wr)


def _merge_body(x_ref, m_ref, wo_ref, g2_ref, rwh_ref, rwl_ref, rb_ref,
                x2_ref, h2_ref, route_ref, cnt_ref, run_ref):
    i = pl.program_id(0)
    tm = x_ref.shape[0]

    @pl.when(i == 0)
    def _():
        run_ref[...] = jnp.zeros_like(run_ref)

    sub = min(tm, MERGE_SUB_ROWS)
    lane = lax.broadcasted_iota(jnp.int32, (sub, LANES), 1)
    neg_inf = jnp.float32(-jnp.inf)
    r_iota = lax.broadcasted_iota(jnp.int32, (sub, sub), 0)
    c_iota = lax.broadcasted_iota(jnp.int32, (sub, sub), 1)
    strict_lower = (c_iota < r_iota).astype(BF16)
    run = run_ref[0:1, :]
    for j in range(tm // sub):
        rows = slice(j * sub, (j + 1) * sub)
        x2 = x_ref[rows, :] + jnp.dot(m_ref[rows, :], wo_ref[...], preferred_element_type=F32)
        x2_ref[rows, :] = x2
        ms = jnp.mean(x2 * x2, axis=-1, keepdims=True)
        h2 = x2 * lax.rsqrt(ms + EPS) * g2_ref[...]
        _slab_store(h2_ref, h2, row0=j * sub)

        h_hi = h2.astype(BF16)
        h_lo = (h2 - h_hi.astype(F32)).astype(BF16)
        logits = (jnp.dot(h_hi, rwh_ref[...], preferred_element_type=F32)
                  + jnp.dot(h_hi, rwl_ref[...], preferred_element_type=F32)
                  + jnp.dot(h_lo, rwh_ref[...], preferred_element_type=F32)) + rb_ref[...]
        work = jnp.where(lane < N_EXPERTS, logits, neg_inf)

        vals, idxs, sels = [], [], []
        for _ in range(TOP_K):
            m = jnp.max(work, axis=-1, keepdims=True)
            idx = jnp.min(jnp.where(work == m, lane, LANES), axis=-1, keepdims=True)
            sel = lane == idx
            vals.append(m)
            idxs.append(idx)
            sels.append(sel)
            work = jnp.where(sel, neg_inf, work)
        exps = [jnp.exp(v - vals[0]) for v in vals]
        denom = exps[0] + exps[1] + exps[2] + exps[3]
        gates = [e / denom for e in exps]

        onehot = jnp.zeros((sub, LANES), F32)
        for sel in sels:
            onehot = onehot + sel.astype(F32)
        before = jnp.dot(strict_lower, onehot.astype(BF16), preferred_element_type=F32) + run
        ranks = [jnp.sum(jnp.where(sel, before, 0.0), axis=-1, keepdims=True) for sel in sels]
        run = run + jnp.sum(onehot, axis=0, keepdims=True)

        route = jnp.zeros((sub, LANES), F32)
        for k in range(TOP_K):
            route = jnp.where(lane == k, idxs[k].astype(F32), route)
            route = jnp.where(lane == TOP_K + k, gates[k], route)
            route = jnp.where(lane == 2 * TOP_K + k, ranks[k], route)
        route_ref[rows, :] = route

    run_ref[...] = jnp.broadcast_to(run, run_ref.shape)
    cnt_ref[...] = jnp.broadcast_to(run, cnt_ref.shape)


def _merge(x2d, merged, wo, g2, rw_hi, rw_lo, rb, *, tm):
    S, D = x2d.shape
    assert S % tm == 0
    const = lambda i: (0, 0)
    resident = functools.partial(pl.BlockSpec, index_map=const, pipeline_mode=pl.Buffered(1))
    return pl.pallas_call(
        _merge_body,
        grid=(S // tm,),
        in_specs=[
            pl.BlockSpec((tm, D), lambda i: (i, 0)),
            pl.BlockSpec((tm, D), lambda i: (i, 0)),
            resident(wo.shape),
            pl.BlockSpec((1, D), const),
            resident(rw_hi.shape),
            resident(rw_lo.shape),
            pl.BlockSpec((1, LANES), const),
        ],
        out_specs=[
            pl.BlockSpec((tm, D), lambda i: (i, 0)),
            pl.BlockSpec((tm * SLAB_PITCH, LANES), lambda i: (i, 0)),
            pl.BlockSpec((tm, LANES), lambda i: (i, 0)),
            pl.BlockSpec((8, LANES), const),
        ],
        out_shape=[
            jax.ShapeDtypeStruct((S, D), F32),
            jax.ShapeDtypeStruct((S * SLAB_PITCH, LANES), U32),
            jax.ShapeDtypeStruct((S, LANES), F32),
            jax.ShapeDtypeStruct((8, LANES), F32),
        ],
        scratch_shapes=[pltpu.VMEM((8, LANES), F32)],
        compiler_params=_cparams(("arbitrary",)),
        name="merge",
    )(x2d, merged, wo, g2, rw_hi, rw_lo, rb)


def _dispatch_body(clear_ref, dest_ref, h2_ref, xs_hbm, zeros_ref, sem, clear_sem, *, tm, blk_rows, nb):
    i = pl.program_id(0)

    def clear_copy(b):
        return pltpu.make_async_copy(zeros_ref, xs_hbm.at[pl.ds(pl.multiple_of(b * blk_rows, blk_rows), blk_rows)],
                                     clear_sem.at[0])

    @pl.when(i == 0)
    def _():
        zeros_ref[...] = jnp.zeros_like(zeros_ref)

        def start(b, carry):
            @pl.when(clear_ref[b] == 1)
            def _():
                clear_copy(b).start()
            return carry

        def drain(b, carry):
            @pl.when(clear_ref[b] == 1)
            def _():
                clear_copy(b).wait()
            return carry

        lax.fori_loop(0, nb, start, 0)
        lax.fori_loop(0, nb, drain, 0)

    def issue(t, carry):
        for k in range(TOP_K):
            d = dest_ref[0, 0, t * TOP_K + k]
            pltpu.make_async_copy(h2_ref.at[pl.ds(t * SLAB_PITCH, SLAB_PITCH)],
                                  xs_hbm.at[pl.ds(d * SLAB_PITCH, SLAB_PITCH)],
                                  sem.at[0]).start(priority=k % DMA_QUEUES)
        return carry

    lax.fori_loop(0, tm, issue, 0, unroll=4)
    for k in range(TOP_K):
        pltpu.make_async_copy(h2_ref, xs_hbm.at[pl.ds(0, tm * SLAB_PITCH)], sem.at[0]).wait()


def _dispatch(h2_slab, dest, clear_flag, *, tm, tm_e):
    nb = clear_flag.shape[0]
    blk_rows = tm_e * SLAB_PITCH
    n = h2_slab.shape[0] // (tm * SLAB_PITCH)
    grid_spec = pltpu.PrefetchScalarGridSpec(
        num_scalar_prefetch=1,
        grid=(n,),
        in_specs=[
            pl.BlockSpec((1, 1, tm * TOP_K), lambda i, cf: (i, 0, 0), memory_space=pltpu.SMEM),
            pl.BlockSpec((tm * SLAB_PITCH, LANES), lambda i, cf: (i, 0)),
        ],
        out_specs=pl.BlockSpec(memory_space=pl.ANY),
        scratch_shapes=[pltpu.VMEM((blk_rows, LANES), U32), pltpu.SemaphoreType.DMA((1,)),
                        pltpu.SemaphoreType.DMA((1,))],
    )
    return pl.pallas_call(
        functools.partial(_dispatch_body, tm=tm, blk_rows=blk_rows, nb=nb),
        grid_spec=grid_spec,
        out_shape=jax.ShapeDtypeStruct((nb * blk_rows, LANES), U32),
        compiler_params=_cparams(("arbitrary",)),
        name="dispatch",
    )(clear_flag, dest, h2_slab)


def _tile_schedule(counts, nblk, blk_start, nvb, n_tiles, nb, tm):
    E = nblk.shape[0]
    n_steps = nb * n_tiles
    steps_per_e = n_tiles * nblk
    e_end = jnp.cumsum(steps_per_e)
    e_start = e_end - steps_per_e
    total = e_end[-1]
    e_ids = jnp.arange(E, dtype=jnp.int32)

    def locate(idx):
        inside = (e_start[None, :] <= idx[:, None]) & (idx[:, None] < e_end[None, :])
        pick = lambda v: jnp.sum(jnp.where(inside, v[None, :], 0), axis=1)
        r = idx - pick(e_start)
        nb_e = jnp.maximum(pick(nblk), 1)
        tile = sum((r >= t * nb_e).astype(jnp.int32) for t in range(1, n_tiles)) if n_tiles > 1 else 0 * r
        return pick(e_ids), tile, r - tile * nb_e, nb_e, pick(blk_start), pick(counts)

    s = jnp.arange(n_steps, dtype=jnp.int32)
    live = s < total
    sl = jnp.minimum(s, total - 1)
    e, tile, b, nb_e, bs_e, cnt_e = locate(sl)
    blk = bs_e + b
    first = (live & (b == 0)).astype(jnp.int32)
    dead = jnp.maximum(s - total, 0)
    out_blk = jnp.where(live, blk, nvb + dead // n_tiles)
    out_tile = jnp.where(live, tile, dead % n_tiles)
    pieces = (jnp.clip(cnt_e - b * tm, 0, tm) + EXPERT_ROW_STEP - 1) // EXPERT_ROW_STEP
    pieces = jnp.where(live, pieces, 0)
    nxt = sl - b + nb_e
    has_next = (live & (nxt < total)).astype(jnp.int32)
    ne, nt, _, _, _, _ = locate(jnp.minimum(nxt, total - 1))
    as_i32 = lambda a: a.astype(jnp.int32)
    return (as_i32(e), as_i32(tile), as_i32(blk), first, as_i32(out_blk), as_i32(out_tile), as_i32(total)[None],
            as_i32(ne), as_i32(nt), has_next, as_i32(pieces))


def _gateup_body(se_ref, st_ref, sb_ref, sf_ref, ob_ref, ot_ref, nl_ref, ne_ref, nt_ref, hn_ref, np_ref,
                 xs_ref, wgu_hbm, bg_ref, bu_ref, act_ref, stage_ref, wgb_ref, wub_ref, sem, *, tf, d_ff):
    s = pl.program_id(0)
    D = stage_ref.shape[1]
    tm = act_ref.shape[0]

    def tile_copies(e, t):
        return [pltpu.make_async_copy(
            wgu_hbm.at[e, pl.ds(0, D), pl.ds(pl.multiple_of(half * d_ff + t * tf, tf), tf)],
            stage_ref.at[half], sem.at[half]) for half in range(2)]

    @pl.when(sf_ref[s] == 1)
    def _():
        @pl.when(s == 0)
        def _():
            for c in tile_copies(se_ref[0], st_ref[0]):
                c.start()

        for c in tile_copies(se_ref[s], st_ref[s]):
            c.wait()

        def cast_rows(r, carry):
            rows = pl.ds(pl.multiple_of(r * CAST_ROWS, CAST_ROWS), CAST_ROWS)
            wgb_ref[rows, :] = stage_ref[0, rows, :].astype(BF16)
            wub_ref[rows, :] = stage_ref[1, rows, :].astype(BF16)
            return carry

        lax.fori_loop(0, D // CAST_ROWS, cast_rows, 0)

        @pl.when(hn_ref[s] == 1)
        def _():
            for c in tile_copies(ne_ref[s], nt_ref[s]):
                c.start()

    for pieces in range(1, tm // EXPERT_ROW_STEP + 1):
        @pl.when(np_ref[s] == pieces)
        def _(m=pieces * EXPERT_ROW_STEP):
            xb = _slab_load(xs_ref, m).astype(BF16)
            gate = jnp.dot(xb, wgb_ref[...], preferred_element_type=F32) + bg_ref[0]
            up = jnp.dot(xb, wub_ref[...], preferred_element_type=F32) + bu_ref[0]
            gate = jnp.minimum(gate, SWIGLU_LIMIT)
            up = jnp.clip(up, -SWIGLU_LIMIT, SWIGLU_LIMIT)
            glu = gate * jax.nn.sigmoid(gate * SWIGLU_ALPHA)
            act_ref[0:m, :] = ((up + 1.0) * glu).astype(act_ref.dtype)
            if m < tm:
                act_ref[m:tm, :] = jnp.zeros((tm - m, act_ref.shape[1]), act_ref.dtype)

    @pl.when(np_ref[s] == 0)
    def _():
        act_ref[...] = jnp.zeros_like(act_ref)


def _gateup(xs, sched, wgu, bgu, *, tm, tf):
    D = wgu.shape[1]
    R = xs.shape[0] // SLAB_PITCH
    d_ff = wgu.shape[2] // 2
    nj = d_ff // tf
    nb = R // tm
    assert d_ff % tf == 0 and R % tm == 0
    grid_spec = pltpu.PrefetchScalarGridSpec(
        num_scalar_prefetch=11,
        grid=(nb * nj,),
        in_specs=[
            pl.BlockSpec((tm * SLAB_PITCH, LANES), lambda s, se, st, sb, *_: (sb[s], 0)),
            pl.BlockSpec(memory_space=pl.ANY),
            pl.BlockSpec((1, 1, tf), lambda s, se, st, *_: (se[s], 0, st[s])),
            pl.BlockSpec((1, 1, tf), lambda s, se, st, *_: (se[s], 0, nj + st[s])),
        ],
        out_specs=pl.BlockSpec((tm, tf), lambda s, se, st, sb, sf, ob, ot, *_: (ob[s], ot[s])),
        scratch_shapes=[pltpu.VMEM((2, D, tf), F32), pltpu.VMEM((D, tf), BF16), pltpu.VMEM((D, tf), BF16),
                        pltpu.SemaphoreType.DMA((2,))],
    )
    return pl.pallas_call(
        functools.partial(_gateup_body, tf=tf, d_ff=d_ff),
        grid_spec=grid_spec,
        out_shape=jax.ShapeDtypeStruct((R, d_ff), BF16),
        compiler_params=_cparams(("arbitrary",)),
        name="gateup",
    )(*sched, xs, wgu, bgu, bgu)


def _down_body(se_ref, st_ref, sb_ref, sf_ref, ob_ref, ot_ref, nl_ref, ne_ref, nt_ref, hn_ref, np_ref,
               act_ref, wd_hbm, bd_ref, y_ref, stage_ref, wdb_ref, sem):
    s = pl.program_id(0)
    tm = act_ref.shape[0]

    def weight_copy(e):
        return pltpu.make_async_copy(wd_hbm.at[e], stage_ref, sem.at[0])

    @pl.when(sf_ref[s] == 1)
    def _():
        @pl.when(s == 0)
        def _():
            weight_copy(se_ref[0]).start()

        weight_copy(se_ref[s]).wait()
        wdb_ref[...] = stage_ref[...].astype(BF16)

        @pl.when(hn_ref[s] == 1)
        def _():
            weight_copy(ne_ref[s]).start()

    for pieces in range(1, tm // EXPERT_ROW_STEP + 1):
        @pl.when(np_ref[s] == pieces)
        def _(m=pieces * EXPERT_ROW_STEP):
            y = jnp.dot(act_ref[0:m, :], wdb_ref[...], preferred_element_type=F32) + bd_ref[0]
            _slab_store(y_ref, y)
            if m < tm:
                y_ref[m * SLAB_PITCH:tm * SLAB_PITCH, :] = jnp.zeros(((tm - m) * SLAB_PITCH, LANES), U32)

    @pl.when(np_ref[s] == 0)
    def _():
        y_ref[...] = jnp.zeros_like(y_ref)


def _down(act, sched, wd, bd, *, tm):
    R, d_ff = act.shape
    D = wd.shape[2]
    nb = R // tm
    grid_spec = pltpu.PrefetchScalarGridSpec(
        num_scalar_prefetch=11,
        grid=(nb,),
        in_specs=[
            pl.BlockSpec((tm, d_ff), lambda s, se, st, sb, *_: (sb[s], 0)),
            pl.BlockSpec(memory_space=pl.ANY),
            pl.BlockSpec((1, 1, D), lambda s, se, *_: (se[s], 0, 0)),
        ],
        out_specs=pl.BlockSpec((tm * SLAB_PITCH, LANES), lambda s, se, st, sb, sf, ob, *_: (ob[s], 0)),
        scratch_shapes=[pltpu.VMEM((d_ff, D), F32), pltpu.VMEM((d_ff, D), BF16), pltpu.SemaphoreType.DMA((1,))],
    )
    return pl.pallas_call(
        _down_body,
        grid_spec=grid_spec,
        out_shape=jax.ShapeDtypeStruct((R * SLAB_PITCH, LANES), U32),
        compiler_params=_cparams(("arbitrary",)),
        name="down",
    )(*sched, act, wd, bd)


def _combine_body(dest_cur_ref, dest_nxt_ref, x2_ref, route_ref, gf_ref, ys_hbm, o_ref, buf0_ref, buf1_ref, sem, *,
                  tm):
    i = pl.program_id(0)
    last = pl.num_programs(0) - 1
    bufs = (buf0_ref, buf1_ref)

    def slab_copy(dest_ref, t, k, slot):
        d = dest_ref[0, 0, t * TOP_K + k]
        return pltpu.make_async_copy(ys_hbm.at[pl.ds(d * SLAB_PITCH, SLAB_ROWS)],
                                     bufs[slot].at[k, pl.ds(t * SLAB_PITCH, SLAB_ROWS)], sem.at[slot])

    def wait_rows(slot):
        for k in range(TOP_K):
            pltpu.make_async_copy(ys_hbm.at[pl.ds(0, tm * SLAB_ROWS)], bufs[slot].at[k, pl.ds(0, tm * SLAB_ROWS)],
                                  sem.at[slot]).wait()

    @pl.when(i == 0)
    def _():
        def issue(t, carry):
            for k in range(TOP_K):
                slab_copy(dest_cur_ref, t, k, 0).start(priority=k % DMA_QUEUES)
            return carry
        lax.fori_loop(0, tm, issue, 0, unroll=4)

    piece = tm // COMBINE_PIECES
    for slot in range(2):
        @pl.when(i % 2 == slot)
        def _(slot=slot):
            wait_rows(slot)
            for p in range(COMBINE_PIECES):
                r0 = p * piece
                for t in range(r0, r0 + piece):
                    for k in range(TOP_K):
                        slab_copy(dest_nxt_ref, t, k, 1 - slot).start(priority=k % DMA_QUEUES)
                x3 = x2_ref[r0:r0 + piece, :]
                route = route_ref[r0:r0 + piece, :]
                for k in range(TOP_K):
                    x3 = x3 + route[:, TOP_K + k:TOP_K + k + 1] * _slab_load(bufs[slot].at[k], piece, row0=r0)
                ms = jnp.mean(x3 * x3, axis=-1, keepdims=True)
                o_ref[r0:r0 + piece, :] = x3 * lax.rsqrt(ms + EPS) * gf_ref[...]

            @pl.when(i == last)
            def _():
                wait_rows(1 - slot)


def _combine(dest, x2, route, gf, ys_slab, *, tm):
    T, D = x2.shape
    n = T // tm
    assert T % tm == 0
    return pl.pallas_call(
        functools.partial(_combine_body, tm=tm),
        grid=(n,),
        in_specs=[
            pl.BlockSpec((1, 1, tm * TOP_K), lambda i: (i, 0, 0), memory_space=pltpu.SMEM),
            pl.BlockSpec((1, 1, tm * TOP_K), lambda i: (jnp.minimum(i + 1, n - 1), 0, 0), memory_space=pltpu.SMEM),
            pl.BlockSpec((tm, D), lambda i: (i, 0)),
            pl.BlockSpec((tm, LANES), lambda i: (i, 0)),
            pl.BlockSpec((1, D), lambda i: (0, 0)),
            pl.BlockSpec(memory_space=pl.ANY),
        ],
        out_specs=pl.BlockSpec((tm, D), lambda i: (i, 0)),
        out_shape=jax.ShapeDtypeStruct((T, D), F32),
        scratch_shapes=[pltpu.VMEM((TOP_K, tm * SLAB_PITCH, LANES), U32),
                        pltpu.VMEM((TOP_K, tm * SLAB_PITCH, LANES), U32), pltpu.SemaphoreType.DMA((2,))],
        compiler_params=_cparams(("arbitrary",)),
        name="combine",
    )(dest, dest, x2, route, gf, ys_slab)


def _retention_tables(S, dk):
    pos = jnp.arange(S, dtype=F32)
    inv_freq = ROPE_BASE ** (-jnp.arange(0, dk, 2, dtype=F32) / dk)
    ang = pos[:, None] * inv_freq[None, :]
    cos, sin = jnp.cos(ang), jnp.sin(ang)
    cos2 = jnp.concatenate([cos, cos], axis=-1)
    sin2 = jnp.concatenate([-sin, sin], axis=-1)
    log_gamma = jnp.log1p(-jnp.exp2(-5.0 - jnp.arange(RET_HEADS, dtype=F32)))
    idx = jnp.arange(CHUNK, dtype=F32)
    diff = idx[:, None] - idx[None, :]
    decay = jnp.where(diff[None] >= 0, jnp.exp(log_gamma[:, None, None] * jnp.maximum(diff, 0.0)[None]), 0.0)
    xi = jnp.exp(log_gamma[:, None] * (idx[None, :] + 1.0))
    zeta = jnp.exp(log_gamma[:, None] * (CHUNK - 1.0 - idx[None, :]))
    cd = jnp.exp(log_gamma * CHUNK)
    xi_b = jnp.broadcast_to(xi[:, :, None], (RET_HEADS, CHUNK, LANES))
    zeta_b = jnp.broadcast_to(zeta[:, :, None], (RET_HEADS, CHUNK, dk))
    return cos2, sin2, decay, xi_b, zeta_b, cd


TM_INPROJ = 1024
TN_INPROJ = 1024
TM_MIXING = 256
TM_MERGE = 512
MERGE_SUB_ROWS = 512
TM_EXPERT = 512
EXPERT_ROW_STEP = 128
CAST_ROWS = 128
TF_EXPERT = 1024
TM_COMBINE = 256
COMBINE_PIECES = 8


def kernel(x, norm1_g, w_in, gm_ln_g, gm_ln_b, gm_ws, gm_b, w_proj_a, w_proj_r, w_out, norm2_g, router_w, router_b,
           w_gate_up, b_gate_up, w_down, b_down, final_norm_g):
    B, S, D = x.shape
    assert B == 1 and norm1_g.shape[0] == 1, "single sequence, depth 1"
    gm_width = w_proj_a.shape[1]
    v_width = w_proj_r.shape[1]
    qk_width = (w_in.shape[2] - 2 * gm_width - 2 * v_width - 2 * D) // 2
    dk = qk_width // RET_HEADS
    assert dk == LANES and gm_ws.shape[2] == CHUNK and D == 2 * SLAB_ROWS * LANES
    G = gm_ws.shape[1]
    E = router_w.shape[2]
    d_ff = w_down.shape[2]
    x2d = x.reshape(S, D)

    cos2, sin2, decay, xi_b, zeta_b, cd = _retention_tables(S, dk)
    tm_p = min(TM_INPROJ, S)
    h = _prenorm(x2d, norm1_g[0][None], tm=tm_p)
    segments = (
        (gm_width, "gelu", ()),
        (gm_width, "gelu_layernorm", (gm_ln_g[0][None], gm_ln_b[0][None])),
        (qk_width, "rope", (cos2, sin2)),
        (qk_width, "rope_scaled", (cos2, sin2)),
        (v_width, "identity", ()),
        (v_width, "silu", ()),
        (2 * D, "sigmoid", ()),
    )
    outs, col0 = [], 0
    for width, kind, extras in segments:
        outs.append(_segment(h, w_in[0], col0, width, kind, extras, tm=tm_p, tn=TN_INPROJ))
        col0 += width
    a_u, a_vn, r_q, r_k, r_v, r_sg, gates = outs

    causal = jnp.tril(jnp.ones((CHUNK, CHUNK), dtype=bool))
    wm = jnp.where(causal[None], gm_ws[0], 0.0).astype(BF16)
    bs_b = jnp.broadcast_to(gm_b[0][:, :, None], (G, CHUNK, GM_GROUP_DIM))
    ga, gr = _mixing(a_u, a_vn, r_q, r_k, r_v, r_sg, wm, bs_b, decay, xi_b, zeta_b, cd, tm=min(TM_MIXING, S))

    rw = jnp.pad(router_w[0], ((0, 0), (0, LANES - E)))
    rw_hi = rw.astype(BF16)
    rw_lo = (rw - rw_hi.astype(F32)).astype(BF16)
    rb = jnp.pad(router_b[0], (0, LANES - E))[None]
    tm_m = min(TM_MERGE, S)
    merged = _branches(ga, gr, gates, w_proj_a[0].astype(BF16), w_proj_r[0].astype(BF16), tm=tm_m)
    x2, h2, route, cnt = _merge(x2d, merged, w_out[0].astype(BF16), norm2_g[0][None], rw_hi, rw_lo, rb, tm=tm_m)

    tm_e = TM_EXPERT
    eidx = route[:, 0:TOP_K].astype(jnp.int32)
    rank = route[:, 2 * TOP_K:3 * TOP_K].astype(jnp.int32)
    counts = cnt[0, :E].astype(jnp.int32)
    nblk = (counts + tm_e - 1) // tm_e
    blk_end = jnp.cumsum(nblk)
    blk_start = blk_end - nblk
    dest = (blk_start * tm_e)[eidx] + rank
    nb = (S * TOP_K + tm_e - 1) // tm_e + E
    nvb = blk_end[-1].astype(jnp.int32)
    blk_ids = jnp.arange(nb, dtype=jnp.int32)
    is_last = jnp.any((blk_ids[:, None] == blk_end[None, :] - 1) & (nblk[None, :] > 0), axis=1)
    clear_flag = (is_last | (blk_ids >= nvb)).astype(jnp.int32)

    tm_c = min(TM_COMBINE, S)
    dest_blocks = dest.reshape(S // tm_c, 1, tm_c * TOP_K)
    xs = _dispatch(h2, dest_blocks, clear_flag, tm=tm_c, tm_e=tm_e)
    tf = min(TF_EXPERT, d_ff)
    act = _gateup(xs, _tile_schedule(counts, nblk, blk_start, nvb, d_ff // tf, nb, tm_e), w_gate_up[0],
                  b_gate_up[0][:, None, :],
                  tm=tm_e, tf=tf)
    ys = _down(act, _tile_schedule(counts, nblk, blk_start, nvb, 1, nb, tm_e), w_down[0], b_down[0][:, None, :],
               tm=tm_e)

    out = _combine(dest_blocks, x2, route, final_norm_g[None], ys, tm=tm_c)
    return out.reshape(B, S, D)
```

```python
import functools

import jax
import jax.numpy as jnp
from jax import lax
from jax.experimental import pallas as pl
from jax.experimental.pallas import tpu as pltpu

F32 = jnp.float32
BF16 = jnp.bfloat16
U32 = jnp.uint32

CHUNK = 128
GM_GROUP_DIM = 128
RET_HEADS = 8
N_EXPERTS = 32
TOP_K = 4
SWIGLU_LIMIT = 7.0
SWIGLU_ALPHA = 1.702
ROPE_BASE = 10000.0
EPS = 1e-6
LANES = 128

VMEM_LIMIT_BYTES = 56 * 1024 * 1024


def _cparams(sem):
    return pltpu.CompilerParams(dimension_semantics=sem, vmem_limit_bytes=VMEM_LIMIT_BYTES)


SLAB_ROWS = 8
SLAB_PITCH = 9
DMA_QUEUES = 2


def _slab_store(ref, value, row0=0):
    rows, width = value.shape
    half = width // 2
    assert half == SLAB_ROWS * LANES
    lo = lax.bitcast_convert_type(value[:, :half].astype(BF16).astype(F32), U32)
    hi = lax.bitcast_convert_type(value[:, half:].astype(BF16).astype(F32), U32)
    words = (lo >> 16) | hi
    base = row0 * SLAB_PITCH
    for c in range(SLAB_ROWS):
        ref[pl.ds(base + c, rows, stride=SLAB_PITCH), :] = words[:, c * LANES:(c + 1) * LANES]
    for c in range(SLAB_ROWS, SLAB_PITCH):
        ref[pl.ds(base + c, rows, stride=SLAB_PITCH), :] = jnp.zeros((rows, LANES), U32)


def _slab_load(ref, rows, row0=0):
    base = row0 * SLAB_PITCH
    words = jnp.concatenate([ref[pl.ds(base + c, rows, stride=SLAB_PITCH), :] for c in range(SLAB_ROWS)], axis=1)
    lo = lax.bitcast_convert_type(words << 16, F32)
    hi = lax.bitcast_convert_type(words & jnp.uint32(0xFFFF0000), F32)
    return jnp.concatenate([lo, hi], axis=1)


def _gelu_exact(a):
    return 0.5 * a * (1.0 + lax.erf(a * (2.0 ** -0.5)))


def _rope_heads(a, cos2, sin2):
    outs = []
    for hd in range(a.shape[1] // LANES):
        ah = a[:, hd * LANES:(hd + 1) * LANES]
        outs.append(ah * cos2 + pltpu.roll(ah, LANES // 2, axis=1) * sin2)
    return jnp.concatenate(outs, axis=1)


def _segment_body(h_ref, w_ref, *refs, kind, k_scale):
    wb_ref = refs[-1]
    o_ref = refs[-3] if kind == "prenorm_gelu" else refs[-2]

    @pl.when(pl.program_id(1) == 0)
    def _():
        def cast_rows(r, carry):
            rows = pl.ds(pl.multiple_of(r * CAST_ROWS, CAST_ROWS), CAST_ROWS)
            wb_ref[rows, :] = w_ref[rows, :].astype(BF16)
            return carry

        lax.fori_loop(0, w_ref.shape[0] // CAST_ROWS, cast_rows, 0)

    if kind == "prenorm_gelu":
        g_ref, hn_ref = refs[0], refs[-2]
        xf = h_ref[...]
        ms = jnp.mean(xf * xf, axis=-1, keepdims=True)
        lhs = (xf * lax.rsqrt(ms + EPS) * g_ref[...]).astype(BF16)
        hn_ref[...] = lhs
    else:
        lhs = h_ref[...]
    acc = jnp.dot(lhs, wb_ref[...], preferred_element_type=F32)
    if kind in ("gelu", "prenorm_gelu"):
        out = _gelu_exact(acc)
    elif kind == "gelu_layernorm":
        lng_ref, lnb_ref = refs[0], refs[1]
        vf = _gelu_exact(acc)
        mu = jnp.mean(vf, axis=-1, keepdims=True)
        vc = vf - mu
        var = jnp.mean(vc * vc, axis=-1, keepdims=True)
        out = vc * lax.rsqrt(var + EPS) * lng_ref[...] + lnb_ref[...]
    elif kind == "rope":
        out = _rope_heads(acc, refs[0][...], refs[1][...])
    elif kind == "rope_scaled":
        out = _rope_heads(acc, refs[0][...], refs[1][...]) * k_scale
    elif kind == "identity":
        out = acc
    elif kind == "silu":
        out = acc * jax.nn.sigmoid(acc)
    elif kind == "sigmoid":
        out = jax.nn.sigmoid(acc)
    else:
        raise ValueError(kind)
    o_ref[...] = out.astype(o_ref.dtype)


def _segment(h, w_in, col0, width, kind, extras=(), *, tm, tn):
    S, D = h.shape
    assert width % tn == 0 and col0 % tn == 0 and S % tm == 0
    blk0 = col0 // tn
    out_specs = [pl.BlockSpec((tm, tn), lambda j, i: (i, j))]
    out_shape = [jax.ShapeDtypeStruct((S, width), BF16)]
    if kind == "gelu_layernorm":
        assert width == tn
        extra_specs = [pl.BlockSpec((1, tn), lambda j, i: (0, 0))] * 2
    elif kind in ("rope", "rope_scaled"):
        extra_specs = [pl.BlockSpec((tm, LANES), lambda j, i: (i, 0))] * 2
    elif kind == "prenorm_gelu":
        assert width == tn
        extra_specs = [pl.BlockSpec((1, D), lambda j, i: (0, 0))]
        out_specs.append(pl.BlockSpec((tm, D), lambda j, i: (i, 0)))
        out_shape.append(jax.ShapeDtypeStruct((S, D), BF16))
    else:
        extra_specs = []
    return pl.pallas_call(
        functools.partial(_segment_body, kind=kind, k_scale=float(LANES) ** -0.5),
        grid=(width // tn, S // tm),
        in_specs=[
            pl.BlockSpec((tm, D), lambda j, i: (i, 0)),
            pl.BlockSpec((D, tn), lambda j, i: (0, blk0 + j)),
        ] + extra_specs,
        out_specs=out_specs,
        out_shape=out_shape,
        scratch_shapes=[pltpu.VMEM((D, tn), BF16)],
        compiler_params=_cparams(("arbitrary", "arbitrary")),
        name="inproj_" + kind,
    )(h, w_in, *extras)


def _mixing_body(cd_ref, u_ref, vn_ref, q_ref, k_ref, v_ref, sg_ref, wm_ref, bs_ref, decay_ref, xi_ref, zeta_ref,
                 ga_ref, gr_ref, state_ref, *, n_chunks, dk, dv):
    @pl.when(pl.program_id(0) == 0)
    def _():
        state_ref[...] = jnp.zeros_like(state_ref)

    def chunk(c, carry):
        rows = pl.ds(pl.multiple_of(c * CHUNK, CHUNK), CHUNK)
        for g in range(wm_ref.shape[0]):
            cols = slice(g * GM_GROUP_DIM, (g + 1) * GM_GROUP_DIM)
            mixed = jnp.dot(wm_ref[g], vn_ref[rows, cols], preferred_element_type=F32) + bs_ref[g]
            ga_ref[rows, cols] = (u_ref[rows, cols].astype(F32) * mixed).astype(ga_ref.dtype)
        for hd in range(RET_HEADS):
            qc = q_ref[rows, hd * dk:(hd + 1) * dk]
            kc = k_ref[rows, hd * dk:(hd + 1) * dk]
            vc = v_ref[rows, hd * dv:(hd + 1) * dv]
            st = state_ref[hd]
            scores = lax.dot_general(qc, kc, (((1,), (1,)), ((), ())), preferred_element_type=F32) * decay_ref[hd]
            inner = jnp.dot(scores.astype(BF16), vc, preferred_element_type=F32)
            cross = jnp.dot(qc, st.astype(BF16), preferred_element_type=F32)
            xi = xi_ref[hd]
            o = inner + cross * jnp.concatenate([xi] * (dv // LANES), axis=1)
            kz = (kc.astype(F32) * zeta_ref[hd]).astype(BF16)
            kv = lax.dot_general(kz, vc, (((0,), (0,)), ((), ())), preferred_element_type=F32)
            state_ref[hd] = st * cd_ref[hd] + kv
            mu = jnp.mean(o, axis=-1, keepdims=True)
            oc = o - mu
            var = jnp.mean(oc * oc, axis=-1, keepdims=True)
            on = oc * lax.rsqrt(var + EPS)
            gr_ref[rows, hd * dv:(hd + 1) * dv] = (on * sg_ref[rows, hd * dv:(hd + 1) * dv].astype(F32)).astype(
                gr_ref.dtype)
        return carry

    lax.fori_loop(0, n_chunks, chunk, 0, unroll=True)


def _mixing(u, vn, q, k, v, sg, wm, bs_b, decay, xi_b, zeta_b, cd, *, tm):
    S, gm_width = u.shape
    qk_width, v_width = q.shape[1], v.shape[1]
    assert S % tm == 0 and tm % CHUNK == 0
    dk, dv = qk_width // RET_HEADS, v_width // RET_HEADS
    G = wm.shape[0]
    body = functools.partial(_mixing_body, n_chunks=tm // CHUNK, dk=dk, dv=dv)
    const3 = lambda i: (0, 0, 0)
    rows = lambda i: (i, 0)
    return pl.pallas_call(
        body,
        grid=(S // tm,),
        in_specs=[
            pl.BlockSpec(memory_space=pltpu.SMEM),
            pl.BlockSpec((tm, gm_width), rows),
            pl.BlockSpec((tm, gm_width), rows),
            pl.BlockSpec((tm, qk_width), rows),
            pl.BlockSpec((tm, qk_width), rows),
            pl.BlockSpec((tm, v_width), rows),
            pl.BlockSpec((tm, v_width), rows),
            pl.BlockSpec((G, CHUNK, CHUNK), const3),
            pl.BlockSpec((G, CHUNK, GM_GROUP_DIM), const3),
            pl.BlockSpec((RET_HEADS, CHUNK, CHUNK), const3),
            pl.BlockSpec((RET_HEADS, CHUNK, LANES), const3),
            pl.BlockSpec((RET_HEADS, CHUNK, dk), const3),
        ],
        out_specs=[
            pl.BlockSpec((tm, gm_width), lambda i: (i, 0)),
            pl.BlockSpec((tm, v_width), lambda i: (i, 0)),
        ],
        out_shape=[
            jax.ShapeDtypeStruct((S, gm_width), BF16),
            jax.ShapeDtypeStruct((S, v_width), BF16),
        ],
        scratch_shapes=[pltpu.VMEM((RET_HEADS, dk, dv), F32)],
        compiler_params=_cparams(("arbitrary",)),
        name="mixing",
    )(cd, u, vn, q, k, v, sg, wm, bs_b, decay, xi_b, zeta_b)


def _branches_body(ga_ref, gr_ref, sa_ref, sr_ref, wa_ref, wr_ref, o_ref):
    y_a = jnp.dot(ga_ref[...], wa_ref[...], preferred_element_type=F32)
    y_r = jnp.dot(gr_ref[...], wr_ref[...], preferred_element_type=F32)
    o_ref[...] = (sa_ref[...].astype(F32) * y_a + sr_ref[...].astype(F32) * y_r).astype(o_ref.dtype)


def _branches(ga, gr, gates, wa, wr, *, tm):
    S = ga.shape[0]
    D = wa.shape[1]
    const = lambda i: (0, 0)
    resident = functools.partial(pl.BlockSpec, index_map=const, pipeline_mode=pl.Buffered(1))
    return pl.pallas_call(
        _branches_body,
        grid=(S // tm,),
        in_specs=[
            pl.BlockSpec((tm, ga.shape[1]), lambda i: (i, 0)),
            pl.BlockSpec((tm, gr.shape[1]), lambda i: (i, 0)),
            pl.BlockSpec((tm, D), lambda i: (i, 0)),
            pl.BlockSpec((tm, D), lambda i: (i, 1)),
            resident(wa.shape),
            resident(wr.shape),
        ],
        out_specs=pl.BlockSpec((tm, D), lambda i: (i, 0)),
        out_shape=jax.ShapeDtypeStruct((S, D), BF16),
        compiler_params=_cparams(("arbitrary",)),
        name="branches",
    )(ga, gr, gates, gates, wa, wr)


def _merge_body(x_ref, m_ref, wo_ref, g2_ref, rwh_ref, rwl_ref, rb_ref,
                x2_ref, h2_ref, route_ref, cnt_ref, run_ref):
    i = pl.program_id(0)
    tm = x_ref.shape[0]

    @pl.when(i == 0)
    def _():
        run_ref[...] = jnp.zeros_like(run_ref)

    sub = min(tm, MERGE_SUB_ROWS)
    lane = lax.broadcasted_iota(jnp.int32, (sub, LANES), 1)
    neg_inf = jnp.float32(-jnp.inf)
    r_iota = lax.broadcasted_iota(jnp.int32, (sub, sub), 0)
    c_iota = lax.broadcasted_iota(jnp.int32, (sub, sub), 1)
    strict_lower = (c_iota < r_iota).astype(BF16)
    run = run_ref[0:1, :]
    for j in range(tm // sub):
        rows = slice(j * sub, (j + 1) * sub)
        x2 = x_ref[rows, :] + jnp.dot(m_ref[rows, :], wo_ref[...], preferred_element_type=F32)
        x2_ref[rows, :] = x2
        ms = jnp.mean(x2 * x2, axis=-1, keepdims=True)
        h2 = x2 * lax.rsqrt(ms + EPS) * g2_ref[...]
        _slab_store(h2_ref, h2, row0=j * sub)

        h_hi = h2.astype(BF16)
        h_lo = (h2 - h_hi.astype(F32)).astype(BF16)
        logits = (jnp.dot(h_hi, rwh_ref[...], preferred_element_type=F32)
                  + jnp.dot(h_hi, rwl_ref[...], preferred_element_type=F32)
                  + jnp.dot(h_lo, rwh_ref[...], preferred_element_type=F32)) + rb_ref[...]
        work = jnp.where(lane < N_EXPERTS, logits, neg_inf)

        vals, idxs, sels = [], [], []
        for _ in range(TOP_K):
            m = jnp.max(work, axis=-1, keepdims=True)
            idx = jnp.min(jnp.where(work == m, lane, LANES), axis=-1, keepdims=True)
            sel = lane == idx
            vals.append(m)
            idxs.append(idx)
            sels.append(sel)
            work = jnp.where(sel, neg_inf, work)
        exps = [jnp.exp(v - vals[0]) for v in vals]
        denom = exps[0] + exps[1] + exps[2] + exps[3]
        gates = [e / denom for e in exps]

        onehot = jnp.zeros((sub, LANES), F32)
        for sel in sels:
            onehot = onehot + sel.astype(F32)
        before = jnp.dot(strict_lower, onehot.astype(BF16), preferred_element_type=F32) + run
        ranks = [jnp.sum(jnp.where(sel, before, 0.0), axis=-1, keepdims=True) for sel in sels]
        run = run + jnp.sum(onehot, axis=0, keepdims=True)

        route = jnp.zeros((sub, LANES), F32)
        for k in range(TOP_K):
            route = jnp.where(lane == k, idxs[k].astype(F32), route)
            route = jnp.where(lane == TOP_K + k, gates[k], route)
            route = jnp.where(lane == 2 * TOP_K + k, ranks[k], route)
        route_ref[rows, :] = route

    run_ref[...] = jnp.broadcast_to(run, run_ref.shape)
    cnt_ref[...] = jnp.broadcast_to(run, cnt_ref.shape)


def _merge(x2d, merged, wo, g2, rw_hi, rw_lo, rb, *, tm):
    S, D = x2d.shape
    assert S % tm == 0
    const = lambda i: (0, 0)
    resident = functools.partial(pl.BlockSpec, index_map=const, pipeline_mode=pl.Buffered(1))
    return pl.pallas_call(
        _merge_body,
        grid=(S // tm,),
        in_specs=[
            pl.BlockSpec((tm, D), lambda i: (i, 0)),
            pl.BlockSpec((tm, D), lambda i: (i, 0)),
            resident(wo.shape),
            pl.BlockSpec((1, D), const),
            resident(rw_hi.shape),
            resident(rw_lo.shape),
            pl.BlockSpec((1, LANES), const),
        ],
        out_specs=[
            pl.BlockSpec((tm, D), lambda i: (i, 0)),
            pl.BlockSpec((tm * SLAB_PITCH, LANES), lambda i: (i, 0)),
            pl.BlockSpec((tm, LANES), lambda i: (i, 0)),
            pl.BlockSpec((8, LANES), const),
        ],
        out_shape=[
            jax.ShapeDtypeStruct((S, D), F32),
            jax.ShapeDtypeStruct((S * SLAB_PITCH, LANES), U32),
            jax.ShapeDtypeStruct((S, LANES), F32),
            jax.ShapeDtypeStruct((8, LANES), F32),
        ],
        scratch_shapes=[pltpu.VMEM((8, LANES), F32)],
        compiler_params=_cparams(("arbitrary",)),
        name="merge",
    )(x2d, merged, wo, g2, rw_hi, rw_lo, rb)


def _dispatch_body(clear_ref, dest_ref, h2_ref, xs_hbm, zeros_ref, sem, clear_sem, *, tm, blk_rows, nb):
    i = pl.program_id(0)

    def clear_copy(b):
        return pltpu.make_async_copy(zeros_ref, xs_hbm.at[pl.ds(pl.multiple_of(b * blk_rows, blk_rows), blk_rows)],
                                     clear_sem.at[0])

    @pl.when(i == 0)
    def _():
        zeros_ref[...] = jnp.zeros_like(zeros_ref)

        def start(b, carry):
            @pl.when(clear_ref[b] == 1)
            def _():
                clear_copy(b).start()
            return carry

        def drain(b, carry):
            @pl.when(clear_ref[b] == 1)
            def _():
                clear_copy(b).wait()
            return carry

        lax.fori_loop(0, nb, start, 0)
        lax.fori_loop(0, nb, drain, 0)

    def issue(t, carry):
        for k in range(TOP_K):
            d = dest_ref[0, 0, t * TOP_K + k]
            pltpu.make_async_copy(h2_ref.at[pl.ds(t * SLAB_PITCH, SLAB_PITCH)],
                                  xs_hbm.at[pl.ds(d * SLAB_PITCH, SLAB_PITCH)],
                                  sem.at[0]).start(priority=k % DMA_QUEUES)
        return carry

    lax.fori_loop(0, tm, issue, 0, unroll=4)
    for k in range(TOP_K):
        pltpu.make_async_copy(h2_ref, xs_hbm.at[pl.ds(0, tm * SLAB_PITCH)], sem.at[0]).wait()


def _dispatch(h2_slab, dest, clear_flag, *, tm, tm_e):
    nb = clear_flag.shape[0]
    blk_rows = tm_e * SLAB_PITCH
    n = h2_slab.shape[0] // (tm * SLAB_PITCH)
    grid_spec = pltpu.PrefetchScalarGridSpec(
        num_scalar_prefetch=1,
        grid=(n,),
        in_specs=[
            pl.BlockSpec((1, 1, tm * TOP_K), lambda i, cf: (i, 0, 0), memory_space=pltpu.SMEM),
            pl.BlockSpec((tm * SLAB_PITCH, LANES), lambda i, cf: (i, 0)),
        ],
        out_specs=pl.BlockSpec(memory_space=pl.ANY),
        scratch_shapes=[pltpu.VMEM((blk_rows, LANES), U32), pltpu.SemaphoreType.DMA((1,)),
                        pltpu.SemaphoreType.DMA((1,))],
    )
    return pl.pallas_call(
        functools.partial(_dispatch_body, tm=tm, blk_rows=blk_rows, nb=nb),
        grid_spec=grid_spec,
        out_shape=jax.ShapeDtypeStruct((nb * blk_rows, LANES), U32),
        compiler_params=_cparams(("arbitrary",)),
        name="dispatch",
    )(clear_flag, dest, h2_slab)


def _tile_schedule(counts, nblk, blk_start, nvb, n_tiles, nb, tm):
    E = nblk.shape[0]
    n_steps = nb * n_tiles
    steps_per_e = n_tiles * nblk
    e_end = jnp.cumsum(steps_per_e)
    e_start = e_end - steps_per_e
    total = e_end[-1]
    e_ids = jnp.arange(E, dtype=jnp.int32)

    def locate(idx):
        inside = (e_start[None, :] <= idx[:, None]) & (idx[:, None] < e_end[None, :])
        pick = lambda v: jnp.sum(jnp.where(inside, v[None, :], 0), axis=1)
        r = idx - pick(e_start)
        nb_e = jnp.maximum(pick(nblk), 1)
        tile = sum((r >= t * nb_e).astype(jnp.int32) for t in range(1, n_tiles)) if n_tiles > 1 else 0 * r
        return pick(e_ids), tile, r - tile * nb_e, nb_e, pick(blk_start), pick(counts)

    s = jnp.arange(n_steps, dtype=jnp.int32)
    live = s < total
    sl = jnp.minimum(s, total - 1)
    e, tile, b, nb_e, bs_e, cnt_e = locate(sl)
    blk = bs_e + b
    first = (live & (b == 0)).astype(jnp.int32)
    dead = jnp.maximum(s - total, 0)
    out_blk = jnp.where(live, blk, nvb + dead // n_tiles)
    out_tile = jnp.where(live, tile, dead % n_tiles)
    pieces = (jnp.clip(cnt_e - b * tm, 0, tm) + EXPERT_ROW_STEP - 1) // EXPERT_ROW_STEP
    pieces = jnp.where(live, pieces, 0)
    nxt = sl - b + nb_e
    has_next = (live & (nxt < total)).astype(jnp.int32)
    ne, nt, _, _, _, _ = locate(jnp.minimum(nxt, total - 1))
    as_i32 = lambda a: a.astype(jnp.int32)
    return (as_i32(e), as_i32(tile), as_i32(blk), first, as_i32(out_blk), as_i32(out_tile), as_i32(total)[None],
            as_i32(ne), as_i32(nt), has_next, as_i32(pieces))


def _gateup_body(se_ref, st_ref, sb_ref, sf_ref, ob_ref, ot_ref, nl_ref, ne_ref, nt_ref, hn_ref, np_ref,
                 xs_ref, wgu_hbm, bg_ref, bu_ref, act_ref, stage_ref, wgb_ref, wub_ref, sem, *, tf, d_ff):
    s = pl.program_id(0)
    D = stage_ref.shape[1]
    tm = act_ref.shape[0]

    def tile_copies(e, t):
        return [pltpu.make_async_copy(
            wgu_hbm.at[e, pl.ds(0, D), pl.ds(pl.multiple_of(half * d_ff + t * tf, tf), tf)],
            stage_ref.at[half], sem.at[half]) for half in range(2)]

    @pl.when(sf_ref[s] == 1)
    def _():
        @pl.when(s == 0)
        def _():
            for c in tile_copies(se_ref[0], st_ref[0]):
                c.start()

        for c in tile_copies(se_ref[s], st_ref[s]):
            c.wait()

        def cast_rows(r, carry):
            rows = pl.ds(pl.multiple_of(r * CAST_ROWS, CAST_ROWS), CAST_ROWS)
            wgb_ref[rows, :] = stage_ref[0, rows, :].astype(BF16)
            wub_ref[rows, :] = stage_ref[1, rows, :].astype(BF16)
            return carry

        lax.fori_loop(0, D // CAST_ROWS, cast_rows, 0)

        @pl.when(hn_ref[s] == 1)
        def _():
            for c in tile_copies(ne_ref[s], nt_ref[s]):
                c.start()

    for pieces in range(1, tm // EXPERT_ROW_STEP + 1):
        @pl.when(np_ref[s] == pieces)
        def _(m=pieces * EXPERT_ROW_STEP):
            xb = _slab_load(xs_ref, m).astype(BF16)
            gate = jnp.dot(xb, wgb_ref[...], preferred_element_type=F32) + bg_ref[0]
            up = jnp.dot(xb, wub_ref[...], preferred_element_type=F32) + bu_ref[0]
            gate = jnp.minimum(gate, SWIGLU_LIMIT)
            up = jnp.clip(up, -SWIGLU_LIMIT, SWIGLU_LIMIT)
            glu = gate * jax.nn.sigmoid(gate * SWIGLU_ALPHA)
            act_ref[0:m, :] = ((up + 1.0) * glu).astype(act_ref.dtype)
            if m < tm:
                act_ref[m:tm, :] = jnp.zeros((tm - m, act_ref.shape[1]), act_ref.dtype)

    @pl.when(np_ref[s] == 0)
    def _():
        act_ref[...] = jnp.zeros_like(act_ref)


def _gateup(xs, sched, wgu, bgu, *, tm, tf):
    D = wgu.shape[1]
    R = xs.shape[0] // SLAB_PITCH
    d_ff = wgu.shape[2] // 2
    nj = d_ff // tf
    nb = R // tm
    assert d_ff % tf == 0 and R % tm == 0
    grid_spec = pltpu.PrefetchScalarGridSpec(
        num_scalar_prefetch=11,
        grid=(nb * nj,),
        in_specs=[
            pl.BlockSpec((tm * SLAB_PITCH, LANES), lambda s, se, st, sb, *_: (sb[s], 0)),
            pl.BlockSpec(memory_space=pl.ANY),
            pl.BlockSpec((1, 1, tf), lambda s, se, st, *_: (se[s], 0, st[s])),
            pl.BlockSpec((1, 1, tf), lambda s, se, st, *_: (se[s], 0, nj + st[s])),
        ],
        out_specs=pl.BlockSpec((tm, tf), lambda s, se, st, sb, sf, ob, ot, *_: (ob[s], ot[s])),
        scratch_shapes=[pltpu.VMEM((2, D, tf), F32), pltpu.VMEM((D, tf), BF16), pltpu.VMEM((D, tf), BF16),
                        pltpu.SemaphoreType.DMA((2,))],
    )
    return pl.pallas_call(
        functools.partial(_gateup_body, tf=tf, d_ff=d_ff),
        grid_spec=grid_spec,
        out_shape=jax.ShapeDtypeStruct((R, d_ff), BF16),
        compiler_params=_cparams(("arbitrary",)),
        name="gateup",
    )(*sched, xs, wgu, bgu, bgu)


def _down_body(se_ref, st_ref, sb_ref, sf_ref, ob_ref, ot_ref, nl_ref, ne_ref, nt_ref, hn_ref, np_ref,
               act_ref, wd_hbm, bd_ref, y_ref, stage_ref, wdb_ref, sem):
    s = pl.program_id(0)
    tm = act_ref.shape[0]

    def weight_copy(e):
        return pltpu.make_async_copy(wd_hbm.at[e], stage_ref, sem.at[0])

    @pl.when(sf_ref[s] == 1)
    def _():
        @pl.when(s == 0)
        def _():
            weight_copy(se_ref[0]).start()

        weight_copy(se_ref[s]).wait()
        wdb_ref[...] = stage_ref[...].astype(BF16)

        @pl.when(hn_ref[s] == 1)
        def _():
            weight_copy(ne_ref[s]).start()

    for pieces in range(1, tm // EXPERT_ROW_STEP + 1):
        @pl.when(np_ref[s] == pieces)
        def _(m=pieces * EXPERT_ROW_STEP):
            y = jnp.dot(act_ref[0:m, :], wdb_ref[...], preferred_element_type=F32) + bd_ref[0]
            _slab_store(y_ref, y)
            if m < tm:
                y_ref[m * SLAB_PITCH:tm * SLAB_PITCH, :] = jnp.zeros(((tm - m) * SLAB_PITCH, LANES), U32)

    @pl.when(np_ref[s] == 0)
    def _():
        y_ref[...] = jnp.zeros_like(y_ref)


def _down(act, sched, wd, bd, *, tm):
    R, d_ff = act.shape
    D = wd.shape[2]
    nb = R // tm
    grid_spec = pltpu.PrefetchScalarGridSpec(
        num_scalar_prefetch=11,
        grid=(nb,),
        in_specs=[
            pl.BlockSpec((tm, d_ff), lambda s, se, st, sb, *_: (sb[s], 0)),
            pl.BlockSpec(memory_space=pl.ANY),
            pl.BlockSpec((1, 1, D), lambda s, se, *_: (se[s], 0, 0)),
        ],
        out_specs=pl.BlockSpec((tm * SLAB_PITCH, LANES), lambda s, se, st, sb, sf, ob, *_: (ob[s], 0)),
        scratch_shapes=[pltpu.VMEM((d_ff, D), F32), pltpu.VMEM((d_ff, D), BF16), pltpu.SemaphoreType.DMA((1,))],
    )
    return pl.pallas_call(
        _down_body,
        grid_spec=grid_spec,
        out_shape=jax.ShapeDtypeStruct((R * SLAB_PITCH, LANES), U32),
        compiler_params=_cparams(("arbitrary",)),
        name="down",
    )(*sched, act, wd, bd)


def _combine_body(dest_cur_ref, dest_nxt_ref, x2_ref, route_ref, gf_ref, ys_hbm, o_ref, buf0_ref, buf1_ref, sem, *,
                  tm):
    i = pl.program_id(0)
    last = pl.num_programs(0) - 1
    bufs = (buf0_ref, buf1_ref)

    def slab_copy(dest_ref, t, k, slot):
        d = dest_ref[0, 0, t * TOP_K + k]
        return pltpu.make_async_copy(ys_hbm.at[pl.ds(d * SLAB_PITCH, SLAB_ROWS)],
                                     bufs[slot].at[k, pl.ds(t * SLAB_PITCH, SLAB_ROWS)], sem.at[slot])

    def wait_rows(slot):
        for k in range(TOP_K):
            pltpu.make_async_copy(ys_hbm.at[pl.ds(0, tm * SLAB_ROWS)], bufs[slot].at[k, pl.ds(0, tm * SLAB_ROWS)],
                                  sem.at[slot]).wait()

    @pl.when(i == 0)
    def _():
        def issue(t, carry):
            for k in range(TOP_K):
                slab_copy(dest_cur_ref, t, k, 0).start(priority=k % DMA_QUEUES)
            return carry
        lax.fori_loop(0, tm, issue, 0, unroll=4)

    piece = tm // COMBINE_PIECES
    for slot in range(2):
        @pl.when(i % 2 == slot)
        def _(slot=slot):
            wait_rows(slot)
            for p in range(COMBINE_PIECES):
                r0 = p * piece
                for t in range(r0, r0 + piece):
                    for k in range(TOP_K):
                        slab_copy(dest_nxt_ref, t, k, 1 - slot).start(priority=k % DMA_QUEUES)
                x3 = x2_ref[r0:r0 + piece, :]
                route = route_ref[r0:r0 + piece, :]
                for k in range(TOP_K):
                    x3 = x3 + route[:, TOP_K + k:TOP_K + k + 1] * _slab_load(bufs[slot].at[k], piece, row0=r0)
                ms = jnp.mean(x3 * x3, axis=-1, keepdims=True)
                o_ref[r0:r0 + piece, :] = x3 * lax.rsqrt(ms + EPS) * gf_ref[...]

            @pl.when(i == last)
            def _():
                wait_rows(1 - slot)


def _combine(dest, x2, route, gf, ys_slab, *, tm):
    T, D = x2.shape
    n = T // tm
    assert T % tm == 0
    return pl.pallas_call(
        functools.partial(_combine_body, tm=tm),
        grid=(n,),
        in_specs=[
            pl.BlockSpec((1, 1, tm * TOP_K), lambda i: (i, 0, 0), memory_space=pltpu.SMEM),
            pl.BlockSpec((1, 1, tm * TOP_K), lambda i: (jnp.minimum(i + 1, n - 1), 0, 0), memory_space=pltpu.SMEM),
            pl.BlockSpec((tm, D), lambda i: (i, 0)),
            pl.BlockSpec((tm, LANES), lambda i: (i, 0)),
            pl.BlockSpec((1, D), lambda i: (0, 0)),
            pl.BlockSpec(memory_space=pl.ANY),
        ],
        out_specs=pl.BlockSpec((tm, D), lambda i: (i, 0)),
        out_shape=jax.ShapeDtypeStruct((T, D), F32),
        scratch_shapes=[pltpu.VMEM((TOP_K, tm * SLAB_PITCH, LANES), U32),
                        pltpu.VMEM((TOP_K, tm * SLAB_PITCH, LANES), U32), pltpu.SemaphoreType.DMA((2,))],
        compiler_params=_cparams(("arbitrary",)),
        name="combine",
    )(dest, dest, x2, route, gf, ys_slab)


def _retention_tables(S, dk):
    pos = jnp.arange(S, dtype=F32)
    inv_freq = ROPE_BASE ** (-jnp.arange(0, dk, 2, dtype=F32) / dk)
    ang = pos[:, None] * inv_freq[None, :]
    cos, sin = jnp.cos(ang), jnp.sin(ang)
    cos2 = jnp.concatenate([cos, cos], axis=-1)
    sin2 = jnp.concatenate([-sin, sin], axis=-1)
    log_gamma = jnp.log1p(-jnp.exp2(-5.0 - jnp.arange(RET_HEADS, dtype=F32)))
    idx = jnp.arange(CHUNK, dtype=F32)
    diff = idx[:, None] - idx[None, :]
    decay = jnp.where(diff[None] >= 0, jnp.exp(log_gamma[:, None, None] * jnp.maximum(diff, 0.0)[None]), 0.0)
    xi = jnp.exp(log_gamma[:, None] * (idx[None, :] + 1.0))
    zeta = jnp.exp(log_gamma[:, None] * (CHUNK - 1.0 - idx[None, :]))
    cd = jnp.exp(log_gamma * CHUNK)
    xi_b = jnp.broadcast_to(xi[:, :, None], (RET_HEADS, CHUNK, LANES))
    zeta_b = jnp.broadcast_to(zeta[:, :, None], (RET_HEADS, CHUNK, dk))
    return cos2, sin2, decay, xi_b, zeta_b, cd


TM_INPROJ = 1024
TM_PRENORM = 512
TN_INPROJ = 1024
TM_MIXING = 256
TM_MERGE = 512
MERGE_SUB_ROWS = 512
TM_EXPERT = 512
EXPERT_ROW_STEP = 128
CAST_ROWS = 128
TF_EXPERT = 1024
TM_DISPATCH = 512
TM_COMBINE = 256
COMBINE_PIECES = 8


def kernel(x, norm1_g, w_in, gm_ln_g, gm_ln_b, gm_ws, gm_b, w_proj_a, w_proj_r, w_out, norm2_g, router_w, router_b,
           w_gate_up, b_gate_up, w_down, b_down, final_norm_g):
    B, S, D = x.shape
    assert B == 1 and norm1_g.shape[0] == 1, "single sequence, depth 1"
    gm_width = w_proj_a.shape[1]
    v_width = w_proj_r.shape[1]
    qk_width = (w_in.shape[2] - 2 * gm_width - 2 * v_width - 2 * D) // 2
    dk = qk_width // RET_HEADS
    assert dk == LANES and gm_ws.shape[2] == CHUNK and D == 2 * SLAB_ROWS * LANES
    G = gm_ws.shape[1]
    E = router_w.shape[2]
    d_ff = w_down.shape[2]
    x2d = x.reshape(S, D)

    cos2, sin2, decay, xi_b, zeta_b, cd = _retention_tables(S, dk)
    tm_p = min(TM_INPROJ, S)
    a_u, h = _segment(x2d, w_in[0], 0, gm_width, "prenorm_gelu", (norm1_g[0][None],),
                      tm=min(TM_PRENORM, S), tn=TN_INPROJ)
    segments = (
        (gm_width, "gelu_layernorm", (gm_ln_g[0][None], gm_ln_b[0][None])),
        (qk_width, "rope", (cos2, sin2)),
        (qk_width, "rope_scaled", (cos2, sin2)),
        (v_width, "identity", ()),
        (v_width, "silu", ()),
        (2 * D, "sigmoid", ()),
    )
    outs, col0 = [], gm_width
    for width, kind, extras in segments:
        outs.append(_segment(h, w_in[0], col0, width, kind, extras, tm=tm_p, tn=TN_INPROJ)[0])
        col0 += width
    a_vn, r_q, r_k, r_v, r_sg, gates = outs

    causal = jnp.tril(jnp.ones((CHUNK, CHUNK), dtype=bool))
    wm = jnp.where(causal[None], gm_ws[0], 0.0).astype(BF16)
    bs_b = jnp.broadcast_to(gm_b[0][:, :, None], (G, CHUNK, GM_GROUP_DIM))
    ga, gr = _mixing(a_u, a_vn, r_q, r_k, r_v, r_sg, wm, bs_b, decay, xi_b, zeta_b, cd, tm=min(TM_MIXING, S))

    rw = jnp.pad(router_w[0], ((0, 0), (0, LANES - E)))
    rw_hi = rw.astype(BF16)
    rw_lo = (rw - rw_hi.astype(F32)).astype(BF16)
    rb = jnp.pad(router_b[0], (0, LANES - E))[None]
    tm_m = min(TM_MERGE, S)
    merged = _branches(ga, gr, gates, w_proj_a[0].astype(BF16), w_proj_r[0].astype(BF16), tm=tm_m)
    x2, h2, route, cnt = _merge(x2d, merged, w_out[0].astype(BF16), norm2_g[0][None], rw_hi, rw_lo, rb, tm=tm_m)

    tm_e = TM_EXPERT
    eidx = route[:, 0:TOP_K].astype(jnp.int32)
    rank = route[:, 2 * TOP_K:3 * TOP_K].astype(jnp.int32)
    counts = cnt[0, :E].astype(jnp.int32)
    nblk = (counts + tm_e - 1) // tm_e
    blk_end = jnp.cumsum(nblk)
    blk_start = blk_end - nblk
    dest = (blk_start * tm_e)[eidx] + rank
    nb = (S * TOP_K + tm_e - 1) // tm_e + E
    nvb = blk_end[-1].astype(jnp.int32)
    blk_ids = jnp.arange(nb, dtype=jnp.int32)
    is_last = jnp.any((blk_ids[:, None] == blk_end[None, :] - 1) & (nblk[None, :] > 0), axis=1)
    clear_flag = (is_last | (blk_ids >= nvb)).astype(jnp.int32)

    tm_c = min(TM_COMBINE, S)
    dest_blocks = dest.reshape(S // tm_c, 1, tm_c * TOP_K)
    tm_d = min(TM_DISPATCH, S)
    xs = _dispatch(h2, dest.reshape(S // tm_d, 1, tm_d * TOP_K), clear_flag, tm=tm_d, tm_e=tm_e)
    tf = min(TF_EXPERT, d_ff)
    act = _gateup(xs, _tile_schedule(counts, nblk, blk_start, nvb, d_ff // tf, nb, tm_e), w_gate_up[0],
                  b_gate_up[0][:, None, :],
                  tm=tm_e, tf=tf)
    ys = _down(act, _tile_schedule(counts, nblk, blk_start, nvb, 1, nb, tm_e), w_down[0], b_down[0][:, None, :],
               tm=tm_e)

    out = _combine(dest_blocks, x2, route, final_norm_g[None], ys, tm=tm_c)
    return out.reshape(B, S, D)
```

```python
import functools

import jax
import jax.numpy as jnp
from jax import lax
from jax.experimental import pallas as pl
from jax.experimental.pallas import tpu as pltpu

F32 = jnp.float32
BF16 = jnp.bfloat16
U32 = jnp.uint32

CHUNK = 128
GM_GROUP_DIM = 128
RET_HEADS = 8
N_EXPERTS = 32
ROUTE_FIELDS = 16
TOP_K = 4
SWIGLU_LIMIT = 7.0
SWIGLU_ALPHA = 1.702
ROPE_BASE = 10000.0
EPS = 1e-6
LANES = 128

VMEM_LIMIT_BYTES = 56 * 1024 * 1024


def _cparams(sem):
    return pltpu.CompilerParams(dimension_semantics=sem, vmem_limit_bytes=VMEM_LIMIT_BYTES)


SLAB_ROWS = 8
SLAB_PITCH = 9
DMA_QUEUES = 2


def _slab_store(ref, value, row0=0):
    rows, width = value.shape
    half = width // 2
    assert half == SLAB_ROWS * LANES
    lo = lax.bitcast_convert_type(value[:, :half].astype(BF16).astype(F32), U32)
    hi = lax.bitcast_convert_type(value[:, half:].astype(BF16).astype(F32), U32)
    words = (lo >> 16) | hi
    base = row0 * SLAB_PITCH
    for c in range(SLAB_ROWS):
        ref[pl.ds(base + c, rows, stride=SLAB_PITCH), :] = words[:, c * LANES:(c + 1) * LANES]
    for c in range(SLAB_ROWS, SLAB_PITCH):
        ref[pl.ds(base + c, rows, stride=SLAB_PITCH), :] = jnp.zeros((rows, LANES), U32)


def _slab_load(ref, rows, row0=0):
    base = row0 * SLAB_PITCH
    words = jnp.concatenate([ref[pl.ds(base + c, rows, stride=SLAB_PITCH), :] for c in range(SLAB_ROWS)], axis=1)
    lo = lax.bitcast_convert_type(words << 16, F32)
    hi = lax.bitcast_convert_type(words & jnp.uint32(0xFFFF0000), F32)
    return jnp.concatenate([lo, hi], axis=1)


def _gelu_exact(a):
    return 0.5 * a * (1.0 + lax.erf(a * (2.0 ** -0.5)))


def _rope_heads(a, cos2, sin2):
    outs = []
    for hd in range(a.shape[1] // LANES):
        ah = a[:, hd * LANES:(hd + 1) * LANES]
        outs.append(ah * cos2 + pltpu.roll(ah, LANES // 2, axis=1) * sin2)
    return jnp.concatenate(outs, axis=1)


def _segment_body(h_ref, w_ref, *refs, kind, k_scale):
    wb_ref = refs[-1]
    o_ref = refs[-3] if kind == "prenorm_gelu" else refs[-2]

    @pl.when(pl.program_id(1) == 0)
    def _():
        def cast_rows(r, carry):
            rows = pl.ds(pl.multiple_of(r * CAST_ROWS, CAST_ROWS), CAST_ROWS)
            wb_ref[rows, :] = w_ref[rows, :].astype(BF16)
            return carry

        lax.fori_loop(0, w_ref.shape[0] // CAST_ROWS, cast_rows, 0)

    if kind == "prenorm_gelu":
        g_ref, hn_ref = refs[0], refs[-2]
        xf = h_ref[...]
        ms = jnp.mean(xf * xf, axis=-1, keepdims=True)
        lhs = (xf * lax.rsqrt(ms + EPS) * g_ref[...]).astype(BF16)
        hn_ref[...] = lhs
    else:
        lhs = h_ref[...]
    acc = jnp.dot(lhs, wb_ref[...], preferred_element_type=F32)
    if kind in ("gelu", "prenorm_gelu"):
        out = _gelu_exact(acc)
    elif kind == "gelu_layernorm":
        lng_ref, lnb_ref = refs[0], refs[1]
        vf = _gelu_exact(acc)
        mu = jnp.mean(vf, axis=-1, keepdims=True)
        vc = vf - mu
        var = jnp.mean(vc * vc, axis=-1, keepdims=True)
        out = vc * lax.rsqrt(var + EPS) * lng_ref[...] + lnb_ref[...]
    elif kind == "rope":
        out = _rope_heads(acc, refs[0][...], refs[1][...])
    elif kind == "rope_scaled":
        out = _rope_heads(acc, refs[0][...], refs[1][...]) * k_scale
    elif kind == "identity":
        out = acc
    elif kind == "silu":
        out = acc * jax.nn.sigmoid(acc)
    elif kind == "sigmoid":
        out = jax.nn.sigmoid(acc)
    else:
        raise ValueError(kind)
    o_ref[...] = out.astype(o_ref.dtype)


def _segment(h, w_in, col0, width, kind, extras=(), *, tm, tn):
    S, D = h.shape
    assert width % tn == 0 and col0 % tn == 0 and S % tm == 0
    blk0 = col0 // tn
    out_specs = [pl.BlockSpec((tm, tn), lambda j, i: (i, j))]
    out_shape = [jax.ShapeDtypeStruct((S, width), BF16)]
    if kind == "gelu_layernorm":
        assert width == tn
        extra_specs = [pl.BlockSpec((1, tn), lambda j, i: (0, 0))] * 2
    elif kind in ("rope", "rope_scaled"):
        extra_specs = [pl.BlockSpec((tm, LANES), lambda j, i: (i, 0))] * 2
    elif kind == "prenorm_gelu":
        assert width == tn
        extra_specs = [pl.BlockSpec((1, D), lambda j, i: (0, 0))]
        out_specs.append(pl.BlockSpec((tm, D), lambda j, i: (i, 0)))
        out_shape.append(jax.ShapeDtypeStruct((S, D), BF16))
    else:
        extra_specs = []
    return pl.pallas_call(
        functools.partial(_segment_body, kind=kind, k_scale=float(LANES) ** -0.5),
        grid=(width // tn, S // tm),
        in_specs=[
            pl.BlockSpec((tm, D), lambda j, i: (i, 0)),
            pl.BlockSpec((D, tn), lambda j, i: (0, blk0 + j)),
        ] + extra_specs,
        out_specs=out_specs,
        out_shape=out_shape,
        scratch_shapes=[pltpu.VMEM((D, tn), BF16)],
        compiler_params=_cparams(("arbitrary", "arbitrary")),
        name="inproj_" + kind,
    )(h, w_in, *extras)


def _mixing_body(cd_ref, u_ref, vn_ref, q_ref, k_ref, v_ref, sg_ref, wm_ref, bs_ref, decay_ref, xi_ref, zeta_ref,
                 ga_ref, gr_ref, state_ref, *, n_chunks, dk, dv):
    @pl.when(pl.program_id(0) == 0)
    def _():
        state_ref[...] = jnp.zeros_like(state_ref)

    def chunk(c, carry):
        rows = pl.ds(pl.multiple_of(c * CHUNK, CHUNK), CHUNK)
        for g in range(wm_ref.shape[0]):
            cols = slice(g * GM_GROUP_DIM, (g + 1) * GM_GROUP_DIM)
            mixed = jnp.dot(wm_ref[g], vn_ref[rows, cols], preferred_element_type=F32) + bs_ref[g]
            ga_ref[rows, cols] = (u_ref[rows, cols].astype(F32) * mixed).astype(ga_ref.dtype)
        for hd in range(RET_HEADS):
            qc = q_ref[rows, hd * dk:(hd + 1) * dk]
            kc = k_ref[rows, hd * dk:(hd + 1) * dk]
            vc = v_ref[rows, hd * dv:(hd + 1) * dv]
            st = state_ref[hd]
            scores = lax.dot_general(qc, kc, (((1,), (1,)), ((), ())), preferred_element_type=F32) * decay_ref[hd]
            inner = jnp.dot(scores.astype(BF16), vc, preferred_element_type=F32)
            cross = jnp.dot(qc, st.astype(BF16), preferred_element_type=F32)
            xi = xi_ref[hd]
            o = inner + cross * jnp.concatenate([xi] * (dv // LANES), axis=1)
            kz = (kc.astype(F32) * zeta_ref[hd]).astype(BF16)
            kv = lax.dot_general(kz, vc, (((0,), (0,)), ((), ())), preferred_element_type=F32)
            state_ref[hd] = st * cd_ref[hd] + kv
            mu = jnp.mean(o, axis=-1, keepdims=True)
            oc = o - mu
            var = jnp.mean(oc * oc, axis=-1, keepdims=True)
            on = oc * lax.rsqrt(var + EPS)
            gr_ref[rows, hd * dv:(hd + 1) * dv] = (on * sg_ref[rows, hd * dv:(hd + 1) * dv].astype(F32)).astype(
                gr_ref.dtype)
        return carry

    lax.fori_loop(0, n_chunks, chunk, 0, unroll=True)


def _mixing(u, vn, q, k, v, sg, wm, bs_b, decay, xi_b, zeta_b, cd, *, tm):
    S, gm_width = u.shape
    qk_width, v_width = q.shape[1], v.shape[1]
    assert S % tm == 0 and tm % CHUNK == 0
    dk, dv = qk_width // RET_HEADS, v_width // RET_HEADS
    G = wm.shape[0]
    body = functools.partial(_mixing_body, n_chunks=tm // CHUNK, dk=dk, dv=dv)
    const3 = lambda i: (0, 0, 0)
    rows = lambda i: (i, 0)
    return pl.pallas_call(
        body,
        grid=(S // tm,),
        in_specs=[
            pl.BlockSpec(memory_space=pltpu.SMEM),
            pl.BlockSpec((tm, gm_width), rows),
            pl.BlockSpec((tm, gm_width), rows),
            pl.BlockSpec((tm, qk_width), rows),
            pl.BlockSpec((tm, qk_width), rows),
            pl.BlockSpec((tm, v_width), rows),
            pl.BlockSpec((tm, v_width), rows),
            pl.BlockSpec((G, CHUNK, CHUNK), const3),
            pl.BlockSpec((G, CHUNK, GM_GROUP_DIM), const3),
            pl.BlockSpec((RET_HEADS, CHUNK, CHUNK), const3),
            pl.BlockSpec((RET_HEADS, CHUNK, LANES), const3),
            pl.BlockSpec((RET_HEADS, CHUNK, dk), const3),
        ],
        out_specs=[
            pl.BlockSpec((tm, gm_width), lambda i: (i, 0)),
            pl.BlockSpec((tm, v_width), lambda i: (i, 0)),
        ],
        out_shape=[
            jax.ShapeDtypeStruct((S, gm_width), BF16),
            jax.ShapeDtypeStruct((S, v_width), BF16),
        ],
        scratch_shapes=[pltpu.VMEM((RET_HEADS, dk, dv), F32)],
        compiler_params=_cparams(("arbitrary",)),
        name="mixing",
    )(cd, u, vn, q, k, v, sg, wm, bs_b, decay, xi_b, zeta_b)


def _branches_body(ga_ref, gr_ref, sa_ref, sr_ref, wa_ref, wr_ref, o_ref):
    y_a = jnp.dot(ga_ref[...], wa_ref[...], preferred_element_type=F32)
    y_r = jnp.dot(gr_ref[...], wr_ref[...], preferred_element_type=F32)
    o_ref[...] = (sa_ref[...].astype(F32) * y_a + sr_ref[...].astype(F32) * y_r).astype(o_ref.dtype)


def _branches(ga, gr, gates, wa, wr, *, tm):
    S = ga.shape[0]
    D = wa.shape[1]
    const = lambda i: (0, 0)
    resident = functools.partial(pl.BlockSpec, index_map=const, pipeline_mode=pl.Buffered(1))
    return pl.pallas_call(
        _branches_body,
        grid=(S // tm,),
        in_specs=[
            pl.BlockSpec((tm, ga.shape[1]), lambda i: (i, 0)),
            pl.BlockSpec((tm, gr.shape[1]), lambda i: (i, 0)),
            pl.BlockSpec((tm, D), lambda i: (i, 0)),
            pl.BlockSpec((tm, D), lambda i: (i, 1)),
            resident(wa.shape),
            resident(wr.shape),
        ],
        out_specs=pl.BlockSpec((tm, D), lambda i: (i, 0)),
        out_shape=jax.ShapeDtypeStruct((S, D), BF16),
        compiler_params=_cparams(("arbitrary",)),
        name="branches",
    )(ga, gr, gates, gates, wa, wr)


def _merge_body(x_ref, m_ref, wo_ref, g2_ref, rw_ref, rb_ref, x2_ref, h2_ref, route_ref, cnt_ref, run_ref):
    i = pl.program_id(0)
    tm = x_ref.shape[0]

    @pl.when(i == 0)
    def _():
        run_ref[...] = jnp.zeros_like(run_ref)

    sub = min(tm, MERGE_SUB_ROWS)
    neg_inf = jnp.float32(-jnp.inf)
    r_iota = lax.broadcasted_iota(jnp.int32, (sub, sub), 0)
    c_iota = lax.broadcasted_iota(jnp.int32, (sub, sub), 1)
    earlier = (r_iota < c_iota).astype(BF16)
    expert = lax.broadcasted_iota(jnp.int32, (N_EXPERTS, sub), 0)
    field = lax.broadcasted_iota(jnp.int32, (ROUTE_FIELDS, sub), 0)
    run = run_ref[:, 0:1]
    for j in range(tm // sub):
        rows = slice(j * sub, (j + 1) * sub)
        x2 = x_ref[rows, :] + jnp.dot(m_ref[rows, :], wo_ref[...], preferred_element_type=F32)
        x2_ref[rows, :] = x2
        ms = jnp.mean(x2 * x2, axis=-1, keepdims=True)
        h2 = x2 * lax.rsqrt(ms + EPS) * g2_ref[...]
        _slab_store(h2_ref, h2, row0=j * sub)

        h_hi = h2.astype(BF16)
        h_lo = (h2 - h_hi.astype(F32)).astype(BF16)
        hi_terms = jnp.dot(h_hi, rw_ref[...], preferred_element_type=F32)
        logits = (hi_terms[:, :LANES] + hi_terms[:, LANES:]
                  + jnp.dot(h_lo, rw_ref[:, :LANES], preferred_element_type=F32)) + rb_ref[...]
        work = logits.T[0:N_EXPERTS, :]

        vals, idxs, sels = [], [], []
        for _ in range(TOP_K):
            m = jnp.max(work, axis=0, keepdims=True)
            idx = jnp.min(jnp.where(work == m, expert, N_EXPERTS), axis=0, keepdims=True)
            sel = expert == idx
            vals.append(m)
            idxs.append(idx)
            sels.append(sel)
            work = jnp.where(sel, neg_inf, work)
        exps = [jnp.exp(v - vals[0]) for v in vals]
        denom = exps[0] + exps[1] + exps[2] + exps[3]
        gates = [e / denom for e in exps]

        onehot = jnp.zeros((N_EXPERTS, sub), F32)
        for sel in sels:
            onehot = onehot + sel.astype(F32)
        before = jnp.dot(onehot.astype(BF16), earlier, preferred_element_type=F32) + run
        ranks = [jnp.sum(jnp.where(sel, before, 0.0), axis=0, keepdims=True) for sel in sels]
        run = run + jnp.sum(onehot, axis=1, keepdims=True)

        fields = jnp.zeros((ROUTE_FIELDS, sub), F32)
        for k in range(TOP_K):
            fields = jnp.where(field == k, idxs[k].astype(F32), fields)
            fields = jnp.where(field == TOP_K + k, gates[k], fields)
            fields = jnp.where(field == 2 * TOP_K + k, ranks[k], fields)
        slab_t = jnp.concatenate([fields, jnp.zeros((LANES - ROUTE_FIELDS, sub), F32)], axis=0)
        route_ref[rows, :] = slab_t.T

    run_ref[...] = jnp.broadcast_to(run, run_ref.shape)
    cnt_ref[...] = jnp.broadcast_to(run, cnt_ref.shape)


def _merge(x2d, merged, wo, g2, rw_split, rb, *, tm):
    S, D = x2d.shape
    assert S % tm == 0
    const = lambda i: (0, 0)
    resident = functools.partial(pl.BlockSpec, index_map=const, pipeline_mode=pl.Buffered(1))
    return pl.pallas_call(
        _merge_body,
        grid=(S // tm,),
        in_specs=[
            pl.BlockSpec((tm, D), lambda i: (i, 0)),
            pl.BlockSpec((tm, D), lambda i: (i, 0)),
            resident(wo.shape),
            pl.BlockSpec((1, D), const),
            resident(rw_split.shape),
            pl.BlockSpec((1, LANES), const),
        ],
        out_specs=[
            pl.BlockSpec((tm, D), lambda i: (i, 0)),
            pl.BlockSpec((tm * SLAB_PITCH, LANES), lambda i: (i, 0)),
            pl.BlockSpec((tm, LANES), lambda i: (i, 0)),
            pl.BlockSpec((N_EXPERTS, LANES), const),
        ],
        out_shape=[
            jax.ShapeDtypeStruct((S, D), F32),
            jax.ShapeDtypeStruct((S * SLAB_PITCH, LANES), U32),
            jax.ShapeDtypeStruct((S, LANES), F32),
            jax.ShapeDtypeStruct((N_EXPERTS, LANES), F32),
        ],
        scratch_shapes=[pltpu.VMEM((N_EXPERTS, LANES), F32)],
        compiler_params=_cparams(("arbitrary",)),
        name="merge",
    )(x2d, merged, wo, g2, rw_split, rb)


def _dispatch_body(clear_ref, dest_ref, h2_ref, xs_hbm, zeros_ref, sem, clear_sem, *, tm, blk_rows, nb):
    i = pl.program_id(0)

    def clear_copy(b):
        return pltpu.make_async_copy(zeros_ref, xs_hbm.at[pl.ds(pl.multiple_of(b * blk_rows, blk_rows), blk_rows)],
                                     clear_sem.at[0])

    @pl.when(i == 0)
    def _():
        zeros_ref[...] = jnp.zeros_like(zeros_ref)

        def start(b, carry):
            @pl.when(clear_ref[b] == 1)
            def _():
                clear_copy(b).start()
            return carry

        def drain(b, carry):
            @pl.when(clear_ref[b] == 1)
            def _():
                clear_copy(b).wait()
            return carry

        lax.fori_loop(0, nb, start, 0)
        lax.fori_loop(0, nb, drain, 0)

    def issue(t, carry):
        for k in range(TOP_K):
            d = dest_ref[0, 0, t * TOP_K + k]
            pltpu.make_async_copy(h2_ref.at[pl.ds(t * SLAB_PITCH, SLAB_PITCH)],
                                  xs_hbm.at[pl.ds(d * SLAB_PITCH, SLAB_PITCH)],
                                  sem.at[0]).start(priority=k % DMA_QUEUES)
        return carry

    lax.fori_loop(0, tm, issue, 0, unroll=4)
    for k in range(TOP_K):
        pltpu.make_async_copy(h2_ref, xs_hbm.at[pl.ds(0, tm * SLAB_PITCH)], sem.at[0]).wait()


def _dispatch(h2_slab, dest, clear_flag, *, tm, tm_e):
    nb = clear_flag.shape[0]
    blk_rows = tm_e * SLAB_PITCH
    n = h2_slab.shape[0] // (tm * SLAB_PITCH)
    grid_spec = pltpu.PrefetchScalarGridSpec(
        num_scalar_prefetch=1,
        grid=(n,),
        in_specs=[
            pl.BlockSpec((1, 1, tm * TOP_K), lambda i, cf: (i, 0, 0), memory_space=pltpu.SMEM),
            pl.BlockSpec((tm * SLAB_PITCH, LANES), lambda i, cf: (i, 0)),
        ],
        out_specs=pl.BlockSpec(memory_space=pl.ANY),
        scratch_shapes=[pltpu.VMEM((blk_rows, LANES), U32), pltpu.SemaphoreType.DMA((1,)),
                        pltpu.SemaphoreType.DMA((1,))],
    )
    return pl.pallas_call(
        functools.partial(_dispatch_body, tm=tm, blk_rows=blk_rows, nb=nb),
        grid_spec=grid_spec,
        out_shape=jax.ShapeDtypeStruct((nb * blk_rows, LANES), U32),
        compiler_params=_cparams(("arbitrary",)),
        name="dispatch",
    )(clear_flag, dest, h2_slab)


def _tile_schedule(counts, nblk, blk_start, nvb, n_tiles, nb, tm):
    E = nblk.shape[0]
    n_steps = nb * n_tiles
    steps_per_e = n_tiles * nblk
    e_end = jnp.cumsum(steps_per_e)
    e_start = e_end - steps_per_e
    total = e_end[-1]
    e_ids = jnp.arange(E, dtype=jnp.int32)

    def locate(idx):
        inside = (e_start[None, :] <= idx[:, None]) & (idx[:, None] < e_end[None, :])
        pick = lambda v: jnp.sum(jnp.where(inside, v[None, :], 0), axis=1)
        r = idx - pick(e_start)
        nb_e = jnp.maximum(pick(nblk), 1)
        tile = sum((r >= t * nb_e).astype(jnp.int32) for t in range(1, n_tiles)) if n_tiles > 1 else 0 * r
        return pick(e_ids), tile, r - tile * nb_e, nb_e, pick(blk_start), pick(counts)

    s = jnp.arange(n_steps, dtype=jnp.int32)
    live = s < total
    sl = jnp.minimum(s, total - 1)
    e, tile, b, nb_e, bs_e, cnt_e = locate(sl)
    blk = bs_e + b
    first = (live & (b == 0)).astype(jnp.int32)
    dead = jnp.maximum(s - total, 0)
    out_blk = jnp.where(live, blk, nvb + dead // n_tiles)
    out_tile = jnp.where(live, tile, dead % n_tiles)
    pieces = (jnp.clip(cnt_e - b * tm, 0, tm) + EXPERT_ROW_STEP - 1) // EXPERT_ROW_STEP
    pieces = jnp.where(live, pieces, 0)
    nxt = sl - b + nb_e
    has_next = (live & (nxt < total)).astype(jnp.int32)
    ne, nt, _, _, _, _ = locate(jnp.minimum(nxt, total - 1))
    as_i32 = lambda a: a.astype(jnp.int32)
    return (as_i32(e), as_i32(tile), as_i32(blk), first, as_i32(out_blk), as_i32(out_tile), as_i32(total)[None],
            as_i32(ne), as_i32(nt), has_next, as_i32(pieces))


def _gateup_body(se_ref, st_ref, sb_ref, sf_ref, ob_ref, ot_ref, nl_ref, ne_ref, nt_ref, hn_ref, np_ref,
                 xs_ref, wgu_hbm, bg_ref, bu_ref, act_ref, stage_ref, wgb_ref, wub_ref, sem, *, tf, d_ff):
    s = pl.program_id(0)
    D = stage_ref.shape[1]
    tm = act_ref.shape[0]

    def tile_copies(e, t):
        return [pltpu.make_async_copy(
            wgu_hbm.at[e, pl.ds(0, D), pl.ds(pl.multiple_of(half * d_ff + t * tf, tf), tf)],
            stage_ref.at[half], sem.at[half]) for half in range(2)]

    @pl.when(sf_ref[s] == 1)
    def _():
        @pl.when(s == 0)
        def _():
            for c in tile_copies(se_ref[0], st_ref[0]):
                c.start()

        for c in tile_copies(se_ref[s], st_ref[s]):
            c.wait()

        def cast_rows(r, carry):
            rows = pl.ds(pl.multiple_of(r * CAST_ROWS, CAST_ROWS), CAST_ROWS)
            wgb_ref[rows, :] = stage_ref[0, rows, :].astype(BF16)
            wub_ref[rows, :] = stage_ref[1, rows, :].astype(BF16)
            return carry

        lax.fori_loop(0, D // CAST_ROWS, cast_rows, 0)

        @pl.when(hn_ref[s] == 1)
        def _():
            for c in tile_copies(ne_ref[s], nt_ref[s]):
                c.start()

    for pieces in range(1, tm // EXPERT_ROW_STEP + 1):
        @pl.when(np_ref[s] == pieces)
        def _(m=pieces * EXPERT_ROW_STEP):
            xb = _slab_load(xs_ref, m).astype(BF16)
            gate = jnp.dot(xb, wgb_ref[...], preferred_element_type=F32) + bg_ref[0]
            up = jnp.dot(xb, wub_ref[...], preferred_element_type=F32) + bu_ref[0]
            gate = jnp.minimum(gate, SWIGLU_LIMIT)
            up = jnp.clip(up, -SWIGLU_LIMIT, SWIGLU_LIMIT)
            glu = gate * jax.nn.sigmoid(gate * SWIGLU_ALPHA)
            act_ref[0:m, :] = ((up + 1.0) * glu).astype(act_ref.dtype)
            if m < tm:
                act_ref[m:tm, :] = jnp.zeros((tm - m, act_ref.shape[1]), act_ref.dtype)

    @pl.when(np_ref[s] == 0)
    def _():
        act_ref[...] = jnp.zeros_like(act_ref)


def _gateup(xs, sched, wgu, bgu, *, tm, tf):
    D = wgu.shape[1]
    R = xs.shape[0] // SLAB_PITCH
    d_ff = wgu.shape[2] // 2
    nj = d_ff // tf
    nb = R // tm
    assert d_ff % tf == 0 and R % tm == 0
    grid_spec = pltpu.PrefetchScalarGridSpec(
        num_scalar_prefetch=11,
        grid=(nb * nj,),
        in_specs=[
            pl.BlockSpec((tm * SLAB_PITCH, LANES), lambda s, se, st, sb, *_: (sb[s], 0)),
            pl.BlockSpec(memory_space=pl.ANY),
            pl.BlockSpec((1, 1, tf), lambda s, se, st, *_: (se[s], 0, st[s])),
            pl.BlockSpec((1, 1, tf), lambda s, se, st, *_: (se[s], 0, nj + st[s])),
        ],
        out_specs=pl.BlockSpec((tm, tf), lambda s, se, st, sb, sf, ob, ot, *_: (ob[s], ot[s])),
        scratch_shapes=[pltpu.VMEM((2, D, tf), F32), pltpu.VMEM((D, tf), BF16), pltpu.VMEM((D, tf), BF16),
                        pltpu.SemaphoreType.DMA((2,))],
    )
    return pl.pallas_call(
        functools.partial(_gateup_body, tf=tf, d_ff=d_ff),
        grid_spec=grid_spec,
        out_shape=jax.ShapeDtypeStruct((R, d_ff), BF16),
        compiler_params=_cparams(("arbitrary",)),
        name="gateup",
    )(*sched, xs, wgu, bgu, bgu)


def _down_body(se_ref, st_ref, sb_ref, sf_ref, ob_ref, ot_ref, nl_ref, ne_ref, nt_ref, hn_ref, np_ref,
               act_ref, wd_hbm, bd_ref, y_ref, stage_ref, wdb_ref, sem):
    s = pl.program_id(0)
    tm = act_ref.shape[0]

    def weight_copy(e):
        return pltpu.make_async_copy(wd_hbm.at[e], stage_ref, sem.at[0])

    @pl.when(sf_ref[s] == 1)
    def _():
        @pl.when(s == 0)
        def _():
            weight_copy(se_ref[0]).start()

        weight_copy(se_ref[s]).wait()
        wdb_ref[...] = stage_ref[...].astype(BF16)

        @pl.when(hn_ref[s] == 1)
        def _():
            weight_copy(ne_ref[s]).start()

    for pieces in range(1, tm // EXPERT_ROW_STEP + 1):
        @pl.when(np_ref[s] == pieces)
        def _(m=pieces * EXPERT_ROW_STEP):
            y = jnp.dot(act_ref[0:m, :], wdb_ref[...], preferred_element_type=F32) + bd_ref[0]
            _slab_store(y_ref, y)
            if m < tm:
                y_ref[m * SLAB_PITCH:tm * SLAB_PITCH, :] = jnp.zeros(((tm - m) * SLAB_PITCH, LANES), U32)

    @pl.when(np_ref[s] == 0)
    def _():
        y_ref[...] = jnp.zeros_like(y_ref)


def _down(act, sched, wd, bd, *, tm):
    R, d_ff = act.shape
    D = wd.shape[2]
    nb = R // tm
    grid_spec = pltpu.PrefetchScalarGridSpec(
        num_scalar_prefetch=11,
        grid=(nb,),
        in_specs=[
            pl.BlockSpec((tm, d_ff), lambda s, se, st, sb, *_: (sb[s], 0)),
            pl.BlockSpec(memory_space=pl.ANY),
            pl.BlockSpec((1, 1, D), lambda s, se, *_: (se[s], 0, 0)),
        ],
        out_specs=pl.BlockSpec((tm * SLAB_PITCH, LANES), lambda s, se, st, sb, sf, ob, *_: (ob[s], 0)),
        scratch_shapes=[pltpu.VMEM((d_ff, D), F32), pltpu.VMEM((d_ff, D), BF16), pltpu.SemaphoreType.DMA((1,))],
    )
    return pl.pallas_call(
        _down_body,
        grid_spec=grid_spec,
        out_shape=jax.ShapeDtypeStruct((R * SLAB_PITCH, LANES), U32),
        compiler_params=_cparams(("arbitrary",)),
        name="down",
    )(*sched, act, wd, bd)


def _combine_body(dest_cur_ref, dest_nxt_ref, x2_ref, route_ref, gf_ref, ys_hbm, o_ref, buf0_ref, buf1_ref, sem, *,
                  tm):
    i = pl.program_id(0)
    last = pl.num_programs(0) - 1
    bufs = (buf0_ref, buf1_ref)

    def slab_copy(dest_ref, t, k, slot):
        d = dest_ref[0, 0, t * TOP_K + k]
        return pltpu.make_async_copy(ys_hbm.at[pl.ds(d * SLAB_PITCH, SLAB_ROWS)],
                                     bufs[slot].at[k, pl.ds(t * SLAB_PITCH, SLAB_ROWS)], sem.at[slot])

    def wait_rows(slot):
        for k in range(TOP_K):
            pltpu.make_async_copy(ys_hbm.at[pl.ds(0, tm * SLAB_ROWS)], bufs[slot].at[k, pl.ds(0, tm * SLAB_ROWS)],
                                  sem.at[slot]).wait()

    @pl.when(i == 0)
    def _():
        def issue(t, carry):
            for k in range(TOP_K):
                slab_copy(dest_cur_ref, t, k, 0).start(priority=k % DMA_QUEUES)
            return carry
        lax.fori_loop(0, tm, issue, 0, unroll=4)

    piece = tm // COMBINE_PIECES
    for slot in range(2):
        @pl.when(i % 2 == slot)
        def _(slot=slot):
            wait_rows(slot)
            for p in range(COMBINE_PIECES):
                r0 = p * piece
                for t in range(r0, r0 + piece):
                    for k in range(TOP_K):
                        slab_copy(dest_nxt_ref, t, k, 1 - slot).start(priority=k % DMA_QUEUES)
                x3 = x2_ref[r0:r0 + piece, :]
                route = route_ref[r0:r0 + piece, :]
                for k in range(TOP_K):
                    x3 = x3 + route[:, TOP_K + k:TOP_K + k + 1] * _slab_load(bufs[slot].at[k], piece, row0=r0)
                ms = jnp.mean(x3 * x3, axis=-1, keepdims=True)
                o_ref[r0:r0 + piece, :] = x3 * lax.rsqrt(ms + EPS) * gf_ref[...]

            @pl.when(i == last)
            def _():
                wait_rows(1 - slot)


def _combine(dest, x2, route, gf, ys_slab, *, tm):
    T, D = x2.shape
    n = T // tm
    assert T % tm == 0
    return pl.pallas_call(
        functools.partial(_combine_body, tm=tm),
        grid=(n,),
        in_specs=[
            pl.BlockSpec((1, 1, tm * TOP_K), lambda i: (i, 0, 0), memory_space=pltpu.SMEM),
            pl.BlockSpec((1, 1, tm * TOP_K), lambda i: (jnp.minimum(i + 1, n - 1), 0, 0), memory_space=pltpu.SMEM),
            pl.BlockSpec((tm, D), lambda i: (i, 0)),
            pl.BlockSpec((tm, LANES), lambda i: (i, 0)),
            pl.BlockSpec((1, D), lambda i: (0, 0)),
            pl.BlockSpec(memory_space=pl.ANY),
        ],
        out_specs=pl.BlockSpec((tm, D), lambda i: (i, 0)),
        out_shape=jax.ShapeDtypeStruct((T, D), F32),
        scratch_shapes=[pltpu.VMEM((TOP_K, tm * SLAB_PITCH, LANES), U32),
                        pltpu.VMEM((TOP_K, tm * SLAB_PITCH, LANES), U32), pltpu.SemaphoreType.DMA((2,))],
        compiler_params=_cparams(("arbitrary",)),
        name="combine",
    )(dest, dest, x2, route, gf, ys_slab)


def _retention_tables(S, dk):
    pos = jnp.arange(S, dtype=F32)
    inv_freq = ROPE_BASE ** (-jnp.arange(0, dk, 2, dtype=F32) / dk)
    ang = pos[:, None] * inv_freq[None, :]
    cos, sin = jnp.cos(ang), jnp.sin(ang)
    cos2 = jnp.concatenate([cos, cos], axis=-1)
    sin2 = jnp.concatenate([-sin, sin], axis=-1)
    log_gamma = jnp.log1p(-jnp.exp2(-5.0 - jnp.arange(RET_HEADS, dtype=F32)))
    idx = jnp.arange(CHUNK, dtype=F32)
    diff = idx[:, None] - idx[None, :]
    decay = jnp.where(diff[None] >= 0, jnp.exp(log_gamma[:, None, None] * jnp.maximum(diff, 0.0)[None]), 0.0)
    xi = jnp.exp(log_gamma[:, None] * (idx[None, :] + 1.0))
    zeta = jnp.exp(log_gamma[:, None] * (CHUNK - 1.0 - idx[None, :]))
    cd = jnp.exp(log_gamma * CHUNK)
    xi_b = jnp.broadcast_to(xi[:, :, None], (RET_HEADS, CHUNK, LANES))
    zeta_b = jnp.broadcast_to(zeta[:, :, None], (RET_HEADS, CHUNK, dk))
    return cos2, sin2, decay, xi_b, zeta_b, cd


TM_INPROJ = 1024
TM_PRENORM = 512
TN_INPROJ = 1024
TM_MIXING = 256
TM_MERGE = 512
MERGE_SUB_ROWS = 512
TM_EXPERT = 512
EXPERT_ROW_STEP = 128
CAST_ROWS = 128
TF_EXPERT = 1024
TM_DISPATCH = 512
TM_COMBINE = 256
COMBINE_PIECES = 8


def kernel(x, norm1_g, w_in, gm_ln_g, gm_ln_b, gm_ws, gm_b, w_proj_a, w_proj_r, w_out, norm2_g, router_w, router_b,
           w_gate_up, b_gate_up, w_down, b_down, final_norm_g):
    B, S, D = x.shape
    assert B == 1 and norm1_g.shape[0] == 1, "single sequence, depth 1"
    gm_width = w_proj_a.shape[1]
    v_width = w_proj_r.shape[1]
    qk_width = (w_in.shape[2] - 2 * gm_width - 2 * v_width - 2 * D) // 2
    dk = qk_width // RET_HEADS
    assert dk == LANES and gm_ws.shape[2] == CHUNK and D == 2 * SLAB_ROWS * LANES
    assert router_w.shape[2] == N_EXPERTS
    G = gm_ws.shape[1]
    E = router_w.shape[2]
    d_ff = w_down.shape[2]
    x2d = x.reshape(S, D)

    cos2, sin2, decay, xi_b, zeta_b, cd = _retention_tables(S, dk)
    tm_p = min(TM_INPROJ, S)
    a_u, h = _segment(x2d, w_in[0], 0, gm_width, "prenorm_gelu", (norm1_g[0][None],),
                      tm=min(TM_PRENORM, S), tn=TN_INPROJ)
    segments = (
        (gm_width, "gelu_layernorm", (gm_ln_g[0][None], gm_ln_b[0][None])),
        (qk_width, "rope", (cos2, sin2)),
        (qk_width, "rope_scaled", (cos2, sin2)),
        (v_width, "identity", ()),
        (v_width, "silu", ()),
        (2 * D, "sigmoid", ()),
    )
    outs, col0 = [], gm_width
    for width, kind, extras in segments:
        outs.append(_segment(h, w_in[0], col0, width, kind, extras, tm=tm_p, tn=TN_INPROJ)[0])
        col0 += width
    a_vn, r_q, r_k, r_v, r_sg, gates = outs

    causal = jnp.tril(jnp.ones((CHUNK, CHUNK), dtype=bool))
    wm = jnp.where(causal[None], gm_ws[0], 0.0).astype(BF16)
    bs_b = jnp.broadcast_to(gm_b[0][:, :, None], (G, CHUNK, GM_GROUP_DIM))
    ga, gr = _mixing(a_u, a_vn, r_q, r_k, r_v, r_sg, wm, bs_b, decay, xi_b, zeta_b, cd, tm=min(TM_MIXING, S))

    rw = jnp.pad(router_w[0], ((0, 0), (0, LANES - E)))
    rw_hi = rw.astype(BF16)
    rw_lo = (rw - rw_hi.astype(F32)).astype(BF16)
    rw_split = jnp.concatenate([rw_hi, rw_lo], axis=1)
    rb = jnp.pad(router_b[0], (0, LANES - E))[None]
    tm_m = min(TM_MERGE, S)
    merged = _branches(ga, gr, gates, w_proj_a[0].astype(BF16), w_proj_r[0].astype(BF16), tm=tm_m)
    x2, h2, route, cnt = _merge(x2d, merged, w_out[0].astype(BF16), norm2_g[0][None], rw_split, rb, tm=tm_m)

    tm_e = TM_EXPERT
    eidx = route[:, 0:TOP_K].astype(jnp.int32)
    rank = route[:, 2 * TOP_K:3 * TOP_K].astype(jnp.int32)
    counts = cnt[:, 0].astype(jnp.int32)
    nblk = (counts + tm_e - 1) // tm_e
    blk_end = jnp.cumsum(nblk)
    blk_start = blk_end - nblk
    dest = (blk_start * tm_e)[eidx] + rank
    nb = (S * TOP_K + tm_e - 1) // tm_e + E
    nvb = blk_end[-1].astype(jnp.int32)
    blk_ids = jnp.arange(nb, dtype=jnp.int32)
    is_last = jnp.any((blk_ids[:, None] == blk_end[None, :] - 1) & (nblk[None, :] > 0), axis=1)
    clear_flag = (is_last | (blk_ids >= nvb)).astype(jnp.int32)

    tm_c = min(TM_COMBINE, S)
    dest_blocks = dest.reshape(S // tm_c, 1, tm_c * TOP_K)
    tm_d = min(TM_DISPATCH, S)
    xs = _dispatch(h2, dest.reshape(S // tm_d, 1, tm_d * TOP_K), clear_flag, tm=tm_d, tm_e=tm_e)
    tf = min(TF_EXPERT, d_ff)
    act = _gateup(xs, _tile_schedule(counts, nblk, blk_start, nvb, d_ff // tf, nb, tm_e), w_gate_up[0],
                  b_gate_up[0][:, None, :],
                  tm=tm_e, tf=tf)
    ys = _down(act, _tile_schedule(counts, nblk, blk_start, nvb, 1, nb, tm_e), w_down[0], b_down[0][:, None, :],
               tm=tm_e)

    out = _combine(dest_blocks, x2, route, final_norm_g[None], ys, tm=tm_c)
    return out.reshape(B, S, D)
```

```python
import functools

import jax
import jax.numpy as jnp
from jax import lax
from jax.experimental import pallas as pl
from jax.experimental.pallas import tpu as pltpu

F32 = jnp.float32
BF16 = jnp.bfloat16
U32 = jnp.uint32

CHUNK = 128
GM_GROUP_DIM = 128
RET_HEADS = 8
N_EXPERTS = 32
ROUTE_FIELDS = 16
TOP_K = 4
SWIGLU_LIMIT = 7.0
SWIGLU_ALPHA = 1.702
ROPE_BASE = 10000.0
EPS = 1e-6
LANES = 128

VMEM_LIMIT_BYTES = 56 * 1024 * 1024


def _cparams(sem):
    return pltpu.CompilerParams(dimension_semantics=sem, vmem_limit_bytes=VMEM_LIMIT_BYTES)


SLAB_ROWS = 8
SLAB_PITCH = 9
DMA_QUEUES = 2


def _slab_store(ref, value, row0=0):
    rows, width = value.shape
    half = width // 2
    assert half == SLAB_ROWS * LANES
    lo = lax.bitcast_convert_type(value[:, :half].astype(BF16).astype(F32), U32)
    hi = lax.bitcast_convert_type(value[:, half:].astype(BF16).astype(F32), U32)
    words = (lo >> 16) | hi
    base = row0 * SLAB_PITCH
    for c in range(SLAB_ROWS):
        ref[pl.ds(base + c, rows, stride=SLAB_PITCH), :] = words[:, c * LANES:(c + 1) * LANES]
    for c in range(SLAB_ROWS, SLAB_PITCH):
        ref[pl.ds(base + c, rows, stride=SLAB_PITCH), :] = jnp.zeros((rows, LANES), U32)


def _slab_load(ref, rows, row0=0):
    base = row0 * SLAB_PITCH
    words = jnp.concatenate([ref[pl.ds(base + c, rows, stride=SLAB_PITCH), :] for c in range(SLAB_ROWS)], axis=1)
    lo = lax.bitcast_convert_type(words << 16, F32)
    hi = lax.bitcast_convert_type(words & jnp.uint32(0xFFFF0000), F32)
    return jnp.concatenate([lo, hi], axis=1)


def _gelu_exact(a):
    return 0.5 * a * (1.0 + lax.erf(a * (2.0 ** -0.5)))


def _rope_heads(a, cos2, sin2):
    outs = []
    for hd in range(a.shape[1] // LANES):
        ah = a[:, hd * LANES:(hd + 1) * LANES]
        outs.append(ah * cos2 + pltpu.roll(ah, LANES // 2, axis=1) * sin2)
    return jnp.concatenate(outs, axis=1)


def _segment_body(h_ref, w_ref, *refs, kind, k_scale):
    wb_ref = refs[-1]
    o_ref = refs[-3] if kind == "prenorm_gelu" else refs[-2]

    @pl.when(pl.program_id(1) == 0)
    def _():
        def cast_rows(r, carry):
            rows = pl.ds(pl.multiple_of(r * CAST_ROWS, CAST_ROWS), CAST_ROWS)
            wb_ref[rows, :] = w_ref[rows, :].astype(BF16)
            return carry

        lax.fori_loop(0, w_ref.shape[0] // CAST_ROWS, cast_rows, 0)

    if kind == "prenorm_gelu":
        g_ref, hn_ref = refs[0], refs[-2]
        xf = h_ref[...]
        ms = jnp.mean(xf * xf, axis=-1, keepdims=True)
        lhs = (xf * lax.rsqrt(ms + EPS) * g_ref[...]).astype(BF16)
        hn_ref[...] = lhs
    else:
        lhs = h_ref[...]
    acc = jnp.dot(lhs, wb_ref[...], preferred_element_type=F32)
    if kind in ("gelu", "prenorm_gelu"):
        out = _gelu_exact(acc)
    elif kind == "gelu_layernorm":
        lng_ref, lnb_ref = refs[0], refs[1]
        vf = _gelu_exact(acc)
        mu = jnp.mean(vf, axis=-1, keepdims=True)
        vc = vf - mu
        var = jnp.mean(vc * vc, axis=-1, keepdims=True)
        out = vc * lax.rsqrt(var + EPS) * lng_ref[...] + lnb_ref[...]
    elif kind == "rope":
        out = _rope_heads(acc, refs[0][...], refs[1][...])
    elif kind == "rope_scaled":
        out = _rope_heads(acc, refs[0][...], refs[1][...]) * k_scale
    elif kind == "identity":
        out = acc
    elif kind == "silu":
        out = acc * jax.nn.sigmoid(acc)
    elif kind == "sigmoid":
        out = jax.nn.sigmoid(acc)
    else:
        raise ValueError(kind)
    o_ref[...] = out.astype(o_ref.dtype)


def _segment(h, w_in, col0, width, kind, extras=(), *, tm, tn):
    S, D = h.shape
    assert width % tn == 0 and col0 % tn == 0 and S % tm == 0
    blk0 = col0 // tn
    out_specs = [pl.BlockSpec((tm, tn), lambda j, i: (i, j))]
    out_shape = [jax.ShapeDtypeStruct((S, width), BF16)]
    if kind == "gelu_layernorm":
        assert width == tn
        extra_specs = [pl.BlockSpec((1, tn), lambda j, i: (0, 0))] * 2
    elif kind in ("rope", "rope_scaled"):
        extra_specs = [pl.BlockSpec((tm, LANES), lambda j, i: (i, 0))] * 2
    elif kind == "prenorm_gelu":
        assert width == tn
        extra_specs = [pl.BlockSpec((1, D), lambda j, i: (0, 0))]
        out_specs.append(pl.BlockSpec((tm, D), lambda j, i: (i, 0)))
        out_shape.append(jax.ShapeDtypeStruct((S, D), BF16))
    else:
        extra_specs = []
    return pl.pallas_call(
        functools.partial(_segment_body, kind=kind, k_scale=float(LANES) ** -0.5),
        grid=(width // tn, S // tm),
        in_specs=[
            pl.BlockSpec((tm, D), lambda j, i: (i, 0)),
            pl.BlockSpec((D, tn), lambda j, i: (0, blk0 + j)),
        ] + extra_specs,
        out_specs=out_specs,
        out_shape=out_shape,
        scratch_shapes=[pltpu.VMEM((D, tn), BF16)],
        compiler_params=_cparams(("arbitrary", "arbitrary")),
        name="inproj_" + kind,
    )(h, w_in, *extras)


def _mixing_body(cd_ref, u_ref, vn_ref, q_ref, k_ref, v_ref, sg_ref, wm_ref, bs_ref, decay_ref, xi_ref, zeta_ref,
                 ga_ref, gr_ref, state_ref, *, n_chunks, dk, dv):
    @pl.when(pl.program_id(0) == 0)
    def _():
        state_ref[...] = jnp.zeros_like(state_ref)

    def chunk(c, carry):
        rows = pl.ds(pl.multiple_of(c * CHUNK, CHUNK), CHUNK)
        for g in range(wm_ref.shape[0]):
            cols = slice(g * GM_GROUP_DIM, (g + 1) * GM_GROUP_DIM)
            mixed = jnp.dot(wm_ref[g], vn_ref[rows, cols], preferred_element_type=F32) + bs_ref[g]
            ga_ref[rows, cols] = (u_ref[rows, cols].astype(F32) * mixed).astype(ga_ref.dtype)
        for hd in range(RET_HEADS):
            qc = q_ref[rows, hd * dk:(hd + 1) * dk]
            kc = k_ref[rows, hd * dk:(hd + 1) * dk]
            vc = v_ref[rows, hd * dv:(hd + 1) * dv]
            st = state_ref[hd]
            scores = lax.dot_general(qc, kc, (((1,), (1,)), ((), ())), preferred_element_type=F32) * decay_ref[hd]
            inner = jnp.dot(scores.astype(BF16), vc, preferred_element_type=F32)
            cross = jnp.dot(qc, st.astype(BF16), preferred_element_type=F32)
            xi = xi_ref[hd]
            o = inner + cross * jnp.concatenate([xi] * (dv // LANES), axis=1)
            kz = (kc.astype(F32) * zeta_ref[hd]).astype(BF16)
            kv = lax.dot_general(kz, vc, (((0,), (0,)), ((), ())), preferred_element_type=F32)
            state_ref[hd] = st * cd_ref[hd] + kv
            mu = jnp.mean(o, axis=-1, keepdims=True)
            oc = o - mu
            var = jnp.mean(oc * oc, axis=-1, keepdims=True)
            on = oc * lax.rsqrt(var + EPS)
            gr_ref[rows, hd * dv:(hd + 1) * dv] = (on * sg_ref[rows, hd * dv:(hd + 1) * dv].astype(F32)).astype(
                gr_ref.dtype)
        return carry

    lax.fori_loop(0, n_chunks, chunk, 0, unroll=True)


def _mixing(u, vn, q, k, v, sg, wm, bs_b, decay, xi_b, zeta_b, cd, *, tm):
    S, gm_width = u.shape
    qk_width, v_width = q.shape[1], v.shape[1]
    assert S % tm == 0 and tm % CHUNK == 0
    dk, dv = qk_width // RET_HEADS, v_width // RET_HEADS
    G = wm.shape[0]
    body = functools.partial(_mixing_body, n_chunks=tm // CHUNK, dk=dk, dv=dv)
    const3 = lambda i: (0, 0, 0)
    rows = lambda i: (i, 0)
    return pl.pallas_call(
        body,
        grid=(S // tm,),
        in_specs=[
            pl.BlockSpec(memory_space=pltpu.SMEM),
            pl.BlockSpec((tm, gm_width), rows),
            pl.BlockSpec((tm, gm_width), rows),
            pl.BlockSpec((tm, qk_width), rows),
            pl.BlockSpec((tm, qk_width), rows),
            pl.BlockSpec((tm, v_width), rows),
            pl.BlockSpec((tm, v_width), rows),
            pl.BlockSpec((G, CHUNK, CHUNK), const3),
            pl.BlockSpec((G, CHUNK, GM_GROUP_DIM), const3),
            pl.BlockSpec((RET_HEADS, CHUNK, CHUNK), const3),
            pl.BlockSpec((RET_HEADS, CHUNK, LANES), const3),
            pl.BlockSpec((RET_HEADS, CHUNK, dk), const3),
        ],
        out_specs=[
            pl.BlockSpec((tm, gm_width), lambda i: (i, 0)),
            pl.BlockSpec((tm, v_width), lambda i: (i, 0)),
        ],
        out_shape=[
            jax.ShapeDtypeStruct((S, gm_width), BF16),
            jax.ShapeDtypeStruct((S, v_width), BF16),
        ],
        scratch_shapes=[pltpu.VMEM((RET_HEADS, dk, dv), F32)],
        compiler_params=_cparams(("arbitrary",)),
        name="mixing",
    )(cd, u, vn, q, k, v, sg, wm, bs_b, decay, xi_b, zeta_b)


def _branches_body(ga_ref, gr_ref, sa_ref, sr_ref, wa_ref, wr_ref, o_ref):
    y_a = jnp.dot(ga_ref[...], wa_ref[...], preferred_element_type=F32)
    y_r = jnp.dot(gr_ref[...], wr_ref[...], preferred_element_type=F32)
    o_ref[...] = (sa_ref[...].astype(F32) * y_a + sr_ref[...].astype(F32) * y_r).astype(o_ref.dtype)


def _branches(ga, gr, gates, wa, wr, *, tm):
    S = ga.shape[0]
    D = wa.shape[1]
    const = lambda i: (0, 0)
    resident = functools.partial(pl.BlockSpec, index_map=const, pipeline_mode=pl.Buffered(1))
    return pl.pallas_call(
        _branches_body,
        grid=(S // tm,),
        in_specs=[
            pl.BlockSpec((tm, ga.shape[1]), lambda i: (i, 0)),
            pl.BlockSpec((tm, gr.shape[1]), lambda i: (i, 0)),
            pl.BlockSpec((tm, D), lambda i: (i, 0)),
            pl.BlockSpec((tm, D), lambda i: (i, 1)),
            resident(wa.shape),
            resident(wr.shape),
        ],
        out_specs=pl.BlockSpec((tm, D), lambda i: (i, 0)),
        out_shape=jax.ShapeDtypeStruct((S, D), BF16),
        compiler_params=_cparams(("arbitrary",)),
        name="branches",
    )(ga, gr, gates, gates, wa, wr)


def _merge_body(x_ref, m_ref, wo_ref, g2_ref, rw_ref, rb_ref, x2_ref, h2_ref, route_ref, route_t_ref, cnt_ref,
                run_ref):
    i = pl.program_id(0)
    tm = x_ref.shape[0]

    @pl.when(i == 0)
    def _():
        run_ref[...] = jnp.zeros_like(run_ref)

    sub = min(tm, MERGE_SUB_ROWS)
    neg_inf = jnp.float32(-jnp.inf)
    r_iota = lax.broadcasted_iota(jnp.int32, (sub, sub), 0)
    c_iota = lax.broadcasted_iota(jnp.int32, (sub, sub), 1)
    earlier = (r_iota < c_iota).astype(BF16)
    expert = lax.broadcasted_iota(jnp.int32, (N_EXPERTS, sub), 0)
    field = lax.broadcasted_iota(jnp.int32, (ROUTE_FIELDS, sub), 0)
    run = run_ref[:, 0:1]
    for j in range(tm // sub):
        rows = slice(j * sub, (j + 1) * sub)
        x2 = x_ref[rows, :] + jnp.dot(m_ref[rows, :], wo_ref[...], preferred_element_type=F32)
        x2_ref[rows, :] = x2
        ms = jnp.mean(x2 * x2, axis=-1, keepdims=True)
        h2 = x2 * lax.rsqrt(ms + EPS) * g2_ref[...]
        _slab_store(h2_ref, h2, row0=j * sub)

        h_hi = h2.astype(BF16)
        h_lo = (h2 - h_hi.astype(F32)).astype(BF16)
        hi_terms = jnp.dot(h_hi, rw_ref[...], preferred_element_type=F32)
        logits = (hi_terms[:, :LANES] + hi_terms[:, LANES:]
                  + jnp.dot(h_lo, rw_ref[:, :LANES], preferred_element_type=F32)) + rb_ref[...]
        work = logits.T[0:N_EXPERTS, :]

        vals, idxs, sels = [], [], []
        for _ in range(TOP_K):
            m = jnp.max(work, axis=0, keepdims=True)
            idx = jnp.min(jnp.where(work == m, expert, N_EXPERTS), axis=0, keepdims=True)
            sel = expert == idx
            vals.append(m)
            idxs.append(idx)
            sels.append(sel)
            work = jnp.where(sel, neg_inf, work)
        exps = [jnp.exp(v - vals[0]) for v in vals]
        denom = exps[0] + exps[1] + exps[2] + exps[3]
        gates = [e / denom for e in exps]

        onehot = jnp.zeros((N_EXPERTS, sub), F32)
        for sel in sels:
            onehot = onehot + sel.astype(F32)
        before = jnp.dot(onehot.astype(BF16), earlier, preferred_element_type=F32) + run
        ranks = [jnp.sum(jnp.where(sel, before, 0.0), axis=0, keepdims=True) for sel in sels]
        run = run + jnp.sum(onehot, axis=1, keepdims=True)

        fields = jnp.zeros((ROUTE_FIELDS, sub), F32)
        for k in range(TOP_K):
            fields = jnp.where(field == k, idxs[k].astype(F32), fields)
            fields = jnp.where(field == TOP_K + k, gates[k], fields)
            fields = jnp.where(field == 2 * TOP_K + k, ranks[k], fields)
        slab_t = jnp.concatenate([fields, jnp.zeros((LANES - ROUTE_FIELDS, sub), F32)], axis=0)
        route_ref[rows, :] = slab_t.T
        route_t_ref[:, rows] = fields

    run_ref[...] = jnp.broadcast_to(run, run_ref.shape)
    cnt_ref[...] = jnp.broadcast_to(run, cnt_ref.shape)


def _merge(x2d, merged, wo, g2, rw_split, rb, *, tm):
    S, D = x2d.shape
    assert S % tm == 0
    const = lambda i: (0, 0)
    resident = functools.partial(pl.BlockSpec, index_map=const, pipeline_mode=pl.Buffered(1))
    return pl.pallas_call(
        _merge_body,
        grid=(S // tm,),
        in_specs=[
            pl.BlockSpec((tm, D), lambda i: (i, 0)),
            pl.BlockSpec((tm, D), lambda i: (i, 0)),
            resident(wo.shape),
            pl.BlockSpec((1, D), const),
            resident(rw_split.shape),
            pl.BlockSpec((1, LANES), const),
        ],
        out_specs=[
            pl.BlockSpec((tm, D), lambda i: (i, 0)),
            pl.BlockSpec((tm * SLAB_PITCH, LANES), lambda i: (i, 0)),
            pl.BlockSpec((tm, LANES), lambda i: (i, 0)),
            pl.BlockSpec((ROUTE_FIELDS, tm), lambda i: (0, i)),
            pl.BlockSpec((N_EXPERTS, LANES), const),
        ],
        out_shape=[
            jax.ShapeDtypeStruct((S, D), F32),
            jax.ShapeDtypeStruct((S * SLAB_PITCH, LANES), U32),
            jax.ShapeDtypeStruct((S, LANES), F32),
            jax.ShapeDtypeStruct((ROUTE_FIELDS, S), F32),
            jax.ShapeDtypeStruct((N_EXPERTS, LANES), F32),
        ],
        scratch_shapes=[pltpu.VMEM((N_EXPERTS, LANES), F32)],
        compiler_params=_cparams(("arbitrary",)),
        name="merge",
    )(x2d, merged, wo, g2, rw_split, rb)


def _dispatch_body(clear_ref, dest_ref, h2_ref, xs_hbm, zeros_ref, sem, clear_sem, *, tm, blk_rows, nb):
    i = pl.program_id(0)

    def clear_copy(b):
        return pltpu.make_async_copy(zeros_ref, xs_hbm.at[pl.ds(pl.multiple_of(b * blk_rows, blk_rows), blk_rows)],
                                     clear_sem.at[0])

    @pl.when(i == 0)
    def _():
        zeros_ref[...] = jnp.zeros_like(zeros_ref)

        def start(b, carry):
            @pl.when(clear_ref[b] == 1)
            def _():
                clear_copy(b).start()
            return carry

        def drain(b, carry):
            @pl.when(clear_ref[b] == 1)
            def _():
                clear_copy(b).wait()
            return carry

        lax.fori_loop(0, nb, start, 0)
        lax.fori_loop(0, nb, drain, 0)

    def issue(t, carry):
        for k in range(TOP_K):
            d = dest_ref[0, 0, k * tm + t]
            pltpu.make_async_copy(h2_ref.at[pl.ds(t * SLAB_PITCH, SLAB_PITCH)],
                                  xs_hbm.at[pl.ds(d * SLAB_PITCH, SLAB_PITCH)],
                                  sem.at[0]).start(priority=k % DMA_QUEUES)
        return carry

    lax.fori_loop(0, tm, issue, 0, unroll=4)
    for k in range(TOP_K):
        pltpu.make_async_copy(h2_ref, xs_hbm.at[pl.ds(0, tm * SLAB_PITCH)], sem.at[0]).wait()


def _dispatch(h2_slab, dest, clear_flag, *, tm, tm_e):
    nb = clear_flag.shape[0]
    blk_rows = tm_e * SLAB_PITCH
    n = h2_slab.shape[0] // (tm * SLAB_PITCH)
    grid_spec = pltpu.PrefetchScalarGridSpec(
        num_scalar_prefetch=1,
        grid=(n,),
        in_specs=[
            pl.BlockSpec((1, 1, tm * TOP_K), lambda i, cf: (i, 0, 0), memory_space=pltpu.SMEM),
            pl.BlockSpec((tm * SLAB_PITCH, LANES), lambda i, cf: (i, 0)),
        ],
        out_specs=pl.BlockSpec(memory_space=pl.ANY),
        scratch_shapes=[pltpu.VMEM((blk_rows, LANES), U32), pltpu.SemaphoreType.DMA((1,)),
                        pltpu.SemaphoreType.DMA((1,))],
    )
    return pl.pallas_call(
        functools.partial(_dispatch_body, tm=tm, blk_rows=blk_rows, nb=nb),
        grid_spec=grid_spec,
        out_shape=jax.ShapeDtypeStruct((nb * blk_rows, LANES), U32),
        compiler_params=_cparams(("arbitrary",)),
        name="dispatch",
    )(clear_flag, dest, h2_slab)


def _tile_schedule(counts, nblk, blk_start, nvb, n_tiles, nb, tm):
    E = nblk.shape[0]
    n_steps = nb * n_tiles
    steps_per_e = n_tiles * nblk
    e_end = jnp.cumsum(steps_per_e)
    e_start = e_end - steps_per_e
    total = e_end[-1]
    e_ids = jnp.arange(E, dtype=jnp.int32)

    def locate(idx):
        inside = (e_start[None, :] <= idx[:, None]) & (idx[:, None] < e_end[None, :])
        pick = lambda v: jnp.sum(jnp.where(inside, v[None, :], 0), axis=1)
        r = idx - pick(e_start)
        nb_e = jnp.maximum(pick(nblk), 1)
        tile = sum((r >= t * nb_e).astype(jnp.int32) for t in range(1, n_tiles)) if n_tiles > 1 else 0 * r
        return pick(e_ids), tile, r - tile * nb_e, nb_e, pick(blk_start), pick(counts)

    s = jnp.arange(n_steps, dtype=jnp.int32)
    live = s < total
    sl = jnp.minimum(s, total - 1)
    e, tile, b, nb_e, bs_e, cnt_e = locate(sl)
    blk = bs_e + b
    first = (live & (b == 0)).astype(jnp.int32)
    dead = jnp.maximum(s - total, 0)
    out_blk = jnp.where(live, blk, nvb + dead // n_tiles)
    out_tile = jnp.where(live, tile, dead % n_tiles)
    pieces = (jnp.clip(cnt_e - b * tm, 0, tm) + EXPERT_ROW_STEP - 1) // EXPERT_ROW_STEP
    pieces = jnp.where(live, pieces, 0)
    nxt = sl - b + nb_e
    has_next = (live & (nxt < total)).astype(jnp.int32)
    ne, nt, _, _, _, _ = locate(jnp.minimum(nxt, total - 1))
    as_i32 = lambda a: a.astype(jnp.int32)
    return (as_i32(e), as_i32(tile), as_i32(blk), first, as_i32(out_blk), as_i32(out_tile), as_i32(total)[None],
            as_i32(ne), as_i32(nt), has_next, as_i32(pieces))


def _gateup_body(se_ref, st_ref, sb_ref, sf_ref, ob_ref, ot_ref, nl_ref, ne_ref, nt_ref, hn_ref, np_ref,
                 xs_ref, wgu_hbm, bg_ref, bu_ref, act_ref, stage_ref, wgb_ref, wub_ref, sem, *, tf, d_ff):
    s = pl.program_id(0)
    D = stage_ref.shape[1]
    tm = act_ref.shape[0]

    def tile_copies(e, t):
        return [pltpu.make_async_copy(
            wgu_hbm.at[e, pl.ds(0, D), pl.ds(pl.multiple_of(half * d_ff + t * tf, tf), tf)],
            stage_ref.at[half], sem.at[half]) for half in range(2)]

    @pl.when(sf_ref[s] == 1)
    def _():
        @pl.when(s == 0)
        def _():
            for c in tile_copies(se_ref[0], st_ref[0]):
                c.start()

        for c in tile_copies(se_ref[s], st_ref[s]):
            c.wait()

        def cast_rows(r, carry):
            rows = pl.ds(pl.multiple_of(r * CAST_ROWS, CAST_ROWS), CAST_ROWS)
            wgb_ref[rows, :] = stage_ref[0, rows, :].astype(BF16)
            wub_ref[rows, :] = stage_ref[1, rows, :].astype(BF16)
            return carry

        lax.fori_loop(0, D // CAST_ROWS, cast_rows, 0)

        @pl.when(hn_ref[s] == 1)
        def _():
            for c in tile_copies(ne_ref[s], nt_ref[s]):
                c.start()

    for pieces in range(1, tm // EXPERT_ROW_STEP + 1):
        @pl.when(np_ref[s] == pieces)
        def _(m=pieces * EXPERT_ROW_STEP):
            xb = _slab_load(xs_ref, m).astype(BF16)
            gate = jnp.dot(xb, wgb_ref[...], preferred_element_type=F32) + bg_ref[0]
            up = jnp.dot(xb, wub_ref[...], preferred_element_type=F32) + bu_ref[0]
            gate = jnp.minimum(gate, SWIGLU_LIMIT)
            up = jnp.clip(up, -SWIGLU_LIMIT, SWIGLU_LIMIT)
            glu = gate * jax.nn.sigmoid(gate * SWIGLU_ALPHA)
            act_ref[0:m, :] = ((up + 1.0) * glu).astype(act_ref.dtype)
            if m < tm:
                act_ref[m:tm, :] = jnp.zeros((tm - m, act_ref.shape[1]), act_ref.dtype)

    @pl.when(np_ref[s] == 0)
    def _():
        act_ref[...] = jnp.zeros_like(act_ref)


def _gateup(xs, sched, wgu, bgu, *, tm, tf):
    D = wgu.shape[1]
    R = xs.shape[0] // SLAB_PITCH
    d_ff = wgu.shape[2] // 2
    nj = d_ff // tf
    nb = R // tm
    assert d_ff % tf == 0 and R % tm == 0
    grid_spec = pltpu.PrefetchScalarGridSpec(
        num_scalar_prefetch=11,
        grid=(nb * nj,),
        in_specs=[
            pl.BlockSpec((tm * SLAB_PITCH, LANES), lambda s, se, st, sb, *_: (sb[s], 0)),
            pl.BlockSpec(memory_space=pl.ANY),
            pl.BlockSpec((1, 1, tf), lambda s, se, st, *_: (se[s], 0, st[s])),
            pl.BlockSpec((1, 1, tf), lambda s, se, st, *_: (se[s], 0, nj + st[s])),
        ],
        out_specs=pl.BlockSpec((tm, tf), lambda s, se, st, sb, sf, ob, ot, *_: (ob[s], ot[s])),
        scratch_shapes=[pltpu.VMEM((2, D, tf), F32), pltpu.VMEM((D, tf), BF16), pltpu.VMEM((D, tf), BF16),
                        pltpu.SemaphoreType.DMA((2,))],
    )
    return pl.pallas_call(
        functools.partial(_gateup_body, tf=tf, d_ff=d_ff),
        grid_spec=grid_spec,
        out_shape=jax.ShapeDtypeStruct((R, d_ff), BF16),
        compiler_params=_cparams(("arbitrary",)),
        name="gateup",
    )(*sched, xs, wgu, bgu, bgu)


def _down_body(se_ref, st_ref, sb_ref, sf_ref, ob_ref, ot_ref, nl_ref, ne_ref, nt_ref, hn_ref, np_ref,
               act_ref, wd_hbm, bd_ref, y_ref, stage_ref, wdb_ref, sem):
    s = pl.program_id(0)
    tm = act_ref.shape[0]

    def weight_copy(e):
        return pltpu.make_async_copy(wd_hbm.at[e], stage_ref, sem.at[0])

    @pl.when(sf_ref[s] == 1)
    def _():
        @pl.when(s == 0)
        def _():
            weight_copy(se_ref[0]).start()

        weight_copy(se_ref[s]).wait()
        wdb_ref[...] = stage_ref[...].astype(BF16)

        @pl.when(hn_ref[s] == 1)
        def _():
            weight_copy(ne_ref[s]).start()

    for pieces in range(1, tm // EXPERT_ROW_STEP + 1):
        @pl.when(np_ref[s] == pieces)
        def _(m=pieces * EXPERT_ROW_STEP):
            y = jnp.dot(act_ref[0:m, :], wdb_ref[...], preferred_element_type=F32) + bd_ref[0]
            _slab_store(y_ref, y)
            if m < tm:
                y_ref[m * SLAB_PITCH:tm * SLAB_PITCH, :] = jnp.zeros(((tm - m) * SLAB_PITCH, LANES), U32)

    @pl.when(np_ref[s] == 0)
    def _():
        y_ref[...] = jnp.zeros_like(y_ref)


def _down(act, sched, wd, bd, *, tm):
    R, d_ff = act.shape
    D = wd.shape[2]
    nb = R // tm
    grid_spec = pltpu.PrefetchScalarGridSpec(
        num_scalar_prefetch=11,
        grid=(nb,),
        in_specs=[
            pl.BlockSpec((tm, d_ff), lambda s, se, st, sb, *_: (sb[s], 0)),
            pl.BlockSpec(memory_space=pl.ANY),
            pl.BlockSpec((1, 1, D), lambda s, se, *_: (se[s], 0, 0)),
        ],
        out_specs=pl.BlockSpec((tm * SLAB_PITCH, LANES), lambda s, se, st, sb, sf, ob, *_: (ob[s], 0)),
        scratch_shapes=[pltpu.VMEM((d_ff, D), F32), pltpu.VMEM((d_ff, D), BF16), pltpu.SemaphoreType.DMA((1,))],
    )
    return pl.pallas_call(
        _down_body,
        grid_spec=grid_spec,
        out_shape=jax.ShapeDtypeStruct((R * SLAB_PITCH, LANES), U32),
        compiler_params=_cparams(("arbitrary",)),
        name="down",
    )(*sched, act, wd, bd)


def _combine_body(dest_cur_ref, dest_nxt_ref, x2_ref, route_ref, gf_ref, ys_hbm, o_ref, buf0_ref, buf1_ref, sem, *,
                  tm):
    i = pl.program_id(0)
    last = pl.num_programs(0) - 1
    bufs = (buf0_ref, buf1_ref)

    def slab_copy(dest_ref, t, k, slot):
        d = dest_ref[0, 0, k * tm + t]
        return pltpu.make_async_copy(ys_hbm.at[pl.ds(d * SLAB_PITCH, SLAB_ROWS)],
                                     bufs[slot].at[k, pl.ds(t * SLAB_PITCH, SLAB_ROWS)], sem.at[slot])

    def wait_rows(slot):
        for k in range(TOP_K):
            pltpu.make_async_copy(ys_hbm.at[pl.ds(0, tm * SLAB_ROWS)], bufs[slot].at[k, pl.ds(0, tm * SLAB_ROWS)],
                                  sem.at[slot]).wait()

    @pl.when(i == 0)
    def _():
        def issue(t, carry):
            for k in range(TOP_K):
                slab_copy(dest_cur_ref, t, k, 0).start(priority=k % DMA_QUEUES)
            return carry
        lax.fori_loop(0, tm, issue, 0, unroll=4)

    piece = tm // COMBINE_PIECES
    for slot in range(2):
        @pl.when(i % 2 == slot)
        def _(slot=slot):
            wait_rows(slot)
            for p in range(COMBINE_PIECES):
                r0 = p * piece
                for t in range(r0, r0 + piece):
                    for k in range(TOP_K):
                        slab_copy(dest_nxt_ref, t, k, 1 - slot).start(priority=k % DMA_QUEUES)
                x3 = x2_ref[r0:r0 + piece, :]
                route = route_ref[r0:r0 + piece, :]
                for k in range(TOP_K):
                    x3 = x3 + route[:, TOP_K + k:TOP_K + k + 1] * _slab_load(bufs[slot].at[k], piece, row0=r0)
                ms = jnp.mean(x3 * x3, axis=-1, keepdims=True)
                o_ref[r0:r0 + piece, :] = x3 * lax.rsqrt(ms + EPS) * gf_ref[...]

            @pl.when(i == last)
            def _():
                wait_rows(1 - slot)


def _combine(dest, x2, route, gf, ys_slab, *, tm):
    T, D = x2.shape
    n = T // tm
    assert T % tm == 0
    return pl.pallas_call(
        functools.partial(_combine_body, tm=tm),
        grid=(n,),
        in_specs=[
            pl.BlockSpec((1, 1, tm * TOP_K), lambda i: (i, 0, 0), memory_space=pltpu.SMEM),
            pl.BlockSpec((1, 1, tm * TOP_K), lambda i: (jnp.minimum(i + 1, n - 1), 0, 0), memory_space=pltpu.SMEM),
            pl.BlockSpec((tm, D), lambda i: (i, 0)),
            pl.BlockSpec((tm, LANES), lambda i: (i, 0)),
            pl.BlockSpec((1, D), lambda i: (0, 0)),
            pl.BlockSpec(memory_space=pl.ANY),
        ],
        out_specs=pl.BlockSpec((tm, D), lambda i: (i, 0)),
        out_shape=jax.ShapeDtypeStruct((T, D), F32),
        scratch_shapes=[pltpu.VMEM((TOP_K, tm * SLAB_PITCH, LANES), U32),
                        pltpu.VMEM((TOP_K, tm * SLAB_PITCH, LANES), U32), pltpu.SemaphoreType.DMA((2,))],
        compiler_params=_cparams(("arbitrary",)),
        name="combine",
    )(dest, dest, x2, route, gf, ys_slab)


def _retention_tables(S, dk):
    pos = jnp.arange(S, dtype=F32)
    inv_freq = ROPE_BASE ** (-jnp.arange(0, dk, 2, dtype=F32) / dk)
    ang = pos[:, None] * inv_freq[None, :]
    cos, sin = jnp.cos(ang), jnp.sin(ang)
    cos2 = jnp.concatenate([cos, cos], axis=-1)
    sin2 = jnp.concatenate([-sin, sin], axis=-1)
    log_gamma = jnp.log1p(-jnp.exp2(-5.0 - jnp.arange(RET_HEADS, dtype=F32)))
    idx = jnp.arange(CHUNK, dtype=F32)
    diff = idx[:, None] - idx[None, :]
    decay = jnp.where(diff[None] >= 0, jnp.exp(log_gamma[:, None, None] * jnp.maximum(diff, 0.0)[None]), 0.0)
    xi = jnp.exp(log_gamma[:, None] * (idx[None, :] + 1.0))
    zeta = jnp.exp(log_gamma[:, None] * (CHUNK - 1.0 - idx[None, :]))
    cd = jnp.exp(log_gamma * CHUNK)
    xi_b = jnp.broadcast_to(xi[:, :, None], (RET_HEADS, CHUNK, LANES))
    zeta_b = jnp.broadcast_to(zeta[:, :, None], (RET_HEADS, CHUNK, dk))
    return cos2, sin2, decay, xi_b, zeta_b, cd


TM_INPROJ = 1024
TM_PRENORM = 512
TN_INPROJ = 1024
TM_MIXING = 256
TM_MERGE = 512
MERGE_SUB_ROWS = 512
TM_EXPERT = 512
EXPERT_ROW_STEP = 128
CAST_ROWS = 128
TF_EXPERT = 1024
TM_DISPATCH = 512
TM_COMBINE = 256
COMBINE_PIECES = 8


def kernel(x, norm1_g, w_in, gm_ln_g, gm_ln_b, gm_ws, gm_b, w_proj_a, w_proj_r, w_out, norm2_g, router_w, router_b,
           w_gate_up, b_gate_up, w_down, b_down, final_norm_g):
    B, S, D = x.shape
    assert B == 1 and norm1_g.shape[0] == 1, "single sequence, depth 1"
    gm_width = w_proj_a.shape[1]
    v_width = w_proj_r.shape[1]
    qk_width = (w_in.shape[2] - 2 * gm_width - 2 * v_width - 2 * D) // 2
    dk = qk_width // RET_HEADS
    assert dk == LANES and gm_ws.shape[2] == CHUNK and D == 2 * SLAB_ROWS * LANES
    assert router_w.shape[2] == N_EXPERTS
    G = gm_ws.shape[1]
    E = router_w.shape[2]
    d_ff = w_down.shape[2]
    x2d = x.reshape(S, D)

    cos2, sin2, decay, xi_b, zeta_b, cd = _retention_tables(S, dk)
    tm_p = min(TM_INPROJ, S)
    a_u, h = _segment(x2d, w_in[0], 0, gm_width, "prenorm_gelu", (norm1_g[0][None],),
                      tm=min(TM_PRENORM, S), tn=TN_INPROJ)
    segments = (
        (gm_width, "gelu_layernorm", (gm_ln_g[0][None], gm_ln_b[0][None])),
        (qk_width, "rope", (cos2, sin2)),
        (qk_width, "rope_scaled", (cos2, sin2)),
        (v_width, "identity", ()),
        (v_width, "silu", ()),
        (2 * D, "sigmoid", ()),
    )
    outs, col0 = [], gm_width
    for width, kind, extras in segments:
        outs.append(_segment(h, w_in[0], col0, width, kind, extras, tm=tm_p, tn=TN_INPROJ)[0])
        col0 += width
    a_vn, r_q, r_k, r_v, r_sg, gates = outs

    causal = jnp.tril(jnp.ones((CHUNK, CHUNK), dtype=bool))
    wm = jnp.where(causal[None], gm_ws[0], 0.0).astype(BF16)
    bs_b = jnp.broadcast_to(gm_b[0][:, :, None], (G, CHUNK, GM_GROUP_DIM))
    ga, gr = _mixing(a_u, a_vn, r_q, r_k, r_v, r_sg, wm, bs_b, decay, xi_b, zeta_b, cd, tm=min(TM_MIXING, S))

    rw = jnp.pad(router_w[0], ((0, 0), (0, LANES - E)))
    rw_hi = rw.astype(BF16)
    rw_lo = (rw - rw_hi.astype(F32)).astype(BF16)
    rw_split = jnp.concatenate([rw_hi, rw_lo], axis=1)
    rb = jnp.pad(router_b[0], (0, LANES - E))[None]
    tm_m = min(TM_MERGE, S)
    merged = _branches(ga, gr, gates, w_proj_a[0].astype(BF16), w_proj_r[0].astype(BF16), tm=tm_m)
    x2, h2, route, route_t, cnt = _merge(x2d, merged, w_out[0].astype(BF16), norm2_g[0][None], rw_split, rb,
                                         tm=tm_m)

    tm_e = TM_EXPERT
    eidx = route_t[0:TOP_K].astype(jnp.int32)
    rank = route_t[2 * TOP_K:3 * TOP_K].astype(jnp.int32)
    counts = cnt[:, 0].astype(jnp.int32)
    nblk = (counts + tm_e - 1) // tm_e
    blk_end = jnp.cumsum(nblk)
    blk_start = blk_end - nblk
    dest = (blk_start * tm_e)[eidx] + rank

    def dest_blocks_of(tm):
        return dest.reshape(TOP_K, S // tm, tm).transpose(1, 0, 2).reshape(S // tm, 1, TOP_K * tm)

    nb = (S * TOP_K + tm_e - 1) // tm_e + E
    nvb = blk_end[-1].astype(jnp.int32)
    blk_ids = jnp.arange(nb, dtype=jnp.int32)
    is_last = jnp.any((blk_ids[:, None] == blk_end[None, :] - 1) & (nblk[None, :] > 0), axis=1)
    clear_flag = (is_last | (blk_ids >= nvb)).astype(jnp.int32)

    tm_c = min(TM_COMBINE, S)
    tm_d = min(TM_DISPATCH, S)
    xs = _dispatch(h2, dest_blocks_of(tm_d), clear_flag, tm=tm_d, tm_e=tm_e)
    tf = min(TF_EXPERT, d_ff)
    act = _gateup(xs, _tile_schedule(counts, nblk, blk_start, nvb, d_ff // tf, nb, tm_e), w_gate_up[0],
                  b_gate_up[0][:, None, :],
                  tm=tm_e, tf=tf)
    ys = _down(act, _tile_schedule(counts, nblk, blk_start, nvb, 1, nb, tm_e), w_down[0], b_down[0][:, None, :],
               tm=tm_e)

    out = _combine(dest_blocks_of(tm_c), x2, route, final_norm_g[None], ys, tm=tm_c)
    return out.reshape(B, S, D)
```

```python
import functools

import jax
import jax.numpy as jnp
from jax import lax
from jax.experimental import pallas as pl
from jax.experimental.pallas import tpu as pltpu

F32 = jnp.float32
BF16 = jnp.bfloat16
U32 = jnp.uint32

CHUNK = 128
GM_GROUP_DIM = 128
RET_HEADS = 8
N_EXPERTS = 32
ROUTE_FIELDS = 16
TOP_K = 4
SWIGLU_LIMIT = 7.0
SWIGLU_ALPHA = 1.702
ROPE_BASE = 10000.0
EPS = 1e-6
LANES = 128

VMEM_LIMIT_BYTES = 56 * 1024 * 1024


def _cparams(sem):
    return pltpu.CompilerParams(dimension_semantics=sem, vmem_limit_bytes=VMEM_LIMIT_BYTES)


SLAB_ROWS = 8
SLAB_PITCH = 9
DMA_QUEUES = 2


def _slab_store(ref, value, row0=0):
    rows, width = value.shape
    half = width // 2
    assert half == SLAB_ROWS * LANES
    lo = lax.bitcast_convert_type(value[:, :half].astype(BF16).astype(F32), U32)
    hi = lax.bitcast_convert_type(value[:, half:].astype(BF16).astype(F32), U32)
    words = (lo >> 16) | hi
    base = row0 * SLAB_PITCH
    for c in range(SLAB_ROWS):
        ref[pl.ds(base + c, rows, stride=SLAB_PITCH), :] = words[:, c * LANES:(c + 1) * LANES]
    for c in range(SLAB_ROWS, SLAB_PITCH):
        ref[pl.ds(base + c, rows, stride=SLAB_PITCH), :] = jnp.zeros((rows, LANES), U32)


def _slab_load(ref, rows, row0=0):
    base = row0 * SLAB_PITCH
    words = jnp.concatenate([ref[pl.ds(base + c, rows, stride=SLAB_PITCH), :] for c in range(SLAB_ROWS)], axis=1)
    lo = lax.bitcast_convert_type(words << 16, F32)
    hi = lax.bitcast_convert_type(words & jnp.uint32(0xFFFF0000), F32)
    return jnp.concatenate([lo, hi], axis=1)


def _gelu_exact(a):
    return 0.5 * a * (1.0 + lax.erf(a * (2.0 ** -0.5)))


def _rope_heads(a, cos2, sin2):
    outs = []
    for hd in range(a.shape[1] // LANES):
        ah = a[:, hd * LANES:(hd + 1) * LANES]
        outs.append(ah * cos2 + pltpu.roll(ah, LANES // 2, axis=1) * sin2)
    return jnp.concatenate(outs, axis=1)


def _segment_body(h_ref, w_ref, *refs, kind, k_scale):
    wb_ref = refs[-1]
    o_ref = refs[-3] if kind == "prenorm_gelu" else refs[-2]

    @pl.when(pl.program_id(1) == 0)
    def _():
        def cast_rows(r, carry):
            rows = pl.ds(pl.multiple_of(r * CAST_ROWS, CAST_ROWS), CAST_ROWS)
            wb_ref[rows, :] = w_ref[rows, :].astype(BF16)
            return carry

        lax.fori_loop(0, w_ref.shape[0] // CAST_ROWS, cast_rows, 0)

    if kind == "prenorm_gelu":
        g_ref, hn_ref = refs[0], refs[-2]
        xf = h_ref[...]
        ms = jnp.mean(xf * xf, axis=-1, keepdims=True)
        lhs = (xf * lax.rsqrt(ms + EPS) * g_ref[...]).astype(BF16)
        hn_ref[...] = lhs
    else:
        lhs = h_ref[...]
    acc = jnp.dot(lhs, wb_ref[...], preferred_element_type=F32)
    if kind in ("gelu", "prenorm_gelu"):
        out = _gelu_exact(acc)
    elif kind == "gelu_layernorm":
        lng_ref, lnb_ref = refs[0], refs[1]
        vf = _gelu_exact(acc)
        mu = jnp.mean(vf, axis=-1, keepdims=True)
        vc = vf - mu
        var = jnp.mean(vc * vc, axis=-1, keepdims=True)
        out = vc * lax.rsqrt(var + EPS) * lng_ref[...] + lnb_ref[...]
    elif kind == "rope":
        out = _rope_heads(acc, refs[0][...], refs[1][...])
    elif kind == "rope_scaled":
        out = _rope_heads(acc, refs[0][...], refs[1][...]) * k_scale
    elif kind == "identity":
        out = acc
    elif kind == "silu":
        out = acc * jax.nn.sigmoid(acc)
    elif kind == "sigmoid":
        out = jax.nn.sigmoid(acc)
    else:
        raise ValueError(kind)
    o_ref[...] = out.astype(o_ref.dtype)


def _segment(h, w_in, col0, width, kind, extras=(), *, tm, tn):
    S, D = h.shape
    assert width % tn == 0 and col0 % tn == 0 and S % tm == 0
    blk0 = col0 // tn
    out_specs = [pl.BlockSpec((tm, tn), lambda j, i: (i, j))]
    out_shape = [jax.ShapeDtypeStruct((S, width), BF16)]
    if kind == "gelu_layernorm":
        assert width == tn
        extra_specs = [pl.BlockSpec((1, tn), lambda j, i: (0, 0))] * 2
    elif kind in ("rope", "rope_scaled"):
        extra_specs = [pl.BlockSpec((tm, LANES), lambda j, i: (i, 0))] * 2
    elif kind == "prenorm_gelu":
        assert width == tn
        extra_specs = [pl.BlockSpec((1, D), lambda j, i: (0, 0))]
        out_specs.append(pl.BlockSpec((tm, D), lambda j, i: (i, 0)))
        out_shape.append(jax.ShapeDtypeStruct((S, D), BF16))
    else:
        extra_specs = []
    return pl.pallas_call(
        functools.partial(_segment_body, kind=kind, k_scale=float(LANES) ** -0.5),
        grid=(width // tn, S // tm),
        in_specs=[
            pl.BlockSpec((tm, D), lambda j, i: (i, 0)),
            pl.BlockSpec((D, tn), lambda j, i: (0, blk0 + j)),
        ] + extra_specs,
        out_specs=out_specs,
        out_shape=out_shape,
        scratch_shapes=[pltpu.VMEM((D, tn), BF16)],
        compiler_params=_cparams(("arbitrary", "arbitrary")),
        name="inproj_" + kind,
    )(h, w_in, *extras)


def _mixing_body(cd_ref, u_ref, vn_ref, q_ref, k_ref, v_ref, sg_ref, wm_ref, bs_ref, decay_ref, xi_ref, zeta_ref,
                 ga_ref, gr_ref, state_ref, *, n_chunks, dk, dv):
    @pl.when(pl.program_id(0) == 0)
    def _():
        state_ref[...] = jnp.zeros_like(state_ref)

    def chunk(c, carry):
        rows = pl.ds(pl.multiple_of(c * CHUNK, CHUNK), CHUNK)
        for g in range(wm_ref.shape[0]):
            cols = slice(g * GM_GROUP_DIM, (g + 1) * GM_GROUP_DIM)
            mixed = jnp.dot(wm_ref[g], vn_ref[rows, cols], preferred_element_type=F32) + bs_ref[g]
            ga_ref[rows, cols] = (u_ref[rows, cols].astype(F32) * mixed).astype(ga_ref.dtype)
        for hd in range(RET_HEADS):
            qc = q_ref[rows, hd * dk:(hd + 1) * dk]
            kc = k_ref[rows, hd * dk:(hd + 1) * dk]
            vc = v_ref[rows, hd * dv:(hd + 1) * dv]
            st = state_ref[hd]
            scores = lax.dot_general(qc, kc, (((1,), (1,)), ((), ())), preferred_element_type=F32) * decay_ref[hd]
            inner = jnp.dot(scores.astype(BF16), vc, preferred_element_type=F32)
            cross = jnp.dot(qc, st.astype(BF16), preferred_element_type=F32)
            xi = xi_ref[hd]
            o = inner + cross * jnp.concatenate([xi] * (dv // LANES), axis=1)
            kz = (kc.astype(F32) * zeta_ref[hd]).astype(BF16)
            kv = lax.dot_general(kz, vc, (((0,), (0,)), ((), ())), preferred_element_type=F32)
            state_ref[hd] = st * cd_ref[hd] + kv
            mu = jnp.mean(o, axis=-1, keepdims=True)
            oc = o - mu
            var = jnp.mean(oc * oc, axis=-1, keepdims=True)
            on = oc * lax.rsqrt(var + EPS)
            gr_ref[rows, hd * dv:(hd + 1) * dv] = (on * sg_ref[rows, hd * dv:(hd + 1) * dv].astype(F32)).astype(
                gr_ref.dtype)
        return carry

    lax.fori_loop(0, n_chunks, chunk, 0, unroll=True)


def _mixing(u, vn, q, k, v, sg, wm, bs_b, decay, xi_b, zeta_b, cd, *, tm):
    S, gm_width = u.shape
    qk_width, v_width = q.shape[1], v.shape[1]
    assert S % tm == 0 and tm % CHUNK == 0
    dk, dv = qk_width // RET_HEADS, v_width // RET_HEADS
    G = wm.shape[0]
    body = functools.partial(_mixing_body, n_chunks=tm // CHUNK, dk=dk, dv=dv)
    const3 = lambda i: (0, 0, 0)
    rows = lambda i: (i, 0)
    return pl.pallas_call(
        body,
        grid=(S // tm,),
        in_specs=[
            pl.BlockSpec(memory_space=pltpu.SMEM),
            pl.BlockSpec((tm, gm_width), rows),
            pl.BlockSpec((tm, gm_width), rows),
            pl.BlockSpec((tm, qk_width), rows),
            pl.BlockSpec((tm, qk_width), rows),
            pl.BlockSpec((tm, v_width), rows),
            pl.BlockSpec((tm, v_width), rows),
            pl.BlockSpec((G, CHUNK, CHUNK), const3),
            pl.BlockSpec((G, CHUNK, GM_GROUP_DIM), const3),
            pl.BlockSpec((RET_HEADS, CHUNK, CHUNK), const3),
            pl.BlockSpec((RET_HEADS, CHUNK, LANES), const3),
            pl.BlockSpec((RET_HEADS, CHUNK, dk), const3),
        ],
        out_specs=[
            pl.BlockSpec((tm, gm_width), lambda i: (i, 0)),
            pl.BlockSpec((tm, v_width), lambda i: (i, 0)),
        ],
        out_shape=[
            jax.ShapeDtypeStruct((S, gm_width), BF16),
            jax.ShapeDtypeStruct((S, v_width), BF16),
        ],
        scratch_shapes=[pltpu.VMEM((RET_HEADS, dk, dv), F32)],
        compiler_params=_cparams(("arbitrary",)),
        name="mixing",
    )(cd, u, vn, q, k, v, sg, wm, bs_b, decay, xi_b, zeta_b)


def _branches_body(ga_ref, gr_ref, sa_ref, sr_ref, wa_ref, wr_ref, o_ref):
    y_a = jnp.dot(ga_ref[...], wa_ref[...], preferred_element_type=F32)
    y_r = jnp.dot(gr_ref[...], wr_ref[...], preferred_element_type=F32)
    o_ref[...] = (sa_ref[...].astype(F32) * y_a + sr_ref[...].astype(F32) * y_r).astype(o_ref.dtype)


def _branches(ga, gr, gates, wa, wr, *, tm):
    S = ga.shape[0]
    D = wa.shape[1]
    const = lambda i: (0, 0)
    resident = functools.partial(pl.BlockSpec, index_map=const, pipeline_mode=pl.Buffered(1))
    return pl.pallas_call(
        _branches_body,
        grid=(S // tm,),
        in_specs=[
            pl.BlockSpec((tm, ga.shape[1]), lambda i: (i, 0)),
            pl.BlockSpec((tm, gr.shape[1]), lambda i: (i, 0)),
            pl.BlockSpec((tm, D), lambda i: (i, 0)),
            pl.BlockSpec((tm, D), lambda i: (i, 1)),
            resident(wa.shape),
            resident(wr.shape),
        ],
        out_specs=pl.BlockSpec((tm, D), lambda i: (i, 0)),
        out_shape=jax.ShapeDtypeStruct((S, D), BF16),
        compiler_params=_cparams(("arbitrary",)),
        name="branches",
    )(ga, gr, gates, gates, wa, wr)


def _merge_body(x_ref, m_ref, wo_ref, g2_ref, rw_ref, rb_ref, x2_ref, h2_ref, route_ref, route_t_ref, cnt_ref,
                run_ref):
    i = pl.program_id(0)
    tm = x_ref.shape[0]

    @pl.when(i == 0)
    def _():
        run_ref[...] = jnp.zeros_like(run_ref)

    sub = min(tm, MERGE_SUB_ROWS)
    neg_inf = jnp.float32(-jnp.inf)
    r_iota = lax.broadcasted_iota(jnp.int32, (sub, sub), 0)
    c_iota = lax.broadcasted_iota(jnp.int32, (sub, sub), 1)
    earlier = (r_iota < c_iota).astype(BF16)
    expert = lax.broadcasted_iota(jnp.int32, (N_EXPERTS, sub), 0)
    field = lax.broadcasted_iota(jnp.int32, (ROUTE_FIELDS, sub), 0)
    run = run_ref[:, 0:1]
    for j in range(tm // sub):
        rows = slice(j * sub, (j + 1) * sub)
        x2 = x_ref[rows, :] + jnp.dot(m_ref[rows, :], wo_ref[...], preferred_element_type=F32)
        x2_ref[rows, :] = x2
        ms = jnp.mean(x2 * x2, axis=-1, keepdims=True)
        h2 = x2 * lax.rsqrt(ms + EPS) * g2_ref[...]
        _slab_store(h2_ref, h2, row0=j * sub)

        h_hi = h2.astype(BF16)
        h_lo = (h2 - h_hi.astype(F32)).astype(BF16)
        hi_terms = jnp.dot(h_hi, rw_ref[...], preferred_element_type=F32)
        logits = (hi_terms[:, :LANES] + hi_terms[:, LANES:]
                  + jnp.dot(h_lo, rw_ref[:, :LANES], preferred_element_type=F32)) + rb_ref[...]
        work = logits.T[0:N_EXPERTS, :]

        vals, idxs, sels = [], [], []
        for _ in range(TOP_K):
            m = jnp.max(work, axis=0, keepdims=True)
            idx = jnp.min(jnp.where(work == m, expert, N_EXPERTS), axis=0, keepdims=True)
            sel = expert == idx
            vals.append(m)
            idxs.append(idx)
            sels.append(sel)
            work = jnp.where(sel, neg_inf, work)
        exps = [jnp.exp(v - vals[0]) for v in vals]
        denom = exps[0] + exps[1] + exps[2] + exps[3]
        gates = [e / denom for e in exps]

        onehot = jnp.zeros((N_EXPERTS, sub), F32)
        for sel in sels:
            onehot = onehot + sel.astype(F32)
        before = jnp.dot(onehot.astype(BF16), earlier, preferred_element_type=F32) + run
        ranks = [jnp.sum(jnp.where(sel, before, 0.0), axis=0, keepdims=True) for sel in sels]
        run = run + jnp.sum(onehot, axis=1, keepdims=True)

        fields = jnp.zeros((ROUTE_FIELDS, sub), F32)
        for k in range(TOP_K):
            fields = jnp.where(field == k, idxs[k].astype(F32), fields)
            fields = jnp.where(field == TOP_K + k, gates[k], fields)
            fields = jnp.where(field == 2 * TOP_K + k, ranks[k], fields)
        slab_t = jnp.concatenate([fields, jnp.zeros((LANES - ROUTE_FIELDS, sub), F32)], axis=0)
        route_ref[rows, :] = slab_t.T
        route_t_ref[:, rows] = fields

    run_ref[...] = jnp.broadcast_to(run, run_ref.shape)
    cnt_ref[...] = jnp.broadcast_to(run, cnt_ref.shape)


def _merge(x2d, merged, wo, g2, rw_split, rb, *, tm):
    S, D = x2d.shape
    assert S % tm == 0
    const = lambda i: (0, 0)
    resident = functools.partial(pl.BlockSpec, index_map=const, pipeline_mode=pl.Buffered(1))
    return pl.pallas_call(
        _merge_body,
        grid=(S // tm,),
        in_specs=[
            pl.BlockSpec((tm, D), lambda i: (i, 0)),
            pl.BlockSpec((tm, D), lambda i: (i, 0)),
            resident(wo.shape),
            pl.BlockSpec((1, D), const),
            resident(rw_split.shape),
            pl.BlockSpec((1, LANES), const),
        ],
        out_specs=[
            pl.BlockSpec((tm, D), lambda i: (i, 0)),
            pl.BlockSpec((tm * SLAB_PITCH, LANES), lambda i: (i, 0)),
            pl.BlockSpec((tm, LANES), lambda i: (i, 0)),
            pl.BlockSpec((ROUTE_FIELDS, tm), lambda i: (0, i)),
            pl.BlockSpec((N_EXPERTS, LANES), const),
        ],
        out_shape=[
            jax.ShapeDtypeStruct((S, D), F32),
            jax.ShapeDtypeStruct((S * SLAB_PITCH, LANES), U32),
            jax.ShapeDtypeStruct((S, LANES), F32),
            jax.ShapeDtypeStruct((ROUTE_FIELDS, S), F32),
            jax.ShapeDtypeStruct((N_EXPERTS, LANES), F32),
        ],
        scratch_shapes=[pltpu.VMEM((N_EXPERTS, LANES), F32)],
        compiler_params=_cparams(("arbitrary",)),
        name="merge",
    )(x2d, merged, wo, g2, rw_split, rb)


def _dispatch_body(clear_ref, dest_ref, h2_ref, xs_hbm, zeros_ref, sem, clear_sem, *, tm, blk_rows, nb):
    i = pl.program_id(0)

    def clear_copy(b):
        return pltpu.make_async_copy(zeros_ref, xs_hbm.at[pl.ds(pl.multiple_of(b * blk_rows, blk_rows), blk_rows)],
                                     clear_sem.at[0])

    @pl.when(i == 0)
    def _():
        zeros_ref[...] = jnp.zeros_like(zeros_ref)

        def start(b, carry):
            @pl.when(clear_ref[b] == 1)
            def _():
                clear_copy(b).start()
            return carry

        def drain(b, carry):
            @pl.when(clear_ref[b] == 1)
            def _():
                clear_copy(b).wait()
            return carry

        lax.fori_loop(0, nb, start, 0)
        lax.fori_loop(0, nb, drain, 0)

    def issue(t, carry):
        for k in range(TOP_K):
            d = dest_ref[0, 0, k * tm + t]
            pltpu.make_async_copy(h2_ref.at[pl.ds(t * SLAB_PITCH, SLAB_PITCH)],
                                  xs_hbm.at[pl.ds(d * SLAB_PITCH, SLAB_PITCH)],
                                  sem.at[0]).start(priority=k % DMA_QUEUES)
        return carry

    lax.fori_loop(0, tm, issue, 0, unroll=4)
    for k in range(TOP_K):
        pltpu.make_async_copy(h2_ref, xs_hbm.at[pl.ds(0, tm * SLAB_PITCH)], sem.at[0]).wait()


def _dispatch(h2_slab, dest, clear_flag, *, tm, tm_e):
    nb = clear_flag.shape[0]
    blk_rows = tm_e * SLAB_PITCH
    n = h2_slab.shape[0] // (tm * SLAB_PITCH)
    grid_spec = pltpu.PrefetchScalarGridSpec(
        num_scalar_prefetch=1,
        grid=(n,),
        in_specs=[
            pl.BlockSpec((1, 1, tm * TOP_K), lambda i, cf: (i, 0, 0), memory_space=pltpu.SMEM),
            pl.BlockSpec((tm * SLAB_PITCH, LANES), lambda i, cf: (i, 0)),
        ],
        out_specs=pl.BlockSpec(memory_space=pl.ANY),
        scratch_shapes=[pltpu.VMEM((blk_rows, LANES), U32), pltpu.SemaphoreType.DMA((1,)),
                        pltpu.SemaphoreType.DMA((1,))],
    )
    return pl.pallas_call(
        functools.partial(_dispatch_body, tm=tm, blk_rows=blk_rows, nb=nb),
        grid_spec=grid_spec,
        out_shape=jax.ShapeDtypeStruct((nb * blk_rows, LANES), U32),
        compiler_params=_cparams(("arbitrary",)),
        name="dispatch",
    )(clear_flag, dest, h2_slab)


def _tile_schedule(counts, nblk, blk_start, nvb, n_tiles, nb, tm):
    E = nblk.shape[0]
    n_steps = nb * n_tiles
    steps_per_e = n_tiles * nblk
    e_end = jnp.cumsum(steps_per_e)
    e_start = e_end - steps_per_e
    total = e_end[-1]
    e_ids = jnp.arange(E, dtype=jnp.int32)

    def locate(idx):
        inside = (e_start[None, :] <= idx[:, None]) & (idx[:, None] < e_end[None, :])
        pick = lambda v: jnp.sum(jnp.where(inside, v[None, :], 0), axis=1)
        r = idx - pick(e_start)
        nb_e = jnp.maximum(pick(nblk), 1)
        tile = sum((r >= t * nb_e).astype(jnp.int32) for t in range(1, n_tiles)) if n_tiles > 1 else 0 * r
        return pick(e_ids), tile, r - tile * nb_e, nb_e, pick(blk_start), pick(counts)

    s = jnp.arange(n_steps, dtype=jnp.int32)
    live = s < total
    sl = jnp.minimum(s, total - 1)
    e, tile, b, nb_e, bs_e, cnt_e = locate(sl)
    blk = bs_e + b
    first = (live & (b == 0)).astype(jnp.int32)
    dead = jnp.maximum(s - total, 0)
    out_blk = jnp.where(live, blk, nvb + dead // n_tiles)
    out_tile = jnp.where(live, tile, dead % n_tiles)
    pieces = (jnp.clip(cnt_e - b * tm, 0, tm) + EXPERT_ROW_STEP - 1) // EXPERT_ROW_STEP
    pieces = jnp.where(live, pieces, 0)
    nxt = sl - b + nb_e
    has_next = (live & (nxt < total)).astype(jnp.int32)
    ne, nt, _, _, _, _ = locate(jnp.minimum(nxt, total - 1))
    as_i32 = lambda a: a.astype(jnp.int32)
    return (as_i32(e), as_i32(tile), as_i32(blk), first, as_i32(out_blk), as_i32(out_tile), as_i32(total)[None],
            as_i32(ne), as_i32(nt), has_next, as_i32(pieces))


def _gateup_body(se_ref, st_ref, sb_ref, sf_ref, ob_ref, ot_ref, nl_ref, ne_ref, nt_ref, hn_ref, np_ref,
                 xs_ref, wgu_hbm, bg_ref, bu_ref, act_ref, stage_ref, wgb_ref, wub_ref, sem, *, tf, d_ff):
    s = pl.program_id(0)
    D = stage_ref.shape[1]
    tm = act_ref.shape[0]

    def tile_copies(e, t):
        return [pltpu.make_async_copy(
            wgu_hbm.at[e, pl.ds(0, D), pl.ds(pl.multiple_of(half * d_ff + t * tf, tf), tf)],
            stage_ref.at[half], sem.at[half]) for half in range(2)]

    @pl.when(sf_ref[s] == 1)
    def _():
        @pl.when(s == 0)
        def _():
            for c in tile_copies(se_ref[0], st_ref[0]):
                c.start()

        for c in tile_copies(se_ref[s], st_ref[s]):
            c.wait()

        def cast_rows(r, carry):
            rows = pl.ds(pl.multiple_of(r * CAST_ROWS, CAST_ROWS), CAST_ROWS)
            wgb_ref[rows, :] = stage_ref[0, rows, :].astype(BF16)
            wub_ref[rows, :] = stage_ref[1, rows, :].astype(BF16)
            return carry

        lax.fori_loop(0, D // CAST_ROWS, cast_rows, 0)

        @pl.when(hn_ref[s] == 1)
        def _():
            for c in tile_copies(ne_ref[s], nt_ref[s]):
                c.start()

    for pieces in range(1, tm // EXPERT_ROW_STEP + 1):
        @pl.when(np_ref[s] == pieces)
        def _(m=pieces * EXPERT_ROW_STEP):
            xb = _slab_load(xs_ref, m).astype(BF16)
            gate = jnp.dot(xb, wgb_ref[...], preferred_element_type=F32) + bg_ref[0]
            up = jnp.dot(xb, wub_ref[...], preferred_element_type=F32) + bu_ref[0]
            gate = jnp.minimum(gate, SWIGLU_LIMIT)
            up = jnp.clip(up, -SWIGLU_LIMIT, SWIGLU_LIMIT)
            glu = gate * jax.nn.sigmoid(gate * SWIGLU_ALPHA)
            act_ref[0:m, :] = ((up + 1.0) * glu).astype(act_ref.dtype)
            if m < tm:
                act_ref[m:tm, :] = jnp.zeros((tm - m, act_ref.shape[1]), act_ref.dtype)

    @pl.when(np_ref[s] == 0)
    def _():
        act_ref[...] = jnp.zeros_like(act_ref)


def _gateup(xs, sched, wgu, bgu, *, tm, tf):
    D = wgu.shape[1]
    R = xs.shape[0] // SLAB_PITCH
    d_ff = wgu.shape[2] // 2
    nj = d_ff // tf
    nb = R // tm
    assert d_ff % tf == 0 and R % tm == 0
    grid_spec = pltpu.PrefetchScalarGridSpec(
        num_scalar_prefetch=11,
        grid=(nb * nj,),
        in_specs=[
            pl.BlockSpec((tm * SLAB_PITCH, LANES), lambda s, se, st, sb, *_: (sb[s], 0)),
            pl.BlockSpec(memory_space=pl.ANY),
            pl.BlockSpec((1, 1, tf), lambda s, se, st, *_: (se[s], 0, st[s])),
            pl.BlockSpec((1, 1, tf), lambda s, se, st, *_: (se[s], 0, nj + st[s])),
        ],
        out_specs=pl.BlockSpec((tm, tf), lambda s, se, st, sb, sf, ob, ot, *_: (ob[s], ot[s])),
        scratch_shapes=[pltpu.VMEM((2, D, tf), F32), pltpu.VMEM((D, tf), BF16), pltpu.VMEM((D, tf), BF16),
                        pltpu.SemaphoreType.DMA((2,))],
    )
    return pl.pallas_call(
        functools.partial(_gateup_body, tf=tf, d_ff=d_ff),
        grid_spec=grid_spec,
        out_shape=jax.ShapeDtypeStruct((R, d_ff), BF16),
        compiler_params=_cparams(("arbitrary",)),
        name="gateup",
    )(*sched, xs, wgu, bgu, bgu)


def _down_body(se_ref, st_ref, sb_ref, sf_ref, ob_ref, ot_ref, nl_ref, ne_ref, nt_ref, hn_ref, np_ref,
               act_ref, wd_hbm, bd_ref, y_ref, stage_ref, wdb_ref, sem):
    s = pl.program_id(0)
    tm = act_ref.shape[0]

    def weight_copy(e):
        return pltpu.make_async_copy(wd_hbm.at[e], stage_ref, sem.at[0])

    @pl.when(sf_ref[s] == 1)
    def _():
        @pl.when(s == 0)
        def _():
            weight_copy(se_ref[0]).start()

        weight_copy(se_ref[s]).wait()
        wdb_ref[...] = stage_ref[...].astype(BF16)

        @pl.when(hn_ref[s] == 1)
        def _():
            weight_copy(ne_ref[s]).start()

    for pieces in range(1, tm // EXPERT_ROW_STEP + 1):
        @pl.when(np_ref[s] == pieces)
        def _(m=pieces * EXPERT_ROW_STEP):
            y = jnp.dot(act_ref[0:m, :], wdb_ref[...], preferred_element_type=F32) + bd_ref[0]
            _slab_store(y_ref, y)
            if m < tm:
                y_ref[m * SLAB_PITCH:tm * SLAB_PITCH, :] = jnp.zeros(((tm - m) * SLAB_PITCH, LANES), U32)

    @pl.when(np_ref[s] == 0)
    def _():
        y_ref[...] = jnp.zeros_like(y_ref)


def _down(act, sched, wd, bd, *, tm):
    R, d_ff = act.shape
    D = wd.shape[2]
    nb = R // tm
    grid_spec = pltpu.PrefetchScalarGridSpec(
        num_scalar_prefetch=11,
        grid=(nb,),
        in_specs=[
            pl.BlockSpec((tm, d_ff), lambda s, se, st, sb, *_: (sb[s], 0)),
            pl.BlockSpec(memory_space=pl.ANY),
            pl.BlockSpec((1, 1, D), lambda s, se, *_: (se[s], 0, 0)),
        ],
        out_specs=pl.BlockSpec((tm * SLAB_PITCH, LANES), lambda s, se, st, sb, sf, ob, *_: (ob[s], 0)),
        scratch_shapes=[pltpu.VMEM((d_ff, D), F32), pltpu.VMEM((d_ff, D), BF16), pltpu.SemaphoreType.DMA((1,))],
    )
    return pl.pallas_call(
        _down_body,
        grid_spec=grid_spec,
        out_shape=jax.ShapeDtypeStruct((R * SLAB_PITCH, LANES), U32),
        compiler_params=_cparams(("arbitrary",)),
        name="down",
    )(*sched, act, wd, bd)


def _combine_body(dest_cur_ref, dest_nxt_ref, x2_ref, route_ref, gf_ref, ys_hbm, o_ref, buf0_ref, buf1_ref, sem, *,
                  tm):
    i = pl.program_id(0)
    last = pl.num_programs(0) - 1
    bufs = (buf0_ref, buf1_ref)

    def slab_copy(dest_ref, t, k, slot):
        d = dest_ref[0, 0, k * tm + t]
        return pltpu.make_async_copy(ys_hbm.at[pl.ds(d * SLAB_PITCH, SLAB_ROWS)],
                                     bufs[slot].at[k, pl.ds(t * SLAB_PITCH, SLAB_ROWS)], sem.at[slot])

    def wait_rows(slot):
        for k in range(TOP_K):
            pltpu.make_async_copy(ys_hbm.at[pl.ds(0, tm * SLAB_ROWS)], bufs[slot].at[k, pl.ds(0, tm * SLAB_ROWS)],
                                  sem.at[slot]).wait()

    @pl.when(i == 0)
    def _():
        def issue(t, carry):
            for k in range(TOP_K):
                slab_copy(dest_cur_ref, t, k, 0).start(priority=k % DMA_QUEUES)
            return carry
        lax.fori_loop(0, tm, issue, 0, unroll=4)

    piece = tm // COMBINE_PIECES
    for slot in range(2):
        @pl.when(i % 2 == slot)
        def _(slot=slot):
            wait_rows(slot)
            for p in range(COMBINE_PIECES):
                r0 = p * piece
                for t in range(r0, r0 + piece):
                    for k in range(TOP_K):
                        slab_copy(dest_nxt_ref, t, k, 1 - slot).start(priority=k % DMA_QUEUES)
                x3 = x2_ref[r0:r0 + piece, :]
                route = route_ref[r0:r0 + piece, :]
                for k in range(TOP_K):
                    x3 = x3 + route[:, TOP_K + k:TOP_K + k + 1] * _slab_load(bufs[slot].at[k], piece, row0=r0)
                ms = jnp.mean(x3 * x3, axis=-1, keepdims=True)
                o_ref[r0:r0 + piece, :] = x3 * lax.rsqrt(ms + EPS) * gf_ref[...]

            @pl.when(i == last)
            def _():
                wait_rows(1 - slot)


def _combine(dest, x2, route, gf, ys_slab, *, tm):
    T, D = x2.shape
    n = T // tm
    assert T % tm == 0
    return pl.pallas_call(
        functools.partial(_combine_body, tm=tm),
        grid=(n,),
        in_specs=[
            pl.BlockSpec((1, 1, tm * TOP_K), lambda i: (i, 0, 0), memory_space=pltpu.SMEM),
            pl.BlockSpec((1, 1, tm * TOP_K), lambda i: (jnp.minimum(i + 1, n - 1), 0, 0), memory_space=pltpu.SMEM),
            pl.BlockSpec((tm, D), lambda i: (i, 0)),
            pl.BlockSpec((tm, LANES), lambda i: (i, 0)),
            pl.BlockSpec((1, D), lambda i: (0, 0)),
            pl.BlockSpec(memory_space=pl.ANY),
        ],
        out_specs=pl.BlockSpec((tm, D), lambda i: (i, 0)),
        out_shape=jax.ShapeDtypeStruct((T, D), F32),
        scratch_shapes=[pltpu.VMEM((TOP_K, tm * SLAB_PITCH, LANES), U32),
                        pltpu.VMEM((TOP_K, tm * SLAB_PITCH, LANES), U32), pltpu.SemaphoreType.DMA((2,))],
        compiler_params=_cparams(("arbitrary",)),
        name="combine",
    )(dest, dest, x2, route, gf, ys_slab)


def _retention_tables(S, dk):
    pos = jnp.arange(S, dtype=F32)
    inv_freq = ROPE_BASE ** (-jnp.arange(0, dk, 2, dtype=F32) / dk)
    ang = pos[:, None] * inv_freq[None, :]
    cos, sin = jnp.cos(ang), jnp.sin(ang)
    cos2 = jnp.concatenate([cos, cos], axis=-1)
    sin2 = jnp.concatenate([-sin, sin], axis=-1)
    log_gamma = jnp.log1p(-jnp.exp2(-5.0 - jnp.arange(RET_HEADS, dtype=F32)))
    idx = jnp.arange(CHUNK, dtype=F32)
    diff = idx[:, None] - idx[None, :]
    decay = jnp.where(diff[None] >= 0, jnp.exp(log_gamma[:, None, None] * jnp.maximum(diff, 0.0)[None]), 0.0)
    xi = jnp.exp(log_gamma[:, None] * (idx[None, :] + 1.0))
    zeta = jnp.exp(log_gamma[:, None] * (CHUNK - 1.0 - idx[None, :]))
    cd = jnp.exp(log_gamma * CHUNK)
    xi_b = jnp.broadcast_to(xi[:, :, None], (RET_HEADS, CHUNK, LANES))
    zeta_b = jnp.broadcast_to(zeta[:, :, None], (RET_HEADS, CHUNK, dk))
    return cos2, sin2, decay, xi_b, zeta_b, cd


TM_INPROJ = 1024
TM_PRENORM = 512
TN_INPROJ = 1024
TM_MIXING = 256
TM_MERGE = 512
MERGE_SUB_ROWS = 512
TM_EXPERT = 512
EXPERT_ROW_STEP = 128
CAST_ROWS = 128
TF_EXPERT = 1024
TM_DISPATCH = 512
TM_COMBINE = 256
COMBINE_PIECES = 8


def kernel(x, norm1_g, w_in, gm_ln_g, gm_ln_b, gm_ws, gm_b, w_proj_a, w_proj_r, w_out, norm2_g, router_w, router_b,
           w_gate_up, b_gate_up, w_down, b_down, final_norm_g):
    B, S, D = x.shape
    assert B == 1 and norm1_g.shape[0] == 1, "single sequence, depth 1"
    gm_width = w_proj_a.shape[1]
    v_width = w_proj_r.shape[1]
    qk_width = (w_in.shape[2] - 2 * gm_width - 2 * v_width - 2 * D) // 2
    dk = qk_width // RET_HEADS
    assert dk == LANES and gm_ws.shape[2] == CHUNK and D == 2 * SLAB_ROWS * LANES
    assert router_w.shape[2] == N_EXPERTS
    G = gm_ws.shape[1]
    E = router_w.shape[2]
    d_ff = w_down.shape[2]
    x2d = x.reshape(S, D)

    cos2, sin2, decay, xi_b, zeta_b, cd = _retention_tables(S, dk)
    tm_p = min(TM_INPROJ, S)
    a_u, h = _segment(x2d, w_in[0], 0, gm_width, "prenorm_gelu", (norm1_g[0][None],),
                      tm=min(TM_PRENORM, S), tn=TN_INPROJ)
    segments = (
        (gm_width, "gelu_layernorm", (gm_ln_g[0][None], gm_ln_b[0][None])),
        (qk_width, "rope", (cos2, sin2)),
        (qk_width, "rope_scaled", (cos2, sin2)),
        (v_width, "identity", ()),
        (v_width, "silu", ()),
        (2 * D, "sigmoid", ()),
    )
    outs, col0 = [], gm_width
    for width, kind, extras in segments:
        outs.append(_segment(h, w_in[0], col0, width, kind, extras, tm=tm_p, tn=TN_INPROJ)[0])
        col0 += width
    a_vn, r_q, r_k, r_v, r_sg, gates = outs

    causal = jnp.tril(jnp.ones((CHUNK, CHUNK), dtype=bool))
    wm = jnp.where(causal[None], gm_ws[0], 0.0).astype(BF16)
    bs_b = jnp.broadcast_to(gm_b[0][:, :, None], (G, CHUNK, GM_GROUP_DIM))
    ga, gr = _mixing(a_u, a_vn, r_q, r_k, r_v, r_sg, wm, bs_b, decay, xi_b, zeta_b, cd, tm=min(TM_MIXING, S))

    rw = jnp.pad(router_w[0], ((0, 0), (0, LANES - E)))
    rw_hi = rw.astype(BF16)
    rw_lo = (rw - rw_hi.astype(F32)).astype(BF16)
    rw_split = jnp.concatenate([rw_hi, rw_lo], axis=1)
    rb = jnp.pad(router_b[0], (0, LANES - E))[None]
    tm_m = min(TM_MERGE, S)
    merged = _branches(ga, gr, gates, w_proj_a[0].astype(BF16), w_proj_r[0].astype(BF16), tm=tm_m)
    x2, h2, route, route_t, cnt = _merge(x2d, merged, w_out[0].astype(BF16), norm2_g[0][None], rw_split, rb,
                                         tm=tm_m)

    tm_e = TM_EXPERT
    eidx = route_t[0:TOP_K].astype(jnp.int32)
    rank = route_t[2 * TOP_K:3 * TOP_K].astype(jnp.int32)
    counts = cnt[:, 0].astype(jnp.int32)
    nblk = (counts + tm_e - 1) // tm_e
    blk_end = jnp.cumsum(nblk)
    blk_start = blk_end - nblk
    row_start = blk_start * tm_e
    dest = rank
    for e in range(E):
        dest = dest + jnp.where(eidx == e, row_start[e], 0)

    def dest_blocks_of(tm):
        return dest.reshape(TOP_K, S // tm, tm).transpose(1, 0, 2).reshape(S // tm, 1, TOP_K * tm)

    nb = (S * TOP_K + tm_e - 1) // tm_e + E
    nvb = blk_end[-1].astype(jnp.int32)
    blk_ids = jnp.arange(nb, dtype=jnp.int32)
    is_last = jnp.any((blk_ids[:, None] == blk_end[None, :] - 1) & (nblk[None, :] > 0), axis=1)
    clear_flag = (is_last | (blk_ids >= nvb)).astype(jnp.int32)

    tm_c = min(TM_COMBINE, S)
    tm_d = min(TM_DISPATCH, S)
    xs = _dispatch(h2, dest_blocks_of(tm_d), clear_flag, tm=tm_d, tm_e=tm_e)
    tf = min(TF_EXPERT, d_ff)
    act = _gateup(xs, _tile_schedule(counts, nblk, blk_start, nvb, d_ff // tf, nb, tm_e), w_gate_up[0],
                  b_gate_up[0][:, None, :],
                  tm=tm_e, tf=tf)
    ys = _down(act, _tile_schedule(counts, nblk, blk_start, nvb, 1, nb, tm_e), w_down[0], b_down[0][:, None, :],
               tm=tm_e)

    out = _combine(dest_blocks_of(tm_c), x2, route, final_norm_g[None], ys, tm=tm_c)
    return out.reshape(B, S, D)
```

```python
import functools

import jax
import jax.numpy as jnp
from jax import lax
from jax.experimental import pallas as pl
from jax.experimental.pallas import tpu as pltpu

F32 = jnp.float32
BF16 = jnp.bfloat16
U32 = jnp.uint32

CHUNK = 128
GM_GROUP_DIM = 128
RET_HEADS = 8
N_EXPERTS = 32
ROUTE_FIELDS = 16
TOP_K = 4
SWIGLU_LIMIT = 7.0
SWIGLU_ALPHA = 1.702
ROPE_BASE = 10000.0
EPS = 1e-6
LANES = 128

VMEM_LIMIT_BYTES = 56 * 1024 * 1024


def _cparams(sem):
    return pltpu.CompilerParams(dimension_semantics=sem, vmem_limit_bytes=VMEM_LIMIT_BYTES)


SLAB_ROWS = 8
SLAB_PITCH = 9
DMA_QUEUES = 2


def _slab_store(ref, value, row0=0):
    rows, width = value.shape
    half = width // 2
    assert half == SLAB_ROWS * LANES
    lo = lax.bitcast_convert_type(value[:, :half].astype(BF16).astype(F32), U32)
    hi = lax.bitcast_convert_type(value[:, half:].astype(BF16).astype(F32), U32)
    words = (lo >> 16) | hi
    base = row0 * SLAB_PITCH
    for c in range(SLAB_ROWS):
        ref[pl.ds(base + c, rows, stride=SLAB_PITCH), :] = words[:, c * LANES:(c + 1) * LANES]
    for c in range(SLAB_ROWS, SLAB_PITCH):
        ref[pl.ds(base + c, rows, stride=SLAB_PITCH), :] = jnp.zeros((rows, LANES), U32)


def _slab_load(ref, rows, row0=0):
    base = row0 * SLAB_PITCH
    words = jnp.concatenate([ref[pl.ds(base + c, rows, stride=SLAB_PITCH), :] for c in range(SLAB_ROWS)], axis=1)
    lo = lax.bitcast_convert_type(words << 16, F32)
    hi = lax.bitcast_convert_type(words & jnp.uint32(0xFFFF0000), F32)
    return jnp.concatenate([lo, hi], axis=1)


def _gelu_exact(a):
    return 0.5 * a * (1.0 + lax.erf(a * (2.0 ** -0.5)))


def _rope_heads(a, cos2, sin2):
    outs = []
    for hd in range(a.shape[1] // LANES):
        ah = a[:, hd * LANES:(hd + 1) * LANES]
        outs.append(ah * cos2 + pltpu.roll(ah, LANES // 2, axis=1) * sin2)
    return jnp.concatenate(outs, axis=1)


def _segment_body(h_ref, w_ref, *refs, kind, k_scale):
    wb_ref = refs[-1]
    o_ref = refs[-3] if kind == "prenorm_gelu" else refs[-2]

    @pl.when(pl.program_id(1) == 0)
    def _():
        def cast_rows(r, carry):
            rows = pl.ds(pl.multiple_of(r * CAST_ROWS, CAST_ROWS), CAST_ROWS)
            wb_ref[rows, :] = w_ref[rows, :].astype(BF16)
            return carry

        lax.fori_loop(0, w_ref.shape[0] // CAST_ROWS, cast_rows, 0)

    if kind == "prenorm_gelu":
        g_ref, hn_ref = refs[0], refs[-2]
        xf = h_ref[...]
        ms = jnp.mean(xf * xf, axis=-1, keepdims=True)
        lhs = (xf * lax.rsqrt(ms + EPS) * g_ref[...]).astype(BF16)
        hn_ref[...] = lhs
    else:
        lhs = h_ref[...]
    acc = jnp.dot(lhs, wb_ref[...], preferred_element_type=F32)
    if kind in ("gelu", "prenorm_gelu"):
        out = _gelu_exact(acc)
    elif kind == "gelu_layernorm":
        lng_ref, lnb_ref = refs[0], refs[1]
        vf = _gelu_exact(acc)
        mu = jnp.mean(vf, axis=-1, keepdims=True)
        vc = vf - mu
        var = jnp.mean(vc * vc, axis=-1, keepdims=True)
        out = vc * lax.rsqrt(var + EPS) * lng_ref[...] + lnb_ref[...]
    elif kind == "rope":
        out = _rope_heads(acc, refs[0][...], refs[1][...])
    elif kind == "rope_scaled":
        out = _rope_heads(acc, refs[0][...], refs[1][...]) * k_scale
    elif kind == "identity":
        out = acc
    elif kind == "silu":
        out = acc * jax.nn.sigmoid(acc)
    elif kind == "sigmoid":
        out = jax.nn.sigmoid(acc)
    else:
        raise ValueError(kind)
    o_ref[...] = out.astype(o_ref.dtype)


def _segment(h, w_in, col0, width, kind, extras=(), *, tm, tn):
    S, D = h.shape
    assert width % tn == 0 and col0 % tn == 0 and S % tm == 0
    blk0 = col0 // tn
    out_specs = [pl.BlockSpec((tm, tn), lambda j, i: (i, j))]
    out_shape = [jax.ShapeDtypeStruct((S, width), BF16)]
    if kind == "gelu_layernorm":
        assert width == tn
        extra_specs = [pl.BlockSpec((1, tn), lambda j, i: (0, 0))] * 2
    elif kind in ("rope", "rope_scaled"):
        extra_specs = [pl.BlockSpec((tm, LANES), lambda j, i: (i, 0))] * 2
    elif kind == "prenorm_gelu":
        assert width == tn
        extra_specs = [pl.BlockSpec((1, D), lambda j, i: (0, 0))]
        out_specs.append(pl.BlockSpec((tm, D), lambda j, i: (i, 0)))
        out_shape.append(jax.ShapeDtypeStruct((S, D), BF16))
    else:
        extra_specs = []
    return pl.pallas_call(
        functools.partial(_segment_body, kind=kind, k_scale=float(LANES) ** -0.5),
        grid=(width // tn, S // tm),
        in_specs=[
            pl.BlockSpec((tm, D), lambda j, i: (i, 0)),
            pl.BlockSpec((D, tn), lambda j, i: (0, blk0 + j)),
        ] + extra_specs,
        out_specs=out_specs,
        out_shape=out_shape,
        scratch_shapes=[pltpu.VMEM((D, tn), BF16)],
        compiler_params=_cparams(("arbitrary", "arbitrary")),
        name="inproj_" + kind,
    )(h, w_in, *extras)


def _mixing_body(cd_ref, u_ref, vn_ref, q_ref, k_ref, v_ref, sg_ref, wm_ref, bs_ref, decay_ref, xi_ref, zeta_ref,
                 ga_ref, gr_ref, state_ref, *, n_chunks, dk, dv):
    @pl.when(pl.program_id(0) == 0)
    def _():
        state_ref[...] = jnp.zeros_like(state_ref)

    def chunk(c, carry):
        rows = pl.ds(pl.multiple_of(c * CHUNK, CHUNK), CHUNK)
        for g in range(wm_ref.shape[0]):
            cols = slice(g * GM_GROUP_DIM, (g + 1) * GM_GROUP_DIM)
            mixed = jnp.dot(wm_ref[g], vn_ref[rows, cols], preferred_element_type=F32) + bs_ref[g]
            ga_ref[rows, cols] = (u_ref[rows, cols].astype(F32) * mixed).astype(ga_ref.dtype)
        for hd in range(RET_HEADS):
            qc = q_ref[rows, hd * dk:(hd + 1) * dk]
            kc = k_ref[rows, hd * dk:(hd + 1) * dk]
            vc = v_ref[rows, hd * dv:(hd + 1) * dv]
            st = state_ref[hd]
            scores = lax.dot_general(qc, kc, (((1,), (1,)), ((), ())), preferred_element_type=F32) * decay_ref[hd]
            inner = jnp.dot(scores.astype(BF16), vc, preferred_element_type=F32)
            cross = jnp.dot(qc, st.astype(BF16), preferred_element_type=F32)
            xi = xi_ref[hd]
            o = inner + cross * jnp.concatenate([xi] * (dv // LANES), axis=1)
            kz = (kc.astype(F32) * zeta_ref[hd]).astype(BF16)
            kv = lax.dot_general(kz, vc, (((0,), (0,)), ((), ())), preferred_element_type=F32)
            state_ref[hd] = st * cd_ref[hd] + kv
            mu = jnp.mean(o, axis=-1, keepdims=True)
            oc = o - mu
            var = jnp.mean(oc * oc, axis=-1, keepdims=True)
            on = oc * lax.rsqrt(var + EPS)
            gr_ref[rows, hd * dv:(hd + 1) * dv] = (on * sg_ref[rows, hd * dv:(hd + 1) * dv].astype(F32)).astype(
                gr_ref.dtype)
        return carry

    lax.fori_loop(0, n_chunks, chunk, 0, unroll=True)


def _mixing(u, vn, q, k, v, sg, wm, bs_b, decay, xi_b, zeta_b, cd, *, tm):
    S, gm_width = u.shape
    qk_width, v_width = q.shape[1], v.shape[1]
    assert S % tm == 0 and tm % CHUNK == 0
    dk, dv = qk_width // RET_HEADS, v_width // RET_HEADS
    G = wm.shape[0]
    body = functools.partial(_mixing_body, n_chunks=tm // CHUNK, dk=dk, dv=dv)
    const3 = lambda i: (0, 0, 0)
    rows = lambda i: (i, 0)
    return pl.pallas_call(
        body,
        grid=(S // tm,),
        in_specs=[
            pl.BlockSpec(memory_space=pltpu.SMEM),
            pl.BlockSpec((tm, gm_width), rows),
            pl.BlockSpec((tm, gm_width), rows),
            pl.BlockSpec((tm, qk_width), rows),
            pl.BlockSpec((tm, qk_width), rows),
            pl.BlockSpec((tm, v_width), rows),
            pl.BlockSpec((tm, v_width), rows),
            pl.BlockSpec((G, CHUNK, CHUNK), const3),
            pl.BlockSpec((G, CHUNK, GM_GROUP_DIM), const3),
            pl.BlockSpec((RET_HEADS, CHUNK, CHUNK), const3),
            pl.BlockSpec((RET_HEADS, CHUNK, LANES), const3),
            pl.BlockSpec((RET_HEADS, CHUNK, dk), const3),
        ],
        out_specs=[
            pl.BlockSpec((tm, gm_width), lambda i: (i, 0)),
            pl.BlockSpec((tm, v_width), lambda i: (i, 0)),
        ],
        out_shape=[
            jax.ShapeDtypeStruct((S, gm_width), BF16),
            jax.ShapeDtypeStruct((S, v_width), BF16),
        ],
        scratch_shapes=[pltpu.VMEM((RET_HEADS, dk, dv), F32)],
        compiler_params=_cparams(("arbitrary",)),
        name="mixing",
    )(cd, u, vn, q, k, v, sg, wm, bs_b, decay, xi_b, zeta_b)


def _branches_body(ga_ref, gr_ref, sa_ref, sr_ref, wa_ref, wr_ref, o_ref):
    y_a = jnp.dot(ga_ref[...], wa_ref[...], preferred_element_type=F32)
    y_r = jnp.dot(gr_ref[...], wr_ref[...], preferred_element_type=F32)
    o_ref[...] = (sa_ref[...].astype(F32) * y_a + sr_ref[...].astype(F32) * y_r).astype(o_ref.dtype)


def _branches(ga, gr, gates, wa, wr, *, tm):
    S = ga.shape[0]
    D = wa.shape[1]
    const = lambda i: (0, 0)
    resident = functools.partial(pl.BlockSpec, index_map=const, pipeline_mode=pl.Buffered(1))
    return pl.pallas_call(
        _branches_body,
        grid=(S // tm,),
        in_specs=[
            pl.BlockSpec((tm, ga.shape[1]), lambda i: (i, 0)),
            pl.BlockSpec((tm, gr.shape[1]), lambda i: (i, 0)),
            pl.BlockSpec((tm, D), lambda i: (i, 0)),
            pl.BlockSpec((tm, D), lambda i: (i, 1)),
            resident(wa.shape),
            resident(wr.shape),
        ],
        out_specs=pl.BlockSpec((tm, D), lambda i: (i, 0)),
        out_shape=jax.ShapeDtypeStruct((S, D), BF16),
        compiler_params=_cparams(("arbitrary",)),
        name="branches",
    )(ga, gr, gates, gates, wa, wr)


def _merge_body(x_ref, m_ref, wo_ref, g2_ref, rw_ref, rb_ref, x2_ref, h2_ref, route_ref, route_t_ref, cnt_ref,
                run_ref):
    i = pl.program_id(0)
    tm = x_ref.shape[0]

    @pl.when(i == 0)
    def _():
        run_ref[...] = jnp.zeros_like(run_ref)

    sub = min(tm, MERGE_SUB_ROWS)
    neg_inf = jnp.float32(-jnp.inf)
    r_iota = lax.broadcasted_iota(jnp.int32, (sub, sub), 0)
    c_iota = lax.broadcasted_iota(jnp.int32, (sub, sub), 1)
    earlier = (r_iota < c_iota).astype(BF16)
    expert = lax.broadcasted_iota(jnp.int32, (N_EXPERTS, sub), 0)
    field = lax.broadcasted_iota(jnp.int32, (ROUTE_FIELDS, sub), 0)
    run = run_ref[:, 0:1]
    for j in range(tm // sub):
        rows = slice(j * sub, (j + 1) * sub)
        x2 = x_ref[rows, :] + jnp.dot(m_ref[rows, :], wo_ref[...], preferred_element_type=F32)
        x2_ref[rows, :] = x2
        ms = jnp.mean(x2 * x2, axis=-1, keepdims=True)
        h2 = x2 * lax.rsqrt(ms + EPS) * g2_ref[...]
        _slab_store(h2_ref, h2, row0=j * sub)

        h_hi = h2.astype(BF16)
        h_lo = (h2 - h_hi.astype(F32)).astype(BF16)
        hi_terms = jnp.dot(h_hi, rw_ref[...], preferred_element_type=F32)
        logits = (hi_terms[:, :LANES] + hi_terms[:, LANES:]
                  + jnp.dot(h_lo, rw_ref[:, :LANES], preferred_element_type=F32)) + rb_ref[...]
        work = logits.T[0:N_EXPERTS, :]

        vals, idxs, sels = [], [], []
        for _ in range(TOP_K):
            m = jnp.max(work, axis=0, keepdims=True)
            idx = jnp.min(jnp.where(work == m, expert, N_EXPERTS), axis=0, keepdims=True)
            sel = expert == idx
            vals.append(m)
            idxs.append(idx)
            sels.append(sel)
            work = jnp.where(sel, neg_inf, work)
        exps = [jnp.exp(v - vals[0]) for v in vals]
        denom = exps[0] + exps[1] + exps[2] + exps[3]
        gates = [e / denom for e in exps]

        onehot = jnp.zeros((N_EXPERTS, sub), F32)
        for sel in sels:
            onehot = onehot + sel.astype(F32)
        before = jnp.dot(onehot.astype(BF16), earlier, preferred_element_type=F32) + run
        ranks = [jnp.sum(jnp.where(sel, before, 0.0), axis=0, keepdims=True) for sel in sels]
        run = run + jnp.sum(onehot, axis=1, keepdims=True)

        fields = jnp.zeros((ROUTE_FIELDS, sub), F32)
        for k in range(TOP_K):
            fields = jnp.where(field == k, idxs[k].astype(F32), fields)
            fields = jnp.where(field == TOP_K + k, gates[k], fields)
            fields = jnp.where(field == 2 * TOP_K + k, ranks[k], fields)
        slab_t = jnp.concatenate([fields, jnp.zeros((LANES - ROUTE_FIELDS, sub), F32)], axis=0)
        route_ref[rows, :] = slab_t.T
        route_t_ref[:, rows] = fields

    run_ref[...] = jnp.broadcast_to(run, run_ref.shape)
    cnt_ref[...] = jnp.broadcast_to(run, cnt_ref.shape)


def _merge(x2d, merged, wo, g2, rw_split, rb, *, tm):
    S, D = x2d.shape
    assert S % tm == 0
    const = lambda i: (0, 0)
    resident = functools.partial(pl.BlockSpec, index_map=const, pipeline_mode=pl.Buffered(1))
    return pl.pallas_call(
        _merge_body,
        grid=(S // tm,),
        in_specs=[
            pl.BlockSpec((tm, D), lambda i: (i, 0)),
            pl.BlockSpec((tm, D), lambda i: (i, 0)),
            resident(wo.shape),
            pl.BlockSpec((1, D), const),
            resident(rw_split.shape),
            pl.BlockSpec((1, LANES), const),
        ],
        out_specs=[
            pl.BlockSpec((tm, D), lambda i: (i, 0)),
            pl.BlockSpec((tm * SLAB_PITCH, LANES), lambda i: (i, 0)),
            pl.BlockSpec((tm, LANES), lambda i: (i, 0)),
            pl.BlockSpec((ROUTE_FIELDS, tm), lambda i: (0, i)),
            pl.BlockSpec((N_EXPERTS, LANES), const),
        ],
        out_shape=[
            jax.ShapeDtypeStruct((S, D), F32),
            jax.ShapeDtypeStruct((S * SLAB_PITCH, LANES), U32),
            jax.ShapeDtypeStruct((S, LANES), F32),
            jax.ShapeDtypeStruct((ROUTE_FIELDS, S), F32),
            jax.ShapeDtypeStruct((N_EXPERTS, LANES), F32),
        ],
        scratch_shapes=[pltpu.VMEM((N_EXPERTS, LANES), F32)],
        compiler_params=_cparams(("arbitrary",)),
        name="merge",
    )(x2d, merged, wo, g2, rw_split, rb)


def _dispatch_body(clear_ref, dest_ref, h2_ref, xs_hbm, zeros_ref, sem, clear_sem, *, tm, blk_rows, nb):
    i = pl.program_id(0)

    def clear_copy(b):
        return pltpu.make_async_copy(zeros_ref, xs_hbm.at[pl.ds(pl.multiple_of(b * blk_rows, blk_rows), blk_rows)],
                                     clear_sem.at[0])

    @pl.when(i == 0)
    def _():
        zeros_ref[...] = jnp.zeros_like(zeros_ref)

        def start(b, carry):
            @pl.when(clear_ref[b] == 1)
            def _():
                clear_copy(b).start()
            return carry

        def drain(b, carry):
            @pl.when(clear_ref[b] == 1)
            def _():
                clear_copy(b).wait()
            return carry

        lax.fori_loop(0, nb, start, 0)
        lax.fori_loop(0, nb, drain, 0)

    def issue(t, carry):
        for k in range(TOP_K):
            d = dest_ref[0, 0, k * tm + t]
            pltpu.make_async_copy(h2_ref.at[pl.ds(t * SLAB_PITCH, SLAB_PITCH)],
                                  xs_hbm.at[pl.ds(d * SLAB_PITCH, SLAB_PITCH)],
                                  sem.at[0]).start(priority=k % DMA_QUEUES)
        return carry

    lax.fori_loop(0, tm, issue, 0, unroll=4)
    for k in range(TOP_K):
        pltpu.make_async_copy(h2_ref, xs_hbm.at[pl.ds(0, tm * SLAB_PITCH)], sem.at[0]).wait()


def _dispatch(h2_slab, dest, clear_flag, *, tm, tm_e):
    nb = clear_flag.shape[0]
    blk_rows = tm_e * SLAB_PITCH
    n = h2_slab.shape[0] // (tm * SLAB_PITCH)
    grid_spec = pltpu.PrefetchScalarGridSpec(
        num_scalar_prefetch=1,
        grid=(n,),
        in_specs=[
            pl.BlockSpec((1, 1, tm * TOP_K), lambda i, cf: (i, 0, 0), memory_space=pltpu.SMEM),
            pl.BlockSpec((tm * SLAB_PITCH, LANES), lambda i, cf: (i, 0)),
        ],
        out_specs=pl.BlockSpec(memory_space=pl.ANY),
        scratch_shapes=[pltpu.VMEM((blk_rows, LANES), U32), pltpu.SemaphoreType.DMA((1,)),
                        pltpu.SemaphoreType.DMA((1,))],
    )
    return pl.pallas_call(
        functools.partial(_dispatch_body, tm=tm, blk_rows=blk_rows, nb=nb),
        grid_spec=grid_spec,
        out_shape=jax.ShapeDtypeStruct((nb * blk_rows, LANES), U32),
        compiler_params=_cparams(("arbitrary",)),
        name="dispatch",
    )(clear_flag, dest, h2_slab)


def _tile_schedule(counts, nblk, blk_start, nvb, n_tiles, nb, tm):
    E = nblk.shape[0]
    n_steps = nb * n_tiles
    steps_per_e = n_tiles * nblk
    e_end = jnp.cumsum(steps_per_e)
    e_start = e_end - steps_per_e
    total = e_end[-1]
    e_ids = jnp.arange(E, dtype=jnp.int32)

    def locate(idx):
        inside = (e_start[None, :] <= idx[:, None]) & (idx[:, None] < e_end[None, :])
        pick = lambda v: jnp.sum(jnp.where(inside, v[None, :], 0), axis=1)
        r = idx - pick(e_start)
        nb_e = jnp.maximum(pick(nblk), 1)
        tile = sum((r >= t * nb_e).astype(jnp.int32) for t in range(1, n_tiles)) if n_tiles > 1 else 0 * r
        return pick(e_ids), tile, r - tile * nb_e, nb_e, pick(blk_start), pick(counts)

    s = jnp.arange(n_steps, dtype=jnp.int32)
    live = s < total
    sl = jnp.minimum(s, total - 1)
    e, tile, b, nb_e, bs_e, cnt_e = locate(sl)
    blk = bs_e + b
    first = (live & (b == 0)).astype(jnp.int32)
    dead = jnp.maximum(s - total, 0)
    out_blk = jnp.where(live, blk, nvb + dead // n_tiles)
    out_tile = jnp.where(live, tile, dead % n_tiles)
    pieces = (jnp.clip(cnt_e - b * tm, 0, tm) + EXPERT_ROW_STEP - 1) // EXPERT_ROW_STEP
    pieces = jnp.where(live, pieces, 0)
    nxt = sl - b + nb_e
    has_next = (live & (nxt < total)).astype(jnp.int32)
    ne, nt, _, _, _, _ = locate(jnp.minimum(nxt, total - 1))
    as_i32 = lambda a: a.astype(jnp.int32)
    return (as_i32(e), as_i32(tile), as_i32(blk), first, as_i32(out_blk), as_i32(out_tile), as_i32(total)[None],
            as_i32(ne), as_i32(nt), has_next, as_i32(pieces))


def _gateup_body(se_ref, st_ref, sb_ref, sf_ref, ob_ref, ot_ref, nl_ref, ne_ref, nt_ref, hn_ref, np_ref,
                 xs_ref, wgu_hbm, bg_ref, bu_ref, act_ref, stage_ref, wgb_ref, wub_ref, sem, *, tf, d_ff):
    s = pl.program_id(0)
    D = stage_ref.shape[1]
    tm = act_ref.shape[0]

    def tile_copies(e, t):
        return [pltpu.make_async_copy(
            wgu_hbm.at[e, pl.ds(0, D), pl.ds(pl.multiple_of(half * d_ff + t * tf, tf), tf)],
            stage_ref.at[half], sem.at[half]) for half in range(2)]

    @pl.when(sf_ref[s] == 1)
    def _():
        @pl.when(s == 0)
        def _():
            for c in tile_copies(se_ref[0], st_ref[0]):
                c.start()

        for c in tile_copies(se_ref[s], st_ref[s]):
            c.wait()

        def cast_rows(r, carry):
            rows = pl.ds(pl.multiple_of(r * CAST_ROWS, CAST_ROWS), CAST_ROWS)
            wgb_ref[rows, :] = stage_ref[0, rows, :].astype(BF16)
            wub_ref[rows, :] = stage_ref[1, rows, :].astype(BF16)
            return carry

        lax.fori_loop(0, D // CAST_ROWS, cast_rows, 0)

        @pl.when(hn_ref[s] == 1)
        def _():
            for c in tile_copies(ne_ref[s], nt_ref[s]):
                c.start()

    for pieces in range(1, tm // EXPERT_ROW_STEP + 1):
        @pl.when(np_ref[s] == pieces)
        def _(m=pieces * EXPERT_ROW_STEP):
            xb = _slab_load(xs_ref, m).astype(BF16)
            gate = jnp.dot(xb, wgb_ref[...], preferred_element_type=F32) + bg_ref[0]
            up = jnp.dot(xb, wub_ref[...], preferred_element_type=F32) + bu_ref[0]
            gate = jnp.minimum(gate, SWIGLU_LIMIT)
            up = jnp.clip(up, -SWIGLU_LIMIT, SWIGLU_LIMIT)
            glu = gate * jax.nn.sigmoid(gate * SWIGLU_ALPHA)
            act_ref[0:m, :] = ((up + 1.0) * glu).astype(act_ref.dtype)
            if m < tm:
                act_ref[m:tm, :] = jnp.zeros((tm - m, act_ref.shape[1]), act_ref.dtype)

    @pl.when(np_ref[s] == 0)
    def _():
        act_ref[...] = jnp.zeros_like(act_ref)


def _gateup(xs, sched, wgu, bgu, *, tm, tf):
    D = wgu.shape[1]
    R = xs.shape[0] // SLAB_PITCH
    d_ff = wgu.shape[2] // 2
    nj = d_ff // tf
    nb = R // tm
    assert d_ff % tf == 0 and R % tm == 0
    grid_spec = pltpu.PrefetchScalarGridSpec(
        num_scalar_prefetch=11,
        grid=(nb * nj,),
        in_specs=[
            pl.BlockSpec((tm * SLAB_PITCH, LANES), lambda s, se, st, sb, *_: (sb[s], 0)),
            pl.BlockSpec(memory_space=pl.ANY),
            pl.BlockSpec((1, 1, tf), lambda s, se, st, *_: (se[s], 0, st[s])),
            pl.BlockSpec((1, 1, tf), lambda s, se, st, *_: (se[s], 0, nj + st[s])),
        ],
        out_specs=pl.BlockSpec((tm, tf), lambda s, se, st, sb, sf, ob, ot, *_: (ob[s], ot[s])),
        scratch_shapes=[pltpu.VMEM((2, D, tf), F32), pltpu.VMEM((D, tf), BF16), pltpu.VMEM((D, tf), BF16),
                        pltpu.SemaphoreType.DMA((2,))],
    )
    return pl.pallas_call(
        functools.partial(_gateup_body, tf=tf, d_ff=d_ff),
        grid_spec=grid_spec,
        out_shape=jax.ShapeDtypeStruct((R, d_ff), BF16),
        compiler_params=_cparams(("arbitrary",)),
        name="gateup",
    )(*sched, xs, wgu, bgu, bgu)


def _down_body(se_ref, st_ref, sb_ref, sf_ref, ob_ref, ot_ref, nl_ref, ne_ref, nt_ref, hn_ref, np_ref,
               act_ref, wd_hbm, bd_ref, y_ref, stage_ref, wdb_ref, sem):
    s = pl.program_id(0)
    tm = act_ref.shape[0]

    def weight_copy(e):
        return pltpu.make_async_copy(wd_hbm.at[e], stage_ref, sem.at[0])

    @pl.when(sf_ref[s] == 1)
    def _():
        @pl.when(s == 0)
        def _():
            weight_copy(se_ref[0]).start()

        weight_copy(se_ref[s]).wait()
        wdb_ref[...] = stage_ref[...].astype(BF16)

        @pl.when(hn_ref[s] == 1)
        def _():
            weight_copy(ne_ref[s]).start()

    for pieces in range(1, tm // EXPERT_ROW_STEP + 1):
        @pl.when(np_ref[s] == pieces)
        def _(m=pieces * EXPERT_ROW_STEP):
            y = jnp.dot(act_ref[0:m, :], wdb_ref[...], preferred_element_type=F32) + bd_ref[0]
            _slab_store(y_ref, y)
            if m < tm:
                y_ref[m * SLAB_PITCH:tm * SLAB_PITCH, :] = jnp.zeros(((tm - m) * SLAB_PITCH, LANES), U32)

    @pl.when(np_ref[s] == 0)
    def _():
        y_ref[...] = jnp.zeros_like(y_ref)


def _down(act, sched, wd, bd, *, tm):
    R, d_ff = act.shape
    D = wd.shape[2]
    nb = R // tm
    grid_spec = pltpu.PrefetchScalarGridSpec(
        num_scalar_prefetch=11,
        grid=(nb,),
        in_specs=[
            pl.BlockSpec((tm, d_ff), lambda s, se, st, sb, *_: (sb[s], 0)),
            pl.BlockSpec(memory_space=pl.ANY),
            pl.BlockSpec((1, 1, D), lambda s, se, *_: (se[s], 0, 0)),
        ],
        out_specs=pl.BlockSpec((tm * SLAB_PITCH, LANES), lambda s, se, st, sb, sf, ob, *_: (ob[s], 0)),
        scratch_shapes=[pltpu.VMEM((d_ff, D), F32), pltpu.VMEM((d_ff, D), BF16), pltpu.SemaphoreType.DMA((1,))],
    )
    return pl.pallas_call(
        _down_body,
        grid_spec=grid_spec,
        out_shape=jax.ShapeDtypeStruct((R * SLAB_PITCH, LANES), U32),
        compiler_params=_cparams(("arbitrary",)),
        name="down",
    )(*sched, act, wd, bd)


def _combine_body(dest_cur_ref, dest_nxt_ref, x2_ref, route_ref, gf_ref, ys_hbm, o_ref, buf0_ref, buf1_ref, sem, *,
                  tm):
    i = pl.program_id(0)
    last = pl.num_programs(0) - 1
    bufs = (buf0_ref, buf1_ref)

    def slab_copy(dest_ref, t, k, slot):
        d = dest_ref[0, 0, k * tm + t]
        return pltpu.make_async_copy(ys_hbm.at[pl.ds(d * SLAB_PITCH, SLAB_ROWS)],
                                     bufs[slot].at[k, pl.ds(t * SLAB_PITCH, SLAB_ROWS)], sem.at[slot])

    def wait_rows(slot):
        for k in range(TOP_K):
            pltpu.make_async_copy(ys_hbm.at[pl.ds(0, tm * SLAB_ROWS)], bufs[slot].at[k, pl.ds(0, tm * SLAB_ROWS)],
                                  sem.at[slot]).wait()

    @pl.when(i == 0)
    def _():
        def issue(t, carry):
            for k in range(TOP_K):
                slab_copy(dest_cur_ref, t, k, 0).start(priority=k % DMA_QUEUES)
            return carry
        lax.fori_loop(0, tm, issue, 0, unroll=4)

    piece = tm // COMBINE_PIECES
    for slot in range(2):
        @pl.when(i % 2 == slot)
        def _(slot=slot):
            wait_rows(slot)
            for p in range(COMBINE_PIECES):
                r0 = p * piece
                for t in range(r0, r0 + piece):
                    for k in range(TOP_K):
                        slab_copy(dest_nxt_ref, t, k, 1 - slot).start(priority=k % DMA_QUEUES)
                x3 = x2_ref[r0:r0 + piece, :]
                route = route_ref[r0:r0 + piece, :]
                for k in range(TOP_K):
                    x3 = x3 + route[:, TOP_K + k:TOP_K + k + 1] * _slab_load(bufs[slot].at[k], piece, row0=r0)
                ms = jnp.mean(x3 * x3, axis=-1, keepdims=True)
                o_ref[r0:r0 + piece, :] = x3 * lax.rsqrt(ms + EPS) * gf_ref[...]

            @pl.when(i == last)
            def _():
                wait_rows(1 - slot)


def _combine(dest, x2, route, gf, ys_slab, *, tm):
    T, D = x2.shape
    n = T // tm
    assert T % tm == 0
    return pl.pallas_call(
        functools.partial(_combine_body, tm=tm),
        grid=(n,),
        in_specs=[
            pl.BlockSpec((1, 1, tm * TOP_K), lambda i: (i, 0, 0), memory_space=pltpu.SMEM),
            pl.BlockSpec((1, 1, tm * TOP_K), lambda i: (jnp.minimum(i + 1, n - 1), 0, 0), memory_space=pltpu.SMEM),
            pl.BlockSpec((tm, D), lambda i: (i, 0)),
            pl.BlockSpec((tm, LANES), lambda i: (i, 0)),
            pl.BlockSpec((1, D), lambda i: (0, 0)),
            pl.BlockSpec(memory_space=pl.ANY),
        ],
        out_specs=pl.BlockSpec((tm, D), lambda i: (i, 0)),
        out_shape=jax.ShapeDtypeStruct((T, D), F32),
        scratch_shapes=[pltpu.VMEM((TOP_K, tm * SLAB_PITCH, LANES), U32),
                        pltpu.VMEM((TOP_K, tm * SLAB_PITCH, LANES), U32), pltpu.SemaphoreType.DMA((2,))],
        compiler_params=_cparams(("arbitrary",)),
        name="combine",
    )(dest, dest, x2, route, gf, ys_slab)


def _retention_tables(S, dk):
    pos = jnp.arange(S, dtype=F32)
    inv_freq = ROPE_BASE ** (-jnp.arange(0, dk, 2, dtype=F32) / dk)
    ang = pos[:, None] * inv_freq[None, :]
    cos, sin = jnp.cos(ang), jnp.sin(ang)
    cos2 = jnp.concatenate([cos, cos], axis=-1)
    sin2 = jnp.concatenate([-sin, sin], axis=-1)
    log_gamma = jnp.log1p(-jnp.exp2(-5.0 - jnp.arange(RET_HEADS, dtype=F32)))
    idx = jnp.arange(CHUNK, dtype=F32)
    diff = idx[:, None] - idx[None, :]
    decay = jnp.where(diff[None] >= 0, jnp.exp(log_gamma[:, None, None] * jnp.maximum(diff, 0.0)[None]), 0.0)
    xi = jnp.exp(log_gamma[:, None] * (idx[None, :] + 1.0))
    zeta = jnp.exp(log_gamma[:, None] * (CHUNK - 1.0 - idx[None, :]))
    cd = jnp.exp(log_gamma * CHUNK)
    xi_b = jnp.broadcast_to(xi[:, :, None], (RET_HEADS, CHUNK, LANES))
    zeta_b = jnp.broadcast_to(zeta[:, :, None], (RET_HEADS, CHUNK, dk))
    return cos2, sin2, decay, xi_b, zeta_b, cd


TM_INPROJ = 1024
TM_PRENORM = 512
TN_INPROJ = 1024
TM_MIXING = 512
TM_MERGE = 512
MERGE_SUB_ROWS = 512
TM_EXPERT = 512
EXPERT_ROW_STEP = 128
CAST_ROWS = 128
TF_EXPERT = 1024
TM_DISPATCH = 512
TM_COMBINE = 256
COMBINE_PIECES = 8


def kernel(x, norm1_g, w_in, gm_ln_g, gm_ln_b, gm_ws, gm_b, w_proj_a, w_proj_r, w_out, norm2_g, router_w, router_b,
           w_gate_up, b_gate_up, w_down, b_down, final_norm_g):
    B, S, D = x.shape
    assert B == 1 and norm1_g.shape[0] == 1, "single sequence, depth 1"
    gm_width = w_proj_a.shape[1]
    v_width = w_proj_r.shape[1]
    qk_width = (w_in.shape[2] - 2 * gm_width - 2 * v_width - 2 * D) // 2
    dk = qk_width // RET_HEADS
    assert dk == LANES and gm_ws.shape[2] == CHUNK and D == 2 * SLAB_ROWS * LANES
    assert router_w.shape[2] == N_EXPERTS
    G = gm_ws.shape[1]
    E = router_w.shape[2]
    d_ff = w_down.shape[2]
    x2d = x.reshape(S, D)

    cos2, sin2, decay, xi_b, zeta_b, cd = _retention_tables(S, dk)
    tm_p = min(TM_INPROJ, S)
    a_u, h = _segment(x2d, w_in[0], 0, gm_width, "prenorm_gelu", (norm1_g[0][None],),
                      tm=min(TM_PRENORM, S), tn=TN_INPROJ)
    segments = (
        (gm_width, "gelu_layernorm", (gm_ln_g[0][None], gm_ln_b[0][None])),
        (qk_width, "rope", (cos2, sin2)),
        (qk_width, "rope_scaled", (cos2, sin2)),
        (v_width, "identity", ()),
        (v_width, "silu", ()),
        (2 * D, "sigmoid", ()),
    )
    outs, col0 = [], gm_width
    for width, kind, extras in segments:
        outs.append(_segment(h, w_in[0], col0, width, kind, extras, tm=tm_p, tn=TN_INPROJ)[0])
        col0 += width
    a_vn, r_q, r_k, r_v, r_sg, gates = outs

    causal = jnp.tril(jnp.ones((CHUNK, CHUNK), dtype=bool))
    wm = jnp.where(causal[None], gm_ws[0], 0.0).astype(BF16)
    bs_b = jnp.broadcast_to(gm_b[0][:, :, None], (G, CHUNK, GM_GROUP_DIM))
    ga, gr = _mixing(a_u, a_vn, r_q, r_k, r_v, r_sg, wm, bs_b, decay, xi_b, zeta_b, cd, tm=min(TM_MIXING, S))

    rw = jnp.pad(router_w[0], ((0, 0), (0, LANES - E)))
    rw_hi = rw.astype(BF16)
    rw_lo = (rw - rw_hi.astype(F32)).astype(BF16)
    rw_split = jnp.concatenate([rw_hi, rw_lo], axis=1)
    rb = jnp.pad(router_b[0], (0, LANES - E))[None]
    tm_m = min(TM_MERGE, S)
    merged = _branches(ga, gr, gates, w_proj_a[0].astype(BF16), w_proj_r[0].astype(BF16), tm=tm_m)
    x2, h2, route, route_t, cnt = _merge(x2d, merged, w_out[0].astype(BF16), norm2_g[0][None], rw_split, rb,
                                         tm=tm_m)

    tm_e = TM_EXPERT
    eidx = route_t[0:TOP_K].astype(jnp.int32)
    rank = route_t[2 * TOP_K:3 * TOP_K].astype(jnp.int32)
    counts = cnt[:, 0].astype(jnp.int32)
    nblk = (counts + tm_e - 1) // tm_e
    blk_end = jnp.cumsum(nblk)
    blk_start = blk_end - nblk
    row_start = blk_start * tm_e
    dest = rank
    for e in range(E):
        dest = dest + jnp.where(eidx == e, row_start[e], 0)

    def dest_blocks_of(tm):
        return dest.reshape(TOP_K, S // tm, tm).transpose(1, 0, 2).reshape(S // tm, 1, TOP_K * tm)

    nb = (S * TOP_K + tm_e - 1) // tm_e + E
    nvb = blk_end[-1].astype(jnp.int32)
    blk_ids = jnp.arange(nb, dtype=jnp.int32)
    is_last = jnp.any((blk_ids[:, None] == blk_end[None, :] - 1) & (nblk[None, :] > 0), axis=1)
    clear_flag = (is_last | (blk_ids >= nvb)).astype(jnp.int32)

    tm_c = min(TM_COMBINE, S)
    tm_d = min(TM_DISPATCH, S)
    xs = _dispatch(h2, dest_blocks_of(tm_d), clear_flag, tm=tm_d, tm_e=tm_e)
    tf = min(TF_EXPERT, d_ff)
    act = _gateup(xs, _tile_schedule(counts, nblk, blk_start, nvb, d_ff // tf, nb, tm_e), w_gate_up[0],
                  b_gate_up[0][:, None, :],
                  tm=tm_e, tf=tf)
    ys = _down(act, _tile_schedule(counts, nblk, blk_start, nvb, 1, nb, tm_e), w_down[0], b_down[0][:, None, :],
               tm=tm_e)

    out = _combine(dest_blocks_of(tm_c), x2, route, final_norm_g[None], ys, tm=tm_c)
    return out.reshape(B, S, D)
```
